```python
import jax, jax.numpy as jnp
from jax import lax
import numpy as np

D_MODEL = 2048
BATCH = 2
SEQ = 4096
DEPTH = 1
DEC_BATCH = 16
DEC_SEQ = 32
PAST_LEN = 4096

CHUNK = 64
Q_BLOCK = 128
D_FF = 5632
NORM_EPS = 1e-6
GDN_HEADS = 8
GDN_DK = 128
GDN_DV = 128
GDN_CONV = 4
GDN_CHUNK = 64
GDN_KEY_DIM = GDN_HEADS * GDN_DK
GDN_VAL_DIM = GDN_HEADS * GDN_DV
GDN_CONV_DIM = 2 * GDN_KEY_DIM + GDN_VAL_DIM
MLA_HEADS = 8
MLA_NOPE = 128
MLA_ROPE = 64
MLA_V = 128
MLA_KV_RANK = 512
MLA_SCALE = (MLA_NOPE + MLA_ROPE) ** -0.5
ROPE_THETA = 10000.0
IN_SPLIT_SIZES = (GDN_CONV_DIM, GDN_VAL_DIM, GDN_HEADS, GDN_HEADS,
                  MLA_HEADS * (MLA_NOPE + MLA_ROPE), MLA_KV_RANK, MLA_ROPE,
                  D_MODEL, D_MODEL)
IN_SPLITS = tuple(int(s) for s in np.cumsum(IN_SPLIT_SIZES)[:-1])
D_IN = int(sum(IN_SPLIT_SIZES))

kernel_name = 'hybrid_gdn_mla_streaming_step'


def rmsnorm(x, g):
    xf = x.astype(jnp.float32)
    y = xf * lax.rsqrt(jnp.mean(xf * xf, axis=-1, keepdims=True) + NORM_EPS)
    return (y * g.astype(jnp.float32)).astype(x.dtype)


def l2norm(x):
    return x * lax.rsqrt(jnp.sum(x * x, axis=-1, keepdims=True) + 1e-6)


def swiglu(x, wg, wu, wd):
    return (jax.nn.silu(x @ wg) * (x @ wu)) @ wd


def rope_tables(pos):
    inv = ROPE_THETA ** (-jnp.arange(0, MLA_ROPE, 2, dtype=jnp.float32) / MLA_ROPE)
    ang = pos.astype(jnp.float32)[:, None] * inv[None, :]
    return jnp.cos(ang), jnp.sin(ang)


def apply_rope(x, cos, sin):
    xf = x.astype(jnp.float32)
    x1, x2 = jnp.split(xf, 2, axis=-1)
    return jnp.concatenate([x1 * cos - x2 * sin, x2 * cos + x1 * sin], axis=-1).astype(x.dtype)


def causal_conv(x, hist, w):
    T = x.shape[1]
    xp = jnp.concatenate([hist.astype(x.dtype), x], axis=1)
    y = xp[:, 0:T] * w[0]
    for j in range(1, GDN_CONV):
        y = y + xp[:, j:j + T] * w[j]
    return jax.nn.silu(y), xp[:, -(GDN_CONV - 1):]


def gdn_chunked(q, k, v, g, beta, s0):
    B, T, H, DK = q.shape
    DV = v.shape[-1]
    C = GDN_CHUNK
    pad = (-T) % C
    n = (T + pad) // C

    def blocks(a):
        a = jnp.pad(a, [(0, 0), (0, pad)] + [(0, 0)] * (a.ndim - 2))
        a = a.reshape((B, n, C) + a.shape[2:])
        return jnp.moveaxis(a, 3, 2).swapaxes(0, 1)

    idx = jnp.arange(C)
    incl = idx[:, None] >= idx[None, :]
    strict = idx[:, None] > idx[None, :]
    eye = jnp.eye(C, dtype=jnp.float32)

    def step(S, blk):
        qc, kc, vc, gc, bc = blk
        gcum = jnp.cumsum(gc, axis=-1)
        diff = gcum[..., :, None] - gcum[..., None, :]
        decay = jnp.where(incl, jnp.exp(jnp.where(incl, diff, 0.0)), 0.0)
        kk = jnp.einsum('bhik,bhjk->bhij', kc, kc)
        a_mat = eye + jnp.where(strict, bc[..., :, None] * kk * decay, 0.0)
        rhs = jnp.concatenate([vc * bc[..., None],
                               kc * (bc * jnp.exp(gcum))[..., None]], axis=-1)
        sol = lax.linalg.triangular_solve(a_mat, rhs, left_side=True, lower=True,
                                          unit_diagonal=True)
        u, w = sol[..., :DV], sol[..., DV:]
        v_new = u - jnp.einsum('bhck,bhkv->bhcv', w, S)
        qk = jnp.einsum('bhik,bhjk->bhij', qc, kc) * decay
        o = (jnp.einsum('bhik,bhkv->bhiv', qc * jnp.exp(gcum)[..., None], S)
             + jnp.einsum('bhij,bhjv->bhiv', qk, v_new))
        g_last = gcum[..., -1:]
        S_new = (S * jnp.exp(g_last)[..., None]
                 + jnp.einsum('bhck,bhcv->bhkv', kc * jnp.exp(g_last - gcum)[..., None], v_new))
        return S_new, o

    s_final, o = lax.scan(step, s0, (blocks(q), blocks(k), blocks(v), blocks(g), blocks(beta)))
    o = jnp.moveaxis(o.swapaxes(0, 1), 2, 3).reshape(B, n * C, H, DV)[:, :T]
    return o, s_final


def mla_attend(q_lat, q_rope, ckv, krope, q_pos, k_pos):
    s = (jnp.einsum('bqhc,bkc->bhqk', q_lat, ckv)
         + jnp.einsum('bqhr,bkr->bhqk', q_rope, krope)).astype(jnp.float32) * MLA_SCALE
    mask = (q_pos[:, None] // CHUNK) >= (k_pos[None, :] // CHUNK)
    p = jax.nn.softmax(jnp.where(mask, s, -jnp.inf), axis=-1).astype(ckv.dtype)
    return jnp.einsum('bhqk,bkc->bqhc', p, ckv)


def token_mixer(h, pos, conv_hist, ssm0, ckv_past, krope_past, p):
    B, T, _ = h.shape
    proj = h @ p['w_in']
    qkv, z, b_raw, a_raw, q_mla, ckv_raw, krope_raw, gate_gdn, gate_mla = jnp.split(
        proj, IN_SPLITS, axis=-1)

    qkv_c, conv_new = causal_conv(qkv, conv_hist, p['gdn_conv_w'])
    q, k, v = jnp.split(qkv_c.astype(jnp.float32), [GDN_KEY_DIM, 2 * GDN_KEY_DIM], axis=-1)
    q = l2norm(q.reshape(B, T, GDN_HEADS, GDN_DK)) * (GDN_DK ** -0.5)
    k = l2norm(k.reshape(B, T, GDN_HEADS, GDN_DK))
    v = v.reshape(B, T, GDN_HEADS, GDN_DV)
    beta = jax.nn.sigmoid(b_raw.astype(jnp.float32))
    g = -jnp.exp(p['gdn_a_log'].astype(jnp.float32)) * jax.nn.softplus(
        a_raw.astype(jnp.float32) + p['gdn_dt_bias'].astype(jnp.float32))
    o, ssm_new = gdn_chunked(q, k, v, g, beta, ssm0.astype(jnp.float32))
    o = rmsnorm(o, p['gdn_norm_w']) * jax.nn.silu(
        z.reshape(B, T, GDN_HEADS, GDN_DV).astype(jnp.float32))
    o_gdn = o.reshape(B, T, GDN_VAL_DIM).astype(h.dtype)

    cos, sin = rope_tables(pos)
    q_mla = q_mla.reshape(B, T, MLA_HEADS, MLA_NOPE + MLA_ROPE)
    q_nope = q_mla[..., :MLA_NOPE]
    q_rope = apply_rope(q_mla[..., MLA_NOPE:], cos[:, None, :], sin[:, None, :])
    ckv = rmsnorm(ckv_raw, p['mla_kv_norm'])
    krope = apply_rope(krope_raw, cos, sin)
    q_lat = jnp.einsum('bthd,chd->bthc', q_nope, p['mla_w_uk'])
    if ckv_past is None:
        nq = T // Q_BLOCK

        def to_blocks(a):
            return a.reshape((B, nq, Q_BLOCK) + a.shape[2:]).swapaxes(0, 1)

        out_lat = lax.map(
            lambda blk: mla_attend(blk[0], blk[1], ckv, krope, blk[2], pos),
            (to_blocks(q_lat), to_blocks(q_rope), pos.reshape(nq, Q_BLOCK)))
        out_lat = out_lat.swapaxes(0, 1).reshape(B, T, MLA_HEADS, MLA_KV_RANK)
    else:
        past = ckv_past.shape[1]
        ckv_all = jnp.concatenate([ckv_past.astype(ckv.dtype), ckv], axis=1)
        krope_all = jnp.concatenate([krope_past.astype(krope.dtype), krope], axis=1)
        out_lat = mla_attend(q_lat, q_rope, ckv_all, krope_all, pos,
                             jnp.arange(past + T, dtype=jnp.int32))
    o_mla = jnp.einsum('bthc,chd->bthd', out_lat, p['mla_w_uv']).reshape(B, T, MLA_HEADS * MLA_V)

    merged = (jax.nn.sigmoid(gate_gdn) * (o_gdn @ p['w_br_gdn'])
              + jax.nn.sigmoid(gate_mla) * (o_mla @ p['w_br_mla']))
    return merged @ p['w_out'], conv_new, ssm_new.astype(ssm0.dtype), ckv, krope


def trunk_layer(x, pos, conv_hist, ssm0, ckv_past, krope_past, p):
    x = x + 0.5 * rmsnorm(swiglu(rmsnorm(x, p['ffn1_norm_pre']), p['ffn1_w_gate'],
                                 p['ffn1_w_up'], p['ffn1_w_down']), p['ffn1_norm_post'])
    m, conv_new, ssm_new, ckv, krope = token_mixer(
        rmsnorm(x, p['mix_norm_pre']), pos, conv_hist, ssm0, ckv_past, krope_past, p)
    x = x + rmsnorm(m, p['mix_norm_post'])
    x = x + 0.5 * rmsnorm(swiglu(rmsnorm(x, p['ffn2_norm_pre']), p['ffn2_w_gate'],
                                 p['ffn2_w_up'], p['ffn2_w_down']), p['ffn2_norm_post'])
    return x, conv_new, ssm_new, ckv, krope


def setup_inputs(seed: int = 0) -> dict:
    key = jax.random.key(seed)
    ks = iter(jax.random.split(key, 32))
    f32 = jnp.float32

    def nrm(shape, scale):
        return jax.random.normal(next(ks), shape, f32) * scale

    def gain(n):
        return 1.0 + 0.01 * jax.random.normal(next(ks), (DEPTH, n), f32)

    d = {}
    d['x_prompt'] = nrm((BATCH, SEQ, D_MODEL), 1.0)
    d['x_sample'] = nrm((DEC_BATCH, DEC_SEQ, D_MODEL), 1.0)
    d['state_gdn_conv'] = nrm((DEPTH, DEC_BATCH, GDN_CONV - 1, GDN_CONV_DIM), 1.0)
    d['state_gdn_ssm'] = nrm((DEPTH, DEC_BATCH, GDN_HEADS, GDN_DK, GDN_DV), 0.1)
    d['cache_mla_ckv'] = nrm((DEPTH, DEC_BATCH, PAST_LEN, MLA_KV_RANK), 1.0)
    d['cache_mla_krope'] = nrm((DEPTH, DEC_BATCH, PAST_LEN, MLA_ROPE), 1.0)
    d['ffn1_norm_pre'] = gain(D_MODEL)
    d['ffn1_w_gate'] = nrm((DEPTH, D_MODEL, D_FF), D_MODEL ** -0.5)
    d['ffn1_w_up'] = nrm((DEPTH, D_MODEL, D_FF), D_MODEL ** -0.5)
    d['ffn1_w_down'] = nrm((DEPTH, D_FF, D_MODEL), D_FF ** -0.5)
    d['ffn1_norm_post'] = gain(D_MODEL)
    d['mix_norm_pre'] = gain(D_MODEL)
    d['w_in'] = nrm((DEPTH, D_MODEL, D_IN), D_MODEL ** -0.5)
    d['gdn_conv_w'] = nrm((DEPTH, GDN_CONV, GDN_CONV_DIM), GDN_CONV ** -0.5)
    d['gdn_a_log'] = jnp.log(jax.random.uniform(next(ks), (DEPTH, GDN_HEADS), f32, 1.0, 16.0))
    dt = jax.random.uniform(next(ks), (DEPTH, GDN_HEADS), f32, 1e-3, 0.1)
    d['gdn_dt_bias'] = dt + jnp.log(-jnp.expm1(-dt))
    d['gdn_norm_w'] = gain(GDN_DV)
    d['mla_kv_norm'] = gain(MLA_KV_RANK)
    d['mla_w_uk'] = nrm((DEPTH, MLA_KV_RANK, MLA_HEADS, MLA_NOPE), MLA_KV_RANK ** -0.5)
    d['mla_w_uv'] = nrm((DEPTH, MLA_KV_RANK, MLA_HEADS, MLA_V), MLA_KV_RANK ** -0.5)
    d['w_br_gdn'] = nrm((DEPTH, GDN_VAL_DIM, D_MODEL), GDN_VAL_DIM ** -0.5)
    d['w_br_mla'] = nrm((DEPTH, MLA_HEADS * MLA_V, D_MODEL), (MLA_HEADS * MLA_V) ** -0.5)
    d['w_out'] = nrm((DEPTH, D_MODEL, D_MODEL), D_MODEL ** -0.5)
    d['mix_norm_post'] = gain(D_MODEL)
    d['ffn2_norm_pre'] = gain(D_MODEL)
    d['ffn2_w_gate'] = nrm((DEPTH, D_MODEL, D_FF), D_MODEL ** -0.5)
    d['ffn2_w_up'] = nrm((DEPTH, D_MODEL, D_FF), D_MODEL ** -0.5)
    d['ffn2_w_down'] = nrm((DEPTH, D_FF, D_MODEL), D_FF ** -0.5)
    d['ffn2_norm_post'] = gain(D_MODEL)
    return d


def reference(x_prompt, x_sample, state_gdn_conv, state_gdn_ssm, cache_mla_ckv, cache_mla_krope,
              ffn1_norm_pre, ffn1_w_gate, ffn1_w_up, ffn1_w_down, ffn1_norm_post,
              mix_norm_pre, w_in, gdn_conv_w, gdn_a_log, gdn_dt_bias, gdn_norm_w,
              mla_kv_norm, mla_w_uk, mla_w_uv, w_br_gdn, w_br_mla, w_out, mix_norm_post,
              ffn2_norm_pre, ffn2_w_gate, ffn2_w_up, ffn2_w_down, ffn2_norm_post):
    layer_params = dict(
        ffn1_norm_pre=ffn1_norm_pre, ffn1_w_gate=ffn1_w_gate, ffn1_w_up=ffn1_w_up,
        ffn1_w_down=ffn1_w_down, ffn1_norm_post=ffn1_norm_post, mix_norm_pre=mix_norm_pre,
        w_in=w_in, gdn_conv_w=gdn_conv_w, gdn_a_log=gdn_a_log, gdn_dt_bias=gdn_dt_bias,
        gdn_norm_w=gdn_norm_w, mla_kv_norm=mla_kv_norm, mla_w_uk=mla_w_uk, mla_w_uv=mla_w_uv,
        w_br_gdn=w_br_gdn, w_br_mla=w_br_mla, w_out=w_out, mix_norm_post=mix_norm_post,
        ffn2_norm_pre=ffn2_norm_pre, ffn2_w_gate=ffn2_w_gate, ffn2_w_up=ffn2_w_up,
        ffn2_w_down=ffn2_w_down, ffn2_norm_post=ffn2_norm_post)
    b_p, t_p = x_prompt.shape[0], x_prompt.shape[1]
    t_s = x_sample.shape[1]
    past = cache_mla_ckv.shape[2]
    pos_prompt = jnp.arange(t_p, dtype=jnp.int32)
    pos_sample = past + jnp.arange(t_s, dtype=jnp.int32)
    yp, ys = x_prompt, x_sample
    conv_p, ssm_p, ckv_p, krope_p = [], [], [], []
    conv_s, ssm_s, ckv_s, krope_s = [], [], [], []
    for l in range(DEPTH):
        p = {name: w[l] for name, w in layer_params.items()}
        yp, c1, s1, k1, r1 = trunk_layer(
            yp, pos_prompt, jnp.zeros((b_p, GDN_CONV - 1, GDN_CONV_DIM), yp.dtype),
            jnp.zeros((b_p, GDN_HEADS, GDN_DK, GDN_DV), state_gdn_ssm.dtype), None, None, p)
        ys, c2, s2, k2, r2 = trunk_layer(
            ys, pos_sample, state_gdn_conv[l], state_gdn_ssm[l], cache_mla_ckv[l],
            cache_mla_krope[l], p)
        conv_p.append(c1); ssm_p.append(s1); ckv_p.append(k1); krope_p.append(r1)
        conv_s.append(c2); ssm_s.append(s2); ckv_s.append(k2); krope_s.append(r2)
    return (yp, ys,
            jnp.stack(conv_p), jnp.stack(ssm_p), jnp.stack(ckv_p), jnp.stack(krope_p),
            jnp.stack(conv_s), jnp.stack(ssm_s), jnp.stack(ckv_s), jnp.stack(krope_s))
```

```python
import functools

import jax
import jax.numpy as jnp
from jax import lax
from jax.experimental import pallas as pl
from jax.experimental.pallas import tpu as pltpu

F32 = jnp.float32
BF16 = jnp.bfloat16

CHUNK = 64
LOG2_CHUNK = 6
NORM_EPS = 1e-6
GDN_HEADS = 8
GDN_DK = 128
GDN_DV = 128
GDN_CONV = 4
GDN_KEY_DIM = GDN_HEADS * GDN_DK
GDN_VAL_DIM = GDN_HEADS * GDN_DV
GDN_CONV_DIM = 2 * GDN_KEY_DIM + GDN_VAL_DIM
MLA_HEADS = 8
MLA_NOPE = 128
MLA_ROPE = 64
MLA_V = 128
MLA_KV_RANK = 512
MLA_SCALE = (MLA_NOPE + MLA_ROPE) ** -0.5
ROPE_THETA = 10000.0

LANES = 128
SUBLANES = 8
VMEM_CAP_BYTES = 56 * 2 ** 20
MLA_QK = MLA_KV_RANK + LANES
HI = lax.Precision.HIGHEST


def _params(semantics, vmem_bytes):
    limit = int(min(max(vmem_bytes, 16 * 2 ** 20), VMEM_CAP_BYTES))
    return pltpu.CompilerParams(dimension_semantics=semantics, vmem_limit_bytes=limit)


def _dot(a, b, prec=None):
    return lax.dot_general(a, b, (((1,), (0,)), ((), ())), precision=prec,
                           preferred_element_type=F32)


def _dot_nt(a, b, prec=None):
    return lax.dot_general(a, b, (((1,), (1,)), ((), ())), precision=prec,
                           preferred_element_type=F32)


def _dot_tn(a, b, prec=None):
    return lax.dot_general(a, b, (((0,), (0,)), ((), ())), precision=prec,
                           preferred_element_type=F32)


def _rms(y, g):
    return y * lax.rsqrt(jnp.mean(y * y, axis=-1, keepdims=True) + NORM_EPS) * g


def _silu(x):
    return x * jax.nn.sigmoid(x)


def _tile(n, pref):
    t = min(n, pref)
    assert n % t == 0, (n, pref)
    return t


def _norm_kernel(x_ref, g_ref, o_ref):
    o_ref[...] = _rms(x_ref[...], g_ref[...]).astype(BF16)


def _norm_cast(x, g):
    m, d = x.shape
    tm = _tile(m, 512)
    return pl.pallas_call(
        _norm_kernel,
        grid=(m // tm,),
        in_specs=[pl.BlockSpec((tm, d), lambda i: (i, 0)),
                  pl.BlockSpec((1, d), lambda i: (0, 0))],
        out_specs=pl.BlockSpec((tm, d), lambda i: (i, 0)),
        out_shape=jax.ShapeDtypeStruct((m, d), BF16),
        compiler_params=_params(("parallel",), 2 * tm * d * 6 + 4 * tm * d * 4),
        name="norm_cast",
    )(x, g)


def _ffn_kernel(xn_ref, x_ref, wg_ref, wu_ref, wd_ref, gpost_ref, gnext_ref, *rest, nf, emit_next):
    if emit_next:
        xo_ref, xno_ref, acc_ref = rest
    else:
        xo_ref, acc_ref = rest
    f = pl.program_id(1)
    xn = xn_ref[...]
    gate = _dot(xn, wg_ref[...])
    up = _dot(xn, wu_ref[...])
    h = (_silu(gate) * up).astype(BF16)
    part = _dot(h, wd_ref[...])

    @pl.when(f == 0)
    def _():
        acc_ref[...] = part

    @pl.when(f > 0)
    def _():
        acc_ref[...] += part

    @pl.when(f == nf - 1)
    def _():
        xnew = x_ref[...] + 0.5 * _rms(acc_ref[...], gpost_ref[...])
        xo_ref[...] = xnew
        if emit_next:
            xno_ref[...] = _rms(xnew, gnext_ref[...]).astype(BF16)


def _ffn(xn, x, wg, wu, wd, gpost, gnext, emit_next):
    m, d = x.shape
    dff = wg.shape[1]
    tm = _tile(m, 512)
    tf = _tile(dff, 512)
    nf = dff // tf
    row = lambda i, f: (i, 0)
    out_specs = [pl.BlockSpec((tm, d), row)]
    out_shape = [jax.ShapeDtypeStruct((m, d), F32)]
    if emit_next:
        out_specs.append(pl.BlockSpec((tm, d), row))
        out_shape.append(jax.ShapeDtypeStruct((m, d), BF16))
    vmem = (2 * (tm * d * 2 + tm * d * 4 + 2 * d * tf * 2 + tf * d * 2 + tm * d * 4 + tm * d * 2)
            + tm * d * 4 + 3 * tm * tf * 4 + 2 * tm * d * 4)
    res = pl.pallas_call(
        functools.partial(_ffn_kernel, nf=nf, emit_next=emit_next),
        grid=(m // tm, nf),
        in_specs=[pl.BlockSpec((tm, d), row),
                  pl.BlockSpec((tm, d), row),
                  pl.BlockSpec((d, tf), lambda i, f: (0, f)),
                  pl.BlockSpec((d, tf), lambda i, f: (0, f)),
                  pl.BlockSpec((tf, d), lambda i, f: (f, 0)),
                  pl.BlockSpec((1, d), lambda i, f: (0, 0)),
                  pl.BlockSpec((1, d), lambda i, f: (0, 0))],
        out_specs=out_specs,
        out_shape=out_shape,
        scratch_shapes=[pltpu.VMEM((tm, d), F32)],
        compiler_params=_params(("parallel", "arbitrary"), vmem),
        name="ffn",
    )(xn, x, wg, wu, wd, gpost, gnext)
    return res if emit_next else (res[0], None)


def _mm_kernel(a_ref, w_ref, o_ref):
    o_ref[...] = _dot(a_ref[...], w_ref[...]).astype(o_ref.dtype)


def _matmul(a, w, out_dtype):
    m, k = a.shape
    n = w.shape[1]
    tm = _tile(m, 1024)
    tn = _tile(n, 1024)
    osz = jnp.dtype(out_dtype).itemsize
    vmem = 2 * (tm * k * 2 + k * tn * 2 + tm * tn * osz) + tm * tn * 4
    return pl.pallas_call(
        _mm_kernel,
        grid=(m // tm, n // tn),
        in_specs=[pl.BlockSpec((tm, k), lambda i, j: (i, 0)),
                  pl.BlockSpec((k, tn), lambda i, j: (0, j))],
        out_specs=pl.BlockSpec((tm, tn), lambda i, j: (i, j)),
        out_shape=jax.ShapeDtypeStruct((m, n), out_dtype),
        compiler_params=_params(("parallel", "arbitrary"), vmem),
        name="in_proj",
    )(a, w)


COL_QKV = 0
COL_Z = COL_QKV + GDN_CONV_DIM
COL_QNOPE = COL_Z + GDN_VAL_DIM
COL_QROPE = COL_QNOPE + MLA_HEADS * MLA_NOPE
COL_GG = COL_QROPE + MLA_HEADS * LANES


def _proj_layout(d_model):
    col_gm = COL_GG + d_model
    col_ckv = col_gm + d_model
    col_kr = col_ckv + MLA_KV_RANK
    col_b = col_kr + LANES
    col_a = col_b + LANES
    n_used = col_a + LANES
    n_tot = -(-n_used // 1024) * 1024
    return dict(gm=col_gm, ckv=col_ckv, kr=col_kr, b=col_b, a=col_a, n=n_tot)


def _rot_half_cols(w):
    half = MLA_ROPE // 2
    return jnp.concatenate([w[:, half:], w[:, :half]], axis=1)


def _build_w_in(w_in, d_model):
    sizes = (GDN_CONV_DIM, GDN_VAL_DIM, GDN_HEADS, GDN_HEADS, MLA_HEADS * (MLA_NOPE + MLA_ROPE),
             MLA_KV_RANK, MLA_ROPE, d_model, d_model)
    offs = [0]
    for s in sizes:
        offs.append(offs[-1] + s)
    part = lambda i: w_in[:, offs[i]:offs[i + 1]]
    qkv, z, b, a, qm, ckv, kr, gg, gm = (part(i) for i in range(9))
    d = w_in.shape[0]
    qm = qm.reshape(d, MLA_HEADS, MLA_NOPE + MLA_ROPE)
    qn = qm[:, :, :MLA_NOPE].reshape(d, MLA_HEADS * MLA_NOPE)
    qr = qm[:, :, MLA_NOPE:]
    half = MLA_ROPE // 2
    qr_pair = jnp.concatenate([qr, qr[:, :, half:], qr[:, :, :half]], axis=2).reshape(d, MLA_HEADS * LANES)
    kr_pair = jnp.concatenate([kr, _rot_half_cols(kr)], axis=1)
    pad8 = lambda w: jnp.pad(w, ((0, 0), (0, LANES - w.shape[1])))
    lay = _proj_layout(d_model)
    cols = [qkv, z, qn, qr_pair, gg, gm, ckv, kr_pair, pad8(b), pad8(a)]
    w = jnp.concatenate(cols, axis=1)
    w = jnp.pad(w, ((0, 0), (0, lay["n"] - w.shape[1])))
    return w.astype(BF16)


def _gdn_kernel(qkv_ref, z_ref, b_ref, a_ref, cst_ref, cw_ref, alog_ref, dtb_ref, nw_ref, s0_ref,
                o_ref, sout_ref, xp_ref, s_ref, *, C, nlev, nc):
    c = pl.program_id(1)
    hist = SUBLANES - (GDN_CONV - 1)

    @pl.when(c == 0)
    def _():
        xp_ref[0:SUBLANES, :] = cst_ref[0]
        s_ref[...] = s0_ref[0]

    xp_ref[SUBLANES:SUBLANES + C, :] = qkv_ref[...]

    beta_all = jax.nn.sigmoid(b_ref[...])
    a_all = a_ref[...] + dtb_ref[...]
    softplus = jnp.maximum(a_all, 0.0) + jnp.log(1.0 + jnp.exp(-jnp.abs(a_all)))
    g_all = -jnp.exp(alog_ref[...]) * softplus
    ri = lax.broadcasted_iota(jnp.int32, (C, C), 0)
    ci = lax.broadcasted_iota(jnp.int32, (C, C), 1)
    incl = ri >= ci
    strict = ri > ci
    ltri = incl.astype(F32)
    eye = (ri == ci).astype(F32)
    gcum = _dot(ltri, g_all, HI)
    gcum_t = _dot_tn(g_all, (ci >= ri).astype(F32), HI)

    def conv(col):
        acc = xp_ref[pl.ds(hist, C), col:col + LANES] * cw_ref[0:1, col:col + LANES]
        for j in range(1, GDN_CONV):
            acc = acc + xp_ref[pl.ds(hist + j, C), col:col + LANES] * cw_ref[j:j + 1, col:col + LANES]
        return _silu(acc)

    for h in range(GDN_HEADS):
        q = conv(h * GDN_DK)
        k = conv(GDN_KEY_DIM + h * GDN_DK)
        v = conv(2 * GDN_KEY_DIM + h * GDN_DV)
        q = q * lax.rsqrt(jnp.sum(q * q, axis=-1, keepdims=True) + 1e-6) * (GDN_DK ** -0.5)
        k = k * lax.rsqrt(jnp.sum(k * k, axis=-1, keepdims=True) + 1e-6)
        bcol = beta_all[:, h:h + 1]
        gcol = gcum[:, h:h + 1]
        grow = gcum_t[h:h + 1, :]
        decay = jnp.where(incl, jnp.exp(jnp.where(incl, gcol - grow, 0.0)), 0.0)
        kk = _dot_nt(k, k, HI)
        qk = _dot_nt(q, k, HI) * decay
        nmat = -jnp.where(strict, bcol * kk * decay, 0.0)
        tinv = eye + nmat
        npow = nmat
        for _ in range(nlev):
            npow = _dot(npow, npow, HI)
            tinv = tinv + _dot(tinv, npow, HI)
        egc = jnp.exp(gcol)
        u = _dot(tinv, v * bcol, HI)
        w = _dot(tinv, k * (bcol * egc), HI)
        s_old = s_ref[h]
        v_new = u - _dot(w, s_old, HI)
        o = _dot(q * egc, s_old, HI) + _dot(qk, v_new, HI)
        glast = gcum[C - 1:C, h:h + 1]
        s_ref[h] = s_old * jnp.exp(glast) + _dot_tn(k * jnp.exp(glast - gcol), v_new, HI)
        zz = z_ref[:, h * GDN_DV:(h + 1) * GDN_DV]
        o_ref[:, h * GDN_DV:(h + 1) * GDN_DV] = (_rms(o, nw_ref[...]) * _silu(zz)).astype(BF16)

    xp_ref[hist:SUBLANES, :] = xp_ref[C + hist:C + SUBLANES, :]

    @pl.when(c == nc - 1)
    def _():
        sout_ref[0] = s_ref[...]


def _gdn(proj, lay, row0, nb, t, conv_state, ssm0, conv_w, alog, dtb, nw):
    C = min(CHUNK, t)
    assert t % C == 0 and C & (C - 1) == 0 and row0 % C == 0
    nc = t // C
    nlev = C.bit_length() - 2
    r0 = row0 // C
    rows = lambda b, c: r0 + b * nc + c
    fixed = lambda b, c: (0, 0)
    cst = jnp.pad(conv_state, ((0, 0), (SUBLANES - (GDN_CONV - 1), 0), (0, 0)))
    vmem = (2 * (C * GDN_CONV_DIM * 4 + C * GDN_VAL_DIM * 4 + 2 * C * LANES * 4 + SUBLANES * GDN_CONV_DIM * 4
                 + C * GDN_VAL_DIM * 2 + 2 * GDN_HEADS * GDN_DK * GDN_DV * 4)
            + (SUBLANES + C) * GDN_CONV_DIM * 4 + GDN_HEADS * GDN_DK * GDN_DV * 4 + 8 * 2 ** 20)
    o, s_new = pl.pallas_call(
        functools.partial(_gdn_kernel, C=C, nlev=nlev, nc=nc),
        grid=(nb, nc),
        in_specs=[pl.BlockSpec((C, GDN_CONV_DIM), lambda b, c: (rows(b, c), COL_QKV // GDN_CONV_DIM)),
                  pl.BlockSpec((C, GDN_VAL_DIM), lambda b, c: (rows(b, c), COL_Z // GDN_VAL_DIM)),
                  pl.BlockSpec((C, LANES), lambda b, c: (rows(b, c), lay["b"] // LANES)),
                  pl.BlockSpec((C, LANES), lambda b, c: (rows(b, c), lay["a"] // LANES)),
                  pl.BlockSpec((1, SUBLANES, GDN_CONV_DIM), lambda b, c: (b, 0, 0)),
                  pl.BlockSpec((GDN_CONV, GDN_CONV_DIM), fixed),
                  pl.BlockSpec((1, LANES), fixed),
                  pl.BlockSpec((1, LANES), fixed),
                  pl.BlockSpec((1, GDN_DV), fixed),
                  pl.BlockSpec((1, GDN_HEADS, GDN_DK, GDN_DV), lambda b, c: (b, 0, 0, 0))],
        out_specs=[pl.BlockSpec((C, GDN_VAL_DIM), lambda b, c: (b * nc + c, 0)),
                   pl.BlockSpec((1, GDN_HEADS, GDN_DK, GDN_DV), lambda b, c: (b, 0, 0, 0))],
        out_shape=[jax.ShapeDtypeStruct((nb * t, GDN_VAL_DIM), BF16),
                   jax.ShapeDtypeStruct((nb, GDN_HEADS, GDN_DK, GDN_DV), F32)],
        scratch_shapes=[pltpu.VMEM((SUBLANES + C, GDN_CONV_DIM), F32),
                        pltpu.VMEM((GDN_HEADS, GDN_DK, GDN_DV), F32)],
        compiler_params=_params(("parallel", "arbitrary"), vmem),
        name="gdn",
    )(proj, proj, proj, proj, cst, conv_w, alog, dtb, nw, ssm0)
    return o, s_new


def _mla_prep_kernel(qn_ref, qr_ref, ckv_ref, kr_ref, inv_ref, wuk_ref, kvn_ref,
                     ckv_o, kr_o, qcat_o, kcat_o, *, tm, pos0):
    j = pl.program_id(1)
    pos = (pos0 + j * tm + lax.broadcasted_iota(jnp.int32, (tm, LANES), 0)).astype(F32)
    lane = lax.broadcasted_iota(jnp.int32, (tm, LANES), 1)
    ang = pos * inv_ref[...]
    cos = jnp.cos(ang)
    sin = jnp.sin(ang)
    half = MLA_ROPE // 2
    cs = jnp.where(lane < MLA_ROPE, cos, jnp.where(lane < MLA_ROPE + half, -sin, sin))
    keep = lane < MLA_ROPE

    def rope(pair):
        prod = pair * cs
        return jnp.where(keep, prod + pltpu.roll(prod, MLA_ROPE, axis=1), 0.0)

    ckv = _rms(ckv_ref[...], kvn_ref[...])
    ckv_o[0] = ckv
    kr = rope(kr_ref[...])
    kr_o[0] = kr[:, :MLA_ROPE]
    kcat_o[0, :, :MLA_KV_RANK] = ckv.astype(BF16)
    kcat_o[0, :, MLA_KV_RANK:] = kr.astype(BF16)
    for h in range(MLA_HEADS):
        qn = qn_ref[:, h * MLA_NOPE:(h + 1) * MLA_NOPE].astype(BF16)
        qlat = _dot(qn, wuk_ref[h]) * MLA_SCALE
        qcat_o[0, h, :, :MLA_KV_RANK] = qlat.astype(BF16)
        qr = rope(qr_ref[:, h * LANES:(h + 1) * LANES]) * MLA_SCALE
        qcat_o[0, h, :, MLA_KV_RANK:] = qr.astype(BF16)


def _mla_prep(proj, lay, row0, nb, t, pos0, inv128, wuk_t, kvn):
    tm = _tile(t, 256)
    nt = t // tm
    assert row0 % tm == 0
    r0 = row0 // tm
    rows = lambda b, j: r0 + b * nt + j
    fixed2 = lambda b, j: (0, 0)
    vmem = (2 * (tm * (MLA_HEADS * MLA_NOPE + MLA_HEADS * LANES + MLA_KV_RANK + LANES) * 4
                 + MLA_HEADS * MLA_NOPE * MLA_KV_RANK * 2
                 + tm * (MLA_KV_RANK + LANES) * 4 + (MLA_HEADS + 1) * tm * MLA_QK * 2) + 8 * 2 ** 20)
    return pl.pallas_call(
        functools.partial(_mla_prep_kernel, tm=tm, pos0=pos0),
        grid=(nb, nt),
        in_specs=[pl.BlockSpec((tm, MLA_HEADS * MLA_NOPE), lambda b, j: (rows(b, j), COL_QNOPE // (MLA_HEADS * MLA_NOPE))),
                  pl.BlockSpec((tm, MLA_HEADS * LANES), lambda b, j: (rows(b, j), COL_QROPE // (MLA_HEADS * LANES))),
                  pl.BlockSpec((tm, MLA_KV_RANK), lambda b, j: (rows(b, j), lay["ckv"] // MLA_KV_RANK)),
                  pl.BlockSpec((tm, LANES), lambda b, j: (rows(b, j), lay["kr"] // LANES)),
                  pl.BlockSpec((1, LANES), fixed2),
                  pl.BlockSpec((MLA_HEADS, MLA_NOPE, MLA_KV_RANK), lambda b, j: (0, 0, 0)),
                  pl.BlockSpec((1, MLA_KV_RANK), fixed2)],
        out_specs=[pl.BlockSpec((1, tm, MLA_KV_RANK), lambda b, j: (b, j, 0)),
                   pl.BlockSpec((1, tm, MLA_ROPE), lambda b, j: (b, j, 0)),
                   pl.BlockSpec((1, MLA_HEADS, tm, MLA_QK), lambda b, j: (b, 0, j, 0)),
                   pl.BlockSpec((1, tm, MLA_QK), lambda b, j: (b, j, 0))],
        out_shape=[jax.ShapeDtypeStruct((nb, t, MLA_KV_RANK), F32),
                   jax.ShapeDtypeStruct((nb, t, MLA_ROPE), F32),
                   jax.ShapeDtypeStruct((nb, MLA_HEADS, t, MLA_QK), BF16),
                   jax.ShapeDtypeStruct((nb, t, MLA_QK), BF16)],
        compiler_params=_params(("parallel", "parallel"), vmem),
        name="mla_prep",
    )(proj, proj, proj, proj, inv128, wuk_t, kvn)


def _attn_kernel(q_ref, k_ref, wuv_ref, o_ref, m_ref, l_ref, acc_ref, *, tq, tk, q_off, n_valid):
    i = pl.program_id(1)
    rows = MLA_HEADS * tq
    q = q_ref[0].reshape(rows, MLA_QK)
    r = lax.broadcasted_iota(jnp.int32, (rows, 1), 0)
    q_first = q_off + i * tq
    qchunk = jnp.right_shift(q_first + jnp.bitwise_and(r, tq - 1), LOG2_CHUNK)
    k_end = jnp.minimum(((q_first + tq - 1) // CHUNK + 1) * CHUNK, n_valid)
    nk = (k_end + tk - 1) // tk
    m_ref[...] = jnp.full(m_ref.shape, -jnp.inf, F32)
    l_ref[...] = jnp.zeros(l_ref.shape, F32)
    acc_ref[...] = jnp.zeros(acc_ref.shape, F32)

    def body(j, carry):
        start = pl.multiple_of(j * tk, tk)
        kb = k_ref[0, pl.ds(start, tk), :]
        s = _dot_nt(q, kb)
        kpos = start + lax.broadcasted_iota(jnp.int32, (1, tk), 1)
        mask = jnp.logical_and(jnp.right_shift(kpos, LOG2_CHUNK) <= qchunk, kpos < n_valid)
        s = jnp.where(mask, s, -jnp.inf)
        m_old = m_ref[...]
        m_new = jnp.maximum(m_old, jnp.max(s, axis=-1, keepdims=True))
        alpha = jnp.exp(m_old - m_new)
        p = jnp.exp(s - m_new)
        l_ref[...] = alpha * l_ref[...] + jnp.sum(p, axis=-1, keepdims=True)
        acc_ref[...] = alpha * acc_ref[...] + _dot(p.astype(BF16), kb[:, :MLA_KV_RANK])
        m_ref[...] = m_new
        return carry

    lax.fori_loop(0, nk, body, 0)
    inv_l = 1.0 / l_ref[...]
    for h in range(MLA_HEADS):
        lat = (acc_ref[h * tq:(h + 1) * tq, :] * inv_l[h * tq:(h + 1) * tq, :]).astype(BF16)
        o_ref[:, h * MLA_V:(h + 1) * MLA_V] = _dot(lat, wuv_ref[h]).astype(BF16)


def _attention(qcat, kcat, wuv_t, q_off, n_valid):
    nb, _, t, _ = qcat.shape
    tkeys = kcat.shape[1]
    tq = _tile(t, 128)
    tk = _tile(tkeys, 512)
    assert tq & (tq - 1) == 0
    nq = t // tq
    rows = MLA_HEADS * tq
    vmem = (2 * (rows * MLA_QK * 2 + tkeys * MLA_QK * 2 + MLA_HEADS * MLA_KV_RANK * MLA_V * 2
                 + tq * MLA_HEADS * MLA_V * 2)
            + 2 * rows * LANES * 4 + rows * MLA_KV_RANK * 4 + 4 * rows * tk * 4 + 4 * 2 ** 20)
    return pl.pallas_call(
        functools.partial(_attn_kernel, tq=tq, tk=tk, q_off=q_off, n_valid=n_valid),
        grid=(nb, nq),
        in_specs=[pl.BlockSpec((1, MLA_HEADS, tq, MLA_QK), lambda b, i: (b, 0, i, 0)),
                  pl.BlockSpec((1, tkeys, MLA_QK), lambda b, i: (b, 0, 0)),
                  pl.BlockSpec((MLA_HEADS, MLA_KV_RANK, MLA_V), lambda b, i: (0, 0, 0))],
        out_specs=pl.BlockSpec((tq, MLA_HEADS * MLA_V), lambda b, i: (b * nq + i, 0)),
        out_shape=jax.ShapeDtypeStruct((nb * t, MLA_HEADS * MLA_V), BF16),
        scratch_shapes=[pltpu.VMEM((rows, 1), F32), pltpu.VMEM((rows, 1), F32),
                        pltpu.VMEM((rows, MLA_KV_RANK), F32)],
        compiler_params=_params(("parallel", "arbitrary"), vmem),
        name="mla_attention",
    )(qcat, kcat, wuv_t)


def _merge_kernel(og_ref, om_ref, wg_ref, wm_ref, gg_ref, gm_ref, o_ref):
    tg = _dot(og_ref[...], wg_ref[...])
    tm_ = _dot(om_ref[...], wm_ref[...])
    o_ref[...] = (jax.nn.sigmoid(gg_ref[...]) * tg + jax.nn.sigmoid(gm_ref[...]) * tm_).astype(BF16)


def _merge(og, om, wbg, wbm, proj, lay, row0):
    m, kg = og.shape
    d = wbg.shape[1]
    tm = _tile(m, 512)
    tn = _tile(d, 512)
    assert row0 % tm == 0 and COL_GG % tn == 0 and lay["gm"] % tn == 0
    r0 = row0 // tm
    vmem = 2 * (2 * tm * kg * 2 + 2 * kg * tn * 2 + 2 * tm * tn * 4 + tm * tn * 2) + 4 * tm * tn * 4
    return pl.pallas_call(
        _merge_kernel,
        grid=(m // tm, d // tn),
        in_specs=[pl.BlockSpec((tm, kg), lambda i, j: (i, 0)),
                  pl.BlockSpec((tm, kg), lambda i, j: (i, 0)),
                  pl.BlockSpec((kg, tn), lambda i, j: (0, j)),
                  pl.BlockSpec((kg, tn), lambda i, j: (0, j)),
                  pl.BlockSpec((tm, tn), lambda i, j: (r0 + i, COL_GG // tn + j)),
                  pl.BlockSpec((tm, tn), lambda i, j: (r0 + i, lay["gm"] // tn + j))],
        out_specs=pl.BlockSpec((tm, tn), lambda i, j: (i, j)),
        out_shape=jax.ShapeDtypeStruct((m, d), BF16),
        compiler_params=_params(("parallel", "arbitrary"), vmem),
        name="merge",
    )(og, om, wbg, wbm, proj, proj)


def _outproj_kernel(a_ref, w_ref, x_ref, gpost_ref, gnext_ref, xo_ref, xno_ref):
    y = _dot(a_ref[...], w_ref[...])
    xnew = x_ref[...] + _rms(y, gpost_ref[...])
    xo_ref[...] = xnew
    xno_ref[...] = _rms(xnew, gnext_ref[...]).astype(BF16)


def _outproj(a, w, x, gpost, gnext):
    m, d = x.shape
    k = a.shape[1]
    tm = _tile(m, 512)
    row = lambda i: (i, 0)
    fixed = lambda i: (0, 0)
    vmem = 2 * (tm * k * 2 + k * d * 2 + tm * d * 4 + tm * d * 4 + tm * d * 2) + 3 * tm * d * 4
    return pl.pallas_call(
        _outproj_kernel,
        grid=(m // tm,),
        in_specs=[pl.BlockSpec((tm, k), row), pl.BlockSpec((k, d), fixed), pl.BlockSpec((tm, d), row),
                  pl.BlockSpec((1, d), fixed), pl.BlockSpec((1, d), fixed)],
        out_specs=[pl.BlockSpec((tm, d), row), pl.BlockSpec((tm, d), row)],
        out_shape=[jax.ShapeDtypeStruct((m, d), F32), jax.ShapeDtypeStruct((m, d), BF16)],
        compiler_params=_params(("parallel",), vmem),
        name="out_proj",
    )(a, w, x, gpost, gnext)


def _layer(x3, pos0, conv_state, ssm0, ckv_past, krope_past, w):
    nb, t, d = x3.shape
    m = nb * t
    x = x3.reshape(m, d)
    lay = w["lay"]

    xn = _norm_cast(x, w["ffn1_norm_pre"])
    x1, hm = _ffn(xn, x, w["ffn1_wg"], w["ffn1_wu"], w["ffn1_wd"], w["ffn1_norm_post"],
                  w["mix_norm_pre"], True)
    proj = _matmul(hm, w["w_in"], F32)

    o_gdn, ssm_new = _gdn(proj, lay, 0, nb, t, conv_state, ssm0, w["conv_w"], w["alog"], w["dtb"], w["gdn_nw"])
    conv_new = proj[:, COL_QKV:COL_QKV + GDN_CONV_DIM].reshape(nb, t, GDN_CONV_DIM)[:, t - (GDN_CONV - 1):, :]

    ckv, krope, qcat, kcat = _mla_prep(proj, lay, 0, nb, t, pos0, w["inv128"], w["wuk_t"], w["kv_norm"])
    if ckv_past is None:
        keys, n_valid = kcat, t
    else:
        past = ckv_past.shape[1]
        zpad = jnp.zeros((nb, past, MLA_QK - MLA_KV_RANK - MLA_ROPE), BF16)
        kpast = jnp.concatenate([ckv_past.astype(BF16), krope_past.astype(BF16), zpad], axis=-1)
        n_valid = past + t
        tot = -(-n_valid // 512) * 512
        keys = jnp.concatenate([kpast, kcat, jnp.zeros((nb, tot - n_valid, MLA_QK), BF16)], axis=1)
    o_mla = _attention(qcat, keys, w["wuv_t"], pos0, n_valid)

    merged = _merge(o_gdn, o_mla, w["w_br_gdn"], w["w_br_mla"], proj, lay, 0)
    x2, xn2 = _outproj(merged, w["w_out"], x1, w["mix_norm_post"], w["ffn2_norm_pre"])
    x3_, _ = _ffn(xn2, x2, w["ffn2_wg"], w["ffn2_wu"], w["ffn2_wd"], w["ffn2_norm_post"],
                  w["ffn2_norm_post"], False)
    return x3_.reshape(nb, t, d), conv_new, ssm_new, ckv, krope


def _prep_weights(l, d_model, **p):
    row = lambda v: v[l].reshape(1, -1).astype(F32)
    pad_lanes = lambda v: jnp.pad(v[l].reshape(1, -1).astype(F32), ((0, 0), (0, LANES - v.shape[1])))
    inv = ROPE_THETA ** (-jnp.arange(0, MLA_ROPE, 2, dtype=F32) / MLA_ROPE)
    w = dict(
        lay=_proj_layout(d_model),
        ffn1_norm_pre=row(p["ffn1_norm_pre"]), ffn1_norm_post=row(p["ffn1_norm_post"]),
        mix_norm_pre=row(p["mix_norm_pre"]), mix_norm_post=row(p["mix_norm_post"]),
        ffn2_norm_pre=row(p["ffn2_norm_pre"]), ffn2_norm_post=row(p["ffn2_norm_post"]),
        ffn1_wg=p["ffn1_w_gate"][l].astype(BF16), ffn1_wu=p["ffn1_w_up"][l].astype(BF16),
        ffn1_wd=p["ffn1_w_down"][l].astype(BF16),
        ffn2_wg=p["ffn2_w_gate"][l].astype(BF16), ffn2_wu=p["ffn2_w_up"][l].astype(BF16),
        ffn2_wd=p["ffn2_w_down"][l].astype(BF16),
        w_in=_build_w_in(p["w_in"][l], d_model),
        conv_w=p["gdn_conv_w"][l].astype(F32),
        alog=pad_lanes(p["gdn_a_log"]), dtb=pad_lanes(p["gdn_dt_bias"]),
        gdn_nw=row(p["gdn_norm_w"]), kv_norm=row(p["mla_kv_norm"]),
        inv128=jnp.tile(inv, LANES // inv.shape[0]).reshape(1, LANES),
        wuk_t=jnp.transpose(p["mla_w_uk"][l], (1, 2, 0)).astype(BF16),
        wuv_t=jnp.transpose(p["mla_w_uv"][l], (1, 0, 2)).astype(BF16),
        w_br_gdn=p["w_br_gdn"][l].astype(BF16), w_br_mla=p["w_br_mla"][l].astype(BF16),
        w_out=p["w_out"][l].astype(BF16),
    )
    return w


def kernel(x_prompt, x_sample, state_gdn_conv, state_gdn_ssm, cache_mla_ckv, cache_mla_krope, ffn1_norm_pre, ffn1_w_gate, ffn1_w_up, ffn1_w_down, ffn1_norm_post, mix_norm_pre, w_in, gdn_conv_w, gdn_a_log, gdn_dt_bias, gdn_norm_w, mla_kv_norm, mla_w_uk, mla_w_uv, w_br_gdn, w_br_mla, w_out, mix_norm_post, ffn2_norm_pre, ffn2_w_gate, ffn2_w_up, ffn2_w_down, ffn2_norm_post):
    params = dict(
        ffn1_norm_pre=ffn1_norm_pre, ffn1_w_gate=ffn1_w_gate, ffn1_w_up=ffn1_w_up,
        ffn1_w_down=ffn1_w_down, ffn1_norm_post=ffn1_norm_post, mix_norm_pre=mix_norm_pre,
        w_in=w_in, gdn_conv_w=gdn_conv_w, gdn_a_log=gdn_a_log, gdn_dt_bias=gdn_dt_bias,
        gdn_norm_w=gdn_norm_w, mla_kv_norm=mla_kv_norm, mla_w_uk=mla_w_uk, mla_w_uv=mla_w_uv,
        w_br_gdn=w_br_gdn, w_br_mla=w_br_mla, w_out=w_out, mix_norm_post=mix_norm_post,
        ffn2_norm_pre=ffn2_norm_pre, ffn2_w_gate=ffn2_w_gate, ffn2_w_up=ffn2_w_up,
        ffn2_w_down=ffn2_w_down, ffn2_norm_post=ffn2_norm_post)
    depth = w_in.shape[0]
    d_model = x_prompt.shape[-1]
    b_p = x_prompt.shape[0]
    past = cache_mla_ckv.shape[2]
    yp, ys = x_prompt, x_sample
    outs_p, outs_s = [], []
    for l in range(depth):
        w = _prep_weights(l, d_model, **params)
        yp, *rest_p = _layer(
            yp, 0, jnp.zeros((b_p, GDN_CONV - 1, GDN_CONV_DIM), F32),
            jnp.zeros((b_p, GDN_HEADS, GDN_DK, GDN_DV), F32), None, None, w)
        ys, *rest_s = _layer(
            ys, past, state_gdn_conv[l], state_gdn_ssm[l], cache_mla_ckv[l], cache_mla_krope[l], w)
        outs_p.append(rest_p)
        outs_s.append(rest_s)
    stack = lambda outs, i: jnp.stack([o[i] for o in outs])
    return (yp, ys,
            stack(outs_p, 0), stack(outs_p, 1), stack(outs_p, 2), stack(outs_p, 3),
            stack(outs_s, 0), stack(outs_s, 1), stack(outs_s, 2), stack(outs_s, 3))
```

```python
import functools

import jax
import jax.numpy as jnp
from jax import lax
from jax.experimental import pallas as pl
from jax.experimental.pallas import tpu as pltpu

F32 = jnp.float32
BF16 = jnp.bfloat16

CHUNK = 64
LOG2_CHUNK = 6
NORM_EPS = 1e-6
GDN_HEADS = 8
GDN_DK = 128
GDN_DV = 128
GDN_CONV = 4
GDN_KEY_DIM = GDN_HEADS * GDN_DK
GDN_VAL_DIM = GDN_HEADS * GDN_DV
GDN_CONV_DIM = 2 * GDN_KEY_DIM + GDN_VAL_DIM
MLA_HEADS = 8
MLA_NOPE = 128
MLA_ROPE = 64
MLA_V = 128
MLA_KV_RANK = 512
MLA_SCALE = (MLA_NOPE + MLA_ROPE) ** -0.5
ROPE_THETA = 10000.0

LANES = 128
SUBLANES = 8
VMEM_CAP_BYTES = 56 * 2 ** 20
MLA_QK = MLA_KV_RANK + LANES
HI = lax.Precision.HIGHEST


def _params(semantics, vmem_bytes):
    limit = int(min(max(vmem_bytes, 16 * 2 ** 20), VMEM_CAP_BYTES))
    return pltpu.CompilerParams(dimension_semantics=semantics, vmem_limit_bytes=limit)


def _dot(a, b, prec=None):
    return lax.dot_general(a, b, (((1,), (0,)), ((), ())), precision=prec,
                           preferred_element_type=F32)


def _dot_nt(a, b, prec=None):
    return lax.dot_general(a, b, (((1,), (1,)), ((), ())), precision=prec,
                           preferred_element_type=F32)


def _dot_tn(a, b, prec=None):
    return lax.dot_general(a, b, (((0,), (0,)), ((), ())), precision=prec,
                           preferred_element_type=F32)


def _rms(y, g):
    return y * lax.rsqrt(jnp.mean(y * y, axis=-1, keepdims=True) + NORM_EPS) * g


def _silu(x):
    return x * jax.nn.sigmoid(x)


def _tile(n, pref):
    t = min(n, pref)
    assert n % t == 0, (n, pref)
    return t


def _norm_kernel(x_ref, g_ref, o_ref):
    o_ref[...] = _rms(x_ref[...], g_ref[...]).astype(BF16)


def _norm_cast(x, g):
    m, d = x.shape
    tm = _tile(m, 512)
    return pl.pallas_call(
        _norm_kernel,
        grid=(m // tm,),
        in_specs=[pl.BlockSpec((tm, d), lambda i: (i, 0)),
                  pl.BlockSpec((1, d), lambda i: (0, 0))],
        out_specs=pl.BlockSpec((tm, d), lambda i: (i, 0)),
        out_shape=jax.ShapeDtypeStruct((m, d), BF16),
        compiler_params=_params(("parallel",), 2 * tm * d * 6 + 4 * tm * d * 4),
        name="norm_cast",
    )(x, g)


def _ffn_kernel(xn_ref, x_ref, wg_ref, wu_ref, wd_ref, gpost_ref, gnext_ref, *rest, nf, emit_next):
    if emit_next:
        xo_ref, xno_ref, acc_ref = rest
    else:
        xo_ref, acc_ref = rest
    f = pl.program_id(1)
    xn = xn_ref[...]
    gate = _dot(xn, wg_ref[...])
    up = _dot(xn, wu_ref[...])
    h = (_silu(gate) * up).astype(BF16)
    part = _dot(h, wd_ref[...])

    @pl.when(f == 0)
    def _():
        acc_ref[...] = part

    @pl.when(f > 0)
    def _():
        acc_ref[...] += part

    @pl.when(f == nf - 1)
    def _():
        xnew = x_ref[...] + 0.5 * _rms(acc_ref[...], gpost_ref[...])
        xo_ref[...] = xnew
        if emit_next:
            xno_ref[...] = _rms(xnew, gnext_ref[...]).astype(BF16)


def _ffn(xn, x, wg, wu, wd, gpost, gnext, emit_next):
    m, d = x.shape
    dff = wg.shape[1]
    tm = _tile(m, 512)
    tf = _tile(dff, 512)
    nf = dff // tf
    row = lambda i, f: (i, 0)
    out_specs = [pl.BlockSpec((tm, d), row)]
    out_shape = [jax.ShapeDtypeStruct((m, d), F32)]
    if emit_next:
        out_specs.append(pl.BlockSpec((tm, d), row))
        out_shape.append(jax.ShapeDtypeStruct((m, d), BF16))
    vmem = (2 * (tm * d * 2 + tm * d * 4 + 2 * d * tf * 2 + tf * d * 2 + tm * d * 4 + tm * d * 2)
            + tm * d * 4 + 3 * tm * tf * 4 + 2 * tm * d * 4)
    res = pl.pallas_call(
        functools.partial(_ffn_kernel, nf=nf, emit_next=emit_next),
        grid=(m // tm, nf),
        in_specs=[pl.BlockSpec((tm, d), row),
                  pl.BlockSpec((tm, d), row),
                  pl.BlockSpec((d, tf), lambda i, f: (0, f)),
                  pl.BlockSpec((d, tf), lambda i, f: (0, f)),
                  pl.BlockSpec((tf, d), lambda i, f: (f, 0)),
                  pl.BlockSpec((1, d), lambda i, f: (0, 0)),
                  pl.BlockSpec((1, d), lambda i, f: (0, 0))],
        out_specs=out_specs,
        out_shape=out_shape,
        scratch_shapes=[pltpu.VMEM((tm, d), F32)],
        compiler_params=_params(("parallel", "arbitrary"), vmem),
        name="ffn",
    )(xn, x, wg, wu, wd, gpost, gnext)
    return res if emit_next else (res[0], None)


def _mm_kernel(a_ref, w_ref, o_ref):
    o_ref[...] = _dot(a_ref[...], w_ref[...]).astype(o_ref.dtype)


def _matmul(a, w, out_dtype):
    m, k = a.shape
    n = w.shape[1]
    tm = _tile(m, 1024)
    tn = _tile(n, 1024)
    osz = jnp.dtype(out_dtype).itemsize
    vmem = 2 * (tm * k * 2 + k * tn * 2 + tm * tn * osz) + tm * tn * 4
    return pl.pallas_call(
        _mm_kernel,
        grid=(m // tm, n // tn),
        in_specs=[pl.BlockSpec((tm, k), lambda i, j: (i, 0)),
                  pl.BlockSpec((k, tn), lambda i, j: (0, j))],
        out_specs=pl.BlockSpec((tm, tn), lambda i, j: (i, j)),
        out_shape=jax.ShapeDtypeStruct((m, n), out_dtype),
        compiler_params=_params(("parallel", "arbitrary"), vmem),
        name="in_proj",
    )(a, w)


COL_QKV = 0
COL_Z = COL_QKV + GDN_CONV_DIM
COL_QNOPE = COL_Z + GDN_VAL_DIM
COL_QROPE = COL_QNOPE + MLA_HEADS * MLA_NOPE
COL_GG = COL_QROPE + MLA_HEADS * LANES


def _proj_layout(d_model):
    col_gm = COL_GG + d_model
    col_ckv = col_gm + d_model
    col_kr = col_ckv + MLA_KV_RANK
    col_b = col_kr + LANES
    col_a = col_b + LANES
    n_used = col_a + LANES
    n_tot = -(-n_used // 1024) * 1024
    return dict(gm=col_gm, ckv=col_ckv, kr=col_kr, b=col_b, a=col_a, n=n_tot)


def _rot_half_cols(w):
    half = MLA_ROPE // 2
    return jnp.concatenate([w[:, half:], w[:, :half]], axis=1)


def _build_w_in(w_in, d_model):
    sizes = (GDN_CONV_DIM, GDN_VAL_DIM, GDN_HEADS, GDN_HEADS, MLA_HEADS * (MLA_NOPE + MLA_ROPE),
             MLA_KV_RANK, MLA_ROPE, d_model, d_model)
    offs = [0]
    for s in sizes:
        offs.append(offs[-1] + s)
    part = lambda i: w_in[:, offs[i]:offs[i + 1]]
    qkv, z, b, a, qm, ckv, kr, gg, gm = (part(i) for i in range(9))
    d = w_in.shape[0]
    qm = qm.reshape(d, MLA_HEADS, MLA_NOPE + MLA_ROPE)
    qn = qm[:, :, :MLA_NOPE].reshape(d, MLA_HEADS * MLA_NOPE)
    qr = qm[:, :, MLA_NOPE:]
    half = MLA_ROPE // 2
    qr_pair = jnp.concatenate([qr, qr[:, :, half:], qr[:, :, :half]], axis=2).reshape(d, MLA_HEADS * LANES)
    kr_pair = jnp.concatenate([kr, _rot_half_cols(kr)], axis=1)
    pad8 = lambda w: jnp.pad(w, ((0, 0), (0, LANES - w.shape[1])))
    lay = _proj_layout(d_model)
    cols = [qkv, z, qn, qr_pair, gg, gm, ckv, kr_pair, pad8(b), pad8(a)]
    w = jnp.concatenate(cols, axis=1)
    w = jnp.pad(w, ((0, 0), (0, lay["n"] - w.shape[1])))
    return w.astype(BF16)


_NN = (((1,), (0,)), ((), ()))
_NT = (((1,), (1,)), ((), ()))
_TN = (((0,), (0,)), ((), ()))
GDN_MODE = dict(kk="bf16", inv="bf16", sol="bf16", state="bf16")
INV_BLOCK = 16


def _mm(a, b, mode, dims=_NN):
    if mode == "f32":
        return lax.dot_general(a, b, dims, precision=HI, preferred_element_type=F32)
    ah = a.astype(BF16)
    bh = b.astype(BF16)
    out = lax.dot_general(ah, bh, dims, preferred_element_type=F32)
    if mode == "bf16x3":
        al = (a - ah.astype(F32)).astype(BF16)
        bl = (b - bh.astype(F32)).astype(BF16)
        out = (out + lax.dot_general(ah, bl, dims, preferred_element_type=F32)
               + lax.dot_general(al, bh, dims, preferred_element_type=F32))
    return out


def _gdn_kernel(qkv_ref, z_ref, b_ref, a_ref, cst_ref, cw_ref, alog_ref, dtb_ref, nw_ref, s0_ref,
                o_ref, sout_ref, xp_ref, s_ref, *, G, C, nlev, nc):
    c = pl.program_id(1)
    hist = SUBLANES - (GDN_CONV - 1)

    @pl.when(c == 0)
    def _():
        xp_ref[:, 0:SUBLANES, :] = cst_ref[...]
        s_ref[...] = s0_ref[...]

    xp_ref[:, SUBLANES:SUBLANES + C, :] = qkv_ref[...]

    ri = lax.broadcasted_iota(jnp.int32, (C, C), 0)
    ci = lax.broadcasted_iota(jnp.int32, (C, C), 1)
    incl = ri >= ci
    strict = ri > ci
    eye = (ri == ci).astype(F32)
    log2_blk = INV_BLOCK.bit_length() - 1
    bdiag = jnp.right_shift(ri, log2_blk) == jnp.right_shift(ci, log2_blk)

    def conv(s, col):
        acc = xp_ref[s, pl.ds(hist, C), col:col + LANES] * cw_ref[0:1, col:col + LANES]
        for j in range(1, GDN_CONV):
            acc = acc + xp_ref[s, pl.ds(hist + j, C), col:col + LANES] * cw_ref[j:j + 1, col:col + LANES]
        return _silu(acc)

    units = [(s, h) for s in range(G) for h in range(GDN_HEADS)]
    every = lambda fn, *lists: [fn(*args) for args in zip(*lists)]
    mm_inv = lambda a, b: _mm(a, b, GDN_MODE["inv"])
    beta_all, gcum, gcum_t = [], [], []
    for s in range(G):
        beta_all.append(jax.nn.sigmoid(b_ref[s]))
        a_all = a_ref[s] + dtb_ref[...]
        softplus = jnp.maximum(a_all, 0.0) + jnp.log(1.0 + jnp.exp(-jnp.abs(a_all)))
        g_all = -jnp.exp(alog_ref[...]) * softplus
        gcum.append(_dot(incl.astype(F32), g_all, HI))
        gcum_t.append(_dot_tn(g_all, (ci >= ri).astype(F32), HI))

    def l2n(x):
        return x * lax.rsqrt(jnp.sum(x * x, axis=-1, keepdims=True) + 1e-6)

    q = [l2n(conv(s, h * GDN_DK)) * (GDN_DK ** -0.5) for s, h in units]
    k = [l2n(conv(s, GDN_KEY_DIM + h * GDN_DK)) for s, h in units]
    v = [conv(s, 2 * GDN_KEY_DIM + h * GDN_DV) for s, h in units]
    bcol = [beta_all[s][:, h:h + 1] for s, h in units]
    gcol = [gcum[s][:, h:h + 1] for s, h in units]
    glast = [gcum[s][C - 1:C, h:h + 1] for s, h in units]
    decay = [jnp.where(incl, jnp.exp(jnp.where(incl, gcum[s][:, h:h + 1] - gcum_t[s][h:h + 1, :], 0.0)), 0.0)
             for s, h in units]
    qkk = every(lambda q_, k_: _mm(jnp.concatenate([q_, k_], axis=0), k_, GDN_MODE["kk"], _NT), q, k)
    qk = every(lambda x, d: x[:C] * d, qkk, decay)
    nmat = every(lambda x, b_, d: -jnp.where(strict, b_ * x[C:] * d, 0.0), qkk, bcol, decay)
    ndiag = every(lambda n_: jnp.where(bdiag, n_, 0.0), nmat)
    tinv = every(lambda n_: eye + n_, ndiag)
    npow = ndiag
    for _ in range(INV_BLOCK.bit_length() - 2):
        npow = every(mm_inv, npow, npow)
        tinv = every(lambda t_, p_: t_ + mm_inv(t_, p_), tinv, npow)
    mpow = every(lambda t_, n_, d_: mm_inv(t_, n_ - d_), tinv, nmat, ndiag)
    for lev in range(nlev):
        if lev:
            mpow = every(mm_inv, mpow, mpow)
        tinv = every(lambda t_, m_: t_ + mm_inv(m_, t_), tinv, mpow)
    egc = every(jnp.exp, gcol)
    sol = every(lambda t_, v_, k_, b_, e_: _mm(t_, jnp.concatenate([v_ * b_, k_ * (b_ * e_)], axis=1),
                                               GDN_MODE["sol"]), tinv, v, k, bcol, egc)
    s_old = [s_ref[s, h] for s, h in units]
    ws_qs = every(lambda x, q_, e_, s_: _mm(jnp.concatenate([x[:, GDN_DV:], q_ * e_], axis=0), s_,
                                            GDN_MODE["state"]), sol, q, egc, s_old)
    v_new = every(lambda x, y: x[:, :GDN_DV] - y[:C], sol, ws_qs)
    o = every(lambda y, a_, vn: y[C:] + _mm(a_, vn, GDN_MODE["state"]), ws_qs, qk, v_new)
    s_new = every(lambda s_, gl, k_, gc, vn: s_ * jnp.exp(gl) + _mm(k_ * jnp.exp(gl - gc), vn, GDN_MODE["state"], _TN),
                  s_old, glast, k, gcol, v_new)
    for (s, h), sn, o_ in zip(units, s_new, o):
        s_ref[s, h] = sn
        zz = z_ref[s, :, h * GDN_DV:(h + 1) * GDN_DV]
        o_ref[s, :, h * GDN_DV:(h + 1) * GDN_DV] = (_rms(o_, nw_ref[...]) * _silu(zz)).astype(BF16)

    xp_ref[:, hist:SUBLANES, :] = xp_ref[:, C + hist:C + SUBLANES, :]

    @pl.when(c == nc - 1)
    def _():
        sout_ref[...] = s_ref[...]


def _gdn(proj, lay, row0, nb, t, conv_state, ssm0, conv_w, alog, dtb, nw):
    C = min(CHUNK, t)
    G = 2 if nb % 2 == 0 else 1
    assert t % C == 0 and C & (C - 1) == 0 and C % INV_BLOCK == 0 and row0 == 0
    nc = t // C
    nlev = (C // INV_BLOCK).bit_length() - 1
    proj3 = proj.reshape(nb, t, proj.shape[1])
    fixed = lambda b, c: (0, 0)
    cst = jnp.pad(conv_state, ((0, 0), (SUBLANES - (GDN_CONV - 1), 0), (0, 0)))
    vmem = G * (2 * (C * GDN_CONV_DIM * 4 + C * GDN_VAL_DIM * 4 + 2 * C * LANES * 4 + SUBLANES * GDN_CONV_DIM * 4
                     + C * GDN_VAL_DIM * 2 + 2 * GDN_HEADS * GDN_DK * GDN_DV * 4)
                + (SUBLANES + C) * GDN_CONV_DIM * 4 + GDN_HEADS * GDN_DK * GDN_DV * 4) + 16 * 2 ** 20
    o, s_new = pl.pallas_call(
        functools.partial(_gdn_kernel, G=G, C=C, nlev=nlev, nc=nc),
        grid=(nb // G, nc),
        in_specs=[pl.BlockSpec((G, C, GDN_CONV_DIM), lambda b, c: (b, c, COL_QKV // GDN_CONV_DIM)),
                  pl.BlockSpec((G, C, GDN_VAL_DIM), lambda b, c: (b, c, COL_Z // GDN_VAL_DIM)),
                  pl.BlockSpec((G, C, LANES), lambda b, c: (b, c, lay["b"] // LANES)),
                  pl.BlockSpec((G, C, LANES), lambda b, c: (b, c, lay["a"] // LANES)),
                  pl.BlockSpec((G, SUBLANES, GDN_CONV_DIM), lambda b, c: (b, 0, 0)),
                  pl.BlockSpec((GDN_CONV, GDN_CONV_DIM), fixed),
                  pl.BlockSpec((1, LANES), fixed),
                  pl.BlockSpec((1, LANES), fixed),
                  pl.BlockSpec((1, GDN_DV), fixed),
                  pl.BlockSpec((G, GDN_HEADS, GDN_DK, GDN_DV), lambda b, c: (b, 0, 0, 0))],
        out_specs=[pl.BlockSpec((G, C, GDN_VAL_DIM), lambda b, c: (b, c, 0)),
                   pl.BlockSpec((G, GDN_HEADS, GDN_DK, GDN_DV), lambda b, c: (b, 0, 0, 0))],
        out_shape=[jax.ShapeDtypeStruct((nb, t, GDN_VAL_DIM), BF16),
                   jax.ShapeDtypeStruct((nb, GDN_HEADS, GDN_DK, GDN_DV), F32)],
        scratch_shapes=[pltpu.VMEM((G, SUBLANES + C, GDN_CONV_DIM), F32),
                        pltpu.VMEM((G, GDN_HEADS, GDN_DK, GDN_DV), F32)],
        compiler_params=_params(("parallel", "arbitrary"), vmem),
        name="gdn",
    )(proj3, proj3, proj3, proj3, cst, conv_w, alog, dtb, nw, ssm0)
    return o.reshape(nb * t, GDN_VAL_DIM), s_new


def _mla_prep_kernel(qn_ref, qr_ref, ckv_ref, kr_ref, inv_ref, wuk_ref, kvn_ref,
                     ckv_o, kr_o, qcat_o, kcat_o, *, tm, pos0):
    j = pl.program_id(1)
    pos = (pos0 + j * tm + lax.broadcasted_iota(jnp.int32, (tm, LANES), 0)).astype(F32)
    lane = lax.broadcasted_iota(jnp.int32, (tm, LANES), 1)
    ang = pos * inv_ref[...]
    cos = jnp.cos(ang)
    sin = jnp.sin(ang)
    half = MLA_ROPE // 2
    cs = jnp.where(lane < MLA_ROPE, cos, jnp.where(lane < MLA_ROPE + half, -sin, sin))
    keep = lane < MLA_ROPE

    def rope(pair):
        prod = pair * cs
        return jnp.where(keep, prod + pltpu.roll(prod, MLA_ROPE, axis=1), 0.0)

    ckv = _rms(ckv_ref[...], kvn_ref[...])
    ckv_o[0] = ckv
    kr = rope(kr_ref[...])
    kr_o[0] = kr[:, :MLA_ROPE]
    kcat_o[0, :, :MLA_KV_RANK] = ckv.astype(BF16)
    kcat_o[0, :, MLA_KV_RANK:] = kr.astype(BF16)
    for h in range(MLA_HEADS):
        qn = qn_ref[:, h * MLA_NOPE:(h + 1) * MLA_NOPE].astype(BF16)
        qlat = _dot(qn, wuk_ref[h]) * MLA_SCALE
        qcat_o[0, h, :, :MLA_KV_RANK] = qlat.astype(BF16)
        qr = rope(qr_ref[:, h * LANES:(h + 1) * LANES]) * MLA_SCALE
        qcat_o[0, h, :, MLA_KV_RANK:] = qr.astype(BF16)


def _mla_prep(proj, lay, row0, nb, t, pos0, inv128, wuk_t, kvn):
    tm = _tile(t, 256)
    nt = t // tm
    assert row0 % tm == 0
    r0 = row0 // tm
    rows = lambda b, j: r0 + b * nt + j
    fixed2 = lambda b, j: (0, 0)
    vmem = (2 * (tm * (MLA_HEADS * MLA_NOPE + MLA_HEADS * LANES + MLA_KV_RANK + LANES) * 4
                 + MLA_HEADS * MLA_NOPE * MLA_KV_RANK * 2
                 + tm * (MLA_KV_RANK + LANES) * 4 + (MLA_HEADS + 1) * tm * MLA_QK * 2) + 8 * 2 ** 20)
    return pl.pallas_call(
        functools.partial(_mla_prep_kernel, tm=tm, pos0=pos0),
        grid=(nb, nt),
        in_specs=[pl.BlockSpec((tm, MLA_HEADS * MLA_NOPE), lambda b, j: (rows(b, j), COL_QNOPE // (MLA_HEADS * MLA_NOPE))),
                  pl.BlockSpec((tm, MLA_HEADS * LANES), lambda b, j: (rows(b, j), COL_QROPE // (MLA_HEADS * LANES))),
                  pl.BlockSpec((tm, MLA_KV_RANK), lambda b, j: (rows(b, j), lay["ckv"] // MLA_KV_RANK)),
                  pl.BlockSpec((tm, LANES), lambda b, j: (rows(b, j), lay["kr"] // LANES)),
                  pl.BlockSpec((1, LANES), fixed2),
                  pl.BlockSpec((MLA_HEADS, MLA_NOPE, MLA_KV_RANK), lambda b, j: (0, 0, 0)),
                  pl.BlockSpec((1, MLA_KV_RANK), fixed2)],
        out_specs=[pl.BlockSpec((1, tm, MLA_KV_RANK), lambda b, j: (b, j, 0)),
                   pl.BlockSpec((1, tm, MLA_ROPE), lambda b, j: (b, j, 0)),
                   pl.BlockSpec((1, MLA_HEADS, tm, MLA_QK), lambda b, j: (b, 0, j, 0)),
                   pl.BlockSpec((1, tm, MLA_QK), lambda b, j: (b, j, 0))],
        out_shape=[jax.ShapeDtypeStruct((nb, t, MLA_KV_RANK), F32),
                   jax.ShapeDtypeStruct((nb, t, MLA_ROPE), F32),
                   jax.ShapeDtypeStruct((nb, MLA_HEADS, t, MLA_QK), BF16),
                   jax.ShapeDtypeStruct((nb, t, MLA_QK), BF16)],
        compiler_params=_params(("parallel", "parallel"), vmem),
        name="mla_prep",
    )(proj, proj, proj, proj, inv128, wuk_t, kvn)


def _attn_kernel(q_ref, k_ref, wuv_ref, o_ref, m_ref, l_ref, acc_ref, *, tq, tk, q_off, n_valid):
    i = pl.program_id(1)
    rows = MLA_HEADS * tq
    q = q_ref[0].reshape(rows, MLA_QK)
    r = lax.broadcasted_iota(jnp.int32, (rows, 1), 0)
    q_first = q_off + i * tq
    qchunk = jnp.right_shift(q_first + jnp.bitwise_and(r, tq - 1), LOG2_CHUNK)
    k_end = jnp.minimum(((q_first + tq - 1) // CHUNK + 1) * CHUNK, n_valid)
    nk = (k_end + tk - 1) // tk
    m_ref[...] = jnp.full(m_ref.shape, -jnp.inf, F32)
    l_ref[...] = jnp.zeros(l_ref.shape, F32)
    acc_ref[...] = jnp.zeros(acc_ref.shape, F32)

    def body(j, carry):
        start = pl.multiple_of(j * tk, tk)
        kb = k_ref[0, pl.ds(start, tk), :]
        s = _dot_nt(q, kb)
        kpos = start + lax.broadcasted_iota(jnp.int32, (1, tk), 1)
        mask = jnp.logical_and(jnp.right_shift(kpos, LOG2_CHUNK) <= qchunk, kpos < n_valid)
        s = jnp.where(mask, s, -jnp.inf)
        m_old = m_ref[...]
        m_new = jnp.maximum(m_old, jnp.max(s, axis=-1, keepdims=True))
        alpha = jnp.exp(m_old - m_new)
        p = jnp.exp(s - m_new)
        l_ref[...] = alpha * l_ref[...] + jnp.sum(p, axis=-1, keepdims=True)
        acc_ref[...] = alpha * acc_ref[...] + _dot(p.astype(BF16), kb[:, :MLA_KV_RANK])
        m_ref[...] = m_new
        return carry

    lax.fori_loop(0, nk, body, 0)
    inv_l = 1.0 / l_ref[...]
    for h in range(MLA_HEADS):
        lat = (acc_ref[h * tq:(h + 1) * tq, :] * inv_l[h * tq:(h + 1) * tq, :]).astype(BF16)
        o_ref[:, h * MLA_V:(h + 1) * MLA_V] = _dot(lat, wuv_ref[h]).astype(BF16)


def _attention(qcat, kcat, wuv_t, q_off, n_valid):
    nb, _, t, _ = qcat.shape
    tkeys = kcat.shape[1]
    tq = _tile(t, 128)
    tk = _tile(tkeys, 512)
    assert tq & (tq - 1) == 0
    nq = t // tq
    rows = MLA_HEADS * tq
    vmem = (2 * (rows * MLA_QK * 2 + tkeys * MLA_QK * 2 + MLA_HEADS * MLA_KV_RANK * MLA_V * 2
                 + tq * MLA_HEADS * MLA_V * 2)
            + 2 * rows * LANES * 4 + rows * MLA_KV_RANK * 4 + 4 * rows * tk * 4 + 4 * 2 ** 20)
    return pl.pallas_call(
        functools.partial(_attn_kernel, tq=tq, tk=tk, q_off=q_off, n_valid=n_valid),
        grid=(nb, nq),
        in_specs=[pl.BlockSpec((1, MLA_HEADS, tq, MLA_QK), lambda b, i: (b, 0, i, 0)),
                  pl.BlockSpec((1, tkeys, MLA_QK), lambda b, i: (b, 0, 0)),
                  pl.BlockSpec((MLA_HEADS, MLA_KV_RANK, MLA_V), lambda b, i: (0, 0, 0))],
        out_specs=pl.BlockSpec((tq, MLA_HEADS * MLA_V), lambda b, i: (b * nq + i, 0)),
        out_shape=jax.ShapeDtypeStruct((nb * t, MLA_HEADS * MLA_V), BF16),
        scratch_shapes=[pltpu.VMEM((rows, 1), F32), pltpu.VMEM((rows, 1), F32),
                        pltpu.VMEM((rows, MLA_KV_RANK), F32)],
        compiler_params=_params(("parallel", "arbitrary"), vmem),
        name="mla_attention",
    )(qcat, kcat, wuv_t)


def _merge_kernel(og_ref, om_ref, wg_ref, wm_ref, gg_ref, gm_ref, o_ref):
    tg = _dot(og_ref[...], wg_ref[...])
    tm_ = _dot(om_ref[...], wm_ref[...])
    o_ref[...] = (jax.nn.sigmoid(gg_ref[...]) * tg + jax.nn.sigmoid(gm_ref[...]) * tm_).astype(BF16)


def _merge(og, om, wbg, wbm, proj, lay, row0):
    m, kg = og.shape
    d = wbg.shape[1]
    tm = _tile(m, 512)
    tn = _tile(d, 512)
    assert row0 % tm == 0 and COL_GG % tn == 0 and lay["gm"] % tn == 0
    r0 = row0 // tm
    vmem = 2 * (2 * tm * kg * 2 + 2 * kg * tn * 2 + 2 * tm * tn * 4 + tm * tn * 2) + 4 * tm * tn * 4
    return pl.pallas_call(
        _merge_kernel,
        grid=(m // tm, d // tn),
        in_specs=[pl.BlockSpec((tm, kg), lambda i, j: (i, 0)),
                  pl.BlockSpec((tm, kg), lambda i, j: (i, 0)),
                  pl.BlockSpec((kg, tn), lambda i, j: (0, j)),
                  pl.BlockSpec((kg, tn), lambda i, j: (0, j)),
                  pl.BlockSpec((tm, tn), lambda i, j: (r0 + i, COL_GG // tn + j)),
                  pl.BlockSpec((tm, tn), lambda i, j: (r0 + i, lay["gm"] // tn + j))],
        out_specs=pl.BlockSpec((tm, tn), lambda i, j: (i, j)),
        out_shape=jax.ShapeDtypeStruct((m, d), BF16),
        compiler_params=_params(("parallel", "arbitrary"), vmem),
        name="merge",
    )(og, om, wbg, wbm, proj, proj)


def _outproj_kernel(a_ref, w_ref, x_ref, gpost_ref, gnext_ref, xo_ref, xno_ref):
    y = _dot(a_ref[...], w_ref[...])
    xnew = x_ref[...] + _rms(y, gpost_ref[...])
    xo_ref[...] = xnew
    xno_ref[...] = _rms(xnew, gnext_ref[...]).astype(BF16)


def _outproj(a, w, x, gpost, gnext):
    m, d = x.shape
    k = a.shape[1]
    tm = _tile(m, 512)
    row = lambda i: (i, 0)
    fixed = lambda i: (0, 0)
    vmem = 2 * (tm * k * 2 + k * d * 2 + tm * d * 4 + tm * d * 4 + tm * d * 2) + 3 * tm * d * 4
    return pl.pallas_call(
        _outproj_kernel,
        grid=(m // tm,),
        in_specs=[pl.BlockSpec((tm, k), row), pl.BlockSpec((k, d), fixed), pl.BlockSpec((tm, d), row),
                  pl.BlockSpec((1, d), fixed), pl.BlockSpec((1, d), fixed)],
        out_specs=[pl.BlockSpec((tm, d), row), pl.BlockSpec((tm, d), row)],
        out_shape=[jax.ShapeDtypeStruct((m, d), F32), jax.ShapeDtypeStruct((m, d), BF16)],
        compiler_params=_params(("parallel",), vmem),
        name="out_proj",
    )(a, w, x, gpost, gnext)


def _layer(x3, pos0, conv_state, ssm0, ckv_past, krope_past, w):
    nb, t, d = x3.shape
    m = nb * t
    x = x3.reshape(m, d)
    lay = w["lay"]

    xn = _norm_cast(x, w["ffn1_norm_pre"])
    x1, hm = _ffn(xn, x, w["ffn1_wg"], w["ffn1_wu"], w["ffn1_wd"], w["ffn1_norm_post"],
                  w["mix_norm_pre"], True)
    proj = _matmul(hm, w["w_in"], F32)

    o_gdn, ssm_new = _gdn(proj, lay, 0, nb, t, conv_state, ssm0, w["conv_w"], w["alog"], w["dtb"], w["gdn_nw"])
    conv_new = proj[:, COL_QKV:COL_QKV + GDN_CONV_DIM].reshape(nb, t, GDN_CONV_DIM)[:, t - (GDN_CONV - 1):, :]

    ckv, krope, qcat, kcat = _mla_prep(proj, lay, 0, nb, t, pos0, w["inv128"], w["wuk_t"], w["kv_norm"])
    if ckv_past is None:
        keys, n_valid = kcat, t
    else:
        past = ckv_past.shape[1]
        zpad = jnp.zeros((nb, past, MLA_QK - MLA_KV_RANK - MLA_ROPE), BF16)
        kpast = jnp.concatenate([ckv_past.astype(BF16), krope_past.astype(BF16), zpad], axis=-1)
        n_valid = past + t
        tot = -(-n_valid // 512) * 512
        keys = jnp.concatenate([kpast, kcat, jnp.zeros((nb, tot - n_valid, MLA_QK), BF16)], axis=1)
    o_mla = _attention(qcat, keys, w["wuv_t"], pos0, n_valid)

    merged = _merge(o_gdn, o_mla, w["w_br_gdn"], w["w_br_mla"], proj, lay, 0)
    x2, xn2 = _outproj(merged, w["w_out"], x1, w["mix_norm_post"], w["ffn2_norm_pre"])
    x3_, _ = _ffn(xn2, x2, w["ffn2_wg"], w["ffn2_wu"], w["ffn2_wd"], w["ffn2_norm_post"],
                  w["ffn2_norm_post"], False)
    return x3_.reshape(nb, t, d), conv_new, ssm_new, ckv, krope


def _prep_weights(l, d_model, **p):
    row = lambda v: v[l].reshape(1, -1).astype(F32)
    pad_lanes = lambda v: jnp.pad(v[l].reshape(1, -1).astype(F32), ((0, 0), (0, LANES - v.shape[1])))
    inv = ROPE_THETA ** (-jnp.arange(0, MLA_ROPE, 2, dtype=F32) / MLA_ROPE)
    w = dict(
        lay=_proj_layout(d_model),
        ffn1_norm_pre=row(p["ffn1_norm_pre"]), ffn1_norm_post=row(p["ffn1_norm_post"]),
        mix_norm_pre=row(p["mix_norm_pre"]), mix_norm_post=row(p["mix_norm_post"]),
        ffn2_norm_pre=row(p["ffn2_norm_pre"]), ffn2_norm_post=row(p["ffn2_norm_post"]),
        ffn1_wg=p["ffn1_w_gate"][l].astype(BF16), ffn1_wu=p["ffn1_w_up"][l].astype(BF16),
        ffn1_wd=p["ffn1_w_down"][l].astype(BF16),
        ffn2_wg=p["ffn2_w_gate"][l].astype(BF16), ffn2_wu=p["ffn2_w_up"][l].astype(BF16),
        ffn2_wd=p["ffn2_w_down"][l].astype(BF16),
        w_in=_build_w_in(p["w_in"][l], d_model),
        conv_w=p["gdn_conv_w"][l].astype(F32),
        alog=pad_lanes(p["gdn_a_log"]), dtb=pad_lanes(p["gdn_dt_bias"]),
        gdn_nw=row(p["gdn_norm_w"]), kv_norm=row(p["mla_kv_norm"]),
        inv128=jnp.tile(inv, LANES // inv.shape[0]).reshape(1, LANES),
        wuk_t=jnp.transpose(p["mla_w_uk"][l], (1, 2, 0)).astype(BF16),
        wuv_t=jnp.transpose(p["mla_w_uv"][l], (1, 0, 2)).astype(BF16),
        w_br_gdn=p["w_br_gdn"][l].astype(BF16), w_br_mla=p["w_br_mla"][l].astype(BF16),
        w_out=p["w_out"][l].astype(BF16),
    )
    return w


def kernel(x_prompt, x_sample, state_gdn_conv, state_gdn_ssm, cache_mla_ckv, cache_mla_krope, ffn1_norm_pre, ffn1_w_gate, ffn1_w_up, ffn1_w_down, ffn1_norm_post, mix_norm_pre, w_in, gdn_conv_w, gdn_a_log, gdn_dt_bias, gdn_norm_w, mla_kv_norm, mla_w_uk, mla_w_uv, w_br_gdn, w_br_mla, w_out, mix_norm_post, ffn2_norm_pre, ffn2_w_gate, ffn2_w_up, ffn2_w_down, ffn2_norm_post):
    params = dict(
        ffn1_norm_pre=ffn1_norm_pre, ffn1_w_gate=ffn1_w_gate, ffn1_w_up=ffn1_w_up,
        ffn1_w_down=ffn1_w_down, ffn1_norm_post=ffn1_norm_post, mix_norm_pre=mix_norm_pre,
        w_in=w_in, gdn_conv_w=gdn_conv_w, gdn_a_log=gdn_a_log, gdn_dt_bias=gdn_dt_bias,
        gdn_norm_w=gdn_norm_w, mla_kv_norm=mla_kv_norm, mla_w_uk=mla_w_uk, mla_w_uv=mla_w_uv,
        w_br_gdn=w_br_gdn, w_br_mla=w_br_mla, w_out=w_out, mix_norm_post=mix_norm_post,
        ffn2_norm_pre=ffn2_norm_pre, ffn2_w_gate=ffn2_w_gate, ffn2_w_up=ffn2_w_up,
        ffn2_w_down=ffn2_w_down, ffn2_norm_post=ffn2_norm_post)
    depth = w_in.shape[0]
    d_model = x_prompt.shape[-1]
    b_p = x_prompt.shape[0]
    past = cache_mla_ckv.shape[2]
    yp, ys = x_prompt, x_sample
    outs_p, outs_s = [], []
    for l in range(depth):
        w = _prep_weights(l, d_model, **params)
        yp, *rest_p = _layer(
            yp, 0, jnp.zeros((b_p, GDN_CONV - 1, GDN_CONV_DIM), F32),
            jnp.zeros((b_p, GDN_HEADS, GDN_DK, GDN_DV), F32), None, None, w)
        ys, *rest_s = _layer(
            ys, past, state_gdn_conv[l], state_gdn_ssm[l], cache_mla_ckv[l], cache_mla_krope[l], w)
        outs_p.append(rest_p)
        outs_s.append(rest_s)
    stack = lambda outs, i: jnp.stack([o[i] for o in outs])
    return (yp, ys,
            stack(outs_p, 0), stack(outs_p, 1), stack(outs_p, 2), stack(outs_p, 3),
            stack(outs_s, 0), stack(outs_s, 1), stack(outs_s, 2), stack(outs_s, 3))
```

```python
import functools

import jax
import jax.numpy as jnp
from jax import lax
from jax.experimental import pallas as pl
from jax.experimental.pallas import tpu as pltpu

F32 = jnp.float32
BF16 = jnp.bfloat16

CHUNK = 64
LOG2_CHUNK = 6
NORM_EPS = 1e-6
GDN_HEADS = 8
GDN_DK = 128
GDN_DV = 128
GDN_CONV = 4
GDN_KEY_DIM = GDN_HEADS * GDN_DK
GDN_VAL_DIM = GDN_HEADS * GDN_DV
GDN_CONV_DIM = 2 * GDN_KEY_DIM + GDN_VAL_DIM
MLA_HEADS = 8
MLA_NOPE = 128
MLA_ROPE = 64
MLA_V = 128
MLA_KV_RANK = 512
MLA_SCALE = (MLA_NOPE + MLA_ROPE) ** -0.5
ROPE_THETA = 10000.0

LANES = 128
SUBLANES = 8
VMEM_CAP_BYTES = 56 * 2 ** 20
MLA_QK = MLA_KV_RANK + LANES
HI = lax.Precision.HIGHEST


def _params(semantics, vmem_bytes):
    limit = int(min(max(vmem_bytes, 16 * 2 ** 20), VMEM_CAP_BYTES))
    return pltpu.CompilerParams(dimension_semantics=semantics, vmem_limit_bytes=limit)


def _dot(a, b, prec=None):
    return lax.dot_general(a, b, (((1,), (0,)), ((), ())), precision=prec,
                           preferred_element_type=F32)


def _dot_nt(a, b, prec=None):
    return lax.dot_general(a, b, (((1,), (1,)), ((), ())), precision=prec,
                           preferred_element_type=F32)


def _dot_tn(a, b, prec=None):
    return lax.dot_general(a, b, (((0,), (0,)), ((), ())), precision=prec,
                           preferred_element_type=F32)


def _rms(y, g):
    return y * lax.rsqrt(jnp.mean(y * y, axis=-1, keepdims=True) + NORM_EPS) * g


def _silu(x):
    return x * jax.nn.sigmoid(x)


def _tile(n, pref):
    t = min(n, pref)
    assert n % t == 0, (n, pref)
    return t


def _norm_kernel(x_ref, g_ref, o_ref):
    o_ref[...] = _rms(x_ref[...], g_ref[...]).astype(BF16)


def _norm_cast(x, g):
    m, d = x.shape
    tm = _tile(m, 512)
    return pl.pallas_call(
        _norm_kernel,
        grid=(m // tm,),
        in_specs=[pl.BlockSpec((tm, d), lambda i: (i, 0)),
                  pl.BlockSpec((1, d), lambda i: (0, 0))],
        out_specs=pl.BlockSpec((tm, d), lambda i: (i, 0)),
        out_shape=jax.ShapeDtypeStruct((m, d), BF16),
        compiler_params=_params(("parallel",), 2 * tm * d * 6 + 4 * tm * d * 4),
        name="norm_cast",
    )(x, g)


def _ffn_kernel(xn_ref, x_ref, wg_ref, wu_ref, wd_ref, gpost_ref, gnext_ref, *rest, nf, emit_next):
    if emit_next:
        xo_ref, xno_ref, acc_ref = rest
    else:
        xo_ref, acc_ref = rest
    f = pl.program_id(1)
    xn = xn_ref[...]
    gate = _dot(xn, wg_ref[...])
    up = _dot(xn, wu_ref[...])
    h = (_silu(gate) * up).astype(BF16)
    part = _dot(h, wd_ref[...])

    @pl.when(f == 0)
    def _():
        acc_ref[...] = part

    @pl.when(f > 0)
    def _():
        acc_ref[...] += part

    @pl.when(f == nf - 1)
    def _():
        xnew = x_ref[...] + 0.5 * _rms(acc_ref[...], gpost_ref[...])
        xo_ref[...] = xnew
        if emit_next:
            xno_ref[...] = _rms(xnew, gnext_ref[...]).astype(BF16)


def _ffn(xn, x, wg, wu, wd, gpost, gnext, emit_next):
    m, d = x.shape
    dff = wg.shape[1]
    tm = _tile(m, 512)
    tf = _tile(dff, 512)
    nf = dff // tf
    row = lambda i, f: (i, 0)
    out_specs = [pl.BlockSpec((tm, d), row)]
    out_shape = [jax.ShapeDtypeStruct((m, d), F32)]
    if emit_next:
        out_specs.append(pl.BlockSpec((tm, d), row))
        out_shape.append(jax.ShapeDtypeStruct((m, d), BF16))
    vmem = (2 * (tm * d * 2 + tm * d * 4 + 2 * d * tf * 2 + tf * d * 2 + tm * d * 4 + tm * d * 2)
            + tm * d * 4 + 3 * tm * tf * 4 + 2 * tm * d * 4)
    res = pl.pallas_call(
        functools.partial(_ffn_kernel, nf=nf, emit_next=emit_next),
        grid=(m // tm, nf),
        in_specs=[pl.BlockSpec((tm, d), row),
                  pl.BlockSpec((tm, d), row),
                  pl.BlockSpec((d, tf), lambda i, f: (0, f)),
                  pl.BlockSpec((d, tf), lambda i, f: (0, f)),
                  pl.BlockSpec((tf, d), lambda i, f: (f, 0)),
                  pl.BlockSpec((1, d), lambda i, f: (0, 0)),
                  pl.BlockSpec((1, d), lambda i, f: (0, 0))],
        out_specs=out_specs,
        out_shape=out_shape,
        scratch_shapes=[pltpu.VMEM((tm, d), F32)],
        compiler_params=_params(("parallel", "arbitrary"), vmem),
        name="ffn",
    )(xn, x, wg, wu, wd, gpost, gnext)
    return res if emit_next else (res[0], None)


def _mm_kernel(a_ref, w_ref, o_ref):
    o_ref[...] = _dot(a_ref[...], w_ref[...]).astype(o_ref.dtype)


def _matmul(a, w, out_dtype):
    m, k = a.shape
    n = w.shape[1]
    tm = _tile(m, 1024)
    tn = _tile(n, 1024)
    osz = jnp.dtype(out_dtype).itemsize
    vmem = 2 * (tm * k * 2 + k * tn * 2 + tm * tn * osz) + tm * tn * 4
    return pl.pallas_call(
        _mm_kernel,
        grid=(m // tm, n // tn),
        in_specs=[pl.BlockSpec((tm, k), lambda i, j: (i, 0)),
                  pl.BlockSpec((k, tn), lambda i, j: (0, j))],
        out_specs=pl.BlockSpec((tm, tn), lambda i, j: (i, j)),
        out_shape=jax.ShapeDtypeStruct((m, n), out_dtype),
        compiler_params=_params(("parallel", "arbitrary"), vmem),
        name="in_proj",
    )(a, w)


COL_QKV = 0
COL_Z = COL_QKV + GDN_CONV_DIM
COL_QNOPE = COL_Z + GDN_VAL_DIM
COL_QROPE = COL_QNOPE + MLA_HEADS * MLA_NOPE
COL_GG = COL_QROPE + MLA_HEADS * LANES


def _proj_layout(d_model):
    col_gm = COL_GG + d_model
    col_ckv = col_gm + d_model
    col_kr = col_ckv + MLA_KV_RANK
    col_b = col_kr + LANES
    col_a = col_b + LANES
    n_used = col_a + LANES
    n_tot = -(-n_used // 1024) * 1024
    return dict(gm=col_gm, ckv=col_ckv, kr=col_kr, b=col_b, a=col_a, n=n_tot)


def _rot_half_cols(w):
    half = MLA_ROPE // 2
    return jnp.concatenate([w[:, half:], w[:, :half]], axis=1)


def _build_w_in(w_in, d_model):
    sizes = (GDN_CONV_DIM, GDN_VAL_DIM, GDN_HEADS, GDN_HEADS, MLA_HEADS * (MLA_NOPE + MLA_ROPE),
             MLA_KV_RANK, MLA_ROPE, d_model, d_model)
    offs = [0]
    for s in sizes:
        offs.append(offs[-1] + s)
    w_in = w_in.astype(BF16)
    part = lambda i: w_in[:, offs[i]:offs[i + 1]]
    qkv, z, b, a, qm, ckv, kr, gg, gm = (part(i) for i in range(9))
    d = w_in.shape[0]
    qm = qm.reshape(d, MLA_HEADS, MLA_NOPE + MLA_ROPE)
    qn = qm[:, :, :MLA_NOPE].reshape(d, MLA_HEADS * MLA_NOPE)
    qr = qm[:, :, MLA_NOPE:]
    half = MLA_ROPE // 2
    qr_pair = jnp.concatenate([qr, qr[:, :, half:], qr[:, :, :half]], axis=2).reshape(d, MLA_HEADS * LANES)
    kr_pair = jnp.concatenate([kr, _rot_half_cols(kr)], axis=1)
    pad8 = lambda w: jnp.pad(w, ((0, 0), (0, LANES - w.shape[1])))
    lay = _proj_layout(d_model)
    cols = [qkv, z, qn, qr_pair, gg, gm, ckv, kr_pair, pad8(b), pad8(a)]
    w = jnp.concatenate(cols, axis=1)
    w = jnp.pad(w, ((0, 0), (0, lay["n"] - w.shape[1])))
    return w.astype(BF16)


_NN = (((1,), (0,)), ((), ()))
_NT = (((1,), (1,)), ((), ()))
_TN = (((0,), (0,)), ((), ()))
GDN_MODE = dict(kk="bf16", inv="bf16", sol="bf16", state="bf16")
INV_BLOCK = 16


def _mm(a, b, mode, dims=_NN):
    if mode == "f32":
        return lax.dot_general(a, b, dims, precision=HI, preferred_element_type=F32)
    ah = a.astype(BF16)
    bh = b.astype(BF16)
    out = lax.dot_general(ah, bh, dims, preferred_element_type=F32)
    if mode == "bf16x3":
        al = (a - ah.astype(F32)).astype(BF16)
        bl = (b - bh.astype(F32)).astype(BF16)
        out = (out + lax.dot_general(ah, bl, dims, preferred_element_type=F32)
               + lax.dot_general(al, bh, dims, preferred_element_type=F32))
    return out


def _gdn_kernel(qkv_ref, z_ref, b_ref, a_ref, cst_ref, cw_ref, alog_ref, dtb_ref, nw_ref, s0_ref,
                o_ref, sout_ref, xp_ref, s_ref, *, G, C, nlev, nc):
    c = pl.program_id(1)
    hist = SUBLANES - (GDN_CONV - 1)

    @pl.when(c == 0)
    def _():
        xp_ref[:, 0:SUBLANES, :] = cst_ref[...]
        s_ref[...] = s0_ref[...]

    xp_ref[:, SUBLANES:SUBLANES + C, :] = qkv_ref[...]

    ri = lax.broadcasted_iota(jnp.int32, (C, C), 0)
    ci = lax.broadcasted_iota(jnp.int32, (C, C), 1)
    incl = ri >= ci
    strict = ri > ci
    eye = (ri == ci).astype(F32)
    log2_blk = INV_BLOCK.bit_length() - 1
    bdiag = jnp.right_shift(ri, log2_blk) == jnp.right_shift(ci, log2_blk)

    def conv(s, col):
        acc = xp_ref[s, pl.ds(hist, C), col:col + LANES] * cw_ref[0:1, col:col + LANES]
        for j in range(1, GDN_CONV):
            acc = acc + xp_ref[s, pl.ds(hist + j, C), col:col + LANES] * cw_ref[j:j + 1, col:col + LANES]
        return _silu(acc)

    units = [(s, h) for s in range(G) for h in range(GDN_HEADS)]
    every = lambda fn, *lists: [fn(*args) for args in zip(*lists)]
    mm_inv = lambda a, b: _mm(a, b, GDN_MODE["inv"])
    beta_all, gcum, gcum_t = [], [], []
    for s in range(G):
        beta_all.append(jax.nn.sigmoid(b_ref[s]))
        a_all = a_ref[s] + dtb_ref[...]
        softplus = jnp.maximum(a_all, 0.0) + jnp.log(1.0 + jnp.exp(-jnp.abs(a_all)))
        g_all = -jnp.exp(alog_ref[...]) * softplus
        gcum.append(_dot(incl.astype(F32), g_all, HI))
        gcum_t.append(_dot_tn(g_all, (ci >= ri).astype(F32), HI))

    def l2n(x):
        return x * lax.rsqrt(jnp.sum(x * x, axis=-1, keepdims=True) + 1e-6)

    q = [l2n(conv(s, h * GDN_DK)) * (GDN_DK ** -0.5) for s, h in units]
    k = [l2n(conv(s, GDN_KEY_DIM + h * GDN_DK)) for s, h in units]
    v = [conv(s, 2 * GDN_KEY_DIM + h * GDN_DV) for s, h in units]
    bcol = [beta_all[s][:, h:h + 1] for s, h in units]
    gcol = [gcum[s][:, h:h + 1] for s, h in units]
    glast = [gcum[s][C - 1:C, h:h + 1] for s, h in units]
    decay = [jnp.where(incl, jnp.exp(jnp.where(incl, gcum[s][:, h:h + 1] - gcum_t[s][h:h + 1, :], 0.0)), 0.0)
             for s, h in units]
    qkk = every(lambda q_, k_: _mm(jnp.concatenate([q_, k_], axis=0), k_, GDN_MODE["kk"], _NT), q, k)
    qk = every(lambda x, d: x[:C] * d, qkk, decay)
    nmat = every(lambda x, b_, d: -jnp.where(strict, b_ * x[C:] * d, 0.0), qkk, bcol, decay)
    ndiag = every(lambda n_: jnp.where(bdiag, n_, 0.0), nmat)
    tinv = every(lambda n_: eye + n_, ndiag)
    npow = ndiag
    for _ in range(INV_BLOCK.bit_length() - 2):
        npow = every(mm_inv, npow, npow)
        tinv = every(lambda t_, p_: t_ + mm_inv(t_, p_), tinv, npow)
    mpow = every(lambda t_, n_, d_: mm_inv(t_, n_ - d_), tinv, nmat, ndiag)
    for lev in range(nlev):
        if lev:
            mpow = every(mm_inv, mpow, mpow)
        tinv = every(lambda t_, m_: t_ + mm_inv(m_, t_), tinv, mpow)
    egc = every(jnp.exp, gcol)
    sol = every(lambda t_, v_, k_, b_, e_: _mm(t_, jnp.concatenate([v_ * b_, k_ * (b_ * e_)], axis=1),
                                               GDN_MODE["sol"]), tinv, v, k, bcol, egc)
    s_old = [s_ref[s, h] for s, h in units]
    ws_qs = every(lambda x, q_, e_, s_: _mm(jnp.concatenate([x[:, GDN_DV:], q_ * e_], axis=0), s_,
                                            GDN_MODE["state"]), sol, q, egc, s_old)
    v_new = every(lambda x, y: x[:, :GDN_DV] - y[:C], sol, ws_qs)
    o = every(lambda y, a_, vn: y[C:] + _mm(a_, vn, GDN_MODE["state"]), ws_qs, qk, v_new)
    s_new = every(lambda s_, gl, k_, gc, vn: s_ * jnp.exp(gl) + _mm(k_ * jnp.exp(gl - gc), vn, GDN_MODE["state"], _TN),
                  s_old, glast, k, gcol, v_new)
    for (s, h), sn, o_ in zip(units, s_new, o):
        s_ref[s, h] = sn
        zz = z_ref[s, :, h * GDN_DV:(h + 1) * GDN_DV]
        o_ref[s, :, h * GDN_DV:(h + 1) * GDN_DV] = (_rms(o_, nw_ref[...]) * _silu(zz)).astype(BF16)

    xp_ref[:, hist:SUBLANES, :] = xp_ref[:, C + hist:C + SUBLANES, :]

    @pl.when(c == nc - 1)
    def _():
        sout_ref[...] = s_ref[...]


def _gdn(proj, lay, row0, nb, t, conv_state, ssm0, conv_w, alog, dtb, nw):
    C = min(CHUNK, t)
    G = 2 if nb % 2 == 0 else 1
    assert t % C == 0 and C & (C - 1) == 0 and C % INV_BLOCK == 0 and row0 == 0
    nc = t // C
    nlev = (C // INV_BLOCK).bit_length() - 1
    proj3 = proj.reshape(nb, t, proj.shape[1])
    fixed = lambda b, c: (0, 0)
    cst = jnp.pad(conv_state, ((0, 0), (SUBLANES - (GDN_CONV - 1), 0), (0, 0)))
    vmem = G * (2 * (C * GDN_CONV_DIM * 4 + C * GDN_VAL_DIM * 4 + 2 * C * LANES * 4 + SUBLANES * GDN_CONV_DIM * 4
                     + C * GDN_VAL_DIM * 2 + 2 * GDN_HEADS * GDN_DK * GDN_DV * 4)
                + (SUBLANES + C) * GDN_CONV_DIM * 4 + GDN_HEADS * GDN_DK * GDN_DV * 4) + 16 * 2 ** 20
    o, s_new = pl.pallas_call(
        functools.partial(_gdn_kernel, G=G, C=C, nlev=nlev, nc=nc),
        grid=(nb // G, nc),
        in_specs=[pl.BlockSpec((G, C, GDN_CONV_DIM), lambda b, c: (b, c, COL_QKV // GDN_CONV_DIM)),
                  pl.BlockSpec((G, C, GDN_VAL_DIM), lambda b, c: (b, c, COL_Z // GDN_VAL_DIM)),
                  pl.BlockSpec((G, C, LANES), lambda b, c: (b, c, lay["b"] // LANES)),
                  pl.BlockSpec((G, C, LANES), lambda b, c: (b, c, lay["a"] // LANES)),
                  pl.BlockSpec((G, SUBLANES, GDN_CONV_DIM), lambda b, c: (b, 0, 0)),
                  pl.BlockSpec((GDN_CONV, GDN_CONV_DIM), fixed),
                  pl.BlockSpec((1, LANES), fixed),
                  pl.BlockSpec((1, LANES), fixed),
                  pl.BlockSpec((1, GDN_DV), fixed),
                  pl.BlockSpec((G, GDN_HEADS, GDN_DK, GDN_DV), lambda b, c: (b, 0, 0, 0))],
        out_specs=[pl.BlockSpec((G, C, GDN_VAL_DIM), lambda b, c: (b, c, 0)),
                   pl.BlockSpec((G, GDN_HEADS, GDN_DK, GDN_DV), lambda b, c: (b, 0, 0, 0))],
        out_shape=[jax.ShapeDtypeStruct((nb, t, GDN_VAL_DIM), BF16),
                   jax.ShapeDtypeStruct((nb, GDN_HEADS, GDN_DK, GDN_DV), F32)],
        scratch_shapes=[pltpu.VMEM((G, SUBLANES + C, GDN_CONV_DIM), F32),
                        pltpu.VMEM((G, GDN_HEADS, GDN_DK, GDN_DV), F32)],
        compiler_params=_params(("parallel", "arbitrary"), vmem),
        name="gdn",
    )(proj3, proj3, proj3, proj3, cst, conv_w, alog, dtb, nw, ssm0)
    return o.reshape(nb * t, GDN_VAL_DIM), s_new


def _mla_prep_kernel(qn_ref, qr_ref, ckv_ref, kr_ref, inv_ref, wuk_ref, kvn_ref,
                     ckv_o, kr_o, qcat_o, kcat_o, *maybe_vt_o, tm, pos0):
    j = pl.program_id(1)
    pos = (pos0 + j * tm + lax.broadcasted_iota(jnp.int32, (tm, LANES), 0)).astype(F32)
    lane = lax.broadcasted_iota(jnp.int32, (tm, LANES), 1)
    ang = pos * inv_ref[...]
    cos = jnp.cos(ang)
    sin = jnp.sin(ang)
    half = MLA_ROPE // 2
    cs = jnp.where(lane < MLA_ROPE, cos, jnp.where(lane < MLA_ROPE + half, -sin, sin))
    keep = lane < MLA_ROPE

    def rope(pair):
        prod = pair * cs
        return jnp.where(keep, prod + pltpu.roll(prod, MLA_ROPE, axis=1), 0.0)

    ckv = _rms(ckv_ref[...], kvn_ref[...])
    ckv_o[0] = ckv
    kr = rope(kr_ref[...])
    kr_o[0] = kr[:, :MLA_ROPE]
    kcat_o[0, :, :MLA_KV_RANK] = ckv.astype(BF16)
    kcat_o[0, :, MLA_KV_RANK:] = kr.astype(BF16)
    if maybe_vt_o:
        maybe_vt_o[0][0] = ckv.T.astype(BF16)
    for h in range(MLA_HEADS):
        qn = qn_ref[:, h * MLA_NOPE:(h + 1) * MLA_NOPE].astype(BF16)
        qlat = _dot(qn, wuk_ref[h]) * MLA_SCALE
        qcat_o[0, h, :, :MLA_KV_RANK] = qlat.astype(BF16)
        qr = rope(qr_ref[:, h * LANES:(h + 1) * LANES]) * MLA_SCALE
        qcat_o[0, h, :, MLA_KV_RANK:] = qr.astype(BF16)


def _mla_prep(proj, lay, row0, nb, t, pos0, inv128, wuk_t, kvn, emit_vt):
    tm = _tile(t, 256)
    nt = t // tm
    assert row0 % tm == 0
    r0 = row0 // tm
    rows = lambda b, j: r0 + b * nt + j
    fixed2 = lambda b, j: (0, 0)
    vmem = (2 * (tm * (MLA_HEADS * MLA_NOPE + MLA_HEADS * LANES + MLA_KV_RANK + LANES) * 4
                 + MLA_HEADS * MLA_NOPE * MLA_KV_RANK * 2
                 + tm * (MLA_KV_RANK + LANES) * 4 + (MLA_HEADS + 1) * tm * MLA_QK * 2) + 8 * 2 ** 20)
    out_specs = [pl.BlockSpec((1, tm, MLA_KV_RANK), lambda b, j: (b, j, 0)),
                 pl.BlockSpec((1, tm, MLA_ROPE), lambda b, j: (b, j, 0)),
                 pl.BlockSpec((1, MLA_HEADS, tm, MLA_QK), lambda b, j: (b, 0, j, 0)),
                 pl.BlockSpec((1, tm, MLA_QK), lambda b, j: (b, j, 0))]
    out_shape = [jax.ShapeDtypeStruct((nb, t, MLA_KV_RANK), F32),
                 jax.ShapeDtypeStruct((nb, t, MLA_ROPE), F32),
                 jax.ShapeDtypeStruct((nb, MLA_HEADS, t, MLA_QK), BF16),
                 jax.ShapeDtypeStruct((nb, t, MLA_QK), BF16)]
    if emit_vt:
        out_specs.append(pl.BlockSpec((1, MLA_KV_RANK, tm), lambda b, j: (b, 0, j)))
        out_shape.append(jax.ShapeDtypeStruct((nb, MLA_KV_RANK, t), BF16))
    return pl.pallas_call(
        functools.partial(_mla_prep_kernel, tm=tm, pos0=pos0),
        grid=(nb, nt),
        in_specs=[pl.BlockSpec((tm, MLA_HEADS * MLA_NOPE), lambda b, j: (rows(b, j), COL_QNOPE // (MLA_HEADS * MLA_NOPE))),
                  pl.BlockSpec((tm, MLA_HEADS * LANES), lambda b, j: (rows(b, j), COL_QROPE // (MLA_HEADS * LANES))),
                  pl.BlockSpec((tm, MLA_KV_RANK), lambda b, j: (rows(b, j), lay["ckv"] // MLA_KV_RANK)),
                  pl.BlockSpec((tm, LANES), lambda b, j: (rows(b, j), lay["kr"] // LANES)),
                  pl.BlockSpec((1, LANES), fixed2),
                  pl.BlockSpec((MLA_HEADS, MLA_NOPE, MLA_KV_RANK), lambda b, j: (0, 0, 0)),
                  pl.BlockSpec((1, MLA_KV_RANK), fixed2)],
        out_specs=out_specs,
        out_shape=out_shape,
        compiler_params=_params(("parallel", "parallel"), vmem),
        name="mla_prep",
    )(proj, proj, proj, proj, inv128, wuk_t, kvn)


ATTN_GROUPS = 2


def _attn_kernel(q_ref, k_ref, vt_ref, wuv_ref, o_ref, s0_ref, s1_ref, m_ref, l_ref, acc_ref,
                 *, tq, tk, n_valid):
    i = pl.program_id(1)
    cols = MLA_HEADS * tq
    cg = cols // ATTN_GROUPS
    q_first = i * tq
    c = lax.broadcasted_iota(jnp.int32, (1, cg), 1)
    qchunk = jnp.right_shift(q_first + jnp.bitwise_and(c, tq - 1), LOG2_CHUNK)
    k_all = jnp.minimum((q_first // CHUNK + 1) * CHUNK, n_valid)
    k_any = jnp.minimum(((q_first + tq - 1) // CHUNK + 1) * CHUNK, n_valid)
    nk = (k_any + tk - 1) // tk
    m_ref[...] = jnp.full(m_ref.shape, -jnp.inf, F32)
    l_ref[...] = jnp.zeros(l_ref.shape, F32)
    acc_ref[...] = jnp.zeros(acc_ref.shape, F32)

    def scores(j, buf):
        start = pl.multiple_of(j * tk, tk)
        q = q_ref[0].reshape(cols, MLA_QK)
        buf[...] = _dot_nt(k_ref[0, pl.ds(start, tk), :], q)

    def fold(x, op):
        while x.shape[0] > SUBLANES:
            half = x.shape[0] // 2
            x = op(x[:half], x[half:])
        return x

    def softmax_values(j, buf, masked):
        start = pl.multiple_of(j * tk, tk)
        vt = vt_ref[0, :, pl.ds(start, tk)]
        if masked:
            kpos = start + lax.broadcasted_iota(jnp.int32, (tk, 1), 0)
            mask = jnp.logical_and(jnp.right_shift(kpos, LOG2_CHUNK) <= qchunk, kpos < n_valid)
        for g in range(ATTN_GROUPS):
            cs = slice(g * cg, (g + 1) * cg)
            s = buf[:, cs]
            if masked:
                s = jnp.where(mask, s, -jnp.inf)
            m_old = m_ref[:, cs]
            m_new = jnp.maximum(m_old, jnp.max(fold(s, jnp.maximum), axis=0, keepdims=True))
            alpha = jnp.exp(m_old - m_new)
            p = jnp.exp(s - m_new)
            l_ref[:, cs] = alpha * l_ref[:, cs] + jnp.sum(fold(p, jnp.add), axis=0, keepdims=True)
            acc_ref[:, cs] = alpha * acc_ref[:, cs] + _dot(vt, p.astype(BF16))
            m_ref[:, cs] = m_new

    def by_parity(j, fn):
        @pl.when(jnp.bitwise_and(j, 1) == 0)
        def _():
            fn(s0_ref, s1_ref)

        @pl.when(jnp.bitwise_and(j, 1) == 1)
        def _():
            fn(s1_ref, s0_ref)

    scores(0, s0_ref)

    def body(j, carry, masked):
        def step(cur, nxt):
            scores(j + 1, nxt)
            softmax_values(j, cur, masked)
        by_parity(j, step)
        return carry

    n_open = jnp.minimum(k_all // tk, nk - 1)
    lax.fori_loop(0, n_open, functools.partial(body, masked=False), 0)
    lax.fori_loop(n_open, nk - 1, functools.partial(body, masked=True), 0)
    by_parity(nk - 1, lambda cur, nxt: softmax_values(nk - 1, cur, True))
    inv_l = 1.0 / l_ref[...]
    for h in range(MLA_HEADS):
        hs = slice(h * tq, (h + 1) * tq)
        lat_t = (acc_ref[:, hs] * inv_l[:, hs]).astype(BF16)
        o_ref[:, h * MLA_V:(h + 1) * MLA_V] = _dot_tn(lat_t, wuv_ref[h]).astype(BF16)


def _attn_cache_kernel(q_ref, ckv_ref, kr_ref, knew_ref, wuv_ref, o_ref, m_ref, l_ref, acc_ref,
                       *, tq, tk, nkb, past):
    j = pl.program_id(1)
    rows = MLA_HEADS * tq
    q = q_ref[0].reshape(rows, MLA_QK)
    r = lax.broadcasted_iota(jnp.int32, (rows, 1), 0)
    qchunk = jnp.right_shift(past + jnp.bitwise_and(r, tq - 1), LOG2_CHUNK)

    @pl.when(j == 0)
    def _():
        m_ref[...] = jnp.full(m_ref.shape, -jnp.inf, F32)
        l_ref[...] = jnp.zeros(l_ref.shape, F32)
        acc_ref[...] = jnp.zeros(acc_ref.shape, F32)

    def update(s, kpos, vals):
        s = jnp.where(jnp.right_shift(kpos, LOG2_CHUNK) <= qchunk, s, -jnp.inf)
        m_old = m_ref[...]
        m_new = jnp.maximum(m_old, jnp.max(s, axis=-1, keepdims=True))
        alpha = jnp.exp(m_old - m_new)
        p = jnp.exp(s - m_new)
        l_ref[...] = alpha * l_ref[...] + jnp.sum(p, axis=-1, keepdims=True)
        acc_ref[...] = alpha * acc_ref[...] + _dot(p.astype(BF16), vals)
        m_ref[...] = m_new

    ck = ckv_ref[0].astype(BF16)
    s = (_dot_nt(q[:, :MLA_KV_RANK], ck)
         + _dot_nt(q[:, MLA_KV_RANK:MLA_KV_RANK + MLA_ROPE], kr_ref[0].astype(BF16)))
    update(s, j * tk + lax.broadcasted_iota(jnp.int32, (1, tk), 1), ck)

    @pl.when(j == nkb - 1)
    def _():
        kn = knew_ref[0]
        update(_dot_nt(q, kn), past + lax.broadcasted_iota(jnp.int32, (1, tq), 1), kn[:, :MLA_KV_RANK])
        inv_l = 1.0 / l_ref[...]
        for h in range(MLA_HEADS):
            hs = slice(h * tq, (h + 1) * tq)
            lat = (acc_ref[hs, :] * inv_l[hs, :]).astype(BF16)
            o_ref[:, h * MLA_V:(h + 1) * MLA_V] = _dot(lat, wuv_ref[h]).astype(BF16)


def _attention(qcat, kcat, vt, wuv_t):
    nb, _, t, _ = qcat.shape
    tq = _tile(t, 128)
    tk = _tile(t, 512)
    assert tq & (tq - 1) == 0
    nq = t // tq
    cols = MLA_HEADS * tq
    vmem = (2 * (cols * MLA_QK * 2 + t * MLA_QK * 2 + MLA_KV_RANK * t * 2
                 + MLA_HEADS * MLA_KV_RANK * MLA_V * 2 + tq * MLA_HEADS * MLA_V * 2)
            + 2 * tk * cols * 4 + 2 * SUBLANES * cols * 4 + MLA_KV_RANK * cols * 4
            + 3 * tk * cols * 4 // ATTN_GROUPS + 4 * 2 ** 20)
    return pl.pallas_call(
        functools.partial(_attn_kernel, tq=tq, tk=tk, n_valid=t),
        grid=(nb, nq),
        in_specs=[pl.BlockSpec((1, MLA_HEADS, tq, MLA_QK), lambda b, i: (b, 0, i, 0)),
                  pl.BlockSpec((1, t, MLA_QK), lambda b, i: (b, 0, 0)),
                  pl.BlockSpec((1, MLA_KV_RANK, t), lambda b, i: (b, 0, 0)),
                  pl.BlockSpec((MLA_HEADS, MLA_KV_RANK, MLA_V), lambda b, i: (0, 0, 0))],
        out_specs=pl.BlockSpec((tq, MLA_HEADS * MLA_V), lambda b, i: (b * nq + i, 0)),
        out_shape=jax.ShapeDtypeStruct((nb * t, MLA_HEADS * MLA_V), BF16),
        scratch_shapes=[pltpu.VMEM((tk, cols), F32), pltpu.VMEM((tk, cols), F32), pltpu.VMEM((1, cols), F32),
                        pltpu.VMEM((1, cols), F32), pltpu.VMEM((MLA_KV_RANK, cols), F32)],
        compiler_params=_params(("parallel", "arbitrary"), vmem),
        name="mla_attention",
    )(qcat, kcat, vt, wuv_t)


def _attention_cached(qcat, ckv_past, krope_past, kcat_new, wuv_t):
    nb, _, t, _ = qcat.shape
    past = ckv_past.shape[1]
    tk = _tile(past, 1024)
    nkb = past // tk
    assert t & (t - 1) == 0
    rows = MLA_HEADS * t
    vmem = (2 * (rows * MLA_QK * 2 + tk * MLA_KV_RANK * 4 + tk * LANES * 4 + t * MLA_QK * 2
                 + MLA_HEADS * MLA_KV_RANK * MLA_V * 2 + t * MLA_HEADS * MLA_V * 2)
            + 2 * rows * LANES * 4 + rows * MLA_KV_RANK * 4 + tk * MLA_QK * 2 + 4 * rows * tk * 4 + 4 * 2 ** 20)
    return pl.pallas_call(
        functools.partial(_attn_cache_kernel, tq=t, tk=tk, nkb=nkb, past=past),
        grid=(nb, nkb),
        in_specs=[pl.BlockSpec((1, MLA_HEADS, t, MLA_QK), lambda b, j: (b, 0, 0, 0)),
                  pl.BlockSpec((1, tk, MLA_KV_RANK), lambda b, j: (b, j, 0)),
                  pl.BlockSpec((1, tk, MLA_ROPE), lambda b, j: (b, j, 0)),
                  pl.BlockSpec((1, t, MLA_QK), lambda b, j: (b, 0, 0)),
                  pl.BlockSpec((MLA_HEADS, MLA_KV_RANK, MLA_V), lambda b, j: (0, 0, 0))],
        out_specs=pl.BlockSpec((t, MLA_HEADS * MLA_V), lambda b, j: (b, 0)),
        out_shape=jax.ShapeDtypeStruct((nb * t, MLA_HEADS * MLA_V), BF16),
        scratch_shapes=[pltpu.VMEM((rows, 1), F32), pltpu.VMEM((rows, 1), F32),
                        pltpu.VMEM((rows, MLA_KV_RANK), F32)],
        compiler_params=_params(("parallel", "arbitrary"), vmem),
        name="mla_attention_cached",
    )(qcat, ckv_past, krope_past, kcat_new, wuv_t)


def _merge_kernel(og_ref, om_ref, wg_ref, wm_ref, gg_ref, gm_ref, o_ref):
    tg = _dot(og_ref[...], wg_ref[...])
    tm_ = _dot(om_ref[...], wm_ref[...])
    o_ref[...] = (jax.nn.sigmoid(gg_ref[...]) * tg + jax.nn.sigmoid(gm_ref[...]) * tm_).astype(BF16)


def _merge(og, om, wbg, wbm, proj, lay, row0):
    m, kg = og.shape
    d = wbg.shape[1]
    tm = _tile(m, 512)
    tn = _tile(d, 512)
    assert row0 % tm == 0 and COL_GG % tn == 0 and lay["gm"] % tn == 0
    r0 = row0 // tm
    vmem = 2 * (2 * tm * kg * 2 + 2 * kg * tn * 2 + 2 * tm * tn * 4 + tm * tn * 2) + 4 * tm * tn * 4
    return pl.pallas_call(
        _merge_kernel,
        grid=(m // tm, d // tn),
        in_specs=[pl.BlockSpec((tm, kg), lambda i, j: (i, 0)),
                  pl.BlockSpec((tm, kg), lambda i, j: (i, 0)),
                  pl.BlockSpec((kg, tn), lambda i, j: (0, j)),
                  pl.BlockSpec((kg, tn), lambda i, j: (0, j)),
                  pl.BlockSpec((tm, tn), lambda i, j: (r0 + i, COL_GG // tn + j)),
                  pl.BlockSpec((tm, tn), lambda i, j: (r0 + i, lay["gm"] // tn + j))],
        out_specs=pl.BlockSpec((tm, tn), lambda i, j: (i, j)),
        out_shape=jax.ShapeDtypeStruct((m, d), BF16),
        compiler_params=_params(("parallel", "arbitrary"), vmem),
        name="merge",
    )(og, om, wbg, wbm, proj, proj)


def _outproj_kernel(a_ref, w_ref, x_ref, gpost_ref, gnext_ref, xo_ref, xno_ref):
    y = _dot(a_ref[...], w_ref[...])
    xnew = x_ref[...] + _rms(y, gpost_ref[...])
    xo_ref[...] = xnew
    xno_ref[...] = _rms(xnew, gnext_ref[...]).astype(BF16)


def _outproj(a, w, x, gpost, gnext):
    m, d = x.shape
    k = a.shape[1]
    tm = _tile(m, 512)
    row = lambda i: (i, 0)
    fixed = lambda i: (0, 0)
    vmem = 2 * (tm * k * 2 + k * d * 2 + tm * d * 4 + tm * d * 4 + tm * d * 2) + 3 * tm * d * 4
    return pl.pallas_call(
        _outproj_kernel,
        grid=(m // tm,),
        in_specs=[pl.BlockSpec((tm, k), row), pl.BlockSpec((k, d), fixed), pl.BlockSpec((tm, d), row),
                  pl.BlockSpec((1, d), fixed), pl.BlockSpec((1, d), fixed)],
        out_specs=[pl.BlockSpec((tm, d), row), pl.BlockSpec((tm, d), row)],
        out_shape=[jax.ShapeDtypeStruct((m, d), F32), jax.ShapeDtypeStruct((m, d), BF16)],
        compiler_params=_params(("parallel",), vmem),
        name="out_proj",
    )(a, w, x, gpost, gnext)


def _layer(x3, pos0, conv_state, ssm0, ckv_past, krope_past, w):
    nb, t, d = x3.shape
    m = nb * t
    x = x3.reshape(m, d)
    lay = w["lay"]

    xn = _norm_cast(x, w["ffn1_norm_pre"])
    x1, hm = _ffn(xn, x, w["ffn1_wg"], w["ffn1_wu"], w["ffn1_wd"], w["ffn1_norm_post"],
                  w["mix_norm_pre"], True)
    proj = _matmul(hm, w["w_in"], F32)

    o_gdn, ssm_new = _gdn(proj, lay, 0, nb, t, conv_state, ssm0, w["conv_w"], w["alog"], w["dtb"], w["gdn_nw"])
    conv_new = proj.reshape(nb, t, -1)[:, t - (GDN_CONV - 1):, COL_QKV:COL_QKV + GDN_CONV_DIM]

    ckv, krope, qcat, kcat, *vt = _mla_prep(proj, lay, 0, nb, t, pos0, w["inv128"], w["wuk_t"],
                                            w["kv_norm"], ckv_past is None)
    if ckv_past is None:
        o_mla = _attention(qcat, kcat, vt[0], w["wuv_t"])
    else:
        o_mla = _attention_cached(qcat, ckv_past, krope_past, kcat, w["wuv_t"])

    merged = _merge(o_gdn, o_mla, w["w_br_gdn"], w["w_br_mla"], proj, lay, 0)
    x2, xn2 = _outproj(merged, w["w_out"], x1, w["mix_norm_post"], w["ffn2_norm_pre"])
    x3_, _ = _ffn(xn2, x2, w["ffn2_wg"], w["ffn2_wu"], w["ffn2_wd"], w["ffn2_norm_post"],
                  w["ffn2_norm_post"], False)
    return x3_.reshape(nb, t, d), conv_new, ssm_new, ckv, krope


def _prep_weights(l, d_model, **p):
    row = lambda v: v[l].reshape(1, -1).astype(F32)
    pad_lanes = lambda v: jnp.pad(v[l].reshape(1, -1).astype(F32), ((0, 0), (0, LANES - v.shape[1])))
    inv = ROPE_THETA ** (-jnp.arange(0, MLA_ROPE, 2, dtype=F32) / MLA_ROPE)
    w = dict(
        lay=_proj_layout(d_model),
        ffn1_norm_pre=row(p["ffn1_norm_pre"]), ffn1_norm_post=row(p["ffn1_norm_post"]),
        mix_norm_pre=row(p["mix_norm_pre"]), mix_norm_post=row(p["mix_norm_post"]),
        ffn2_norm_pre=row(p["ffn2_norm_pre"]), ffn2_norm_post=row(p["ffn2_norm_post"]),
        ffn1_wg=p["ffn1_w_gate"][l].astype(BF16), ffn1_wu=p["ffn1_w_up"][l].astype(BF16),
        ffn1_wd=p["ffn1_w_down"][l].astype(BF16),
        ffn2_wg=p["ffn2_w_gate"][l].astype(BF16), ffn2_wu=p["ffn2_w_up"][l].astype(BF16),
        ffn2_wd=p["ffn2_w_down"][l].astype(BF16),
        w_in=_build_w_in(p["w_in"][l], d_model),
        conv_w=p["gdn_conv_w"][l].astype(F32),
        alog=pad_lanes(p["gdn_a_log"]), dtb=pad_lanes(p["gdn_dt_bias"]),
        gdn_nw=row(p["gdn_norm_w"]), kv_norm=row(p["mla_kv_norm"]),
        inv128=jnp.tile(inv, LANES // inv.shape[0]).reshape(1, LANES),
        wuk_t=jnp.transpose(p["mla_w_uk"][l], (1, 2, 0)).astype(BF16),
        wuv_t=jnp.transpose(p["mla_w_uv"][l], (1, 0, 2)).astype(BF16),
        w_br_gdn=p["w_br_gdn"][l].astype(BF16), w_br_mla=p["w_br_mla"][l].astype(BF16),
        w_out=p["w_out"][l].astype(BF16),
    )
    return w


def kernel(x_prompt, x_sample, state_gdn_conv, state_gdn_ssm, cache_mla_ckv, cache_mla_krope, ffn1_norm_pre, ffn1_w_gate, ffn1_w_up, ffn1_w_down, ffn1_norm_post, mix_norm_pre, w_in, gdn_conv_w, gdn_a_log, gdn_dt_bias, gdn_norm_w, mla_kv_norm, mla_w_uk, mla_w_uv, w_br_gdn, w_br_mla, w_out, mix_norm_post, ffn2_norm_pre, ffn2_w_gate, ffn2_w_up, ffn2_w_down, ffn2_norm_post):
    params = dict(
        ffn1_norm_pre=ffn1_norm_pre, ffn1_w_gate=ffn1_w_gate, ffn1_w_up=ffn1_w_up,
        ffn1_w_down=ffn1_w_down, ffn1_norm_post=ffn1_norm_post, mix_norm_pre=mix_norm_pre,
        w_in=w_in, gdn_conv_w=gdn_conv_w, gdn_a_log=gdn_a_log, gdn_dt_bias=gdn_dt_bias,
        gdn_norm_w=gdn_norm_w, mla_kv_norm=mla_kv_norm, mla_w_uk=mla_w_uk, mla_w_uv=mla_w_uv,
        w_br_gdn=w_br_gdn, w_br_mla=w_br_mla, w_out=w_out, mix_norm_post=mix_norm_post,
        ffn2_norm_pre=ffn2_norm_pre, ffn2_w_gate=ffn2_w_gate, ffn2_w_up=ffn2_w_up,
        ffn2_w_down=ffn2_w_down, ffn2_norm_post=ffn2_norm_post)
    depth = w_in.shape[0]
    d_model = x_prompt.shape[-1]
    b_p = x_prompt.shape[0]
    past = cache_mla_ckv.shape[2]
    yp, ys = x_prompt, x_sample
    outs_p, outs_s = [], []
    for l in range(depth):
        w = _prep_weights(l, d_model, **params)
        yp, *rest_p = _layer(
            yp, 0, jnp.zeros((b_p, GDN_CONV - 1, GDN_CONV_DIM), F32),
            jnp.zeros((b_p, GDN_HEADS, GDN_DK, GDN_DV), F32), None, None, w)
        ys, *rest_s = _layer(
            ys, past, state_gdn_conv[l], state_gdn_ssm[l], cache_mla_ckv[l], cache_mla_krope[l], w)
        outs_p.append(rest_p)
        outs_s.append(rest_s)
    stack = lambda outs, i: jnp.stack([o[i] for o in outs])
    return (yp, ys,
            stack(outs_p, 0), stack(outs_p, 1), stack(outs_p, 2), stack(outs_p, 3),
            stack(outs_s, 0), stack(outs_s, 1), stack(outs_s, 2), stack(outs_s, 3))
```

```python
import functools

import jax
import jax.numpy as jnp
from jax import lax
from jax.experimental import pallas as pl
from jax.experimental.pallas import tpu as pltpu

F32 = jnp.float32
BF16 = jnp.bfloat16

CHUNK = 64
LOG2_CHUNK = 6
NORM_EPS = 1e-6
GDN_HEADS = 8
GDN_DK = 128
GDN_DV = 128
GDN_CONV = 4
GDN_KEY_DIM = GDN_HEADS * GDN_DK
GDN_VAL_DIM = GDN_HEADS * GDN_DV
GDN_CONV_DIM = 2 * GDN_KEY_DIM + GDN_VAL_DIM
MLA_HEADS = 8
MLA_NOPE = 128
MLA_ROPE = 64
MLA_V = 128
MLA_KV_RANK = 512
MLA_SCALE = (MLA_NOPE + MLA_ROPE) ** -0.5
ROPE_THETA = 10000.0

LANES = 128
SUBLANES = 8
VMEM_CAP_BYTES = 56 * 2 ** 20
MLA_QK = MLA_KV_RANK + LANES
HI = lax.Precision.HIGHEST


def _params(semantics, vmem_bytes):
    limit = int(min(max(vmem_bytes, 16 * 2 ** 20), VMEM_CAP_BYTES))
    return pltpu.CompilerParams(dimension_semantics=semantics, vmem_limit_bytes=limit)


def _dot(a, b, prec=None):
    return lax.dot_general(a, b, (((1,), (0,)), ((), ())), precision=prec,
                           preferred_element_type=F32)


def _dot_nt(a, b, prec=None):
    return lax.dot_general(a, b, (((1,), (1,)), ((), ())), precision=prec,
                           preferred_element_type=F32)


def _dot_tn(a, b, prec=None):
    return lax.dot_general(a, b, (((0,), (0,)), ((), ())), precision=prec,
                           preferred_element_type=F32)


def _rms(y, g):
    return y * lax.rsqrt(jnp.mean(y * y, axis=-1, keepdims=True) + NORM_EPS) * g


def _silu(x):
    return x * jax.nn.sigmoid(x)


def _tile(n, pref):
    t = min(n, pref)
    assert n % t == 0, (n, pref)
    return t


def _norm_kernel(x_ref, g_ref, o_ref):
    o_ref[...] = _rms(x_ref[...], g_ref[...]).astype(BF16)


def _norm_cast(x, g):
    m, d = x.shape
    tm = _tile(m, 512)
    return pl.pallas_call(
        _norm_kernel,
        grid=(m // tm,),
        in_specs=[pl.BlockSpec((tm, d), lambda i: (i, 0)),
                  pl.BlockSpec((1, d), lambda i: (0, 0))],
        out_specs=pl.BlockSpec((tm, d), lambda i: (i, 0)),
        out_shape=jax.ShapeDtypeStruct((m, d), BF16),
        compiler_params=_params(("parallel",), 2 * tm * d * 6 + 4 * tm * d * 4),
        name="norm_cast",
    )(x, g)


def _ffn_kernel(xn_ref, x_ref, wg_ref, wu_ref, wd_ref, gpost_ref, gnext_ref, *rest, nf, emit_next):
    if emit_next:
        xo_ref, xno_ref, acc_ref = rest
    else:
        xo_ref, acc_ref = rest
    f = pl.program_id(1)

    @pl.when(f == 0)
    def _():
        acc_ref[...] = jnp.zeros(acc_ref.shape, F32)

    xn = xn_ref[...]
    gate = _dot(xn, wg_ref[...])
    up = _dot(xn, wu_ref[...])
    h = (_silu(gate) * up).astype(BF16)
    acc_ref[...] += _dot(h, wd_ref[...])

    @pl.when(f == nf - 1)
    def _():
        xnew = x_ref[...] + 0.5 * _rms(acc_ref[...], gpost_ref[...])
        xo_ref[...] = xnew
        if emit_next:
            xno_ref[...] = _rms(xnew, gnext_ref[...]).astype(BF16)


def _ffn(xn, x, wg, wu, wd, gpost, gnext, emit_next):
    m, d = x.shape
    dff = wg.shape[1]
    tm = _tile(m, 512)
    tf = _tile(dff, 512)
    nf = dff // tf
    row = lambda i, f: (i, 0)
    out_specs = [pl.BlockSpec((tm, d), row)]
    out_shape = [jax.ShapeDtypeStruct((m, d), F32)]
    if emit_next:
        out_specs.append(pl.BlockSpec((tm, d), row))
        out_shape.append(jax.ShapeDtypeStruct((m, d), BF16))
    vmem = (2 * (tm * d * 2 + tm * d * 4 + 2 * d * tf * 2 + tf * d * 2 + tm * d * 4 + tm * d * 2)
            + tm * d * 4 + 3 * tm * tf * 4 + 2 * tm * d * 4)
    res = pl.pallas_call(
        functools.partial(_ffn_kernel, nf=nf, emit_next=emit_next),
        grid=(m // tm, nf),
        in_specs=[pl.BlockSpec((tm, d), row),
                  pl.BlockSpec((tm, d), row),
                  pl.BlockSpec((d, tf), lambda i, f: (0, f)),
                  pl.BlockSpec((d, tf), lambda i, f: (0, f)),
                  pl.BlockSpec((tf, d), lambda i, f: (f, 0)),
                  pl.BlockSpec((1, d), lambda i, f: (0, 0)),
                  pl.BlockSpec((1, d), lambda i, f: (0, 0))],
        out_specs=out_specs,
        out_shape=out_shape,
        scratch_shapes=[pltpu.VMEM((tm, d), F32)],
        compiler_params=_params(("parallel", "arbitrary"), vmem),
        name="ffn",
    )(xn, x, wg, wu, wd, gpost, gnext)
    return res if emit_next else (res[0], None)


def _mm_kernel(a_ref, w_ref, o_ref):
    o_ref[...] = _dot(a_ref[...], w_ref[...]).astype(o_ref.dtype)


def _matmul(a, w, out_dtype):
    m, k = a.shape
    n = w.shape[1]
    tm = _tile(m, 1024)
    tn = _tile(n, 1024)
    osz = jnp.dtype(out_dtype).itemsize
    vmem = 2 * (tm * k * 2 + k * tn * 2 + tm * tn * osz) + tm * tn * 4
    return pl.pallas_call(
        _mm_kernel,
        grid=(m // tm, n // tn),
        in_specs=[pl.BlockSpec((tm, k), lambda i, j: (i, 0)),
                  pl.BlockSpec((k, tn), lambda i, j: (0, j))],
        out_specs=pl.BlockSpec((tm, tn), lambda i, j: (i, j)),
        out_shape=jax.ShapeDtypeStruct((m, n), out_dtype),
        compiler_params=_params(("parallel", "arbitrary"), vmem),
        name="in_proj",
    )(a, w)


COL_QKV = 0
COL_Z = COL_QKV + GDN_CONV_DIM
COL_QNOPE = 0
COL_QROPE = COL_QNOPE + MLA_HEADS * MLA_NOPE
COL_CKV = COL_QROPE + MLA_HEADS * LANES
COL_KR = COL_CKV + MLA_KV_RANK
COL_B = COL_KR + LANES
COL_A = COL_B + LANES
N_MLA = 3072


def _build_w_in(w_in, d_model):
    sizes = (GDN_CONV_DIM, GDN_VAL_DIM, GDN_HEADS, GDN_HEADS, MLA_HEADS * (MLA_NOPE + MLA_ROPE),
             MLA_KV_RANK, MLA_ROPE, d_model, d_model)
    offs = [0]
    for s in sizes:
        offs.append(offs[-1] + s)
    w_gdn = w_in[:, :offs[2]].astype(BF16)
    w_gate = w_in[:, offs[7]:].astype(BF16)
    part = lambda i: w_in[:, offs[i]:offs[i + 1]].astype(BF16)
    b, a, qm, ckv, kr = (part(i) for i in range(2, 7))
    d = w_in.shape[0]
    half = MLA_ROPE // 2
    qm = qm.reshape(d, MLA_HEADS, MLA_NOPE + MLA_ROPE)
    qn = qm[:, :, :MLA_NOPE].reshape(d, MLA_HEADS * MLA_NOPE)
    qr = qm[:, :, MLA_NOPE:]
    qr_pair = jnp.concatenate([qr, qr[:, :, half:], qr[:, :, :half]], axis=2).reshape(d, MLA_HEADS * LANES)
    kr_pair = jnp.concatenate([kr, kr[:, half:], kr[:, :half]], axis=1)
    pad8 = lambda w: jnp.pad(w, ((0, 0), (0, LANES - w.shape[1])))
    w_mla = jnp.concatenate([qn, qr_pair, ckv, kr_pair, pad8(b), pad8(a)], axis=1)
    w_mla = jnp.pad(w_mla, ((0, 0), (0, N_MLA - w_mla.shape[1])))
    return w_gdn, w_gate, w_mla


_NN = (((1,), (0,)), ((), ()))
_NT = (((1,), (1,)), ((), ()))
_TN = (((0,), (0,)), ((), ()))
GDN_MODE = dict(kk="bf16", inv="bf16", sol="bf16", state="bf16")
INV_BLOCK = 16


def _mm(a, b, mode, dims=_NN):
    if mode == "f32":
        return lax.dot_general(a, b, dims, precision=HI, preferred_element_type=F32)
    ah = a.astype(BF16)
    bh = b.astype(BF16)
    out = lax.dot_general(ah, bh, dims, preferred_element_type=F32)
    if mode == "bf16x3":
        al = (a - ah.astype(F32)).astype(BF16)
        bl = (b - bh.astype(F32)).astype(BF16)
        out = (out + lax.dot_general(ah, bl, dims, preferred_element_type=F32)
               + lax.dot_general(al, bh, dims, preferred_element_type=F32))
    return out


def _gdn_kernel(qkv_ref, z_ref, b_ref, a_ref, cst_ref, cw_ref, alog_ref, dtb_ref, nw_ref, s0_ref,
                o_ref, sout_ref, xp_ref, s_ref, *, G, C, nlev, nc):
    c = pl.program_id(1)
    hist = SUBLANES - (GDN_CONV - 1)

    @pl.when(c == 0)
    def _():
        xp_ref[:, 0:SUBLANES, :] = cst_ref[...]
        s_ref[...] = s0_ref[...]

    xp_ref[:, SUBLANES:SUBLANES + C, :] = qkv_ref[...]

    ri = lax.broadcasted_iota(jnp.int32, (C, C), 0)
    ci = lax.broadcasted_iota(jnp.int32, (C, C), 1)
    incl = ri >= ci
    strict = ri > ci
    eye = (ri == ci).astype(F32)
    log2_blk = INV_BLOCK.bit_length() - 1
    bdiag = jnp.right_shift(ri, log2_blk) == jnp.right_shift(ci, log2_blk)

    def conv(s, col):
        acc = xp_ref[s, pl.ds(hist, C), col:col + LANES] * cw_ref[0:1, col:col + LANES]
        for j in range(1, GDN_CONV):
            acc = acc + xp_ref[s, pl.ds(hist + j, C), col:col + LANES] * cw_ref[j:j + 1, col:col + LANES]
        return _silu(acc)

    units = [(s, h) for s in range(G) for h in range(GDN_HEADS)]
    every = lambda fn, *lists: [fn(*args) for args in zip(*lists)]
    mm_inv = lambda a, b: _mm(a, b, GDN_MODE["inv"])
    beta_all, gcum, gcum_t = [], [], []
    for s in range(G):
        beta_all.append(jax.nn.sigmoid(b_ref[s]))
        a_all = a_ref[s] + dtb_ref[...]
        softplus = jnp.maximum(a_all, 0.0) + jnp.log(1.0 + jnp.exp(-jnp.abs(a_all)))
        g_all = -jnp.exp(alog_ref[...]) * softplus
        gcum.append(_dot(incl.astype(F32), g_all, HI))
        gcum_t.append(_dot_tn(g_all, (ci >= ri).astype(F32), HI))

    def l2n(x):
        return x * lax.rsqrt(jnp.sum(x * x, axis=-1, keepdims=True) + 1e-6)

    q = [l2n(conv(s, h * GDN_DK)) * (GDN_DK ** -0.5) for s, h in units]
    k = [l2n(conv(s, GDN_KEY_DIM + h * GDN_DK)) for s, h in units]
    v = [conv(s, 2 * GDN_KEY_DIM + h * GDN_DV) for s, h in units]
    bcol = [beta_all[s][:, h:h + 1] for s, h in units]
    gcol = [gcum[s][:, h:h + 1] for s, h in units]
    glast = [gcum[s][C - 1:C, h:h + 1] for s, h in units]
    decay = [jnp.where(incl, jnp.exp(jnp.where(incl, gcum[s][:, h:h + 1] - gcum_t[s][h:h + 1, :], 0.0)), 0.0)
             for s, h in units]
    qkk = every(lambda q_, k_: _mm(jnp.concatenate([q_, k_], axis=0), k_, GDN_MODE["kk"], _NT), q, k)
    qk = every(lambda x, d: x[:C] * d, qkk, decay)
    nmat = every(lambda x, b_, d: -jnp.where(strict, b_ * x[C:] * d, 0.0), qkk, bcol, decay)
    ndiag = every(lambda n_: jnp.where(bdiag, n_, 0.0), nmat)
    tinv = every(lambda n_: eye + n_, ndiag)
    npow = ndiag
    for _ in range(INV_BLOCK.bit_length() - 2):
        npow = every(mm_inv, npow, npow)
        tinv = every(lambda t_, p_: t_ + mm_inv(t_, p_), tinv, npow)
    mpow = every(lambda t_, n_, d_: mm_inv(t_, n_ - d_), tinv, nmat, ndiag)
    for lev in range(nlev):
        if lev:
            mpow = every(mm_inv, mpow, mpow)
        tinv = every(lambda t_, m_: t_ + mm_inv(m_, t_), tinv, mpow)
    egc = every(jnp.exp, gcol)
    sol = every(lambda t_, v_, k_, b_, e_: _mm(t_, jnp.concatenate([v_ * b_, k_ * (b_ * e_)], axis=1),
                                               GDN_MODE["sol"]), tinv, v, k, bcol, egc)
    s_old = [s_ref[s, h] for s, h in units]
    ws_qs = every(lambda x, q_, e_, s_: _mm(jnp.concatenate([x[:, GDN_DV:], q_ * e_], axis=0), s_,
                                            GDN_MODE["state"]), sol, q, egc, s_old)
    v_new = every(lambda x, y: x[:, :GDN_DV] - y[:C], sol, ws_qs)
    o = every(lambda y, a_, vn: y[C:] + _mm(a_, vn, GDN_MODE["state"]), ws_qs, qk, v_new)
    s_new = every(lambda s_, gl, k_, gc, vn: s_ * jnp.exp(gl) + _mm(k_ * jnp.exp(gl - gc), vn, GDN_MODE["state"], _TN),
                  s_old, glast, k, gcol, v_new)
    for (s, h), sn, o_ in zip(units, s_new, o):
        s_ref[s, h] = sn
        zz = z_ref[s, :, h * GDN_DV:(h + 1) * GDN_DV]
        o_ref[s, :, h * GDN_DV:(h + 1) * GDN_DV] = (_rms(o_, nw_ref[...]) * _silu(zz)).astype(BF16)

    xp_ref[:, hist:SUBLANES, :] = xp_ref[:, C + hist:C + SUBLANES, :]

    @pl.when(c == nc - 1)
    def _():
        sout_ref[...] = s_ref[...]


def _gdn(proj_gdn, proj_mla, nb, t, conv_state, ssm0, conv_w, alog, dtb, nw):
    C = min(CHUNK, t)
    G = 2 if nb % 2 == 0 else 1
    assert t % C == 0 and C & (C - 1) == 0 and C % INV_BLOCK == 0
    nc = t // C
    nlev = (C // INV_BLOCK).bit_length() - 1
    pg = proj_gdn.reshape(nb, t, proj_gdn.shape[1])
    pm = proj_mla.reshape(nb, t, proj_mla.shape[1])
    fixed = lambda b, c: (0, 0)
    cst = jnp.pad(conv_state, ((0, 0), (SUBLANES - (GDN_CONV - 1), 0), (0, 0)))
    vmem = G * (2 * (C * GDN_CONV_DIM * 4 + C * GDN_VAL_DIM * 4 + 2 * C * LANES * 4 + SUBLANES * GDN_CONV_DIM * 4
                     + C * GDN_VAL_DIM * 2 + 2 * GDN_HEADS * GDN_DK * GDN_DV * 4)
                + (SUBLANES + C) * GDN_CONV_DIM * 4 + GDN_HEADS * GDN_DK * GDN_DV * 4) + 16 * 2 ** 20
    o, s_new = pl.pallas_call(
        functools.partial(_gdn_kernel, G=G, C=C, nlev=nlev, nc=nc),
        grid=(nb // G, nc),
        in_specs=[pl.BlockSpec((G, C, GDN_CONV_DIM), lambda b, c: (b, c, COL_QKV // GDN_CONV_DIM)),
                  pl.BlockSpec((G, C, GDN_VAL_DIM), lambda b, c: (b, c, COL_Z // GDN_VAL_DIM)),
                  pl.BlockSpec((G, C, LANES), lambda b, c: (b, c, COL_B // LANES)),
                  pl.BlockSpec((G, C, LANES), lambda b, c: (b, c, COL_A // LANES)),
                  pl.BlockSpec((G, SUBLANES, GDN_CONV_DIM), lambda b, c: (b, 0, 0)),
                  pl.BlockSpec((GDN_CONV, GDN_CONV_DIM), fixed),
                  pl.BlockSpec((1, LANES), fixed),
                  pl.BlockSpec((1, LANES), fixed),
                  pl.BlockSpec((1, GDN_DV), fixed),
                  pl.BlockSpec((G, GDN_HEADS, GDN_DK, GDN_DV), lambda b, c: (b, 0, 0, 0))],
        out_specs=[pl.BlockSpec((G, C, GDN_VAL_DIM), lambda b, c: (b, c, 0)),
                   pl.BlockSpec((G, GDN_HEADS, GDN_DK, GDN_DV), lambda b, c: (b, 0, 0, 0))],
        out_shape=[jax.ShapeDtypeStruct((nb, t, GDN_VAL_DIM), BF16),
                   jax.ShapeDtypeStruct((nb, GDN_HEADS, GDN_DK, GDN_DV), F32)],
        scratch_shapes=[pltpu.VMEM((G, SUBLANES + C, GDN_CONV_DIM), F32),
                        pltpu.VMEM((G, GDN_HEADS, GDN_DK, GDN_DV), F32)],
        compiler_params=_params(("parallel", "arbitrary"), vmem),
        name="gdn",
    )(pg, pg, pm, pm, cst, conv_w, alog, dtb, nw, ssm0)
    return o.reshape(nb * t, GDN_VAL_DIM), s_new


def _mla_prep_kernel(qn_ref, qr_ref, ckv_ref, kr_ref, inv_ref, wuk_ref, kvn_ref,
                     ckv_o, kr_o, qcat_o, kcat_o, *maybe_vt_o, tm, pos0):
    j = pl.program_id(1)
    pos = (pos0 + j * tm + lax.broadcasted_iota(jnp.int32, (tm, LANES), 0)).astype(F32)
    lane = lax.broadcasted_iota(jnp.int32, (tm, LANES), 1)
    ang = pos * inv_ref[...]
    cos = jnp.cos(ang)
    sin = jnp.sin(ang)
    half = MLA_ROPE // 2
    cs = jnp.where(lane < MLA_ROPE, cos, jnp.where(lane < MLA_ROPE + half, -sin, sin))
    keep = lane < MLA_ROPE

    def rope(pair):
        prod = pair * cs
        return jnp.where(keep, prod + pltpu.roll(prod, MLA_ROPE, axis=1), 0.0)

    ckv = _rms(ckv_ref[...], kvn_ref[...])
    ckv_o[0] = ckv
    kr = rope(kr_ref[...])
    kr_o[0] = kr[:, :MLA_ROPE]
    kcat_o[0, :, :MLA_KV_RANK] = ckv.astype(BF16)
    kcat_o[0, :, MLA_KV_RANK:] = kr.astype(BF16)
    if maybe_vt_o:
        maybe_vt_o[0][0] = ckv.T.astype(BF16)
    for h in range(MLA_HEADS):
        qn = qn_ref[:, h * MLA_NOPE:(h + 1) * MLA_NOPE].astype(BF16)
        qlat = _dot(qn, wuk_ref[h]) * MLA_SCALE
        qcat_o[0, h, :, :MLA_KV_RANK] = qlat.astype(BF16)
        qr = rope(qr_ref[:, h * LANES:(h + 1) * LANES]) * MLA_SCALE
        qcat_o[0, h, :, MLA_KV_RANK:] = qr.astype(BF16)


def _mla_prep(proj, nb, t, pos0, inv128, wuk_t, kvn, emit_vt):
    tm = _tile(t, 256)
    nt = t // tm
    rows = lambda b, j: b * nt + j
    fixed2 = lambda b, j: (0, 0)
    vmem = (2 * (tm * (MLA_HEADS * MLA_NOPE + MLA_HEADS * LANES + MLA_KV_RANK + LANES) * 4
                 + MLA_HEADS * MLA_NOPE * MLA_KV_RANK * 2
                 + tm * (MLA_KV_RANK + LANES) * 4 + (MLA_HEADS + 1) * tm * MLA_QK * 2) + 8 * 2 ** 20)
    out_specs = [pl.BlockSpec((1, tm, MLA_KV_RANK), lambda b, j: (b, j, 0)),
                 pl.BlockSpec((1, tm, MLA_ROPE), lambda b, j: (b, j, 0)),
                 pl.BlockSpec((1, MLA_HEADS, tm, MLA_QK), lambda b, j: (b, 0, j, 0)),
                 pl.BlockSpec((1, tm, MLA_QK), lambda b, j: (b, j, 0))]
    out_shape = [jax.ShapeDtypeStruct((nb, t, MLA_KV_RANK), F32),
                 jax.ShapeDtypeStruct((nb, t, MLA_ROPE), F32),
                 jax.ShapeDtypeStruct((nb, MLA_HEADS, t, MLA_QK), BF16),
                 jax.ShapeDtypeStruct((nb, t, MLA_QK), BF16)]
    if emit_vt:
        out_specs.append(pl.BlockSpec((1, MLA_KV_RANK, tm), lambda b, j: (b, 0, j)))
        out_shape.append(jax.ShapeDtypeStruct((nb, MLA_KV_RANK, t), BF16))
    return pl.pallas_call(
        functools.partial(_mla_prep_kernel, tm=tm, pos0=pos0),
        grid=(nb, nt),
        in_specs=[pl.BlockSpec((tm, MLA_HEADS * MLA_NOPE), lambda b, j: (rows(b, j), COL_QNOPE // (MLA_HEADS * MLA_NOPE))),
                  pl.BlockSpec((tm, MLA_HEADS * LANES), lambda b, j: (rows(b, j), COL_QROPE // (MLA_HEADS * LANES))),
                  pl.BlockSpec((tm, MLA_KV_RANK), lambda b, j: (rows(b, j), COL_CKV // MLA_KV_RANK)),
                  pl.BlockSpec((tm, LANES), lambda b, j: (rows(b, j), COL_KR // LANES)),
                  pl.BlockSpec((1, LANES), fixed2),
                  pl.BlockSpec((MLA_HEADS, MLA_NOPE, MLA_KV_RANK), lambda b, j: (0, 0, 0)),
                  pl.BlockSpec((1, MLA_KV_RANK), fixed2)],
        out_specs=out_specs,
        out_shape=out_shape,
        compiler_params=_params(("parallel", "parallel"), vmem),
        name="mla_prep",
    )(proj, proj, proj, proj, inv128, wuk_t, kvn)


ATTN_GROUPS = 2


def _attn_kernel(q_ref, k_ref, vt_ref, wuv_ref, o_ref, s0_ref, s1_ref, m_ref, l_ref, acc_ref,
                 *, tq, tk, n_valid):
    i = pl.program_id(1)
    cols = MLA_HEADS * tq
    cg = cols // ATTN_GROUPS
    q_first = i * tq
    c = lax.broadcasted_iota(jnp.int32, (1, cg), 1)
    qchunk = jnp.right_shift(q_first + jnp.bitwise_and(c, tq - 1), LOG2_CHUNK)
    k_all = jnp.minimum((q_first // CHUNK + 1) * CHUNK, n_valid)
    k_any = jnp.minimum(((q_first + tq - 1) // CHUNK + 1) * CHUNK, n_valid)
    nk = (k_any + tk - 1) // tk
    m_ref[...] = jnp.full(m_ref.shape, -jnp.inf, F32)
    l_ref[...] = jnp.zeros(l_ref.shape, F32)
    acc_ref[...] = jnp.zeros(acc_ref.shape, F32)

    def scores(j, buf):
        start = pl.multiple_of(j * tk, tk)
        q = q_ref[0].reshape(cols, MLA_QK)
        buf[...] = _dot_nt(k_ref[0, pl.ds(start, tk), :], q)

    def fold(x, op):
        while x.shape[0] > SUBLANES:
            half = x.shape[0] // 2
            x = op(x[:half], x[half:])
        return x

    def softmax_values(j, buf, masked):
        start = pl.multiple_of(j * tk, tk)
        vt = vt_ref[0, :, pl.ds(start, tk)]
        if masked:
            kpos = start + lax.broadcasted_iota(jnp.int32, (tk, 1), 0)
            mask = jnp.logical_and(jnp.right_shift(kpos, LOG2_CHUNK) <= qchunk, kpos < n_valid)
        for g in range(ATTN_GROUPS):
            cs = slice(g * cg, (g + 1) * cg)
            s = buf[:, cs]
            if masked:
                s = jnp.where(mask, s, -jnp.inf)
            m_old = m_ref[:, cs]
            m_new = jnp.maximum(m_old, jnp.max(fold(s, jnp.maximum), axis=0, keepdims=True))
            alpha = jnp.exp(m_old - m_new)
            p = jnp.exp(s - m_new)
            l_ref[:, cs] = alpha * l_ref[:, cs] + jnp.sum(fold(p, jnp.add), axis=0, keepdims=True)
            acc_ref[:, cs] = alpha * acc_ref[:, cs] + _dot(vt, p.astype(BF16))
            m_ref[:, cs] = m_new

    def by_parity(j, fn):
        @pl.when(jnp.bitwise_and(j, 1) == 0)
        def _():
            fn(s0_ref, s1_ref)

        @pl.when(jnp.bitwise_and(j, 1) == 1)
        def _():
            fn(s1_ref, s0_ref)

    scores(0, s0_ref)

    def body(j, carry, masked):
        def step(cur, nxt):
            scores(j + 1, nxt)
            softmax_values(j, cur, masked)
        by_parity(j, step)
        return carry

    n_open = jnp.minimum(k_all // tk, nk - 1)
    lax.fori_loop(0, n_open, functools.partial(body, masked=False), 0)
    lax.fori_loop(n_open, nk - 1, functools.partial(body, masked=True), 0)
    by_parity(nk - 1, lambda cur, nxt: softmax_values(nk - 1, cur, True))
    inv_l = 1.0 / l_ref[...]
    for h in range(MLA_HEADS):
        hs = slice(h * tq, (h + 1) * tq)
        lat_t = (acc_ref[:, hs] * inv_l[:, hs]).astype(BF16)
        o_ref[:, h * MLA_V:(h + 1) * MLA_V] = _dot_tn(lat_t, wuv_ref[h]).astype(BF16)


def _attn_cache_kernel(q_ref, ckv_ref, kr_ref, knew_ref, wuv_ref, o_ref, m_ref, l_ref, acc_ref,
                       *, tq, tk, nkb, past):
    j = pl.program_id(1)
    rows = MLA_HEADS * tq
    q = q_ref[0].reshape(rows, MLA_QK)
    r = lax.broadcasted_iota(jnp.int32, (rows, 1), 0)
    qchunk = jnp.right_shift(past + jnp.bitwise_and(r, tq - 1), LOG2_CHUNK)

    @pl.when(j == 0)
    def _():
        m_ref[...] = jnp.full(m_ref.shape, -jnp.inf, F32)
        l_ref[...] = jnp.zeros(l_ref.shape, F32)
        acc_ref[...] = jnp.zeros(acc_ref.shape, F32)

    def update(s, kpos, vals):
        s = jnp.where(jnp.right_shift(kpos, LOG2_CHUNK) <= qchunk, s, -jnp.inf)
        m_old = m_ref[...]
        m_new = jnp.maximum(m_old, jnp.max(s, axis=-1, keepdims=True))
        alpha = jnp.exp(m_old - m_new)
        p = jnp.exp(s - m_new)
        l_ref[...] = alpha * l_ref[...] + jnp.sum(p, axis=-1, keepdims=True)
        acc_ref[...] = alpha * acc_ref[...] + _dot(p.astype(BF16), vals)
        m_ref[...] = m_new

    ck = ckv_ref[0].astype(BF16)
    s = (_dot_nt(q[:, :MLA_KV_RANK], ck)
         + _dot_nt(q[:, MLA_KV_RANK:MLA_KV_RANK + MLA_ROPE], kr_ref[0].astype(BF16)))
    update(s, j * tk + lax.broadcasted_iota(jnp.int32, (1, tk), 1), ck)

    @pl.when(j == nkb - 1)
    def _():
        kn = knew_ref[0]
        update(_dot_nt(q, kn), past + lax.broadcasted_iota(jnp.int32, (1, tq), 1), kn[:, :MLA_KV_RANK])
        inv_l = 1.0 / l_ref[...]
        for h in range(MLA_HEADS):
            hs = slice(h * tq, (h + 1) * tq)
            lat = (acc_ref[hs, :] * inv_l[hs, :]).astype(BF16)
            o_ref[:, h * MLA_V:(h + 1) * MLA_V] = _dot(lat, wuv_ref[h]).astype(BF16)


def _attention(qcat, kcat, vt, wuv_t):
    nb, _, t, _ = qcat.shape
    tq = _tile(t, 128)
    tk = _tile(t, 512)
    assert tq & (tq - 1) == 0
    nq = t // tq
    cols = MLA_HEADS * tq
    vmem = (2 * (cols * MLA_QK * 2 + t * MLA_QK * 2 + MLA_KV_RANK * t * 2
                 + MLA_HEADS * MLA_KV_RANK * MLA_V * 2 + tq * MLA_HEADS * MLA_V * 2)
            + 2 * tk * cols * 4 + 2 * SUBLANES * cols * 4 + MLA_KV_RANK * cols * 4
            + 3 * tk * cols * 4 // ATTN_GROUPS + 4 * 2 ** 20)
    return pl.pallas_call(
        functools.partial(_attn_kernel, tq=tq, tk=tk, n_valid=t),
        grid=(nb, nq),
        in_specs=[pl.BlockSpec((1, MLA_HEADS, tq, MLA_QK), lambda b, i: (b, 0, i, 0)),
                  pl.BlockSpec((1, t, MLA_QK), lambda b, i: (b, 0, 0)),
                  pl.BlockSpec((1, MLA_KV_RANK, t), lambda b, i: (b, 0, 0)),
                  pl.BlockSpec((MLA_HEADS, MLA_KV_RANK, MLA_V), lambda b, i: (0, 0, 0))],
        out_specs=pl.BlockSpec((tq, MLA_HEADS * MLA_V), lambda b, i: (b * nq + i, 0)),
        out_shape=jax.ShapeDtypeStruct((nb * t, MLA_HEADS * MLA_V), BF16),
        scratch_shapes=[pltpu.VMEM((tk, cols), F32), pltpu.VMEM((tk, cols), F32), pltpu.VMEM((1, cols), F32),
                        pltpu.VMEM((1, cols), F32), pltpu.VMEM((MLA_KV_RANK, cols), F32)],
        compiler_params=_params(("parallel", "arbitrary"), vmem),
        name="mla_attention",
    )(qcat, kcat, vt, wuv_t)


def _attention_cached(qcat, ckv_past, krope_past, kcat_new, wuv_t):
    nb, _, t, _ = qcat.shape
    past = ckv_past.shape[1]
    tk = _tile(past, 1024)
    nkb = past // tk
    assert t & (t - 1) == 0
    rows = MLA_HEADS * t
    vmem = (2 * (rows * MLA_QK * 2 + tk * MLA_KV_RANK * 4 + tk * LANES * 4 + t * MLA_QK * 2
                 + MLA_HEADS * MLA_KV_RANK * MLA_V * 2 + t * MLA_HEADS * MLA_V * 2)
            + 2 * rows * LANES * 4 + rows * MLA_KV_RANK * 4 + tk * MLA_QK * 2 + 4 * rows * tk * 4 + 4 * 2 ** 20)
    return pl.pallas_call(
        functools.partial(_attn_cache_kernel, tq=t, tk=tk, nkb=nkb, past=past),
        grid=(nb, nkb),
        in_specs=[pl.BlockSpec((1, MLA_HEADS, t, MLA_QK), lambda b, j: (b, 0, 0, 0)),
                  pl.BlockSpec((1, tk, MLA_KV_RANK), lambda b, j: (b, j, 0)),
                  pl.BlockSpec((1, tk, MLA_ROPE), lambda b, j: (b, j, 0)),
                  pl.BlockSpec((1, t, MLA_QK), lambda b, j: (b, 0, 0)),
                  pl.BlockSpec((MLA_HEADS, MLA_KV_RANK, MLA_V), lambda b, j: (0, 0, 0))],
        out_specs=pl.BlockSpec((t, MLA_HEADS * MLA_V), lambda b, j: (b, 0)),
        out_shape=jax.ShapeDtypeStruct((nb * t, MLA_HEADS * MLA_V), BF16),
        scratch_shapes=[pltpu.VMEM((rows, 1), F32), pltpu.VMEM((rows, 1), F32),
                        pltpu.VMEM((rows, MLA_KV_RANK), F32)],
        compiler_params=_params(("parallel", "arbitrary"), vmem),
        name="mla_attention_cached",
    )(qcat, ckv_past, krope_past, kcat_new, wuv_t)


def _merge_kernel(og_ref, om_ref, wg_ref, wm_ref, gg_ref, gm_ref, o_ref):
    tg = _dot(og_ref[...], wg_ref[...])
    tm_ = _dot(om_ref[...], wm_ref[...])
    o_ref[...] = (jax.nn.sigmoid(gg_ref[...]) * tg + jax.nn.sigmoid(gm_ref[...]) * tm_).astype(BF16)


def _merge(og, om, wbg, wbm, proj_gate):
    m, kg = og.shape
    d = wbg.shape[1]
    tm = _tile(m, 512)
    tn = _tile(d, 512)
    vmem = 2 * (2 * tm * kg * 2 + 2 * kg * tn * 2 + 2 * tm * tn * 4 + tm * tn * 2) + 4 * tm * tn * 4
    return pl.pallas_call(
        _merge_kernel,
        grid=(m // tm, d // tn),
        in_specs=[pl.BlockSpec((tm, kg), lambda i, j: (i, 0)),
                  pl.BlockSpec((tm, kg), lambda i, j: (i, 0)),
                  pl.BlockSpec((kg, tn), lambda i, j: (0, j)),
                  pl.BlockSpec((kg, tn), lambda i, j: (0, j)),
                  pl.BlockSpec((tm, tn), lambda i, j: (i, j)),
                  pl.BlockSpec((tm, tn), lambda i, j: (i, d // tn + j))],
        out_specs=pl.BlockSpec((tm, tn), lambda i, j: (i, j)),
        out_shape=jax.ShapeDtypeStruct((m, d), BF16),
        compiler_params=_params(("parallel", "arbitrary"), vmem),
        name="merge",
    )(og, om, wbg, wbm, proj_gate, proj_gate)


def _outproj_kernel(a_ref, w_ref, x_ref, gpost_ref, gnext_ref, xo_ref, xno_ref):
    y = _dot(a_ref[...], w_ref[...])
    xnew = x_ref[...] + _rms(y, gpost_ref[...])
    xo_ref[...] = xnew
    xno_ref[...] = _rms(xnew, gnext_ref[...]).astype(BF16)


def _outproj(a, w, x, gpost, gnext):
    m, d = x.shape
    k = a.shape[1]
    tm = _tile(m, 512)
    row = lambda i: (i, 0)
    fixed = lambda i: (0, 0)
    vmem = 2 * (tm * k * 2 + k * d * 2 + tm * d * 4 + tm * d * 4 + tm * d * 2) + 3 * tm * d * 4
    return pl.pallas_call(
        _outproj_kernel,
        grid=(m // tm,),
        in_specs=[pl.BlockSpec((tm, k), row), pl.BlockSpec((k, d), fixed), pl.BlockSpec((tm, d), row),
                  pl.BlockSpec((1, d), fixed), pl.BlockSpec((1, d), fixed)],
        out_specs=[pl.BlockSpec((tm, d), row), pl.BlockSpec((tm, d), row)],
        out_shape=[jax.ShapeDtypeStruct((m, d), F32), jax.ShapeDtypeStruct((m, d), BF16)],
        compiler_params=_params(("parallel",), vmem),
        name="out_proj",
    )(a, w, x, gpost, gnext)


def _layer(x3, pos0, conv_state, ssm0, ckv_past, krope_past, w):
    nb, t, d = x3.shape
    m = nb * t
    x = x3.reshape(m, d)
    xn = _norm_cast(x, w["ffn1_norm_pre"])
    x1, hm = _ffn(xn, x, w["ffn1_wg"], w["ffn1_wu"], w["ffn1_wd"], w["ffn1_norm_post"],
                  w["mix_norm_pre"], True)
    proj_gdn = _matmul(hm, w["w_in_gdn"], F32)
    proj_gate = _matmul(hm, w["w_in_gate"], F32)
    proj_mla = _matmul(hm, w["w_in_mla"], F32)

    o_gdn, ssm_new = _gdn(proj_gdn, proj_mla, nb, t, conv_state, ssm0, w["conv_w"], w["alog"], w["dtb"],
                          w["gdn_nw"])
    conv_new = proj_gdn.reshape(nb, t, -1)[:, t - (GDN_CONV - 1):, COL_QKV:COL_QKV + GDN_CONV_DIM]

    ckv, krope, qcat, kcat, *vt = _mla_prep(proj_mla, nb, t, pos0, w["inv128"], w["wuk_t"],
                                            w["kv_norm"], ckv_past is None)
    if ckv_past is None:
        o_mla = _attention(qcat, kcat, vt[0], w["wuv_t"])
    else:
        o_mla = _attention_cached(qcat, ckv_past, krope_past, kcat, w["wuv_t"])

    merged = _merge(o_gdn, o_mla, w["w_br_gdn"], w["w_br_mla"], proj_gate)
    x2, xn2 = _outproj(merged, w["w_out"], x1, w["mix_norm_post"], w["ffn2_norm_pre"])
    x3_, _ = _ffn(xn2, x2, w["ffn2_wg"], w["ffn2_wu"], w["ffn2_wd"], w["ffn2_norm_post"],
                  w["ffn2_norm_post"], False)
    return x3_.reshape(nb, t, d), conv_new, ssm_new, ckv, krope


def _prep_weights(l, d_model, **p):
    row = lambda v: v[l].reshape(1, -1).astype(F32)
    pad_lanes = lambda v: jnp.pad(v[l].reshape(1, -1).astype(F32), ((0, 0), (0, LANES - v.shape[1])))
    inv = ROPE_THETA ** (-jnp.arange(0, MLA_ROPE, 2, dtype=F32) / MLA_ROPE)
    w_in_gdn, w_in_gate, w_in_mla = _build_w_in(p["w_in"][l], d_model)
    w = dict(
        w_in_gdn=w_in_gdn, w_in_gate=w_in_gate, w_in_mla=w_in_mla,
        ffn1_norm_pre=row(p["ffn1_norm_pre"]), ffn1_norm_post=row(p["ffn1_norm_post"]),
        mix_norm_pre=row(p["mix_norm_pre"]), mix_norm_post=row(p["mix_norm_post"]),
        ffn2_norm_pre=row(p["ffn2_norm_pre"]), ffn2_norm_post=row(p["ffn2_norm_post"]),
        ffn1_wg=p["ffn1_w_gate"][l].astype(BF16), ffn1_wu=p["ffn1_w_up"][l].astype(BF16),
        ffn1_wd=p["ffn1_w_down"][l].astype(BF16),
        ffn2_wg=p["ffn2_w_gate"][l].astype(BF16), ffn2_wu=p["ffn2_w_up"][l].astype(BF16),
        ffn2_wd=p["ffn2_w_down"][l].astype(BF16),
        conv_w=p["gdn_conv_w"][l].astype(F32),
        alog=pad_lanes(p["gdn_a_log"]), dtb=pad_lanes(p["gdn_dt_bias"]),
        gdn_nw=row(p["gdn_norm_w"]), kv_norm=row(p["mla_kv_norm"]),
        inv128=jnp.tile(inv, LANES // inv.shape[0]).reshape(1, LANES),
        wuk_t=jnp.transpose(p["mla_w_uk"][l], (1, 2, 0)).astype(BF16),
        wuv_t=jnp.transpose(p["mla_w_uv"][l], (1, 0, 2)).astype(BF16),
        w_br_gdn=p["w_br_gdn"][l].astype(BF16), w_br_mla=p["w_br_mla"][l].astype(BF16),
        w_out=p["w_out"][l].astype(BF16),
    )
    return w


def kernel(x_prompt, x_sample, state_gdn_conv, state_gdn_ssm, cache_mla_ckv, cache_mla_krope, ffn1_norm_pre, ffn1_w_gate, ffn1_w_up, ffn1_w_down, ffn1_norm_post, mix_norm_pre, w_in, gdn_conv_w, gdn_a_log, gdn_dt_bias, gdn_norm_w, mla_kv_norm, mla_w_uk, mla_w_uv, w_br_gdn, w_br_mla, w_out, mix_norm_post, ffn2_norm_pre, ffn2_w_gate, ffn2_w_up, ffn2_w_down, ffn2_norm_post):
    params = dict(
        ffn1_norm_pre=ffn1_norm_pre, ffn1_w_gate=ffn1_w_gate, ffn1_w_up=ffn1_w_up,
        ffn1_w_down=ffn1_w_down, ffn1_norm_post=ffn1_norm_post, mix_norm_pre=mix_norm_pre,
        w_in=w_in, gdn_conv_w=gdn_conv_w, gdn_a_log=gdn_a_log, gdn_dt_bias=gdn_dt_bias,
        gdn_norm_w=gdn_norm_w, mla_kv_norm=mla_kv_norm, mla_w_uk=mla_w_uk, mla_w_uv=mla_w_uv,
        w_br_gdn=w_br_gdn, w_br_mla=w_br_mla, w_out=w_out, mix_norm_post=mix_norm_post,
        ffn2_norm_pre=ffn2_norm_pre, ffn2_w_gate=ffn2_w_gate, ffn2_w_up=ffn2_w_up,
        ffn2_w_down=ffn2_w_down, ffn2_norm_post=ffn2_norm_post)
    depth = w_in.shape[0]
    d_model = x_prompt.shape[-1]
    b_p = x_prompt.shape[0]
    past = cache_mla_ckv.shape[2]
    yp, ys = x_prompt, x_sample
    outs_p, outs_s = [], []
    for l in range(depth):
        w = _prep_weights(l, d_model, **params)
        yp, *rest_p = _layer(
            yp, 0, jnp.zeros((b_p, GDN_CONV - 1, GDN_CONV_DIM), F32),
            jnp.zeros((b_p, GDN_HEADS, GDN_DK, GDN_DV), F32), None, None, w)
        ys, *rest_s = _layer(
            ys, past, state_gdn_conv[l], state_gdn_ssm[l], cache_mla_ckv[l], cache_mla_krope[l], w)
        outs_p.append(rest_p)
        outs_s.append(rest_s)
    stack = lambda outs, i: jnp.stack([o[i] for o in outs])
    return (yp, ys,
            stack(outs_p, 0), stack(outs_p, 1), stack(outs_p, 2), stack(outs_p, 3),
            stack(outs_s, 0), stack(outs_s, 1), stack(outs_s, 2), stack(outs_s, 3))
```

```python
import functools

import jax
import jax.numpy as jnp
from jax import lax
from jax.experimental import pallas as pl
from jax.experimental.pallas import tpu as pltpu

F32 = jnp.float32
BF16 = jnp.bfloat16

CHUNK = 64
LOG2_CHUNK = 6
NORM_EPS = 1e-6
GDN_HEADS = 8
GDN_DK = 128
GDN_DV = 128
GDN_CONV = 4
GDN_KEY_DIM = GDN_HEADS * GDN_DK
GDN_VAL_DIM = GDN_HEADS * GDN_DV
GDN_CONV_DIM = 2 * GDN_KEY_DIM + GDN_VAL_DIM
MLA_HEADS = 8
MLA_NOPE = 128
MLA_ROPE = 64
MLA_V = 128
MLA_KV_RANK = 512
MLA_SCALE = (MLA_NOPE + MLA_ROPE) ** -0.5
ROPE_THETA = 10000.0

LANES = 128
SUBLANES = 8
VMEM_CAP_BYTES = 56 * 2 ** 20
MLA_QK = MLA_KV_RANK + LANES
HI = lax.Precision.HIGHEST


def _params(semantics, vmem_bytes):
    limit = int(min(max(vmem_bytes, 16 * 2 ** 20), VMEM_CAP_BYTES))
    return pltpu.CompilerParams(dimension_semantics=semantics, vmem_limit_bytes=limit)


def _dot(a, b, prec=None):
    return lax.dot_general(a, b, (((1,), (0,)), ((), ())), precision=prec,
                           preferred_element_type=F32)


def _dot_nt(a, b, prec=None):
    return lax.dot_general(a, b, (((1,), (1,)), ((), ())), precision=prec,
                           preferred_element_type=F32)


def _dot_tn(a, b, prec=None):
    return lax.dot_general(a, b, (((0,), (0,)), ((), ())), precision=prec,
                           preferred_element_type=F32)


def _rms(y, g):
    return y * lax.rsqrt(jnp.mean(y * y, axis=-1, keepdims=True) + NORM_EPS) * g


def _silu(x):
    return x * jax.nn.sigmoid(x)


def _tile(n, pref):
    t = min(n, pref)
    assert n % t == 0, (n, pref)
    return t


def _ffn_kernel(*refs, nf, norm_in, emit_next):
    refs = list(refs)
    xn_ref = None if norm_in else refs.pop(0)
    x_ref, wg_ref, wu_ref, wd_ref = refs[:4]
    del refs[:4]
    gpre_ref = refs.pop(0) if norm_in else None
    gpost_ref = refs.pop(0)
    gnext_ref = refs.pop(0) if emit_next else None
    xo_ref = refs.pop(0)
    xno_ref = refs.pop(0) if emit_next else None
    acc_ref = refs.pop(0)
    if norm_in:
        xn_ref = refs.pop(0)
    f = pl.program_id(1)

    @pl.when(f == 0)
    def _():
        acc_ref[...] = jnp.zeros(acc_ref.shape, F32)
        if norm_in:
            xn_ref[...] = _rms(x_ref[...], gpre_ref[...]).astype(BF16)

    xn = xn_ref[...]
    gate = _dot(xn, wg_ref[...])
    up = _dot(xn, wu_ref[...])
    h = (_silu(gate) * up).astype(BF16)
    acc_ref[...] += _dot(h, wd_ref[...])

    @pl.when(f == nf - 1)
    def _():
        xnew = x_ref[...] + 0.5 * _rms(acc_ref[...], gpost_ref[...])
        xo_ref[...] = xnew
        if emit_next:
            xno_ref[...] = _rms(xnew, gnext_ref[...]).astype(BF16)


def _ffn(x, wg, wu, wd, gpost, xn=None, gpre=None, gnext=None):
    norm_in = xn is None
    emit_next = gnext is not None
    assert norm_in == (gpre is not None)
    m, d = x.shape
    dff = wg.shape[1]
    tm = _tile(m, 512)
    tf = _tile(dff, 512)
    nf = dff // tf
    row = lambda i, f: (i, 0)
    vec = pl.BlockSpec((1, d), lambda i, f: (0, 0))
    ins = [] if norm_in else [xn]
    in_specs = [] if norm_in else [pl.BlockSpec((tm, d), row)]
    ins += [x, wg, wu, wd]
    in_specs += [pl.BlockSpec((tm, d), row),
                 pl.BlockSpec((d, tf), lambda i, f: (0, f)),
                 pl.BlockSpec((d, tf), lambda i, f: (0, f)),
                 pl.BlockSpec((tf, d), lambda i, f: (f, 0))]
    for g in (gpre, gpost, gnext):
        if g is not None:
            ins.append(g)
            in_specs.append(vec)
    out_specs = [pl.BlockSpec((tm, d), row)]
    out_shape = [jax.ShapeDtypeStruct((m, d), F32)]
    scratch = [pltpu.VMEM((tm, d), F32)]
    if emit_next:
        out_specs.append(pl.BlockSpec((tm, d), row))
        out_shape.append(jax.ShapeDtypeStruct((m, d), BF16))
    if norm_in:
        scratch.append(pltpu.VMEM((tm, d), BF16))
    vmem = (2 * (tm * d * 2 + tm * d * 4 + 2 * d * tf * 2 + tf * d * 2 + tm * d * 4 + tm * d * 2)
            + tm * d * 4 + 3 * tm * tf * 4 + 2 * tm * d * 4)
    res = pl.pallas_call(
        functools.partial(_ffn_kernel, nf=nf, norm_in=norm_in, emit_next=emit_next),
        grid=(m // tm, nf),
        in_specs=in_specs,
        out_specs=out_specs,
        out_shape=out_shape,
        scratch_shapes=scratch,
        compiler_params=_params(("parallel", "arbitrary"), vmem),
        name="ffn",
    )(*ins)
    return res if emit_next else (res[0], None)


def _mm_kernel(a_ref, w_ref, o_ref):
    o_ref[...] = _dot(a_ref[...], w_ref[...]).astype(o_ref.dtype)


def _matmul(a, w, out_dtype):
    m, k = a.shape
    n = w.shape[1]
    tm = _tile(m, 1024)
    tn = _tile(n, 1024)
    osz = jnp.dtype(out_dtype).itemsize
    vmem = 2 * (tm * k * 2 + k * tn * 2 + tm * tn * osz) + tm * tn * 4
    return pl.pallas_call(
        _mm_kernel,
        grid=(m // tm, n // tn),
        in_specs=[pl.BlockSpec((tm, k), lambda i, j: (i, 0)),
                  pl.BlockSpec((k, tn), lambda i, j: (0, j))],
        out_specs=pl.BlockSpec((tm, tn), lambda i, j: (i, j)),
        out_shape=jax.ShapeDtypeStruct((m, n), out_dtype),
        compiler_params=_params(("parallel", "arbitrary"), vmem),
        name="in_proj",
    )(a, w)


COL_QKV = 0
COL_Z = COL_QKV + GDN_CONV_DIM
COL_QNOPE = 0
COL_QROPE = COL_QNOPE + MLA_HEADS * MLA_NOPE
COL_CKV = COL_QROPE + MLA_HEADS * LANES
COL_KR = COL_CKV + MLA_KV_RANK
COL_B = COL_KR + LANES
COL_A = COL_B + LANES
N_MLA = 3072


def _build_w_in(w_in, d_model):
    sizes = (GDN_CONV_DIM, GDN_VAL_DIM, GDN_HEADS, GDN_HEADS, MLA_HEADS * (MLA_NOPE + MLA_ROPE),
             MLA_KV_RANK, MLA_ROPE, d_model, d_model)
    offs = [0]
    for s in sizes:
        offs.append(offs[-1] + s)
    w_gdn = w_in[:, :offs[2]].astype(BF16)
    w_gate = w_in[:, offs[7]:].astype(BF16)
    part = lambda i: w_in[:, offs[i]:offs[i + 1]].astype(BF16)
    b, a, qm, ckv, kr = (part(i) for i in range(2, 7))
    d = w_in.shape[0]
    half = MLA_ROPE // 2
    qm = qm.reshape(d, MLA_HEADS, MLA_NOPE + MLA_ROPE)
    qn = qm[:, :, :MLA_NOPE].reshape(d, MLA_HEADS * MLA_NOPE)
    qr = qm[:, :, MLA_NOPE:]
    qr_pair = jnp.concatenate([qr, qr[:, :, half:], qr[:, :, :half]], axis=2).reshape(d, MLA_HEADS * LANES)
    kr_pair = jnp.concatenate([kr, kr[:, half:], kr[:, :half]], axis=1)
    pad8 = lambda w: jnp.pad(w, ((0, 0), (0, LANES - w.shape[1])))
    w_mla = jnp.concatenate([qn, qr_pair, ckv, kr_pair, pad8(b), pad8(a)], axis=1)
    w_mla = jnp.pad(w_mla, ((0, 0), (0, N_MLA - w_mla.shape[1])))
    return w_gdn, w_gate, w_mla


_NN = (((1,), (0,)), ((), ()))
_NT = (((1,), (1,)), ((), ()))
_TN = (((0,), (0,)), ((), ()))
GDN_MODE = dict(kk="bf16", inv="bf16", sol="bf16", state="bf16")
INV_BLOCK = 16


def _mm(a, b, mode, dims=_NN):
    if mode == "f32":
        return lax.dot_general(a, b, dims, precision=HI, preferred_element_type=F32)
    ah = a.astype(BF16)
    bh = b.astype(BF16)
    out = lax.dot_general(ah, bh, dims, preferred_element_type=F32)
    if mode == "bf16x3":
        al = (a - ah.astype(F32)).astype(BF16)
        bl = (b - bh.astype(F32)).astype(BF16)
        out = (out + lax.dot_general(ah, bl, dims, preferred_element_type=F32)
               + lax.dot_general(al, bh, dims, preferred_element_type=F32))
    return out


def _gdn_kernel(qkv_ref, z_ref, b_ref, a_ref, cst_ref, cw_ref, alog_ref, dtb_ref, nw_ref, s0_ref,
                o_ref, sout_ref, xp_ref, s_ref, *, G, C, nlev, nc):
    c = pl.program_id(1)
    hist = SUBLANES - (GDN_CONV - 1)

    @pl.when(c == 0)
    def _():
        xp_ref[:, 0:SUBLANES, :] = cst_ref[...]
        s_ref[...] = s0_ref[...]

    xp_ref[:, SUBLANES:SUBLANES + C, :] = qkv_ref[...]

    ri = lax.broadcasted_iota(jnp.int32, (C, C), 0)
    ci = lax.broadcasted_iota(jnp.int32, (C, C), 1)
    incl = ri >= ci
    strict = ri > ci
    eye = (ri == ci).astype(F32)
    log2_blk = INV_BLOCK.bit_length() - 1
    bdiag = jnp.right_shift(ri, log2_blk) == jnp.right_shift(ci, log2_blk)

    def conv(s, col):
        acc = xp_ref[s, pl.ds(hist, C), col:col + LANES] * cw_ref[0:1, col:col + LANES]
        for j in range(1, GDN_CONV):
            acc = acc + xp_ref[s, pl.ds(hist + j, C), col:col + LANES] * cw_ref[j:j + 1, col:col + LANES]
        return _silu(acc)

    units = [(s, h) for s in range(G) for h in range(GDN_HEADS)]
    every = lambda fn, *lists: [fn(*args) for args in zip(*lists)]
    mm_inv = lambda a, b: _mm(a, b, GDN_MODE["inv"])
    beta_all, gcum, gcum_t = [], [], []
    for s in range(G):
        beta_all.append(jax.nn.sigmoid(b_ref[s]))
        a_all = a_ref[s] + dtb_ref[...]
        softplus = jnp.maximum(a_all, 0.0) + jnp.log(1.0 + jnp.exp(-jnp.abs(a_all)))
        g_all = -jnp.exp(alog_ref[...]) * softplus
        gcum.append(_dot(incl.astype(F32), g_all, HI))
        gcum_t.append(_dot_tn(g_all, (ci >= ri).astype(F32), HI))

    def l2n(x):
        return x * lax.rsqrt(jnp.sum(x * x, axis=-1, keepdims=True) + 1e-6)

    q = [l2n(conv(s, h * GDN_DK)) * (GDN_DK ** -0.5) for s, h in units]
    k = [l2n(conv(s, GDN_KEY_DIM + h * GDN_DK)) for s, h in units]
    v = [conv(s, 2 * GDN_KEY_DIM + h * GDN_DV) for s, h in units]
    bcol = [beta_all[s][:, h:h + 1] for s, h in units]
    gcol = [gcum[s][:, h:h + 1] for s, h in units]
    glast = [gcum[s][C - 1:C, h:h + 1] for s, h in units]
    decay = [jnp.where(incl, jnp.exp(jnp.where(incl, gcum[s][:, h:h + 1] - gcum_t[s][h:h + 1, :], 0.0)), 0.0)
             for s, h in units]
    qkk = every(lambda q_, k_: _mm(jnp.concatenate([q_, k_], axis=0), k_, GDN_MODE["kk"], _NT), q, k)
    qk = every(lambda x, d: x[:C] * d, qkk, decay)
    nmat = every(lambda x, b_, d: -jnp.where(strict, b_ * x[C:] * d, 0.0), qkk, bcol, decay)
    ndiag = every(lambda n_: jnp.where(bdiag, n_, 0.0), nmat)
    tinv = every(lambda n_: eye + n_, ndiag)
    npow = ndiag
    for _ in range(INV_BLOCK.bit_length() - 2):
        npow = every(mm_inv, npow, npow)
        tinv = every(lambda t_, p_: t_ + mm_inv(t_, p_), tinv, npow)
    mpow = every(lambda t_, n_, d_: mm_inv(t_, n_ - d_), tinv, nmat, ndiag)
    for lev in range(nlev):
        if lev:
            mpow = every(mm_inv, mpow, mpow)
        tinv = every(lambda t_, m_: t_ + mm_inv(m_, t_), tinv, mpow)
    egc = every(jnp.exp, gcol)
    sol = every(lambda t_, v_, k_, b_, e_: _mm(t_, jnp.concatenate([v_ * b_, k_ * (b_ * e_)], axis=1),
                                               GDN_MODE["sol"]), tinv, v, k, bcol, egc)
    s_old = [s_ref[s, h] for s, h in units]
    ws_qs = every(lambda x, q_, e_, s_: _mm(jnp.concatenate([x[:, GDN_DV:], q_ * e_], axis=0), s_,
                                            GDN_MODE["state"]), sol, q, egc, s_old)
    v_new = every(lambda x, y: x[:, :GDN_DV] - y[:C], sol, ws_qs)
    o = every(lambda y, a_, vn: y[C:] + _mm(a_, vn, GDN_MODE["state"]), ws_qs, qk, v_new)
    s_new = every(lambda s_, gl, k_, gc, vn: s_ * jnp.exp(gl) + _mm(k_ * jnp.exp(gl - gc), vn, GDN_MODE["state"], _TN),
                  s_old, glast, k, gcol, v_new)
    for (s, h), sn, o_ in zip(units, s_new, o):
        s_ref[s, h] = sn
        zz = z_ref[s, :, h * GDN_DV:(h + 1) * GDN_DV]
        o_ref[s, :, h * GDN_DV:(h + 1) * GDN_DV] = (_rms(o_, nw_ref[...]) * _silu(zz)).astype(BF16)

    xp_ref[:, hist:SUBLANES, :] = xp_ref[:, C + hist:C + SUBLANES, :]

    @pl.when(c == nc - 1)
    def _():
        sout_ref[...] = s_ref[...]


def _gdn(proj_gdn, proj_mla, nb, t, conv_state, ssm0, conv_w, alog, dtb, nw):
    C = min(CHUNK, t)
    G = 2 if nb % 2 == 0 else 1
    assert t % C == 0 and C & (C - 1) == 0 and C % INV_BLOCK == 0
    nc = t // C
    nlev = (C // INV_BLOCK).bit_length() - 1
    pg = proj_gdn.reshape(nb, t, proj_gdn.shape[1])
    pm = proj_mla.reshape(nb, t, proj_mla.shape[1])
    fixed = lambda b, c: (0, 0)
    cst = jnp.pad(conv_state, ((0, 0), (SUBLANES - (GDN_CONV - 1), 0), (0, 0)))
    vmem = G * (2 * (C * GDN_CONV_DIM * 4 + C * GDN_VAL_DIM * 4 + 2 * C * LANES * 4 + SUBLANES * GDN_CONV_DIM * 4
                     + C * GDN_VAL_DIM * 2 + 2 * GDN_HEADS * GDN_DK * GDN_DV * 4)
                + (SUBLANES + C) * GDN_CONV_DIM * 4 + GDN_HEADS * GDN_DK * GDN_DV * 4) + 16 * 2 ** 20
    o, s_new = pl.pallas_call(
        functools.partial(_gdn_kernel, G=G, C=C, nlev=nlev, nc=nc),
        grid=(nb // G, nc),
        in_specs=[pl.BlockSpec((G, C, GDN_CONV_DIM), lambda b, c: (b, c, COL_QKV // GDN_CONV_DIM)),
                  pl.BlockSpec((G, C, GDN_VAL_DIM), lambda b, c: (b, c, COL_Z // GDN_VAL_DIM)),
                  pl.BlockSpec((G, C, LANES), lambda b, c: (b, c, COL_B // LANES)),
                  pl.BlockSpec((G, C, LANES), lambda b, c: (b, c, COL_A // LANES)),
                  pl.BlockSpec((G, SUBLANES, GDN_CONV_DIM), lambda b, c: (b, 0, 0)),
                  pl.BlockSpec((GDN_CONV, GDN_CONV_DIM), fixed),
                  pl.BlockSpec((1, LANES), fixed),
                  pl.BlockSpec((1, LANES), fixed),
                  pl.BlockSpec((1, GDN_DV), fixed),
                  pl.BlockSpec((G, GDN_HEADS, GDN_DK, GDN_DV), lambda b, c: (b, 0, 0, 0))],
        out_specs=[pl.BlockSpec((G, C, GDN_VAL_DIM), lambda b, c: (b, c, 0)),
                   pl.BlockSpec((G, GDN_HEADS, GDN_DK, GDN_DV), lambda b, c: (b, 0, 0, 0))],
        out_shape=[jax.ShapeDtypeStruct((nb, t, GDN_VAL_DIM), BF16),
                   jax.ShapeDtypeStruct((nb, GDN_HEADS, GDN_DK, GDN_DV), F32)],
        scratch_shapes=[pltpu.VMEM((G, SUBLANES + C, GDN_CONV_DIM), F32),
                        pltpu.VMEM((G, GDN_HEADS, GDN_DK, GDN_DV), F32)],
        compiler_params=_params(("parallel", "arbitrary"), vmem),
        name="gdn",
    )(pg, pg, pm, pm, cst, conv_w, alog, dtb, nw, ssm0)
    return o.reshape(nb * t, GDN_VAL_DIM), s_new


def _mla_prep_kernel(qn_ref, qr_ref, ckv_ref, kr_ref, inv_ref, wuk_ref, kvn_ref,
                     ckv_o, kr_o, qcat_o, kcat_o, *maybe_vt_o, tm, pos0):
    j = pl.program_id(1)
    pos = (pos0 + j * tm + lax.broadcasted_iota(jnp.int32, (tm, LANES), 0)).astype(F32)
    lane = lax.broadcasted_iota(jnp.int32, (tm, LANES), 1)
    ang = pos * inv_ref[...]
    cos = jnp.cos(ang)
    sin = jnp.sin(ang)
    half = MLA_ROPE // 2
    cs = jnp.where(lane < MLA_ROPE, cos, jnp.where(lane < MLA_ROPE + half, -sin, sin))
    keep = lane < MLA_ROPE

    def rope(pair):
        prod = pair * cs
        return jnp.where(keep, prod + pltpu.roll(prod, MLA_ROPE, axis=1), 0.0)

    ckv = _rms(ckv_ref[...], kvn_ref[...])
    ckv_o[0] = ckv
    kr = rope(kr_ref[...])
    kr_o[0] = kr[:, :MLA_ROPE]
    kcat_o[0, :, :MLA_KV_RANK] = ckv.astype(BF16)
    kcat_o[0, :, MLA_KV_RANK:] = kr.astype(BF16)
    if maybe_vt_o:
        maybe_vt_o[0][0] = ckv.T.astype(BF16)
    for h in range(MLA_HEADS):
        qn = qn_ref[:, h * MLA_NOPE:(h + 1) * MLA_NOPE].astype(BF16)
        qlat = _dot(qn, wuk_ref[h]) * MLA_SCALE
        qcat_o[0, h, :, :MLA_KV_RANK] = qlat.astype(BF16)
        qr = rope(qr_ref[:, h * LANES:(h + 1) * LANES]) * MLA_SCALE
        qcat_o[0, h, :, MLA_KV_RANK:] = qr.astype(BF16)


def _mla_prep(proj, nb, t, pos0, inv128, wuk_t, kvn, emit_vt):
    tm = _tile(t, 256)
    nt = t // tm
    rows = lambda b, j: b * nt + j
    fixed2 = lambda b, j: (0, 0)
    vmem = (2 * (tm * (MLA_HEADS * MLA_NOPE + MLA_HEADS * LANES + MLA_KV_RANK + LANES) * 4
                 + MLA_HEADS * MLA_NOPE * MLA_KV_RANK * 2
                 + tm * (MLA_KV_RANK + LANES) * 4 + (MLA_HEADS + 1) * tm * MLA_QK * 2) + 8 * 2 ** 20)
    out_specs = [pl.BlockSpec((1, tm, MLA_KV_RANK), lambda b, j: (b, j, 0)),
                 pl.BlockSpec((1, tm, MLA_ROPE), lambda b, j: (b, j, 0)),
                 pl.BlockSpec((1, MLA_HEADS, tm, MLA_QK), lambda b, j: (b, 0, j, 0)),
                 pl.BlockSpec((1, tm, MLA_QK), lambda b, j: (b, j, 0))]
    out_shape = [jax.ShapeDtypeStruct((nb, t, MLA_KV_RANK), F32),
                 jax.ShapeDtypeStruct((nb, t, MLA_ROPE), F32),
                 jax.ShapeDtypeStruct((nb, MLA_HEADS, t, MLA_QK), BF16),
                 jax.ShapeDtypeStruct((nb, t, MLA_QK), BF16)]
    if emit_vt:
        out_specs.append(pl.BlockSpec((1, MLA_KV_RANK, tm), lambda b, j: (b, 0, j)))
        out_shape.append(jax.ShapeDtypeStruct((nb, MLA_KV_RANK, t), BF16))
    return pl.pallas_call(
        functools.partial(_mla_prep_kernel, tm=tm, pos0=pos0),
        grid=(nb, nt),
        in_specs=[pl.BlockSpec((tm, MLA_HEADS * MLA_NOPE), lambda b, j: (rows(b, j), COL_QNOPE // (MLA_HEADS * MLA_NOPE))),
                  pl.BlockSpec((tm, MLA_HEADS * LANES), lambda b, j: (rows(b, j), COL_QROPE // (MLA_HEADS * LANES))),
                  pl.BlockSpec((tm, MLA_KV_RANK), lambda b, j: (rows(b, j), COL_CKV // MLA_KV_RANK)),
                  pl.BlockSpec((tm, LANES), lambda b, j: (rows(b, j), COL_KR // LANES)),
                  pl.BlockSpec((1, LANES), fixed2),
                  pl.BlockSpec((MLA_HEADS, MLA_NOPE, MLA_KV_RANK), lambda b, j: (0, 0, 0)),
                  pl.BlockSpec((1, MLA_KV_RANK), fixed2)],
        out_specs=out_specs,
        out_shape=out_shape,
        compiler_params=_params(("parallel", "parallel"), vmem),
        name="mla_prep",
    )(proj, proj, proj, proj, inv128, wuk_t, kvn)


ATTN_GROUPS = 2


def _attn_kernel(q_ref, k_ref, vt_ref, wuv_ref, o_ref, s0_ref, s1_ref, m_ref, l_ref, acc_ref,
                 *, tq, tk, n_valid):
    i = pl.program_id(1)
    cols = MLA_HEADS * tq
    cg = cols // ATTN_GROUPS
    q_first = i * tq
    c = lax.broadcasted_iota(jnp.int32, (1, cg), 1)
    qchunk = jnp.right_shift(q_first + jnp.bitwise_and(c, tq - 1), LOG2_CHUNK)
    k_all = jnp.minimum((q_first // CHUNK + 1) * CHUNK, n_valid)
    k_any = jnp.minimum(((q_first + tq - 1) // CHUNK + 1) * CHUNK, n_valid)
    nk = (k_any + tk - 1) // tk
    m_ref[...] = jnp.full(m_ref.shape, -jnp.inf, F32)
    l_ref[...] = jnp.zeros(l_ref.shape, F32)
    acc_ref[...] = jnp.zeros(acc_ref.shape, F32)

    def scores(j, buf):
        start = pl.multiple_of(j * tk, tk)
        q = q_ref[0].reshape(cols, MLA_QK)
        buf[...] = _dot_nt(k_ref[0, pl.ds(start, tk), :], q)

    def fold(x, op):
        while x.shape[0] > SUBLANES:
            half = x.shape[0] // 2
            x = op(x[:half], x[half:])
        return x

    def softmax_values(j, buf, masked):
        start = pl.multiple_of(j * tk, tk)
        vt = vt_ref[0, :, pl.ds(start, tk)]
        if masked:
            kpos = start + lax.broadcasted_iota(jnp.int32, (tk, 1), 0)
            mask = jnp.logical_and(jnp.right_shift(kpos, LOG2_CHUNK) <= qchunk, kpos < n_valid)
        for g in range(ATTN_GROUPS):
            cs = slice(g * cg, (g + 1) * cg)
            s = buf[:, cs]
            if masked:
                s = jnp.where(mask, s, -jnp.inf)
            m_old = m_ref[:, cs]
            m_new = jnp.maximum(m_old, jnp.max(fold(s, jnp.maximum), axis=0, keepdims=True))
            alpha = jnp.exp(m_old - m_new)
            p = jnp.exp(s - m_new)
            l_ref[:, cs] = alpha * l_ref[:, cs] + jnp.sum(fold(p, jnp.add), axis=0, keepdims=True)
            acc_ref[:, cs] = alpha * acc_ref[:, cs] + _dot(vt, p.astype(BF16))
            m_ref[:, cs] = m_new

    def by_parity(j, fn):
        @pl.when(jnp.bitwise_and(j, 1) == 0)
        def _():
            fn(s0_ref, s1_ref)

        @pl.when(jnp.bitwise_and(j, 1) == 1)
        def _():
            fn(s1_ref, s0_ref)

    scores(0, s0_ref)

    def body(j, carry, masked):
        def step(cur, nxt):
            scores(j + 1, nxt)
            softmax_values(j, cur, masked)
        by_parity(j, step)
        return carry

    n_open = jnp.minimum(k_all // tk, nk - 1)
    lax.fori_loop(0, n_open, functools.partial(body, masked=False), 0)
    lax.fori_loop(n_open, nk - 1, functools.partial(body, masked=True), 0)
    by_parity(nk - 1, lambda cur, nxt: softmax_values(nk - 1, cur, True))
    inv_l = 1.0 / l_ref[...]
    for h in range(MLA_HEADS):
        hs = slice(h * tq, (h + 1) * tq)
        lat_t = (acc_ref[:, hs] * inv_l[:, hs]).astype(BF16)
        o_ref[:, h * MLA_V:(h + 1) * MLA_V] = _dot_tn(lat_t, wuv_ref[h]).astype(BF16)


def _attn_cache_kernel(q_ref, ckv_ref, kr_ref, knew_ref, wuv_ref, o_ref, m_ref, l_ref, acc_ref,
                       *, tq, tk, nkb, past):
    j = pl.program_id(1)
    rows = MLA_HEADS * tq
    q = q_ref[0].reshape(rows, MLA_QK)
    r = lax.broadcasted_iota(jnp.int32, (rows, 1), 0)
    qchunk = jnp.right_shift(past + jnp.bitwise_and(r, tq - 1), LOG2_CHUNK)

    @pl.when(j == 0)
    def _():
        m_ref[...] = jnp.full(m_ref.shape, -jnp.inf, F32)
        l_ref[...] = jnp.zeros(l_ref.shape, F32)
        acc_ref[...] = jnp.zeros(acc_ref.shape, F32)

    def update(s, kpos, vals):
        s = jnp.where(jnp.right_shift(kpos, LOG2_CHUNK) <= qchunk, s, -jnp.inf)
        m_old = m_ref[...]
        m_new = jnp.maximum(m_old, jnp.max(s, axis=-1, keepdims=True))
        alpha = jnp.exp(m_old - m_new)
        p = jnp.exp(s - m_new)
        l_ref[...] = alpha * l_ref[...] + jnp.sum(p, axis=-1, keepdims=True)
        acc_ref[...] = alpha * acc_ref[...] + _dot(p.astype(BF16), vals)
        m_ref[...] = m_new

    ck = ckv_ref[0].astype(BF16)
    s = (_dot_nt(q[:, :MLA_KV_RANK], ck)
         + _dot_nt(q[:, MLA_KV_RANK:MLA_KV_RANK + MLA_ROPE], kr_ref[0].astype(BF16)))
    update(s, j * tk + lax.broadcasted_iota(jnp.int32, (1, tk), 1), ck)

    @pl.when(j == nkb - 1)
    def _():
        kn = knew_ref[0]
        update(_dot_nt(q, kn), past + lax.broadcasted_iota(jnp.int32, (1, tq), 1), kn[:, :MLA_KV_RANK])
        inv_l = 1.0 / l_ref[...]
        for h in range(MLA_HEADS):
            hs = slice(h * tq, (h + 1) * tq)
            lat = (acc_ref[hs, :] * inv_l[hs, :]).astype(BF16)
            o_ref[:, h * MLA_V:(h + 1) * MLA_V] = _dot(lat, wuv_ref[h]).astype(BF16)


def _attention(qcat, kcat, vt, wuv_t):
    nb, _, t, _ = qcat.shape
    tq = _tile(t, 128)
    tk = _tile(t, 512)
    assert tq & (tq - 1) == 0
    nq = t // tq
    cols = MLA_HEADS * tq
    vmem = (2 * (cols * MLA_QK * 2 + t * MLA_QK * 2 + MLA_KV_RANK * t * 2
                 + MLA_HEADS * MLA_KV_RANK * MLA_V * 2 + tq * MLA_HEADS * MLA_V * 2)
            + 2 * tk * cols * 4 + 2 * SUBLANES * cols * 4 + MLA_KV_RANK * cols * 4
            + 3 * tk * cols * 4 // ATTN_GROUPS + 4 * 2 ** 20)
    return pl.pallas_call(
        functools.partial(_attn_kernel, tq=tq, tk=tk, n_valid=t),
        grid=(nb, nq),
        in_specs=[pl.BlockSpec((1, MLA_HEADS, tq, MLA_QK), lambda b, i: (b, 0, i, 0)),
                  pl.BlockSpec((1, t, MLA_QK), lambda b, i: (b, 0, 0)),
                  pl.BlockSpec((1, MLA_KV_RANK, t), lambda b, i: (b, 0, 0)),
                  pl.BlockSpec((MLA_HEADS, MLA_KV_RANK, MLA_V), lambda b, i: (0, 0, 0))],
        out_specs=pl.BlockSpec((tq, MLA_HEADS * MLA_V), lambda b, i: (b * nq + i, 0)),
        out_shape=jax.ShapeDtypeStruct((nb * t, MLA_HEADS * MLA_V), BF16),
        scratch_shapes=[pltpu.VMEM((tk, cols), F32), pltpu.VMEM((tk, cols), F32), pltpu.VMEM((1, cols), F32),
                        pltpu.VMEM((1, cols), F32), pltpu.VMEM((MLA_KV_RANK, cols), F32)],
        compiler_params=_params(("parallel", "arbitrary"), vmem),
        name="mla_attention",
    )(qcat, kcat, vt, wuv_t)


def _attention_cached(qcat, ckv_past, krope_past, kcat_new, wuv_t):
    nb, _, t, _ = qcat.shape
    past = ckv_past.shape[1]
    tk = _tile(past, 1024)
    nkb = past // tk
    assert t & (t - 1) == 0
    rows = MLA_HEADS * t
    vmem = (2 * (rows * MLA_QK * 2 + tk * MLA_KV_RANK * 4 + tk * LANES * 4 + t * MLA_QK * 2
                 + MLA_HEADS * MLA_KV_RANK * MLA_V * 2 + t * MLA_HEADS * MLA_V * 2)
            + 2 * rows * LANES * 4 + rows * MLA_KV_RANK * 4 + tk * MLA_QK * 2 + 4 * rows * tk * 4 + 4 * 2 ** 20)
    return pl.pallas_call(
        functools.partial(_attn_cache_kernel, tq=t, tk=tk, nkb=nkb, past=past),
        grid=(nb, nkb),
        in_specs=[pl.BlockSpec((1, MLA_HEADS, t, MLA_QK), lambda b, j: (b, 0, 0, 0)),
                  pl.BlockSpec((1, tk, MLA_KV_RANK), lambda b, j: (b, j, 0)),
                  pl.BlockSpec((1, tk, MLA_ROPE), lambda b, j: (b, j, 0)),
                  pl.BlockSpec((1, t, MLA_QK), lambda b, j: (b, 0, 0)),
                  pl.BlockSpec((MLA_HEADS, MLA_KV_RANK, MLA_V), lambda b, j: (0, 0, 0))],
        out_specs=pl.BlockSpec((t, MLA_HEADS * MLA_V), lambda b, j: (b, 0)),
        out_shape=jax.ShapeDtypeStruct((nb * t, MLA_HEADS * MLA_V), BF16),
        scratch_shapes=[pltpu.VMEM((rows, 1), F32), pltpu.VMEM((rows, 1), F32),
                        pltpu.VMEM((rows, MLA_KV_RANK), F32)],
        compiler_params=_params(("parallel", "arbitrary"), vmem),
        name="mla_attention_cached",
    )(qcat, ckv_past, krope_past, kcat_new, wuv_t)


def _merge_out_kernel(og_ref, om_ref, wg_ref, wm_ref, gg_ref, gm_ref, wo_ref, x_ref, gpost_ref, gnext_ref,
                      xo_ref, xno_ref):
    tg = _dot(og_ref[...], wg_ref[...])
    tm_ = _dot(om_ref[...], wm_ref[...])
    merged = (jax.nn.sigmoid(gg_ref[...].astype(F32)) * tg
              + jax.nn.sigmoid(gm_ref[...].astype(F32)) * tm_).astype(BF16)
    y = _dot(merged, wo_ref[...])
    xnew = x_ref[...] + _rms(y, gpost_ref[...])
    xo_ref[...] = xnew
    xno_ref[...] = _rms(xnew, gnext_ref[...]).astype(BF16)


def _merge_out(og, om, wbg, wbm, proj_gate, wout, x, gpost, gnext):
    m, kg = og.shape
    d = wbg.shape[1]
    tm = _tile(m, 256)
    gsz = proj_gate.dtype.itemsize
    row = lambda i: (i, 0)
    fixed = lambda i: (0, 0)
    resident = dict(pipeline_mode=pl.Buffered(1))
    vmem = ((2 * kg * d + d * d) * 2
            + 2 * (2 * tm * kg * 2 + 2 * tm * d * gsz + 2 * tm * d * 4 + tm * d * 2)
            + 4 * tm * d * 4 + 2 * 2 ** 20)
    return pl.pallas_call(
        _merge_out_kernel,
        grid=(m // tm,),
        in_specs=[pl.BlockSpec((tm, kg), row),
                  pl.BlockSpec((tm, kg), row),
                  pl.BlockSpec((kg, d), fixed, **resident),
                  pl.BlockSpec((kg, d), fixed, **resident),
                  pl.BlockSpec((tm, d), lambda i: (i, 0)),
                  pl.BlockSpec((tm, d), lambda i: (i, 1)),
                  pl.BlockSpec((d, d), fixed, **resident),
                  pl.BlockSpec((tm, d), row),
                  pl.BlockSpec((1, d), fixed),
                  pl.BlockSpec((1, d), fixed)],
        out_specs=[pl.BlockSpec((tm, d), row), pl.BlockSpec((tm, d), row)],
        out_shape=[jax.ShapeDtypeStruct((m, d), F32), jax.ShapeDtypeStruct((m, d), BF16)],
        compiler_params=_params(("parallel",), vmem),
        name="merge_out",
    )(og, om, wbg, wbm, proj_gate, proj_gate, wout, x, gpost, gnext)


def _layer(x3, pos0, conv_state, ssm0, ckv_past, krope_past, w):
    nb, t, d = x3.shape
    m = nb * t
    x = x3.reshape(m, d)
    x1, hm = _ffn(x, w["ffn1_wg"], w["ffn1_wu"], w["ffn1_wd"], w["ffn1_norm_post"],
                  gpre=w["ffn1_norm_pre"], gnext=w["mix_norm_pre"])
    proj_gdn = _matmul(hm, w["w_in_gdn"], F32)
    proj_gate = _matmul(hm, w["w_in_gate"], BF16)
    proj_mla = _matmul(hm, w["w_in_mla"], F32)

    o_gdn, ssm_new = _gdn(proj_gdn, proj_mla, nb, t, conv_state, ssm0, w["conv_w"], w["alog"], w["dtb"],
                          w["gdn_nw"])
    conv_new = proj_gdn.reshape(nb, t, -1)[:, t - (GDN_CONV - 1):, COL_QKV:COL_QKV + GDN_CONV_DIM]

    ckv, krope, qcat, kcat, *vt = _mla_prep(proj_mla, nb, t, pos0, w["inv128"], w["wuk_t"],
                                            w["kv_norm"], ckv_past is None)
    if ckv_past is None:
        o_mla = _attention(qcat, kcat, vt[0], w["wuv_t"])
    else:
        o_mla = _attention_cached(qcat, ckv_past, krope_past, kcat, w["wuv_t"])

    x2, xn2 = _merge_out(o_gdn, o_mla, w["w_br_gdn"], w["w_br_mla"], proj_gate, w["w_out"], x1,
                         w["mix_norm_post"], w["ffn2_norm_pre"])
    x3_, _ = _ffn(x2, w["ffn2_wg"], w["ffn2_wu"], w["ffn2_wd"], w["ffn2_norm_post"], xn=xn2)
    return x3_.reshape(nb, t, d), conv_new, ssm_new, ckv, krope


def _prep_weights(l, d_model, **p):
    row = lambda v: v[l].reshape(1, -1).astype(F32)
    pad_lanes = lambda v: jnp.pad(v[l].reshape(1, -1).astype(F32), ((0, 0), (0, LANES - v.shape[1])))
    inv = ROPE_THETA ** (-jnp.arange(0, MLA_ROPE, 2, dtype=F32) / MLA_ROPE)
    w_in_gdn, w_in_gate, w_in_mla = _build_w_in(p["w_in"][l], d_model)
    w = dict(
        w_in_gdn=w_in_gdn, w_in_gate=w_in_gate, w_in_mla=w_in_mla,
        ffn1_norm_pre=row(p["ffn1_norm_pre"]), ffn1_norm_post=row(p["ffn1_norm_post"]),
        mix_norm_pre=row(p["mix_norm_pre"]), mix_norm_post=row(p["mix_norm_post"]),
        ffn2_norm_pre=row(p["ffn2_norm_pre"]), ffn2_norm_post=row(p["ffn2_norm_post"]),
        ffn1_wg=p["ffn1_w_gate"][l].astype(BF16), ffn1_wu=p["ffn1_w_up"][l].astype(BF16),
        ffn1_wd=p["ffn1_w_down"][l].astype(BF16),
        ffn2_wg=p["ffn2_w_gate"][l].astype(BF16), ffn2_wu=p["ffn2_w_up"][l].astype(BF16),
        ffn2_wd=p["ffn2_w_down"][l].astype(BF16),
        conv_w=p["gdn_conv_w"][l].astype(F32),
        alog=pad_lanes(p["gdn_a_log"]), dtb=pad_lanes(p["gdn_dt_bias"]),
        gdn_nw=row(p["gdn_norm_w"]), kv_norm=row(p["mla_kv_norm"]),
        inv128=jnp.tile(inv, LANES // inv.shape[0]).reshape(1, LANES),
        wuk_t=jnp.transpose(p["mla_w_uk"][l], (1, 2, 0)).astype(BF16),
        wuv_t=jnp.transpose(p["mla_w_uv"][l], (1, 0, 2)).astype(BF16),
        w_br_gdn=p["w_br_gdn"][l].astype(BF16), w_br_mla=p["w_br_mla"][l].astype(BF16),
        w_out=p["w_out"][l].astype(BF16),
    )
    return w


def kernel(x_prompt, x_sample, state_gdn_conv, state_gdn_ssm, cache_mla_ckv, cache_mla_krope, ffn1_norm_pre, ffn1_w_gate, ffn1_w_up, ffn1_w_down, ffn1_norm_post, mix_norm_pre, w_in, gdn_conv_w, gdn_a_log, gdn_dt_bias, gdn_norm_w, mla_kv_norm, mla_w_uk, mla_w_uv, w_br_gdn, w_br_mla, w_out, mix_norm_post, ffn2_norm_pre, ffn2_w_gate, ffn2_w_up, ffn2_w_down, ffn2_norm_post):
    params = dict(
        ffn1_norm_pre=ffn1_norm_pre, ffn1_w_gate=ffn1_w_gate, ffn1_w_up=ffn1_w_up,
        ffn1_w_down=ffn1_w_down, ffn1_norm_post=ffn1_norm_post, mix_norm_pre=mix_norm_pre,
        w_in=w_in, gdn_conv_w=gdn_conv_w, gdn_a_log=gdn_a_log, gdn_dt_bias=gdn_dt_bias,
        gdn_norm_w=gdn_norm_w, mla_kv_norm=mla_kv_norm, mla_w_uk=mla_w_uk, mla_w_uv=mla_w_uv,
        w_br_gdn=w_br_gdn, w_br_mla=w_br_mla, w_out=w_out, mix_norm_post=mix_norm_post,
        ffn2_norm_pre=ffn2_norm_pre, ffn2_w_gate=ffn2_w_gate, ffn2_w_up=ffn2_w_up,
        ffn2_w_down=ffn2_w_down, ffn2_norm_post=ffn2_norm_post)
    depth = w_in.shape[0]
    d_model = x_prompt.shape[-1]
    b_p = x_prompt.shape[0]
    past = cache_mla_ckv.shape[2]
    yp, ys = x_prompt, x_sample
    outs_p, outs_s = [], []
    for l in range(depth):
        w = _prep_weights(l, d_model, **params)
        yp, *rest_p = _layer(
            yp, 0, jnp.zeros((b_p, GDN_CONV - 1, GDN_CONV_DIM), F32),
            jnp.zeros((b_p, GDN_HEADS, GDN_DK, GDN_DV), F32), None, None, w)
        ys, *rest_s = _layer(
            ys, past, state_gdn_conv[l], state_gdn_ssm[l], cache_mla_ckv[l], cache_mla_krope[l], w)
        outs_p.append(rest_p)
        outs_s.append(rest_s)
    stack = lambda outs, i: jnp.stack([o[i] for o in outs])
    return (yp, ys,
            stack(outs_p, 0), stack(outs_p, 1), stack(outs_p, 2), stack(outs_p, 3),
            stack(outs_s, 0), stack(outs_s, 1), stack(outs_s, 2), stack(outs_s, 3))
```

```python
import functools

import jax
import jax.numpy as jnp
from jax import lax
from jax.experimental import pallas as pl
from jax.experimental.pallas import tpu as pltpu

F32 = jnp.float32
BF16 = jnp.bfloat16

CHUNK = 64
LOG2_CHUNK = 6
NORM_EPS = 1e-6
GDN_HEADS = 8
GDN_DK = 128
GDN_DV = 128
GDN_CONV = 4
GDN_KEY_DIM = GDN_HEADS * GDN_DK
GDN_VAL_DIM = GDN_HEADS * GDN_DV
GDN_CONV_DIM = 2 * GDN_KEY_DIM + GDN_VAL_DIM
MLA_HEADS = 8
MLA_NOPE = 128
MLA_ROPE = 64
MLA_V = 128
MLA_KV_RANK = 512
MLA_SCALE = (MLA_NOPE + MLA_ROPE) ** -0.5
ROPE_THETA = 10000.0

LANES = 128
SUBLANES = 8
VMEM_CAP_BYTES = 56 * 2 ** 20
MLA_QK = MLA_KV_RANK + LANES
HI = lax.Precision.HIGHEST


def _params(semantics, vmem_bytes):
    limit = int(min(max(vmem_bytes, 16 * 2 ** 20), VMEM_CAP_BYTES))
    return pltpu.CompilerParams(dimension_semantics=semantics, vmem_limit_bytes=limit)


def _dot(a, b, prec=None):
    return lax.dot_general(a, b, (((1,), (0,)), ((), ())), precision=prec,
                           preferred_element_type=F32)


def _dot_nt(a, b, prec=None):
    return lax.dot_general(a, b, (((1,), (1,)), ((), ())), precision=prec,
                           preferred_element_type=F32)


def _dot_tn(a, b, prec=None):
    return lax.dot_general(a, b, (((0,), (0,)), ((), ())), precision=prec,
                           preferred_element_type=F32)


def _rms(y, g):
    return y * lax.rsqrt(jnp.mean(y * y, axis=-1, keepdims=True) + NORM_EPS) * g


def _silu(x):
    return x * jax.nn.sigmoid(x)


def _tile(n, pref):
    t = min(n, pref)
    assert n % t == 0, (n, pref)
    return t


def _ffn_kernel(*refs, nf, norm_in, emit_next, cast_w):
    refs = list(refs)
    xn_ref = None if norm_in else refs.pop(0)
    x_ref, wg_ref, wu_ref, wd_ref = refs[:4]
    del refs[:4]
    gpre_ref = refs.pop(0) if norm_in else None
    gpost_ref = refs.pop(0)
    gnext_ref = refs.pop(0) if emit_next else None
    xo_ref = refs.pop(0)
    xno_ref = refs.pop(0) if emit_next else None
    if cast_w:
        wgo_ref, wuo_ref, wdo_ref = refs[:3]
        del refs[:3]
    acc_ref = refs.pop(0)
    if norm_in:
        xn_ref = refs.pop(0)
    f = pl.program_id(1)

    @pl.when(f == 0)
    def _():
        acc_ref[...] = jnp.zeros(acc_ref.shape, F32)
        if norm_in:
            xn_ref[...] = _rms(x_ref[...], gpre_ref[...]).astype(BF16)

    xn = xn_ref[...]
    wg, wu, wd = wg_ref[...], wu_ref[...], wd_ref[...]
    if cast_w:
        wg, wu, wd = wg.astype(BF16), wu.astype(BF16), wd.astype(BF16)
        wgo_ref[...] = wg
        wuo_ref[...] = wu
        wdo_ref[...] = wd
    gate = _dot(xn, wg)
    up = _dot(xn, wu)
    h = (_silu(gate) * up).astype(BF16)
    acc_ref[...] += _dot(h, wd)

    @pl.when(f == nf - 1)
    def _():
        xnew = x_ref[...] + 0.5 * _rms(acc_ref[...], gpost_ref[...])
        xo_ref[...] = xnew
        if emit_next:
            xno_ref[...] = _rms(xnew, gnext_ref[...]).astype(BF16)


def _ffn(x, wg, wu, wd, gpost, xn=None, gpre=None, gnext=None):
    norm_in = xn is None
    emit_next = gnext is not None
    cast_w = wg.dtype == F32
    assert norm_in == (gpre is not None)
    m, d = x.shape
    dff = wg.shape[1]
    tm = _tile(m, 512)
    tf = _tile(dff, 256 if cast_w else 512)
    nf = dff // tf
    assert not cast_w or m == tm
    wsz = wg.dtype.itemsize
    row = lambda i, f: (i, 0)
    vec = pl.BlockSpec((1, d), lambda i, f: (0, 0))
    w_specs = [pl.BlockSpec((d, tf), lambda i, f: (0, f)),
               pl.BlockSpec((d, tf), lambda i, f: (0, f)),
               pl.BlockSpec((tf, d), lambda i, f: (f, 0))]
    ins = [] if norm_in else [xn]
    in_specs = [] if norm_in else [pl.BlockSpec((tm, d), row)]
    ins += [x, wg, wu, wd]
    in_specs += [pl.BlockSpec((tm, d), row)] + w_specs
    for g in (gpre, gpost, gnext):
        if g is not None:
            ins.append(g)
            in_specs.append(vec)
    out_specs = [pl.BlockSpec((tm, d), row)]
    out_shape = [jax.ShapeDtypeStruct((m, d), F32)]
    scratch = [pltpu.VMEM((tm, d), F32)]
    if emit_next:
        out_specs.append(pl.BlockSpec((tm, d), row))
        out_shape.append(jax.ShapeDtypeStruct((m, d), BF16))
    if cast_w:
        out_specs += w_specs
        out_shape += [jax.ShapeDtypeStruct(w.shape, BF16) for w in (wg, wu, wd)]
    if norm_in:
        scratch.append(pltpu.VMEM((tm, d), BF16))
    vmem = (2 * (tm * d * 2 + tm * d * 4 + 3 * d * tf * wsz + tm * d * 4 + tm * d * 2)
            + (2 * 3 * d * tf * 2 + 3 * d * tf * 2 if cast_w else 0)
            + tm * d * 4 + 3 * tm * tf * 4 + 2 * tm * d * 4)
    res = pl.pallas_call(
        functools.partial(_ffn_kernel, nf=nf, norm_in=norm_in, emit_next=emit_next, cast_w=cast_w),
        grid=(m // tm, nf),
        in_specs=in_specs,
        out_specs=out_specs,
        out_shape=out_shape,
        scratch_shapes=scratch,
        compiler_params=_params(("parallel", "arbitrary"), vmem),
        name="ffn_cast" if cast_w else "ffn",
    )(*ins)
    res = list(res)
    x_new = res.pop(0)
    xn_next = res.pop(0) if emit_next else None
    return x_new, xn_next, (tuple(res) if cast_w else (wg, wu, wd))


def _mm_kernel(a_ref, w_ref, o_ref):
    o_ref[...] = _dot(a_ref[...], w_ref[...]).astype(o_ref.dtype)


def _matmul(a, w, out_dtype):
    m, k = a.shape
    n = w.shape[1]
    tm = _tile(m, 1024)
    tn = _tile(n, 1024)
    osz = jnp.dtype(out_dtype).itemsize
    vmem = 2 * (tm * k * 2 + k * tn * 2 + tm * tn * osz) + tm * tn * 4
    return pl.pallas_call(
        _mm_kernel,
        grid=(m // tm, n // tn),
        in_specs=[pl.BlockSpec((tm, k), lambda i, j: (i, 0)),
                  pl.BlockSpec((k, tn), lambda i, j: (0, j))],
        out_specs=pl.BlockSpec((tm, tn), lambda i, j: (i, j)),
        out_shape=jax.ShapeDtypeStruct((m, n), out_dtype),
        compiler_params=_params(("parallel", "arbitrary"), vmem),
        name="in_proj",
    )(a, w)


COL_QKV = 0
COL_Z = COL_QKV + GDN_CONV_DIM
COL_QNOPE = 0
COL_QROPE = COL_QNOPE + MLA_HEADS * MLA_NOPE
COL_CKV = COL_QROPE + MLA_HEADS * LANES
COL_KR = COL_CKV + MLA_KV_RANK
COL_B = COL_KR + LANES
COL_A = COL_B + LANES
N_MLA = 3072


def _build_w_in(w_in, d_model):
    sizes = (GDN_CONV_DIM, GDN_VAL_DIM, GDN_HEADS, GDN_HEADS, MLA_HEADS * (MLA_NOPE + MLA_ROPE),
             MLA_KV_RANK, MLA_ROPE, d_model, d_model)
    offs = [0]
    for s in sizes:
        offs.append(offs[-1] + s)
    w_gdn = w_in[:, :offs[2]].astype(BF16)
    w_gate = w_in[:, offs[7]:].astype(BF16)
    part = lambda i: w_in[:, offs[i]:offs[i + 1]].astype(BF16)
    b, a, qm, ckv, kr = (part(i) for i in range(2, 7))
    d = w_in.shape[0]
    half = MLA_ROPE // 2
    qm = qm.reshape(d, MLA_HEADS, MLA_NOPE + MLA_ROPE)
    qn = qm[:, :, :MLA_NOPE].reshape(d, MLA_HEADS * MLA_NOPE)
    qr = qm[:, :, MLA_NOPE:]
    qr_pair = jnp.concatenate([qr, qr[:, :, half:], qr[:, :, :half]], axis=2).reshape(d, MLA_HEADS * LANES)
    kr_pair = jnp.concatenate([kr, kr[:, half:], kr[:, :half]], axis=1)
    pad8 = lambda w: jnp.pad(w, ((0, 0), (0, LANES - w.shape[1])))
    w_mla = jnp.concatenate([qn, qr_pair, ckv, kr_pair, pad8(b), pad8(a)], axis=1)
    w_mla = jnp.pad(w_mla, ((0, 0), (0, N_MLA - w_mla.shape[1])))
    return w_gdn, w_gate, w_mla


_NN = (((1,), (0,)), ((), ()))
_NT = (((1,), (1,)), ((), ()))
_TN = (((0,), (0,)), ((), ()))
GDN_MODE = dict(kk="bf16", inv="bf16", sol="bf16", state="bf16")
INV_BLOCK = 16


def _mm(a, b, mode, dims=_NN):
    if mode == "f32":
        return lax.dot_general(a, b, dims, precision=HI, preferred_element_type=F32)
    ah = a.astype(BF16)
    bh = b.astype(BF16)
    out = lax.dot_general(ah, bh, dims, preferred_element_type=F32)
    if mode == "bf16x3":
        al = (a - ah.astype(F32)).astype(BF16)
        bl = (b - bh.astype(F32)).astype(BF16)
        out = (out + lax.dot_general(ah, bl, dims, preferred_element_type=F32)
               + lax.dot_general(al, bh, dims, preferred_element_type=F32))
    return out


def _gdn_kernel(qkv_ref, z_ref, b_ref, a_ref, cst_ref, cw_ref, alog_ref, dtb_ref, nw_ref, s0_ref,
                o_ref, sout_ref, xp_ref, s_ref, *, G, C, nlev, nc):
    c = pl.program_id(1)
    hist = SUBLANES - (GDN_CONV - 1)

    @pl.when(c == 0)
    def _():
        xp_ref[:, 0:SUBLANES, :] = cst_ref[...]
        s_ref[...] = s0_ref[...]

    xp_ref[:, SUBLANES:SUBLANES + C, :] = qkv_ref[...]

    ri = lax.broadcasted_iota(jnp.int32, (C, C), 0)
    ci = lax.broadcasted_iota(jnp.int32, (C, C), 1)
    incl = ri >= ci
    strict = ri > ci
    eye = (ri == ci).astype(F32)
    log2_blk = INV_BLOCK.bit_length() - 1
    bdiag = jnp.right_shift(ri, log2_blk) == jnp.right_shift(ci, log2_blk)

    def conv(s, col):
        acc = xp_ref[s, pl.ds(hist, C), col:col + LANES] * cw_ref[0:1, col:col + LANES]
        for j in range(1, GDN_CONV):
            acc = acc + xp_ref[s, pl.ds(hist + j, C), col:col + LANES] * cw_ref[j:j + 1, col:col + LANES]
        return _silu(acc)

    units = [(s, h) for s in range(G) for h in range(GDN_HEADS)]
    every = lambda fn, *lists: [fn(*args) for args in zip(*lists)]
    mm_inv = lambda a, b: _mm(a, b, GDN_MODE["inv"])
    beta_all, gcum, gcum_t = [], [], []
    for s in range(G):
        beta_all.append(jax.nn.sigmoid(b_ref[s]))
        a_all = a_ref[s] + dtb_ref[...]
        softplus = jnp.maximum(a_all, 0.0) + jnp.log(1.0 + jnp.exp(-jnp.abs(a_all)))
        g_all = -jnp.exp(alog_ref[...]) * softplus
        gcum.append(_dot(incl.astype(F32), g_all, HI))
        gcum_t.append(_dot_tn(g_all, (ci >= ri).astype(F32), HI))

    def l2n(x):
        return x * lax.rsqrt(jnp.sum(x * x, axis=-1, keepdims=True) + 1e-6)

    q = [l2n(conv(s, h * GDN_DK)) * (GDN_DK ** -0.5) for s, h in units]
    k = [l2n(conv(s, GDN_KEY_DIM + h * GDN_DK)) for s, h in units]
    v = [conv(s, 2 * GDN_KEY_DIM + h * GDN_DV) for s, h in units]
    bcol = [beta_all[s][:, h:h + 1] for s, h in units]
    gcol = [gcum[s][:, h:h + 1] for s, h in units]
    glast = [gcum[s][C - 1:C, h:h + 1] for s, h in units]
    decay = [jnp.where(incl, jnp.exp(jnp.where(incl, gcum[s][:, h:h + 1] - gcum_t[s][h:h + 1, :], 0.0)), 0.0)
             for s, h in units]
    qkk = every(lambda q_, k_: _mm(jnp.concatenate([q_, k_], axis=0), k_, GDN_MODE["kk"], _NT), q, k)
    qk = every(lambda x, d: x[:C] * d, qkk, decay)
    nmat = every(lambda x, b_, d: -jnp.where(strict, b_ * x[C:] * d, 0.0), qkk, bcol, decay)
    ndiag = every(lambda n_: jnp.where(bdiag, n_, 0.0), nmat)
    tinv = every(lambda n_: eye + n_, ndiag)
    npow = ndiag
    for _ in range(INV_BLOCK.bit_length() - 2):
        npow = every(mm_inv, npow, npow)
        tinv = every(lambda t_, p_: t_ + mm_inv(t_, p_), tinv, npow)
    mpow = every(lambda t_, n_, d_: mm_inv(t_, n_ - d_), tinv, nmat, ndiag)
    for lev in range(nlev):
        if lev:
            mpow = every(mm_inv, mpow, mpow)
        tinv = every(lambda t_, m_: t_ + mm_inv(m_, t_), tinv, mpow)
    egc = every(jnp.exp, gcol)
    sol = every(lambda t_, v_, k_, b_, e_: _mm(t_, jnp.concatenate([v_ * b_, k_ * (b_ * e_)], axis=1),
                                               GDN_MODE["sol"]), tinv, v, k, bcol, egc)
    s_old = [s_ref[s, h] for s, h in units]
    ws_qs = every(lambda x, q_, e_, s_: _mm(jnp.concatenate([x[:, GDN_DV:], q_ * e_], axis=0), s_,
                                            GDN_MODE["state"]), sol, q, egc, s_old)
    v_new = every(lambda x, y: x[:, :GDN_DV] - y[:C], sol, ws_qs)
    o = every(lambda y, a_, vn: y[C:] + _mm(a_, vn, GDN_MODE["state"]), ws_qs, qk, v_new)
    s_new = every(lambda s_, gl, k_, gc, vn: s_ * jnp.exp(gl) + _mm(k_ * jnp.exp(gl - gc), vn, GDN_MODE["state"], _TN),
                  s_old, glast, k, gcol, v_new)
    for (s, h), sn, o_ in zip(units, s_new, o):
        s_ref[s, h] = sn
        zz = z_ref[s, :, h * GDN_DV:(h + 1) * GDN_DV]
        o_ref[s, :, h * GDN_DV:(h + 1) * GDN_DV] = (_rms(o_, nw_ref[...]) * _silu(zz)).astype(BF16)

    xp_ref[:, hist:SUBLANES, :] = xp_ref[:, C + hist:C + SUBLANES, :]

    @pl.when(c == nc - 1)
    def _():
        sout_ref[...] = s_ref[...]


def _gdn(proj_gdn, proj_mla, nb, t, conv_state, ssm0, conv_w, alog, dtb, nw):
    C = min(CHUNK, t)
    G = 2 if nb % 2 == 0 else 1
    assert t % C == 0 and C & (C - 1) == 0 and C % INV_BLOCK == 0
    nc = t // C
    nlev = (C // INV_BLOCK).bit_length() - 1
    pg = proj_gdn.reshape(nb, t, proj_gdn.shape[1])
    pm = proj_mla.reshape(nb, t, proj_mla.shape[1])
    fixed = lambda b, c: (0, 0)
    cst = jnp.pad(conv_state, ((0, 0), (SUBLANES - (GDN_CONV - 1), 0), (0, 0)))
    vmem = G * (2 * (C * GDN_CONV_DIM * 4 + C * GDN_VAL_DIM * 4 + 2 * C * LANES * 4 + SUBLANES * GDN_CONV_DIM * 4
                     + C * GDN_VAL_DIM * 2 + 2 * GDN_HEADS * GDN_DK * GDN_DV * 4)
                + (SUBLANES + C) * GDN_CONV_DIM * 4 + GDN_HEADS * GDN_DK * GDN_DV * 4) + 16 * 2 ** 20
    o, s_new = pl.pallas_call(
        functools.partial(_gdn_kernel, G=G, C=C, nlev=nlev, nc=nc),
        grid=(nb // G, nc),
        in_specs=[pl.BlockSpec((G, C, GDN_CONV_DIM), lambda b, c: (b, c, COL_QKV // GDN_CONV_DIM)),
                  pl.BlockSpec((G, C, GDN_VAL_DIM), lambda b, c: (b, c, COL_Z // GDN_VAL_DIM)),
                  pl.BlockSpec((G, C, LANES), lambda b, c: (b, c, COL_B // LANES)),
                  pl.BlockSpec((G, C, LANES), lambda b, c: (b, c, COL_A // LANES)),
                  pl.BlockSpec((G, SUBLANES, GDN_CONV_DIM), lambda b, c: (b, 0, 0)),
                  pl.BlockSpec((GDN_CONV, GDN_CONV_DIM), fixed),
                  pl.BlockSpec((1, LANES), fixed),
                  pl.BlockSpec((1, LANES), fixed),
                  pl.BlockSpec((1, GDN_DV), fixed),
                  pl.BlockSpec((G, GDN_HEADS, GDN_DK, GDN_DV), lambda b, c: (b, 0, 0, 0))],
        out_specs=[pl.BlockSpec((G, C, GDN_VAL_DIM), lambda b, c: (b, c, 0)),
                   pl.BlockSpec((G, GDN_HEADS, GDN_DK, GDN_DV), lambda b, c: (b, 0, 0, 0))],
        out_shape=[jax.ShapeDtypeStruct((nb, t, GDN_VAL_DIM), BF16),
                   jax.ShapeDtypeStruct((nb, GDN_HEADS, GDN_DK, GDN_DV), F32)],
        scratch_shapes=[pltpu.VMEM((G, SUBLANES + C, GDN_CONV_DIM), F32),
                        pltpu.VMEM((G, GDN_HEADS, GDN_DK, GDN_DV), F32)],
        compiler_params=_params(("parallel", "arbitrary"), vmem),
        name="gdn",
    )(pg, pg, pm, pm, cst, conv_w, alog, dtb, nw, ssm0)
    return o.reshape(nb * t, GDN_VAL_DIM), s_new


def _mla_prep_kernel(qn_ref, qr_ref, ckv_ref, kr_ref, inv_ref, wuk_ref, kvn_ref,
                     ckv_o, kr_o, qcat_o, kcat_o, *maybe_vt_o, tm, pos0):
    j = pl.program_id(1)
    pos = (pos0 + j * tm + lax.broadcasted_iota(jnp.int32, (tm, LANES), 0)).astype(F32)
    lane = lax.broadcasted_iota(jnp.int32, (tm, LANES), 1)
    ang = pos * inv_ref[...]
    cos = jnp.cos(ang)
    sin = jnp.sin(ang)
    half = MLA_ROPE // 2
    cs = jnp.where(lane < MLA_ROPE, cos, jnp.where(lane < MLA_ROPE + half, -sin, sin))
    keep = lane < MLA_ROPE

    def rope(pair):
        prod = pair * cs
        return jnp.where(keep, prod + pltpu.roll(prod, MLA_ROPE, axis=1), 0.0)

    ckv = _rms(ckv_ref[...], kvn_ref[...])
    ckv_o[0] = ckv
    kr = rope(kr_ref[...])
    kr_o[0] = kr[:, :MLA_ROPE]
    kcat_o[0, :, :MLA_KV_RANK] = ckv.astype(BF16)
    kcat_o[0, :, MLA_KV_RANK:] = kr.astype(BF16)
    if maybe_vt_o:
        maybe_vt_o[0][0] = ckv.T.astype(BF16)
    for h in range(MLA_HEADS):
        qn = qn_ref[:, h * MLA_NOPE:(h + 1) * MLA_NOPE].astype(BF16)
        qlat = _dot(qn, wuk_ref[h]) * MLA_SCALE
        qcat_o[0, h, :, :MLA_KV_RANK] = qlat.astype(BF16)
        qr = rope(qr_ref[:, h * LANES:(h + 1) * LANES]) * MLA_SCALE
        qcat_o[0, h, :, MLA_KV_RANK:] = qr.astype(BF16)


def _mla_prep(proj, nb, t, pos0, inv128, wuk_t, kvn, emit_vt):
    tm = _tile(t, 256)
    nt = t // tm
    rows = lambda b, j: b * nt + j
    fixed2 = lambda b, j: (0, 0)
    vmem = (2 * (tm * (MLA_HEADS * MLA_NOPE + MLA_HEADS * LANES + MLA_KV_RANK + LANES) * 4
                 + MLA_HEADS * MLA_NOPE * MLA_KV_RANK * 2
                 + tm * (MLA_KV_RANK + LANES) * 4 + (MLA_HEADS + 1) * tm * MLA_QK * 2) + 8 * 2 ** 20)
    out_specs = [pl.BlockSpec((1, tm, MLA_KV_RANK), lambda b, j: (b, j, 0)),
                 pl.BlockSpec((1, tm, MLA_ROPE), lambda b, j: (b, j, 0)),
                 pl.BlockSpec((1, MLA_HEADS, tm, MLA_QK), lambda b, j: (b, 0, j, 0)),
                 pl.BlockSpec((1, tm, MLA_QK), lambda b, j: (b, j, 0))]
    out_shape = [jax.ShapeDtypeStruct((nb, t, MLA_KV_RANK), F32),
                 jax.ShapeDtypeStruct((nb, t, MLA_ROPE), F32),
                 jax.ShapeDtypeStruct((nb, MLA_HEADS, t, MLA_QK), BF16),
                 jax.ShapeDtypeStruct((nb, t, MLA_QK), BF16)]
    if emit_vt:
        out_specs.append(pl.BlockSpec((1, MLA_KV_RANK, tm), lambda b, j: (b, 0, j)))
        out_shape.append(jax.ShapeDtypeStruct((nb, MLA_KV_RANK, t), BF16))
    return pl.pallas_call(
        functools.partial(_mla_prep_kernel, tm=tm, pos0=pos0),
        grid=(nb, nt),
        in_specs=[pl.BlockSpec((tm, MLA_HEADS * MLA_NOPE), lambda b, j: (rows(b, j), COL_QNOPE // (MLA_HEADS * MLA_NOPE))),
                  pl.BlockSpec((tm, MLA_HEADS * LANES), lambda b, j: (rows(b, j), COL_QROPE // (MLA_HEADS * LANES))),
                  pl.BlockSpec((tm, MLA_KV_RANK), lambda b, j: (rows(b, j), COL_CKV // MLA_KV_RANK)),
                  pl.BlockSpec((tm, LANES), lambda b, j: (rows(b, j), COL_KR // LANES)),
                  pl.BlockSpec((1, LANES), fixed2),
                  pl.BlockSpec((MLA_HEADS, MLA_NOPE, MLA_KV_RANK), lambda b, j: (0, 0, 0)),
                  pl.BlockSpec((1, MLA_KV_RANK), fixed2)],
        out_specs=out_specs,
        out_shape=out_shape,
        compiler_params=_params(("parallel", "parallel"), vmem),
        name="mla_prep",
    )(proj, proj, proj, proj, inv128, wuk_t, kvn)


ATTN_GROUPS = 2


def _attn_kernel(q_ref, k_ref, vt_ref, wuv_ref, o_ref, s0_ref, s1_ref, m_ref, l_ref, acc_ref,
                 *, tq, tk, n_valid):
    i = pl.program_id(1)
    cols = MLA_HEADS * tq
    cg = cols // ATTN_GROUPS
    q_first = i * tq
    c = lax.broadcasted_iota(jnp.int32, (1, cg), 1)
    qchunk = jnp.right_shift(q_first + jnp.bitwise_and(c, tq - 1), LOG2_CHUNK)
    k_all = jnp.minimum((q_first // CHUNK + 1) * CHUNK, n_valid)
    k_any = jnp.minimum(((q_first + tq - 1) // CHUNK + 1) * CHUNK, n_valid)
    nk = (k_any + tk - 1) // tk
    m_ref[...] = jnp.full(m_ref.shape, -jnp.inf, F32)
    l_ref[...] = jnp.zeros(l_ref.shape, F32)
    acc_ref[...] = jnp.zeros(acc_ref.shape, F32)

    def scores(j, buf):
        start = pl.multiple_of(j * tk, tk)
        q = q_ref[0].reshape(cols, MLA_QK)
        buf[...] = _dot_nt(k_ref[0, pl.ds(start, tk), :], q)

    def fold(x, op):
        while x.shape[0] > SUBLANES:
            half = x.shape[0] // 2
            x = op(x[:half], x[half:])
        return x

    def softmax_values(j, buf, masked):
        start = pl.multiple_of(j * tk, tk)
        vt = vt_ref[0, :, pl.ds(start, tk)]
        if masked:
            kpos = start + lax.broadcasted_iota(jnp.int32, (tk, 1), 0)
            mask = jnp.logical_and(jnp.right_shift(kpos, LOG2_CHUNK) <= qchunk, kpos < n_valid)
        for g in range(ATTN_GROUPS):
            cs = slice(g * cg, (g + 1) * cg)
            s = buf[:, cs]
            if masked:
                s = jnp.where(mask, s, -jnp.inf)
            m_old = m_ref[:, cs]
            m_new = jnp.maximum(m_old, jnp.max(fold(s, jnp.maximum), axis=0, keepdims=True))
            alpha = jnp.exp(m_old - m_new)
            p = jnp.exp(s - m_new)
            l_ref[:, cs] = alpha * l_ref[:, cs] + jnp.sum(fold(p, jnp.add), axis=0, keepdims=True)
            acc_ref[:, cs] = alpha * acc_ref[:, cs] + _dot(vt, p.astype(BF16))
            m_ref[:, cs] = m_new

    def by_parity(j, fn):
        @pl.when(jnp.bitwise_and(j, 1) == 0)
        def _():
            fn(s0_ref, s1_ref)

        @pl.when(jnp.bitwise_and(j, 1) == 1)
        def _():
            fn(s1_ref, s0_ref)

    scores(0, s0_ref)

    def body(j, carry, masked):
        def step(cur, nxt):
            scores(j + 1, nxt)
            softmax_values(j, cur, masked)
        by_parity(j, step)
        return carry

    n_open = jnp.minimum(k_all // tk, nk - 1)
    lax.fori_loop(0, n_open, functools.partial(body, masked=False), 0)
    lax.fori_loop(n_open, nk - 1, functools.partial(body, masked=True), 0)
    by_parity(nk - 1, lambda cur, nxt: softmax_values(nk - 1, cur, True))
    inv_l = 1.0 / l_ref[...]
    for h in range(MLA_HEADS):
        hs = slice(h * tq, (h + 1) * tq)
        lat_t = (acc_ref[:, hs] * inv_l[:, hs]).astype(BF16)
        o_ref[:, h * MLA_V:(h + 1) * MLA_V] = _dot_tn(lat_t, wuv_ref[h]).astype(BF16)


def _attn_cache_kernel(q_ref, ckv_ref, kr_ref, knew_ref, wuv_ref, o_ref, m_ref, l_ref, acc_ref,
                       *, tq, tk, nkb, past):
    j = pl.program_id(1)
    rows = MLA_HEADS * tq
    q = q_ref[0].reshape(rows, MLA_QK)
    r = lax.broadcasted_iota(jnp.int32, (rows, 1), 0)
    qchunk = jnp.right_shift(past + jnp.bitwise_and(r, tq - 1), LOG2_CHUNK)

    @pl.when(j == 0)
    def _():
        m_ref[...] = jnp.full(m_ref.shape, -jnp.inf, F32)
        l_ref[...] = jnp.zeros(l_ref.shape, F32)
        acc_ref[...] = jnp.zeros(acc_ref.shape, F32)

    def fold(x, op):
        while x.shape[1] > LANES:
            half = x.shape[1] // 2
            x = op(x[:, :half], x[:, half:])
        return x

    def update(s, kpos, vals):
        s = jnp.where(jnp.right_shift(kpos, LOG2_CHUNK) <= qchunk, s, -jnp.inf)
        m_old = m_ref[...]
        m_new = jnp.maximum(m_old, jnp.max(fold(s, jnp.maximum), axis=-1, keepdims=True))
        alpha = jnp.exp(m_old - m_new)
        p = jnp.exp(s - m_new)
        l_ref[...] = alpha * l_ref[...] + jnp.sum(fold(p, jnp.add), axis=-1, keepdims=True)
        acc_ref[...] = alpha * acc_ref[...] + _dot(p.astype(BF16), vals)
        m_ref[...] = m_new

    ck = ckv_ref[0].astype(BF16)
    s = (_dot_nt(q[:, :MLA_KV_RANK], ck)
         + _dot_nt(q[:, MLA_KV_RANK:MLA_KV_RANK + MLA_ROPE], kr_ref[0].astype(BF16)))
    update(s, j * tk + lax.broadcasted_iota(jnp.int32, (1, tk), 1), ck)

    @pl.when(j == nkb - 1)
    def _():
        kn = knew_ref[0]
        update(_dot_nt(q, kn), past + lax.broadcasted_iota(jnp.int32, (1, tq), 1), kn[:, :MLA_KV_RANK])
        inv_l = 1.0 / l_ref[...]
        for h in range(MLA_HEADS):
            hs = slice(h * tq, (h + 1) * tq)
            lat = (acc_ref[hs, :] * inv_l[hs, :]).astype(BF16)
            o_ref[:, h * MLA_V:(h + 1) * MLA_V] = _dot(lat, wuv_ref[h]).astype(BF16)


def _attention(qcat, kcat, vt, wuv_t):
    nb, _, t, _ = qcat.shape
    tq = _tile(t, 128)
    tk = _tile(t, 512)
    assert tq & (tq - 1) == 0
    nq = t // tq
    cols = MLA_HEADS * tq
    vmem = (2 * (cols * MLA_QK * 2 + t * MLA_QK * 2 + MLA_KV_RANK * t * 2
                 + MLA_HEADS * MLA_KV_RANK * MLA_V * 2 + tq * MLA_HEADS * MLA_V * 2)
            + 2 * tk * cols * 4 + 2 * SUBLANES * cols * 4 + MLA_KV_RANK * cols * 4
            + 3 * tk * cols * 4 // ATTN_GROUPS + 4 * 2 ** 20)
    return pl.pallas_call(
        functools.partial(_attn_kernel, tq=tq, tk=tk, n_valid=t),
        grid=(nb, nq),
        in_specs=[pl.BlockSpec((1, MLA_HEADS, tq, MLA_QK), lambda b, i: (b, 0, i, 0)),
                  pl.BlockSpec((1, t, MLA_QK), lambda b, i: (b, 0, 0)),
                  pl.BlockSpec((1, MLA_KV_RANK, t), lambda b, i: (b, 0, 0)),
                  pl.BlockSpec((MLA_HEADS, MLA_KV_RANK, MLA_V), lambda b, i: (0, 0, 0))],
        out_specs=pl.BlockSpec((tq, MLA_HEADS * MLA_V), lambda b, i: (b * nq + i, 0)),
        out_shape=jax.ShapeDtypeStruct((nb * t, MLA_HEADS * MLA_V), BF16),
        scratch_shapes=[pltpu.VMEM((tk, cols), F32), pltpu.VMEM((tk, cols), F32), pltpu.VMEM((1, cols), F32),
                        pltpu.VMEM((1, cols), F32), pltpu.VMEM((MLA_KV_RANK, cols), F32)],
        compiler_params=_params(("parallel", "arbitrary"), vmem),
        name="mla_attention",
    )(qcat, kcat, vt, wuv_t)


def _attention_cached(qcat, ckv_past, krope_past, kcat_new, wuv_t):
    nb, _, t, _ = qcat.shape
    past = ckv_past.shape[1]
    tk = _tile(past, 1024)
    nkb = past // tk
    assert t & (t - 1) == 0
    rows = MLA_HEADS * t
    vmem = (2 * (rows * MLA_QK * 2 + tk * MLA_KV_RANK * 4 + tk * LANES * 4 + t * MLA_QK * 2
                 + MLA_HEADS * MLA_KV_RANK * MLA_V * 2 + t * MLA_HEADS * MLA_V * 2)
            + 2 * rows * LANES * 4 + rows * MLA_KV_RANK * 4 + tk * MLA_QK * 2 + 4 * rows * tk * 4 + 4 * 2 ** 20)
    return pl.pallas_call(
        functools.partial(_attn_cache_kernel, tq=t, tk=tk, nkb=nkb, past=past),
        grid=(nb, nkb),
        in_specs=[pl.BlockSpec((1, MLA_HEADS, t, MLA_QK), lambda b, j: (b, 0, 0, 0)),
                  pl.BlockSpec((1, tk, MLA_KV_RANK), lambda b, j: (b, j, 0)),
                  pl.BlockSpec((1, tk, MLA_ROPE), lambda b, j: (b, j, 0)),
                  pl.BlockSpec((1, t, MLA_QK), lambda b, j: (b, 0, 0)),
                  pl.BlockSpec((MLA_HEADS, MLA_KV_RANK, MLA_V), lambda b, j: (0, 0, 0))],
        out_specs=pl.BlockSpec((t, MLA_HEADS * MLA_V), lambda b, j: (b, 0)),
        out_shape=jax.ShapeDtypeStruct((nb * t, MLA_HEADS * MLA_V), BF16),
        scratch_shapes=[pltpu.VMEM((rows, 1), F32), pltpu.VMEM((rows, 1), F32),
                        pltpu.VMEM((rows, MLA_KV_RANK), F32)],
        compiler_params=_params(("parallel", "arbitrary"), vmem),
        name="mla_attention_cached",
    )(qcat, ckv_past, krope_past, kcat_new, wuv_t)


def _merge_out_kernel(og_ref, om_ref, wg_ref, wm_ref, gg_ref, gm_ref, wo_ref, x_ref, gpost_ref, gnext_ref,
                      xo_ref, xno_ref):
    tg = _dot(og_ref[...], wg_ref[...])
    tm_ = _dot(om_ref[...], wm_ref[...])
    merged = (jax.nn.sigmoid(gg_ref[...].astype(F32)) * tg
              + jax.nn.sigmoid(gm_ref[...].astype(F32)) * tm_).astype(BF16)
    y = _dot(merged, wo_ref[...])
    xnew = x_ref[...] + _rms(y, gpost_ref[...])
    xo_ref[...] = xnew
    xno_ref[...] = _rms(xnew, gnext_ref[...]).astype(BF16)


def _merge_out(og, om, wbg, wbm, proj_gate, wout, x, gpost, gnext):
    m, kg = og.shape
    d = wbg.shape[1]
    tm = _tile(m, 256)
    gsz = proj_gate.dtype.itemsize
    row = lambda i: (i, 0)
    fixed = lambda i: (0, 0)
    resident = dict(pipeline_mode=pl.Buffered(1))
    vmem = ((2 * kg * d + d * d) * 2
            + 2 * (2 * tm * kg * 2 + 2 * tm * d * gsz + 2 * tm * d * 4 + tm * d * 2)
            + 4 * tm * d * 4 + 2 * 2 ** 20)
    return pl.pallas_call(
        _merge_out_kernel,
        grid=(m // tm,),
        in_specs=[pl.BlockSpec((tm, kg), row),
                  pl.BlockSpec((tm, kg), row),
                  pl.BlockSpec((kg, d), fixed, **resident),
                  pl.BlockSpec((kg, d), fixed, **resident),
                  pl.BlockSpec((tm, d), lambda i: (i, 0)),
                  pl.BlockSpec((tm, d), lambda i: (i, 1)),
                  pl.BlockSpec((d, d), fixed, **resident),
                  pl.BlockSpec((tm, d), row),
                  pl.BlockSpec((1, d), fixed),
                  pl.BlockSpec((1, d), fixed)],
        out_specs=[pl.BlockSpec((tm, d), row), pl.BlockSpec((tm, d), row)],
        out_shape=[jax.ShapeDtypeStruct((m, d), F32), jax.ShapeDtypeStruct((m, d), BF16)],
        compiler_params=_params(("parallel",), vmem),
        name="merge_out",
    )(og, om, wbg, wbm, proj_gate, proj_gate, wout, x, gpost, gnext)


def _layer(x3, pos0, conv_state, ssm0, ckv_past, krope_past, w):
    nb, t, d = x3.shape
    m = nb * t
    x = x3.reshape(m, d)
    x1, hm, ffn1_w = _ffn(x, *w["ffn1_w"], w["ffn1_norm_post"],
                          gpre=w["ffn1_norm_pre"], gnext=w["mix_norm_pre"])
    proj_gdn = _matmul(hm, w["w_in_gdn"], F32)
    proj_gate = _matmul(hm, w["w_in_gate"], BF16)
    proj_mla = _matmul(hm, w["w_in_mla"], F32)

    o_gdn, ssm_new = _gdn(proj_gdn, proj_mla, nb, t, conv_state, ssm0, w["conv_w"], w["alog"], w["dtb"],
                          w["gdn_nw"])
    conv_new = proj_gdn.reshape(nb, t, -1)[:, t - (GDN_CONV - 1):, COL_QKV:COL_QKV + GDN_CONV_DIM]

    ckv, krope, qcat, kcat, *vt = _mla_prep(proj_mla, nb, t, pos0, w["inv128"], w["wuk_t"],
                                            w["kv_norm"], ckv_past is None)
    if ckv_past is None:
        o_mla = _attention(qcat, kcat, vt[0], w["wuv_t"])
    else:
        o_mla = _attention_cached(qcat, ckv_past, krope_past, kcat, w["wuv_t"])

    x2, xn2 = _merge_out(o_gdn, o_mla, w["w_br_gdn"], w["w_br_mla"], proj_gate, w["w_out"], x1,
                         w["mix_norm_post"], w["ffn2_norm_pre"])
    x3_, _, ffn2_w = _ffn(x2, *w["ffn2_w"], w["ffn2_norm_post"], xn=xn2)
    return (x3_.reshape(nb, t, d), conv_new, ssm_new, ckv, krope), dict(ffn1_w=ffn1_w, ffn2_w=ffn2_w)


def _prep_weights(l, d_model, **p):
    row = lambda v: v[l].reshape(1, -1).astype(F32)
    pad_lanes = lambda v: jnp.pad(v[l].reshape(1, -1).astype(F32), ((0, 0), (0, LANES - v.shape[1])))
    inv = ROPE_THETA ** (-jnp.arange(0, MLA_ROPE, 2, dtype=F32) / MLA_ROPE)
    w_in_gdn, w_in_gate, w_in_mla = _build_w_in(p["w_in"][l], d_model)
    w = dict(
        w_in_gdn=w_in_gdn, w_in_gate=w_in_gate, w_in_mla=w_in_mla,
        ffn1_norm_pre=row(p["ffn1_norm_pre"]), ffn1_norm_post=row(p["ffn1_norm_post"]),
        mix_norm_pre=row(p["mix_norm_pre"]), mix_norm_post=row(p["mix_norm_post"]),
        ffn2_norm_pre=row(p["ffn2_norm_pre"]), ffn2_norm_post=row(p["ffn2_norm_post"]),
        ffn1_w=(p["ffn1_w_gate"][l], p["ffn1_w_up"][l], p["ffn1_w_down"][l]),
        ffn2_w=(p["ffn2_w_gate"][l], p["ffn2_w_up"][l], p["ffn2_w_down"][l]),
        conv_w=p["gdn_conv_w"][l].astype(F32),
        alog=pad_lanes(p["gdn_a_log"]), dtb=pad_lanes(p["gdn_dt_bias"]),
        gdn_nw=row(p["gdn_norm_w"]), kv_norm=row(p["mla_kv_norm"]),
        inv128=jnp.tile(inv, LANES // inv.shape[0]).reshape(1, LANES),
        wuk_t=jnp.transpose(p["mla_w_uk"][l], (1, 2, 0)).astype(BF16),
        wuv_t=jnp.transpose(p["mla_w_uv"][l], (1, 0, 2)).astype(BF16),
        w_br_gdn=p["w_br_gdn"][l].astype(BF16), w_br_mla=p["w_br_mla"][l].astype(BF16),
        w_out=p["w_out"][l].astype(BF16),
    )
    return w


def kernel(x_prompt, x_sample, state_gdn_conv, state_gdn_ssm, cache_mla_ckv, cache_mla_krope, ffn1_norm_pre, ffn1_w_gate, ffn1_w_up, ffn1_w_down, ffn1_norm_post, mix_norm_pre, w_in, gdn_conv_w, gdn_a_log, gdn_dt_bias, gdn_norm_w, mla_kv_norm, mla_w_uk, mla_w_uv, w_br_gdn, w_br_mla, w_out, mix_norm_post, ffn2_norm_pre, ffn2_w_gate, ffn2_w_up, ffn2_w_down, ffn2_norm_post):
    params = dict(
        ffn1_norm_pre=ffn1_norm_pre, ffn1_w_gate=ffn1_w_gate, ffn1_w_up=ffn1_w_up,
        ffn1_w_down=ffn1_w_down, ffn1_norm_post=ffn1_norm_post, mix_norm_pre=mix_norm_pre,
        w_in=w_in, gdn_conv_w=gdn_conv_w, gdn_a_log=gdn_a_log, gdn_dt_bias=gdn_dt_bias,
        gdn_norm_w=gdn_norm_w, mla_kv_norm=mla_kv_norm, mla_w_uk=mla_w_uk, mla_w_uv=mla_w_uv,
        w_br_gdn=w_br_gdn, w_br_mla=w_br_mla, w_out=w_out, mix_norm_post=mix_norm_post,
        ffn2_norm_pre=ffn2_norm_pre, ffn2_w_gate=ffn2_w_gate, ffn2_w_up=ffn2_w_up,
        ffn2_w_down=ffn2_w_down, ffn2_norm_post=ffn2_norm_post)
    depth = w_in.shape[0]
    d_model = x_prompt.shape[-1]
    b_p = x_prompt.shape[0]
    past = cache_mla_ckv.shape[2]
    yp, ys = x_prompt, x_sample
    outs_p, outs_s = [], []
    for l in range(depth):
        w = _prep_weights(l, d_model, **params)
        (ys, *rest_s), w_bf16 = _layer(
            ys, past, state_gdn_conv[l], state_gdn_ssm[l], cache_mla_ckv[l], cache_mla_krope[l], w)
        (yp, *rest_p), _ = _layer(
            yp, 0, jnp.zeros((b_p, GDN_CONV - 1, GDN_CONV_DIM), F32),
            jnp.zeros((b_p, GDN_HEADS, GDN_DK, GDN_DV), F32), None, None, {**w, **w_bf16})
        outs_p.append(rest_p)
        outs_s.append(rest_s)
    stack = lambda outs, i: jnp.stack([o[i] for o in outs])
    return (yp, ys,
            stack(outs_p, 0), stack(outs_p, 1), stack(outs_p, 2), stack(outs_p, 3),
            stack(outs_s, 0), stack(outs_s, 1), stack(outs_s, 2), stack(outs_s, 3))
```

```python
import functools

import jax
import jax.numpy as jnp
from jax import lax
from jax.experimental import pallas as pl
from jax.experimental.pallas import tpu as pltpu

F32 = jnp.float32
BF16 = jnp.bfloat16

CHUNK = 64
LOG2_CHUNK = 6
NORM_EPS = 1e-6
GDN_HEADS = 8
GDN_DK = 128
GDN_DV = 128
GDN_CONV = 4
GDN_KEY_DIM = GDN_HEADS * GDN_DK
GDN_VAL_DIM = GDN_HEADS * GDN_DV
GDN_CONV_DIM = 2 * GDN_KEY_DIM + GDN_VAL_DIM
MLA_HEADS = 8
MLA_NOPE = 128
MLA_ROPE = 64
MLA_V = 128
MLA_KV_RANK = 512
MLA_SCALE = (MLA_NOPE + MLA_ROPE) ** -0.5
ROPE_THETA = 10000.0

LANES = 128
SUBLANES = 8
VMEM_CAP_BYTES = 56 * 2 ** 20
MLA_QK = MLA_KV_RANK + LANES
HI = lax.Precision.HIGHEST


def _params(semantics, vmem_bytes):
    limit = int(min(max(vmem_bytes, 16 * 2 ** 20), VMEM_CAP_BYTES))
    return pltpu.CompilerParams(dimension_semantics=semantics, vmem_limit_bytes=limit)


def _dot(a, b, prec=None):
    return lax.dot_general(a, b, (((1,), (0,)), ((), ())), precision=prec,
                           preferred_element_type=F32)


def _dot_nt(a, b, prec=None):
    return lax.dot_general(a, b, (((1,), (1,)), ((), ())), precision=prec,
                           preferred_element_type=F32)


def _dot_tn(a, b, prec=None):
    return lax.dot_general(a, b, (((0,), (0,)), ((), ())), precision=prec,
                           preferred_element_type=F32)


def _rms(y, g):
    return y * lax.rsqrt(jnp.mean(y * y, axis=-1, keepdims=True) + NORM_EPS) * g


def _silu(x):
    return x * jax.nn.sigmoid(x)


def _tile(n, pref):
    t = min(n, pref)
    assert n % t == 0, (n, pref)
    return t


def _ffn_kernel(*refs, nf, norm_in, emit_next, cast_w):
    refs = list(refs)
    xn_ref = None if norm_in else refs.pop(0)
    x_ref, wg_ref, wu_ref, wd_ref = refs[:4]
    del refs[:4]
    gpre_ref = refs.pop(0) if norm_in else None
    gpost_ref = refs.pop(0)
    gnext_ref = refs.pop(0) if emit_next else None
    xo_ref = refs.pop(0)
    xno_ref = refs.pop(0) if emit_next else None
    if cast_w:
        wgo_ref, wuo_ref, wdo_ref = refs[:3]
        del refs[:3]
    acc_ref = refs.pop(0)
    if norm_in:
        xn_ref = refs.pop(0)
    f = pl.program_id(1)

    @pl.when(f == 0)
    def _():
        acc_ref[...] = jnp.zeros(acc_ref.shape, F32)
        if norm_in:
            xn_ref[...] = _rms(x_ref[...], gpre_ref[...]).astype(BF16)

    xn = xn_ref[...]
    wg, wu, wd = wg_ref[...], wu_ref[...], wd_ref[...]
    if cast_w:
        wg, wu, wd = wg.astype(BF16), wu.astype(BF16), wd.astype(BF16)
        wgo_ref[...] = wg
        wuo_ref[...] = wu
        wdo_ref[...] = wd
    gate = _dot(xn, wg)
    up = _dot(xn, wu)
    h = (_silu(gate) * up).astype(BF16)
    acc_ref[...] += _dot(h, wd)

    @pl.when(f == nf - 1)
    def _():
        xnew = x_ref[...] + 0.5 * _rms(acc_ref[...], gpost_ref[...])
        xo_ref[...] = xnew
        if emit_next:
            xno_ref[...] = _rms(xnew, gnext_ref[...]).astype(BF16)


def _ffn(x, wg, wu, wd, gpost, xn=None, gpre=None, gnext=None):
    norm_in = xn is None
    emit_next = gnext is not None
    cast_w = wg.dtype == F32
    assert norm_in == (gpre is not None)
    m, d = x.shape
    dff = wg.shape[1]
    tm = _tile(m, 512)
    tf = _tile(dff, 256 if cast_w else 512)
    nf = dff // tf
    assert not cast_w or m == tm
    wsz = wg.dtype.itemsize
    row = lambda i, f: (i, 0)
    vec = pl.BlockSpec((1, d), lambda i, f: (0, 0))
    w_specs = [pl.BlockSpec((d, tf), lambda i, f: (0, f)),
               pl.BlockSpec((d, tf), lambda i, f: (0, f)),
               pl.BlockSpec((tf, d), lambda i, f: (f, 0))]
    ins = [] if norm_in else [xn]
    in_specs = [] if norm_in else [pl.BlockSpec((tm, d), row)]
    ins += [x, wg, wu, wd]
    in_specs += [pl.BlockSpec((tm, d), row)] + w_specs
    for g in (gpre, gpost, gnext):
        if g is not None:
            ins.append(g)
            in_specs.append(vec)
    out_specs = [pl.BlockSpec((tm, d), row)]
    out_shape = [jax.ShapeDtypeStruct((m, d), F32)]
    scratch = [pltpu.VMEM((tm, d), F32)]
    if emit_next:
        out_specs.append(pl.BlockSpec((tm, d), row))
        out_shape.append(jax.ShapeDtypeStruct((m, d), BF16))
    if cast_w:
        out_specs += w_specs
        out_shape += [jax.ShapeDtypeStruct(w.shape, BF16) for w in (wg, wu, wd)]
    if norm_in:
        scratch.append(pltpu.VMEM((tm, d), BF16))
    vmem = (2 * (tm * d * 2 + tm * d * 4 + 3 * d * tf * wsz + tm * d * 4 + tm * d * 2)
            + (2 * 3 * d * tf * 2 + 3 * d * tf * 2 if cast_w else 0)
            + tm * d * 4 + 3 * tm * tf * 4 + 2 * tm * d * 4)
    res = pl.pallas_call(
        functools.partial(_ffn_kernel, nf=nf, norm_in=norm_in, emit_next=emit_next, cast_w=cast_w),
        grid=(m // tm, nf),
        in_specs=in_specs,
        out_specs=out_specs,
        out_shape=out_shape,
        scratch_shapes=scratch,
        compiler_params=_params(("parallel", "arbitrary"), vmem),
        name="ffn_cast" if cast_w else "ffn",
    )(*ins)
    res = list(res)
    x_new = res.pop(0)
    xn_next = res.pop(0) if emit_next else None
    return x_new, xn_next, (tuple(res) if cast_w else (wg, wu, wd))


def _mm_kernel(a_ref, w_ref, o_ref):
    o_ref[...] = _dot(a_ref[...], w_ref[...]).astype(o_ref.dtype)


def _matmul(a, w, out_dtype):
    m, k = a.shape
    n = w.shape[1]
    tm = _tile(m, 1024)
    tn = _tile(n, 1024)
    osz = jnp.dtype(out_dtype).itemsize
    vmem = 2 * (tm * k * 2 + k * tn * 2 + tm * tn * osz) + tm * tn * 4
    return pl.pallas_call(
        _mm_kernel,
        grid=(m // tm, n // tn),
        in_specs=[pl.BlockSpec((tm, k), lambda i, j: (i, 0)),
                  pl.BlockSpec((k, tn), lambda i, j: (0, j))],
        out_specs=pl.BlockSpec((tm, tn), lambda i, j: (i, j)),
        out_shape=jax.ShapeDtypeStruct((m, n), out_dtype),
        compiler_params=_params(("parallel", "arbitrary"), vmem),
        name="in_proj",
    )(a, w)


COL_QKV = 0
COL_Z = COL_QKV + GDN_CONV_DIM
COL_QNOPE = 0
COL_QROPE = COL_QNOPE + MLA_HEADS * MLA_NOPE
COL_CKV = COL_QROPE + MLA_HEADS * LANES
COL_KR = COL_CKV + MLA_KV_RANK
COL_B = COL_KR + LANES
COL_A = COL_B + LANES
N_MLA = 3072


def _build_w_in(w_in, d_model):
    sizes = (GDN_CONV_DIM, GDN_VAL_DIM, GDN_HEADS, GDN_HEADS, MLA_HEADS * (MLA_NOPE + MLA_ROPE),
             MLA_KV_RANK, MLA_ROPE, d_model, d_model)
    offs = [0]
    for s in sizes:
        offs.append(offs[-1] + s)
    w_gdn = w_in[:, :offs[2]].astype(BF16)
    w_gate = w_in[:, offs[7]:].astype(BF16)
    part = lambda i: w_in[:, offs[i]:offs[i + 1]].astype(BF16)
    b, a, qm, ckv, kr = (part(i) for i in range(2, 7))
    d = w_in.shape[0]
    half = MLA_ROPE // 2
    qm = qm.reshape(d, MLA_HEADS, MLA_NOPE + MLA_ROPE)
    qn = qm[:, :, :MLA_NOPE].reshape(d, MLA_HEADS * MLA_NOPE)
    qr = qm[:, :, MLA_NOPE:]
    qr_pair = jnp.concatenate([qr, qr[:, :, half:], qr[:, :, :half]], axis=2).reshape(d, MLA_HEADS * LANES)
    kr_pair = jnp.concatenate([kr, kr[:, half:], kr[:, :half]], axis=1)
    pad8 = lambda w: jnp.pad(w, ((0, 0), (0, LANES - w.shape[1])))
    w_mla = jnp.concatenate([qn, qr_pair, ckv, kr_pair, pad8(b), pad8(a)], axis=1)
    w_mla = jnp.pad(w_mla, ((0, 0), (0, N_MLA - w_mla.shape[1])))
    return w_gdn, w_gate, w_mla


_NN = (((1,), (0,)), ((), ()))
_NT = (((1,), (1,)), ((), ()))
_TN = (((0,), (0,)), ((), ()))
GDN_MODE = dict(kk="bf16", inv="bf16", sol="bf16", state="bf16")
INV_BLOCK = 16


def _mm(a, b, mode, dims=_NN):
    if mode == "f32":
        return lax.dot_general(a, b, dims, precision=HI, preferred_element_type=F32)
    ah = a.astype(BF16)
    bh = b.astype(BF16)
    out = lax.dot_general(ah, bh, dims, preferred_element_type=F32)
    if mode == "bf16x3":
        al = (a - ah.astype(F32)).astype(BF16)
        bl = (b - bh.astype(F32)).astype(BF16)
        out = (out + lax.dot_general(ah, bl, dims, preferred_element_type=F32)
               + lax.dot_general(al, bh, dims, preferred_element_type=F32))
    return out


def _gdn_kernel(qkv_ref, z_ref, b_ref, a_ref, cst_ref, cw_ref, alog_ref, dtb_ref, nw_ref, s0_ref,
                o_ref, sout_ref, past_ref, s_ref, *, G, C, nlev, nc):
    c = pl.program_id(1)

    @pl.when(c == 0)
    def _():
        past_ref[...] = cst_ref[...]
        s_ref[...] = s0_ref[...]

    ri = lax.broadcasted_iota(jnp.int32, (C, C), 0)
    ci = lax.broadcasted_iota(jnp.int32, (C, C), 1)
    incl = ri >= ci
    strict = ri > ci
    eye = (ri == ci).astype(F32)
    log2_blk = INV_BLOCK.bit_length() - 1
    bdiag = jnp.right_shift(ri, log2_blk) == jnp.right_shift(ci, log2_blk)

    first_row = lax.broadcasted_iota(jnp.int32, (C, 1), 0) == 0

    def conv(s, col):
        cols = slice(col, col + LANES)
        x = qkv_ref[s, :, cols]
        past = past_ref[s, :, cols]
        w = [cw_ref[j:j + 1, cols] for j in range(GDN_CONV)]
        acc = w[0] * x
        for j in range(1, GDN_CONV):
            top = sum(w[i] * past[SUBLANES - 1 - (j - 1 - i):SUBLANES - (j - 1 - i)] for i in range(j))
            acc = w[j] * x + jnp.where(first_row, top, pltpu.roll(acc, 1, axis=0))
        return _silu(acc)

    units = [(s, h) for s in range(G) for h in range(GDN_HEADS)]
    every = lambda fn, *lists: [fn(*args) for args in zip(*lists)]
    mm_inv = lambda a, b: _mm(a, b, GDN_MODE["inv"])
    beta_all, gcum, gcum_t = [], [], []
    for s in range(G):
        beta_all.append(jax.nn.sigmoid(b_ref[s]))
        a_all = a_ref[s] + dtb_ref[...]
        softplus = jnp.maximum(a_all, 0.0) + jnp.log(1.0 + jnp.exp(-jnp.abs(a_all)))
        g_all = -jnp.exp(alog_ref[...]) * softplus
        gcum.append(_dot(incl.astype(F32), g_all, HI))
        gcum_t.append(_dot_tn(g_all, (ci >= ri).astype(F32), HI))

    def l2n(x):
        return x * lax.rsqrt(jnp.sum(x * x, axis=-1, keepdims=True) + 1e-6)

    q = [l2n(conv(s, h * GDN_DK)) * (GDN_DK ** -0.5) for s, h in units]
    k = [l2n(conv(s, GDN_KEY_DIM + h * GDN_DK)) for s, h in units]
    v = [conv(s, 2 * GDN_KEY_DIM + h * GDN_DV) for s, h in units]
    bcol = [beta_all[s][:, h:h + 1] for s, h in units]
    gcol = [gcum[s][:, h:h + 1] for s, h in units]
    glast = [gcum[s][C - 1:C, h:h + 1] for s, h in units]
    decay = [jnp.where(incl, jnp.exp(jnp.where(incl, gcum[s][:, h:h + 1] - gcum_t[s][h:h + 1, :], 0.0)), 0.0)
             for s, h in units]
    qkk = every(lambda q_, k_: _mm(jnp.concatenate([q_, k_], axis=0), k_, GDN_MODE["kk"], _NT), q, k)
    qk = every(lambda x, d: x[:C] * d, qkk, decay)
    nmat = every(lambda x, b_, d: -jnp.where(strict, b_ * x[C:] * d, 0.0), qkk, bcol, decay)
    ndiag = every(lambda n_: jnp.where(bdiag, n_, 0.0), nmat)
    tinv = every(lambda n_: eye + n_, ndiag)
    npow = ndiag
    for _ in range(INV_BLOCK.bit_length() - 2):
        npow = every(mm_inv, npow, npow)
        tinv = every(lambda t_, p_: t_ + mm_inv(t_, p_), tinv, npow)
    mpow = every(lambda t_, n_, d_: mm_inv(t_, n_ - d_), tinv, nmat, ndiag)
    for lev in range(nlev):
        if lev:
            mpow = every(mm_inv, mpow, mpow)
        tinv = every(lambda t_, m_: t_ + mm_inv(m_, t_), tinv, mpow)
    egc = every(jnp.exp, gcol)
    sol = every(lambda t_, v_, k_, b_, e_: _mm(t_, jnp.concatenate([v_ * b_, k_ * (b_ * e_)], axis=1),
                                               GDN_MODE["sol"]), tinv, v, k, bcol, egc)
    s_old = [s_ref[s, h] for s, h in units]
    ws_qs = every(lambda x, q_, e_, s_: _mm(jnp.concatenate([x[:, GDN_DV:], q_ * e_], axis=0), s_,
                                            GDN_MODE["state"]), sol, q, egc, s_old)
    v_new = every(lambda x, y: x[:, :GDN_DV] - y[:C], sol, ws_qs)
    o = every(lambda y, a_, vn: y[C:] + _mm(a_, vn, GDN_MODE["state"]), ws_qs, qk, v_new)
    s_new = every(lambda s_, gl, k_, gc, vn: s_ * jnp.exp(gl) + _mm(k_ * jnp.exp(gl - gc), vn, GDN_MODE["state"], _TN),
                  s_old, glast, k, gcol, v_new)
    for (s, h), sn, o_ in zip(units, s_new, o):
        s_ref[s, h] = sn
        zz = z_ref[s, :, h * GDN_DV:(h + 1) * GDN_DV]
        o_ref[s, :, h * GDN_DV:(h + 1) * GDN_DV] = (_rms(o_, nw_ref[...]) * _silu(zz)).astype(BF16)

    past_ref[...] = qkv_ref[:, C - SUBLANES:C, :]

    @pl.when(c == nc - 1)
    def _():
        sout_ref[...] = s_ref[...]


def _gdn(proj_gdn, ba, nb, t, conv_state, ssm0, conv_w, alog, dtb, nw):
    C = min(CHUNK, t)
    G = 2 if nb % 2 == 0 else 1
    assert t % C == 0 and C & (C - 1) == 0 and C % INV_BLOCK == 0
    nc = t // C
    nlev = (C // INV_BLOCK).bit_length() - 1
    pg = proj_gdn.reshape(nb, t, proj_gdn.shape[1])
    pm = ba.reshape(nb, t, ba.shape[1])
    fixed = lambda b, c: (0, 0)
    cst = jnp.pad(conv_state, ((0, 0), (SUBLANES - (GDN_CONV - 1), 0), (0, 0)))
    vmem = G * (2 * (C * GDN_CONV_DIM * 4 + C * GDN_VAL_DIM * 4 + 2 * C * LANES * 4 + SUBLANES * GDN_CONV_DIM * 4
                     + C * GDN_VAL_DIM * 2 + 2 * GDN_HEADS * GDN_DK * GDN_DV * 4)
                + SUBLANES * GDN_CONV_DIM * 4 + GDN_HEADS * GDN_DK * GDN_DV * 4) + 16 * 2 ** 20
    o, s_new = pl.pallas_call(
        functools.partial(_gdn_kernel, G=G, C=C, nlev=nlev, nc=nc),
        grid=(nb // G, nc),
        in_specs=[pl.BlockSpec((G, C, GDN_CONV_DIM), lambda b, c: (b, c, COL_QKV // GDN_CONV_DIM)),
                  pl.BlockSpec((G, C, GDN_VAL_DIM), lambda b, c: (b, c, COL_Z // GDN_VAL_DIM)),
                  pl.BlockSpec((G, C, LANES), lambda b, c: (b, c, 0)),
                  pl.BlockSpec((G, C, LANES), lambda b, c: (b, c, 1)),
                  pl.BlockSpec((G, SUBLANES, GDN_CONV_DIM), lambda b, c: (b, 0, 0)),
                  pl.BlockSpec((GDN_CONV, GDN_CONV_DIM), fixed),
                  pl.BlockSpec((1, LANES), fixed),
                  pl.BlockSpec((1, LANES), fixed),
                  pl.BlockSpec((1, GDN_DV), fixed),
                  pl.BlockSpec((G, GDN_HEADS, GDN_DK, GDN_DV), lambda b, c: (b, 0, 0, 0))],
        out_specs=[pl.BlockSpec((G, C, GDN_VAL_DIM), lambda b, c: (b, c, 0)),
                   pl.BlockSpec((G, GDN_HEADS, GDN_DK, GDN_DV), lambda b, c: (b, 0, 0, 0))],
        out_shape=[jax.ShapeDtypeStruct((nb, t, GDN_VAL_DIM), BF16),
                   jax.ShapeDtypeStruct((nb, GDN_HEADS, GDN_DK, GDN_DV), F32)],
        scratch_shapes=[pltpu.VMEM((G, SUBLANES, GDN_CONV_DIM), F32),
                        pltpu.VMEM((G, GDN_HEADS, GDN_DK, GDN_DV), F32)],
        compiler_params=_params(("parallel", "arbitrary"), vmem),
        name="gdn",
    )(pg, pg, pm, pm, cst, conv_w, alog, dtb, nw, ssm0)
    return o.reshape(nb * t, GDN_VAL_DIM), s_new


def _mla_prep_kernel(hm_ref, w_ref, inv_ref, wuk_ref, kvn_ref,
                     ckv_o, kr_o, qcat_o, kcat_o, ba_o, *maybe_vt_o, tm, pos0):
    j = pl.program_id(1)
    proj = _dot(hm_ref[...], w_ref[...])
    ba_o[...] = proj[:, COL_B:COL_A + LANES]
    pos = (pos0 + j * tm + lax.broadcasted_iota(jnp.int32, (tm, LANES), 0)).astype(F32)
    lane = lax.broadcasted_iota(jnp.int32, (tm, LANES), 1)
    ang = pos * inv_ref[...]
    cos = jnp.cos(ang)
    sin = jnp.sin(ang)
    half = MLA_ROPE // 2
    cs = jnp.where(lane < MLA_ROPE, cos, jnp.where(lane < MLA_ROPE + half, -sin, sin))
    keep = lane < MLA_ROPE

    def rope(pair):
        prod = pair * cs
        return jnp.where(keep, prod + pltpu.roll(prod, MLA_ROPE, axis=1), 0.0)

    ckv = _rms(proj[:, COL_CKV:COL_CKV + MLA_KV_RANK], kvn_ref[...])
    ckv_o[0] = ckv
    kr = rope(proj[:, COL_KR:COL_KR + LANES])
    kr_o[0] = kr[:, :MLA_ROPE]
    kcat_o[0, :, :MLA_KV_RANK] = ckv.astype(BF16)
    kcat_o[0, :, MLA_KV_RANK:] = kr.astype(BF16)
    if maybe_vt_o:
        maybe_vt_o[0][0] = ckv.T.astype(BF16)
    for h in range(MLA_HEADS):
        qn = proj[:, COL_QNOPE + h * MLA_NOPE:COL_QNOPE + (h + 1) * MLA_NOPE].astype(BF16)
        qlat = _dot(qn, wuk_ref[h]) * MLA_SCALE
        qcat_o[0, h, :, :MLA_KV_RANK] = qlat.astype(BF16)
        qr = rope(proj[:, COL_QROPE + h * LANES:COL_QROPE + (h + 1) * LANES]) * MLA_SCALE
        qcat_o[0, h, :, MLA_KV_RANK:] = qr.astype(BF16)


def _mla_prep(hm, w_mla, nb, t, pos0, inv128, wuk_t, kvn, emit_vt):
    d = hm.shape[1]
    tm = _tile(t, 512)
    nt = t // tm
    rows = lambda b, j: b * nt + j
    fixed2 = lambda b, j: (0, 0)
    vmem = (d * N_MLA * 2 + 2 * (tm * d * 2 + MLA_HEADS * MLA_NOPE * MLA_KV_RANK * 2
                                 + tm * (MLA_KV_RANK + 3 * LANES) * 4 + (MLA_HEADS + 2) * tm * MLA_QK * 2)
            + 2 * tm * N_MLA * 4 + 8 * 2 ** 20)
    out_specs = [pl.BlockSpec((1, tm, MLA_KV_RANK), lambda b, j: (b, j, 0)),
                 pl.BlockSpec((1, tm, MLA_ROPE), lambda b, j: (b, j, 0)),
                 pl.BlockSpec((1, MLA_HEADS, tm, MLA_QK), lambda b, j: (b, 0, j, 0)),
                 pl.BlockSpec((1, tm, MLA_QK), lambda b, j: (b, j, 0)),
                 pl.BlockSpec((tm, 2 * LANES), lambda b, j: (rows(b, j), 0))]
    out_shape = [jax.ShapeDtypeStruct((nb, t, MLA_KV_RANK), F32),
                 jax.ShapeDtypeStruct((nb, t, MLA_ROPE), F32),
                 jax.ShapeDtypeStruct((nb, MLA_HEADS, t, MLA_QK), BF16),
                 jax.ShapeDtypeStruct((nb, t, MLA_QK), BF16),
                 jax.ShapeDtypeStruct((nb * t, 2 * LANES), F32)]
    if emit_vt:
        out_specs.append(pl.BlockSpec((1, MLA_KV_RANK, tm), lambda b, j: (b, 0, j)))
        out_shape.append(jax.ShapeDtypeStruct((nb, MLA_KV_RANK, t), BF16))
    return pl.pallas_call(
        functools.partial(_mla_prep_kernel, tm=tm, pos0=pos0),
        grid=(nb, nt),
        in_specs=[pl.BlockSpec((tm, d), lambda b, j: (rows(b, j), 0)),
                  pl.BlockSpec((d, N_MLA), fixed2, pipeline_mode=pl.Buffered(1)),
                  pl.BlockSpec((1, LANES), fixed2),
                  pl.BlockSpec((MLA_HEADS, MLA_NOPE, MLA_KV_RANK), lambda b, j: (0, 0, 0)),
                  pl.BlockSpec((1, MLA_KV_RANK), fixed2)],
        out_specs=out_specs,
        out_shape=out_shape,
        compiler_params=_params(("parallel", "parallel"), vmem),
        name="mla_proj_prep",
    )(hm, w_mla, inv128, wuk_t, kvn)


ATTN_GROUPS = 2


def _attn_kernel(q_ref, k_ref, vt_ref, wuv_ref, o_ref, s0_ref, s1_ref, m_ref, l_ref, acc_ref,
                 *, tq, tk, n_valid):
    i = pl.program_id(1)
    cols = MLA_HEADS * tq
    cg = cols // ATTN_GROUPS
    q_first = i * tq
    c = lax.broadcasted_iota(jnp.int32, (1, cg), 1)
    qchunk = jnp.right_shift(q_first + jnp.bitwise_and(c, tq - 1), LOG2_CHUNK)
    k_all = jnp.minimum((q_first // CHUNK + 1) * CHUNK, n_valid)
    k_any = jnp.minimum(((q_first + tq - 1) // CHUNK + 1) * CHUNK, n_valid)
    nk = (k_any + tk - 1) // tk
    m_ref[...] = jnp.full(m_ref.shape, -jnp.inf, F32)
    l_ref[...] = jnp.zeros(l_ref.shape, F32)
    acc_ref[...] = jnp.zeros(acc_ref.shape, F32)

    def scores(j, buf):
        start = pl.multiple_of(j * tk, tk)
        q = q_ref[0].reshape(cols, MLA_QK)
        buf[...] = _dot_nt(k_ref[0, pl.ds(start, tk), :], q)

    def fold(x, op):
        while x.shape[0] > SUBLANES:
            half = x.shape[0] // 2
            x = op(x[:half], x[half:])
        return x

    def softmax_values(j, buf, masked):
        start = pl.multiple_of(j * tk, tk)
        vt = vt_ref[0, :, pl.ds(start, tk)]
        if masked:
            kpos = start + lax.broadcasted_iota(jnp.int32, (tk, 1), 0)
            mask = jnp.logical_and(jnp.right_shift(kpos, LOG2_CHUNK) <= qchunk, kpos < n_valid)
        for g in range(ATTN_GROUPS):
            cs = slice(g * cg, (g + 1) * cg)
            s = buf[:, cs]
            if masked:
                s = jnp.where(mask, s, -jnp.inf)
            m_old = m_ref[:, cs]
            m_new = jnp.maximum(m_old, jnp.max(fold(s, jnp.maximum), axis=0, keepdims=True))
            alpha = jnp.exp(m_old - m_new)
            p = jnp.exp(s - m_new)
            l_ref[:, cs] = alpha * l_ref[:, cs] + jnp.sum(fold(p, jnp.add), axis=0, keepdims=True)
            acc_ref[:, cs] = alpha * acc_ref[:, cs] + _dot(vt, p.astype(BF16))
            m_ref[:, cs] = m_new

    def by_parity(j, fn):
        @pl.when(jnp.bitwise_and(j, 1) == 0)
        def _():
            fn(s0_ref, s1_ref)

        @pl.when(jnp.bitwise_and(j, 1) == 1)
        def _():
            fn(s1_ref, s0_ref)

    scores(0, s0_ref)

    def body(j, carry, masked):
        def step(cur, nxt):
            scores(j + 1, nxt)
            softmax_values(j, cur, masked)
        by_parity(j, step)
        return carry

    n_open = jnp.minimum(k_all // tk, nk - 1)
    lax.fori_loop(0, n_open, functools.partial(body, masked=False), 0)
    lax.fori_loop(n_open, nk - 1, functools.partial(body, masked=True), 0)
    by_parity(nk - 1, lambda cur, nxt: softmax_values(nk - 1, cur, True))
    inv_l = 1.0 / l_ref[...]
    for h in range(MLA_HEADS):
        hs = slice(h * tq, (h + 1) * tq)
        lat_t = (acc_ref[:, hs] * inv_l[:, hs]).astype(BF16)
        o_ref[:, h * MLA_V:(h + 1) * MLA_V] = _dot_tn(lat_t, wuv_ref[h]).astype(BF16)


def _attn_cache_kernel(q_ref, ckv_ref, kr_ref, knew_ref, wuv_ref, o_ref, m_ref, l_ref, acc_ref,
                       *, tq, tk, nkb, past):
    j = pl.program_id(1)
    rows = MLA_HEADS * tq
    q = q_ref[0].reshape(rows, MLA_QK)
    r = lax.broadcasted_iota(jnp.int32, (rows, 1), 0)
    qchunk = jnp.right_shift(past + jnp.bitwise_and(r, tq - 1), LOG2_CHUNK)

    @pl.when(j == 0)
    def _():
        m_ref[...] = jnp.full(m_ref.shape, -jnp.inf, F32)
        l_ref[...] = jnp.zeros(l_ref.shape, F32)
        acc_ref[...] = jnp.zeros(acc_ref.shape, F32)

    def fold(x, op):
        while x.shape[1] > LANES:
            half = x.shape[1] // 2
            x = op(x[:, :half], x[:, half:])
        return x

    def update(s, kpos, vals):
        s = jnp.where(jnp.right_shift(kpos, LOG2_CHUNK) <= qchunk, s, -jnp.inf)
        m_old = m_ref[...]
        m_new = jnp.maximum(m_old, jnp.max(fold(s, jnp.maximum), axis=-1, keepdims=True))
        alpha = jnp.exp(m_old - m_new)
        p = jnp.exp(s - m_new)
        l_ref[...] = alpha * l_ref[...] + jnp.sum(fold(p, jnp.add), axis=-1, keepdims=True)
        acc_ref[...] = alpha * acc_ref[...] + _dot(p.astype(BF16), vals)
        m_ref[...] = m_new

    ck = ckv_ref[0].astype(BF16)
    s = (_dot_nt(q[:, :MLA_KV_RANK], ck)
         + _dot_nt(q[:, MLA_KV_RANK:MLA_KV_RANK + MLA_ROPE], kr_ref[0].astype(BF16)))
    update(s, j * tk + lax.broadcasted_iota(jnp.int32, (1, tk), 1), ck)

    @pl.when(j == nkb - 1)
    def _():
        kn = knew_ref[0]
        update(_dot_nt(q, kn), past + lax.broadcasted_iota(jnp.int32, (1, tq), 1), kn[:, :MLA_KV_RANK])
        inv_l = 1.0 / l_ref[...]
        for h in range(MLA_HEADS):
            hs = slice(h * tq, (h + 1) * tq)
            lat = (acc_ref[hs, :] * inv_l[hs, :]).astype(BF16)
            o_ref[:, h * MLA_V:(h + 1) * MLA_V] = _dot(lat, wuv_ref[h]).astype(BF16)


def _attention(qcat, kcat, vt, wuv_t):
    nb, _, t, _ = qcat.shape
    tq = _tile(t, 128)
    tk = _tile(t, 512)
    assert tq & (tq - 1) == 0
    nq = t // tq
    cols = MLA_HEADS * tq
    vmem = (2 * (cols * MLA_QK * 2 + t * MLA_QK * 2 + MLA_KV_RANK * t * 2
                 + MLA_HEADS * MLA_KV_RANK * MLA_V * 2 + tq * MLA_HEADS * MLA_V * 2)
            + 2 * tk * cols * 4 + 2 * SUBLANES * cols * 4 + MLA_KV_RANK * cols * 4
            + 3 * tk * cols * 4 // ATTN_GROUPS + 4 * 2 ** 20)
    return pl.pallas_call(
        functools.partial(_attn_kernel, tq=tq, tk=tk, n_valid=t),
        grid=(nb, nq),
        in_specs=[pl.BlockSpec((1, MLA_HEADS, tq, MLA_QK), lambda b, i: (b, 0, i, 0)),
                  pl.BlockSpec((1, t, MLA_QK), lambda b, i: (b, 0, 0)),
                  pl.BlockSpec((1, MLA_KV_RANK, t), lambda b, i: (b, 0, 0)),
                  pl.BlockSpec((MLA_HEADS, MLA_KV_RANK, MLA_V), lambda b, i: (0, 0, 0))],
        out_specs=pl.BlockSpec((tq, MLA_HEADS * MLA_V), lambda b, i: (b * nq + i, 0)),
        out_shape=jax.ShapeDtypeStruct((nb * t, MLA_HEADS * MLA_V), BF16),
        scratch_shapes=[pltpu.VMEM((tk, cols), F32), pltpu.VMEM((tk, cols), F32), pltpu.VMEM((1, cols), F32),
                        pltpu.VMEM((1, cols), F32), pltpu.VMEM((MLA_KV_RANK, cols), F32)],
        compiler_params=_params(("parallel", "arbitrary"), vmem),
        name="mla_attention",
    )(qcat, kcat, vt, wuv_t)


def _attention_cached(qcat, ckv_past, krope_past, kcat_new, wuv_t):
    nb, _, t, _ = qcat.shape
    past = ckv_past.shape[1]
    tk = _tile(past, 1024)
    nkb = past // tk
    assert t & (t - 1) == 0
    rows = MLA_HEADS * t
    vmem = (2 * (rows * MLA_QK * 2 + tk * MLA_KV_RANK * 4 + tk * LANES * 4 + t * MLA_QK * 2
                 + MLA_HEADS * MLA_KV_RANK * MLA_V * 2 + t * MLA_HEADS * MLA_V * 2)
            + 2 * rows * LANES * 4 + rows * MLA_KV_RANK * 4 + tk * MLA_QK * 2 + 4 * rows * tk * 4 + 4 * 2 ** 20)
    return pl.pallas_call(
        functools.partial(_attn_cache_kernel, tq=t, tk=tk, nkb=nkb, past=past),
        grid=(nb, nkb),
        in_specs=[pl.BlockSpec((1, MLA_HEADS, t, MLA_QK), lambda b, j: (b, 0, 0, 0)),
                  pl.BlockSpec((1, tk, MLA_KV_RANK), lambda b, j: (b, j, 0)),
                  pl.BlockSpec((1, tk, MLA_ROPE), lambda b, j: (b, j, 0)),
                  pl.BlockSpec((1, t, MLA_QK), lambda b, j: (b, 0, 0)),
                  pl.BlockSpec((MLA_HEADS, MLA_KV_RANK, MLA_V), lambda b, j: (0, 0, 0))],
        out_specs=pl.BlockSpec((t, MLA_HEADS * MLA_V), lambda b, j: (b, 0)),
        out_shape=jax.ShapeDtypeStruct((nb * t, MLA_HEADS * MLA_V), BF16),
        scratch_shapes=[pltpu.VMEM((rows, 1), F32), pltpu.VMEM((rows, 1), F32),
                        pltpu.VMEM((rows, MLA_KV_RANK), F32)],
        compiler_params=_params(("parallel", "arbitrary"), vmem),
        name="mla_attention_cached",
    )(qcat, ckv_past, krope_past, kcat_new, wuv_t)


def _merge_out_kernel(og_ref, om_ref, wg_ref, wm_ref, gg_ref, gm_ref, wo_ref, x_ref, gpost_ref, gnext_ref,
                      xo_ref, xno_ref):
    tg = _dot(og_ref[...], wg_ref[...])
    tm_ = _dot(om_ref[...], wm_ref[...])
    merged = (jax.nn.sigmoid(gg_ref[...].astype(F32)) * tg
              + jax.nn.sigmoid(gm_ref[...].astype(F32)) * tm_).astype(BF16)
    y = _dot(merged, wo_ref[...])
    xnew = x_ref[...] + _rms(y, gpost_ref[...])
    xo_ref[...] = xnew
    xno_ref[...] = _rms(xnew, gnext_ref[...]).astype(BF16)


def _merge_out(og, om, wbg, wbm, proj_gate, wout, x, gpost, gnext):
    m, kg = og.shape
    d = wbg.shape[1]
    tm = _tile(m, 256)
    gsz = proj_gate.dtype.itemsize
    row = lambda i: (i, 0)
    fixed = lambda i: (0, 0)
    resident = dict(pipeline_mode=pl.Buffered(1))
    vmem = ((2 * kg * d + d * d) * 2
            + 2 * (2 * tm * kg * 2 + 2 * tm * d * gsz + 2 * tm * d * 4 + tm * d * 2)
            + 4 * tm * d * 4 + 2 * 2 ** 20)
    return pl.pallas_call(
        _merge_out_kernel,
        grid=(m // tm,),
        in_specs=[pl.BlockSpec((tm, kg), row),
                  pl.BlockSpec((tm, kg), row),
                  pl.BlockSpec((kg, d), fixed, **resident),
                  pl.BlockSpec((kg, d), fixed, **resident),
                  pl.BlockSpec((tm, d), lambda i: (i, 0)),
                  pl.BlockSpec((tm, d), lambda i: (i, 1)),
                  pl.BlockSpec((d, d), fixed, **resident),
                  pl.BlockSpec((tm, d), row),
                  pl.BlockSpec((1, d), fixed),
                  pl.BlockSpec((1, d), fixed)],
        out_specs=[pl.BlockSpec((tm, d), row), pl.BlockSpec((tm, d), row)],
        out_shape=[jax.ShapeDtypeStruct((m, d), F32), jax.ShapeDtypeStruct((m, d), BF16)],
        compiler_params=_params(("parallel",), vmem),
        name="merge_out",
    )(og, om, wbg, wbm, proj_gate, proj_gate, wout, x, gpost, gnext)


def _layer(x3, pos0, conv_state, ssm0, ckv_past, krope_past, w):
    nb, t, d = x3.shape
    m = nb * t
    x = x3.reshape(m, d)
    x1, hm, ffn1_w = _ffn(x, *w["ffn1_w"], w["ffn1_norm_post"],
                          gpre=w["ffn1_norm_pre"], gnext=w["mix_norm_pre"])
    proj_gdn = _matmul(hm, w["w_in_gdn"], F32)
    proj_gate = _matmul(hm, w["w_in_gate"], BF16)
    ckv, krope, qcat, kcat, ba, *vt = _mla_prep(hm, w["w_in_mla"], nb, t, pos0, w["inv128"], w["wuk_t"],
                                                w["kv_norm"], ckv_past is None)

    o_gdn, ssm_new = _gdn(proj_gdn, ba, nb, t, conv_state, ssm0, w["conv_w"], w["alog"], w["dtb"],
                          w["gdn_nw"])
    conv_new = proj_gdn.reshape(nb, t, -1)[:, t - (GDN_CONV - 1):, COL_QKV:COL_QKV + GDN_CONV_DIM]

    if ckv_past is None:
        o_mla = _attention(qcat, kcat, vt[0], w["wuv_t"])
    else:
        o_mla = _attention_cached(qcat, ckv_past, krope_past, kcat, w["wuv_t"])

    x2, xn2 = _merge_out(o_gdn, o_mla, w["w_br_gdn"], w["w_br_mla"], proj_gate, w["w_out"], x1,
                         w["mix_norm_post"], w["ffn2_norm_pre"])
    x3_, _, ffn2_w = _ffn(x2, *w["ffn2_w"], w["ffn2_norm_post"], xn=xn2)
    return (x3_.reshape(nb, t, d), conv_new, ssm_new, ckv, krope), dict(ffn1_w=ffn1_w, ffn2_w=ffn2_w)


def _prep_weights(l, d_model, **p):
    row = lambda v: v[l].reshape(1, -1).astype(F32)
    pad_lanes = lambda v: jnp.pad(v[l].reshape(1, -1).astype(F32), ((0, 0), (0, LANES - v.shape[1])))
    inv = ROPE_THETA ** (-jnp.arange(0, MLA_ROPE, 2, dtype=F32) / MLA_ROPE)
    w_in_gdn, w_in_gate, w_in_mla = _build_w_in(p["w_in"][l], d_model)
    w = dict(
        w_in_gdn=w_in_gdn, w_in_gate=w_in_gate, w_in_mla=w_in_mla,
        ffn1_norm_pre=row(p["ffn1_norm_pre"]), ffn1_norm_post=row(p["ffn1_norm_post"]),
        mix_norm_pre=row(p["mix_norm_pre"]), mix_norm_post=row(p["mix_norm_post"]),
        ffn2_norm_pre=row(p["ffn2_norm_pre"]), ffn2_norm_post=row(p["ffn2_norm_post"]),
        ffn1_w=(p["ffn1_w_gate"][l], p["ffn1_w_up"][l], p["ffn1_w_down"][l]),
        ffn2_w=(p["ffn2_w_gate"][l], p["ffn2_w_up"][l], p["ffn2_w_down"][l]),
        conv_w=p["gdn_conv_w"][l].astype(F32),
        alog=pad_lanes(p["gdn_a_log"]), dtb=pad_lanes(p["gdn_dt_bias"]),
        gdn_nw=row(p["gdn_norm_w"]), kv_norm=row(p["mla_kv_norm"]),
        inv128=jnp.tile(inv, LANES // inv.shape[0]).reshape(1, LANES),
        wuk_t=jnp.transpose(p["mla_w_uk"][l], (1, 2, 0)).astype(BF16),
        wuv_t=jnp.transpose(p["mla_w_uv"][l], (1, 0, 2)).astype(BF16),
        w_br_gdn=p["w_br_gdn"][l].astype(BF16), w_br_mla=p["w_br_mla"][l].astype(BF16),
        w_out=p["w_out"][l].astype(BF16),
    )
    return w


def kernel(x_prompt, x_sample, state_gdn_conv, state_gdn_ssm, cache_mla_ckv, cache_mla_krope, ffn1_norm_pre, ffn1_w_gate, ffn1_w_up, ffn1_w_down, ffn1_norm_post, mix_norm_pre, w_in, gdn_conv_w, gdn_a_log, gdn_dt_bias, gdn_norm_w, mla_kv_norm, mla_w_uk, mla_w_uv, w_br_gdn, w_br_mla, w_out, mix_norm_post, ffn2_norm_pre, ffn2_w_gate, ffn2_w_up, ffn2_w_down, ffn2_norm_post):
    params = dict(
        ffn1_norm_pre=ffn1_norm_pre, ffn1_w_gate=ffn1_w_gate, ffn1_w_up=ffn1_w_up,
        ffn1_w_down=ffn1_w_down, ffn1_norm_post=ffn1_norm_post, mix_norm_pre=mix_norm_pre,
        w_in=w_in, gdn_conv_w=gdn_conv_w, gdn_a_log=gdn_a_log, gdn_dt_bias=gdn_dt_bias,
        gdn_norm_w=gdn_norm_w, mla_kv_norm=mla_kv_norm, mla_w_uk=mla_w_uk, mla_w_uv=mla_w_uv,
        w_br_gdn=w_br_gdn, w_br_mla=w_br_mla, w_out=w_out, mix_norm_post=mix_norm_post,
        ffn2_norm_pre=ffn2_norm_pre, ffn2_w_gate=ffn2_w_gate, ffn2_w_up=ffn2_w_up,
        ffn2_w_down=ffn2_w_down, ffn2_norm_post=ffn2_norm_post)
    depth = w_in.shape[0]
    d_model = x_prompt.shape[-1]
    b_p = x_prompt.shape[0]
    past = cache_mla_ckv.shape[2]
    yp, ys = x_prompt, x_sample
    outs_p, outs_s = [], []
    for l in range(depth):
        w = _prep_weights(l, d_model, **params)
        (ys, *rest_s), w_bf16 = _layer(
            ys, past, state_gdn_conv[l], state_gdn_ssm[l], cache_mla_ckv[l], cache_mla_krope[l], w)
        (yp, *rest_p), _ = _layer(
            yp, 0, jnp.zeros((b_p, GDN_CONV - 1, GDN_CONV_DIM), F32),
            jnp.zeros((b_p, GDN_HEADS, GDN_DK, GDN_DV), F32), None, None, {**w, **w_bf16})
        outs_p.append(rest_p)
        outs_s.append(rest_s)
    stack = lambda outs, i: jnp.stack([o[i] for o in outs])
    return (yp, ys,
            stack(outs_p, 0), stack(outs_p, 1), stack(outs_p, 2), stack(outs_p, 3),
            stack(outs_s, 0), stack(outs_s, 1), stack(outs_s, 2), stack(outs_s, 3))
```

```python
import functools

import jax
import jax.numpy as jnp
from jax import lax
from jax.experimental import pallas as pl
from jax.experimental.pallas import tpu as pltpu

F32 = jnp.float32
BF16 = jnp.bfloat16

CHUNK = 64
LOG2_CHUNK = 6
NORM_EPS = 1e-6
GDN_HEADS = 8
GDN_DK = 128
GDN_DV = 128
GDN_CONV = 4
GDN_KEY_DIM = GDN_HEADS * GDN_DK
GDN_VAL_DIM = GDN_HEADS * GDN_DV
GDN_CONV_DIM = 2 * GDN_KEY_DIM + GDN_VAL_DIM
MLA_HEADS = 8
MLA_NOPE = 128
MLA_ROPE = 64
MLA_V = 128
MLA_KV_RANK = 512
MLA_SCALE = (MLA_NOPE + MLA_ROPE) ** -0.5
ROPE_THETA = 10000.0

LANES = 128
SUBLANES = 8
VMEM_CAP_BYTES = 56 * 2 ** 20
MLA_QK = MLA_KV_RANK + LANES
HI = lax.Precision.HIGHEST


def _params(semantics, vmem_bytes):
    limit = int(min(max(vmem_bytes, 16 * 2 ** 20), VMEM_CAP_BYTES))
    return pltpu.CompilerParams(dimension_semantics=semantics, vmem_limit_bytes=limit)


def _dot(a, b, prec=None):
    return lax.dot_general(a, b, (((1,), (0,)), ((), ())), precision=prec,
                           preferred_element_type=F32)


def _dot_nt(a, b, prec=None):
    return lax.dot_general(a, b, (((1,), (1,)), ((), ())), precision=prec,
                           preferred_element_type=F32)


def _dot_tn(a, b, prec=None):
    return lax.dot_general(a, b, (((0,), (0,)), ((), ())), precision=prec,
                           preferred_element_type=F32)


def _rms(y, g):
    return y * lax.rsqrt(jnp.mean(y * y, axis=-1, keepdims=True) + NORM_EPS) * g


def _silu(x):
    return x * jax.nn.sigmoid(x)


def _tile(n, pref):
    t = min(n, pref)
    assert n % t == 0, (n, pref)
    return t


def _ffn_kernel(*refs, nf, norm_in, emit_next, cast_w):
    refs = list(refs)
    xn_ref = None if norm_in else refs.pop(0)
    x_ref, wg_ref, wu_ref, wd_ref = refs[:4]
    del refs[:4]
    gpre_ref = refs.pop(0) if norm_in else None
    gpost_ref = refs.pop(0)
    gnext_ref = refs.pop(0) if emit_next else None
    xo_ref = refs.pop(0)
    xno_ref = refs.pop(0) if emit_next else None
    if cast_w:
        wgo_ref, wuo_ref, wdo_ref = refs[:3]
        del refs[:3]
    acc_ref = refs.pop(0)
    if norm_in:
        xn_ref = refs.pop(0)
    f = pl.program_id(1)

    @pl.when(f == 0)
    def _():
        acc_ref[...] = jnp.zeros(acc_ref.shape, F32)
        if norm_in:
            xn_ref[...] = _rms(x_ref[...], gpre_ref[...]).astype(BF16)

    xn = xn_ref[...]
    wg, wu, wd = wg_ref[...], wu_ref[...], wd_ref[...]
    if cast_w:
        wg, wu, wd = wg.astype(BF16), wu.astype(BF16), wd.astype(BF16)
        wgo_ref[...] = wg
        wuo_ref[...] = wu
        wdo_ref[...] = wd
    gate = _dot(xn, wg)
    up = _dot(xn, wu)
    h = (_silu(gate) * up).astype(BF16)
    acc_ref[...] += _dot(h, wd)

    @pl.when(f == nf - 1)
    def _():
        xnew = x_ref[...] + 0.5 * _rms(acc_ref[...], gpost_ref[...])
        xo_ref[...] = xnew
        if emit_next:
            xno_ref[...] = _rms(xnew, gnext_ref[...]).astype(BF16)


def _ffn(x, wg, wu, wd, gpost, xn=None, gpre=None, gnext=None):
    norm_in = xn is None
    emit_next = gnext is not None
    cast_w = wg.dtype == F32
    assert norm_in == (gpre is not None)
    m, d = x.shape
    dff = wg.shape[1]
    tm = _tile(m, 512)
    tf = _tile(dff, 256 if cast_w else 512)
    nf = dff // tf
    assert not cast_w or m == tm
    wsz = wg.dtype.itemsize
    row = lambda i, f: (i, 0)
    vec = pl.BlockSpec((1, d), lambda i, f: (0, 0))
    w_specs = [pl.BlockSpec((d, tf), lambda i, f: (0, f)),
               pl.BlockSpec((d, tf), lambda i, f: (0, f)),
               pl.BlockSpec((tf, d), lambda i, f: (f, 0))]
    ins = [] if norm_in else [xn]
    in_specs = [] if norm_in else [pl.BlockSpec((tm, d), row)]
    ins += [x, wg, wu, wd]
    in_specs += [pl.BlockSpec((tm, d), row)] + w_specs
    for g in (gpre, gpost, gnext):
        if g is not None:
            ins.append(g)
            in_specs.append(vec)
    out_specs = [pl.BlockSpec((tm, d), row)]
    out_shape = [jax.ShapeDtypeStruct((m, d), F32)]
    scratch = [pltpu.VMEM((tm, d), F32)]
    if emit_next:
        out_specs.append(pl.BlockSpec((tm, d), row))
        out_shape.append(jax.ShapeDtypeStruct((m, d), BF16))
    if cast_w:
        out_specs += w_specs
        out_shape += [jax.ShapeDtypeStruct(w.shape, BF16) for w in (wg, wu, wd)]
    if norm_in:
        scratch.append(pltpu.VMEM((tm, d), BF16))
    vmem = (2 * (tm * d * 2 + tm * d * 4 + 3 * d * tf * wsz + tm * d * 4 + tm * d * 2)
            + (2 * 3 * d * tf * 2 + 3 * d * tf * 2 if cast_w else 0)
            + tm * d * 4 + 3 * tm * tf * 4 + 2 * tm * d * 4)
    res = pl.pallas_call(
        functools.partial(_ffn_kernel, nf=nf, norm_in=norm_in, emit_next=emit_next, cast_w=cast_w),
        grid=(m // tm, nf),
        in_specs=in_specs,
        out_specs=out_specs,
        out_shape=out_shape,
        scratch_shapes=scratch,
        compiler_params=_params(("parallel", "arbitrary"), vmem),
        name="ffn_cast" if cast_w else "ffn",
    )(*ins)
    res = list(res)
    x_new = res.pop(0)
    xn_next = res.pop(0) if emit_next else None
    return x_new, xn_next, (tuple(res) if cast_w else (wg, wu, wd))


def _mm_kernel(a_ref, w_ref, o_ref, *maybe_wo_ref):
    w = w_ref[...]
    if maybe_wo_ref:
        w = w.astype(BF16)
        maybe_wo_ref[0][...] = w
    o_ref[...] = _dot(a_ref[...], w).astype(o_ref.dtype)


def _matmul(a, w, out_dtype, ncols=None):
    m, k = a.shape
    n = w.shape[1] if ncols is None else ncols
    cast_w = w.dtype == F32
    tm = _tile(m, 1024)
    tn = _tile(n, 1024)
    assert not cast_w or m == tm
    osz = jnp.dtype(out_dtype).itemsize
    w_spec = pl.BlockSpec((k, tn), lambda i, j: (0, j))
    out_specs = [pl.BlockSpec((tm, tn), lambda i, j: (i, j))]
    out_shape = [jax.ShapeDtypeStruct((m, n), out_dtype)]
    if cast_w:
        out_specs.append(w_spec)
        out_shape.append(jax.ShapeDtypeStruct((k, n), BF16))
    vmem = (2 * (tm * k * 2 + k * tn * w.dtype.itemsize + tm * tn * osz + (k * tn * 2 if cast_w else 0))
            + tm * tn * 4 + (k * tn * 2 if cast_w else 0))
    res = pl.pallas_call(
        _mm_kernel,
        grid=(m // tm, n // tn),
        in_specs=[pl.BlockSpec((tm, k), lambda i, j: (i, 0)), w_spec],
        out_specs=out_specs,
        out_shape=out_shape,
        compiler_params=_params(("parallel", "arbitrary"), vmem),
        name="in_proj_cast" if cast_w else "in_proj",
    )(a, w)
    return res[0], (res[1] if cast_w else w)


COL_QKV = 0
COL_Z = COL_QKV + GDN_CONV_DIM
COL_QNOPE = 0
COL_QROPE = COL_QNOPE + MLA_HEADS * MLA_NOPE
COL_CKV = COL_QROPE + MLA_HEADS * LANES
COL_KR = COL_CKV + MLA_KV_RANK
COL_B = COL_KR + LANES
COL_A = COL_B + LANES
N_MLA = 3072
N_GDN = GDN_CONV_DIM + GDN_VAL_DIM


def _build_w_in(w_in, d_model):
    sizes = (GDN_CONV_DIM, GDN_VAL_DIM, GDN_HEADS, GDN_HEADS, MLA_HEADS * (MLA_NOPE + MLA_ROPE),
             MLA_KV_RANK, MLA_ROPE, d_model, d_model)
    offs = [0]
    for s in sizes:
        offs.append(offs[-1] + s)
    assert offs[2] == N_GDN
    w_gdn = w_in
    w_gate = w_in[:, offs[7]:]
    part = lambda i: w_in[:, offs[i]:offs[i + 1]].astype(BF16)
    b, a, qm, ckv, kr = (part(i) for i in range(2, 7))
    d = w_in.shape[0]
    half = MLA_ROPE // 2
    qm = qm.reshape(d, MLA_HEADS, MLA_NOPE + MLA_ROPE)
    qn = qm[:, :, :MLA_NOPE].reshape(d, MLA_HEADS * MLA_NOPE)
    qr = qm[:, :, MLA_NOPE:]
    qr_pair = jnp.concatenate([qr, qr[:, :, half:], qr[:, :, :half]], axis=2).reshape(d, MLA_HEADS * LANES)
    kr_pair = jnp.concatenate([kr, kr[:, half:], kr[:, :half]], axis=1)
    pad8 = lambda w: jnp.pad(w, ((0, 0), (0, LANES - w.shape[1])))
    w_mla = jnp.concatenate([qn, qr_pair, ckv, kr_pair, pad8(b), pad8(a)], axis=1)
    w_mla = jnp.pad(w_mla, ((0, 0), (0, N_MLA - w_mla.shape[1])))
    return w_gdn, w_gate, w_mla


_NN = (((1,), (0,)), ((), ()))
_NT = (((1,), (1,)), ((), ()))
_TN = (((0,), (0,)), ((), ()))
GDN_MODE = dict(kk="bf16", inv="bf16", sol="bf16", state="bf16")
INV_BLOCK = 16


def _mm(a, b, mode, dims=_NN):
    if mode == "f32":
        return lax.dot_general(a, b, dims, precision=HI, preferred_element_type=F32)
    ah = a.astype(BF16)
    bh = b.astype(BF16)
    out = lax.dot_general(ah, bh, dims, preferred_element_type=F32)
    if mode == "bf16x3":
        al = (a - ah.astype(F32)).astype(BF16)
        bl = (b - bh.astype(F32)).astype(BF16)
        out = (out + lax.dot_general(ah, bl, dims, preferred_element_type=F32)
               + lax.dot_general(al, bh, dims, preferred_element_type=F32))
    return out


def _gdn_kernel(qkv_ref, z_ref, b_ref, a_ref, cst_ref, cw_ref, alog_ref, dtb_ref, nw_ref, s0_ref,
                o_ref, sout_ref, past_ref, s_ref, *, G, C, nlev, nc):
    c = pl.program_id(1)

    @pl.when(c == 0)
    def _():
        past_ref[...] = cst_ref[...]
        s_ref[...] = s0_ref[...]

    ri = lax.broadcasted_iota(jnp.int32, (C, C), 0)
    ci = lax.broadcasted_iota(jnp.int32, (C, C), 1)
    incl = ri >= ci
    strict = ri > ci
    eye = (ri == ci).astype(F32)
    log2_blk = INV_BLOCK.bit_length() - 1
    bdiag = jnp.right_shift(ri, log2_blk) == jnp.right_shift(ci, log2_blk)

    first_row = lax.broadcasted_iota(jnp.int32, (C, 1), 0) == 0

    def conv(s, col):
        cols = slice(col, col + LANES)
        x = qkv_ref[s, :, cols]
        past = past_ref[s, :, cols]
        w = [cw_ref[j:j + 1, cols] for j in range(GDN_CONV)]
        acc = w[0] * x
        for j in range(1, GDN_CONV):
            top = sum(w[i] * past[SUBLANES - 1 - (j - 1 - i):SUBLANES - (j - 1 - i)] for i in range(j))
            acc = w[j] * x + jnp.where(first_row, top, pltpu.roll(acc, 1, axis=0))
        return _silu(acc)

    units = [(s, h) for s in range(G) for h in range(GDN_HEADS)]
    every = lambda fn, *lists: [fn(*args) for args in zip(*lists)]
    mm_inv = lambda a, b: _mm(a, b, GDN_MODE["inv"])
    beta_all, gcum, gcum_t = [], [], []
    for s in range(G):
        beta_all.append(jax.nn.sigmoid(b_ref[s]))
        a_all = a_ref[s] + dtb_ref[...]
        softplus = jnp.maximum(a_all, 0.0) + jnp.log(1.0 + jnp.exp(-jnp.abs(a_all)))
        g_all = -jnp.exp(alog_ref[...]) * softplus
        gcum.append(_dot(incl.astype(F32), g_all, HI))
        gcum_t.append(_dot_tn(g_all, (ci >= ri).astype(F32), HI))

    def l2n(x):
        return x * lax.rsqrt(jnp.sum(x * x, axis=-1, keepdims=True) + 1e-6)

    q = [l2n(conv(s, h * GDN_DK)) * (GDN_DK ** -0.5) for s, h in units]
    k = [l2n(conv(s, GDN_KEY_DIM + h * GDN_DK)) for s, h in units]
    v = [conv(s, 2 * GDN_KEY_DIM + h * GDN_DV) for s, h in units]
    bcol = [beta_all[s][:, h:h + 1] for s, h in units]
    gcol = [gcum[s][:, h:h + 1] for s, h in units]
    glast = [gcum[s][C - 1:C, h:h + 1] for s, h in units]
    decay = [jnp.where(incl, jnp.exp(jnp.where(incl, gcum[s][:, h:h + 1] - gcum_t[s][h:h + 1, :], 0.0)), 0.0)
             for s, h in units]
    qkk = every(lambda q_, k_: _mm(jnp.concatenate([q_, k_], axis=0), k_, GDN_MODE["kk"], _NT), q, k)
    qk = every(lambda x, d: x[:C] * d, qkk, decay)
    nmat = every(lambda x, b_, d: -jnp.where(strict, b_ * x[C:] * d, 0.0), qkk, bcol, decay)
    ndiag = every(lambda n_: jnp.where(bdiag, n_, 0.0), nmat)
    tinv = every(lambda n_: eye + n_, ndiag)
    npow = ndiag
    for _ in range(INV_BLOCK.bit_length() - 2):
        npow = every(mm_inv, npow, npow)
        tinv = every(lambda t_, p_: t_ + mm_inv(t_, p_), tinv, npow)
    mpow = every(lambda t_, n_, d_: mm_inv(t_, n_ - d_), tinv, nmat, ndiag)
    for lev in range(nlev):
        if lev:
            mpow = every(mm_inv, mpow, mpow)
        tinv = every(lambda t_, m_: t_ + mm_inv(m_, t_), tinv, mpow)
    egc = every(jnp.exp, gcol)
    sol = every(lambda t_, v_, k_, b_, e_: _mm(t_, jnp.concatenate([v_ * b_, k_ * (b_ * e_)], axis=1),
                                               GDN_MODE["sol"]), tinv, v, k, bcol, egc)
    s_old = [s_ref[s, h] for s, h in units]
    ws_qs = every(lambda x, q_, e_, s_: _mm(jnp.concatenate([x[:, GDN_DV:], q_ * e_], axis=0), s_,
                                            GDN_MODE["state"]), sol, q, egc, s_old)
    v_new = every(lambda x, y: x[:, :GDN_DV] - y[:C], sol, ws_qs)
    o = every(lambda y, a_, vn: y[C:] + _mm(a_, vn, GDN_MODE["state"]), ws_qs, qk, v_new)
    s_new = every(lambda s_, gl, k_, gc, vn: s_ * jnp.exp(gl) + _mm(k_ * jnp.exp(gl - gc), vn, GDN_MODE["state"], _TN),
                  s_old, glast, k, gcol, v_new)
    for (s, h), sn, o_ in zip(units, s_new, o):
        s_ref[s, h] = sn
        zz = z_ref[s, :, h * GDN_DV:(h + 1) * GDN_DV]
        o_ref[s, :, h * GDN_DV:(h + 1) * GDN_DV] = (_rms(o_, nw_ref[...]) * _silu(zz)).astype(BF16)

    past_ref[...] = qkv_ref[:, C - SUBLANES:C, :]

    @pl.when(c == nc - 1)
    def _():
        sout_ref[...] = s_ref[...]


def _gdn(proj_gdn, ba, nb, t, conv_state, ssm0, conv_w, alog, dtb, nw):
    C = min(CHUNK, t)
    G = 2 if nb % 2 == 0 else 1
    assert t % C == 0 and C & (C - 1) == 0 and C % INV_BLOCK == 0
    nc = t // C
    nlev = (C // INV_BLOCK).bit_length() - 1
    pg = proj_gdn.reshape(nb, t, proj_gdn.shape[1])
    pm = ba.reshape(nb, t, ba.shape[1])
    fixed = lambda b, c: (0, 0)
    cst = jnp.pad(conv_state, ((0, 0), (SUBLANES - (GDN_CONV - 1), 0), (0, 0)))
    vmem = G * (2 * (C * GDN_CONV_DIM * 4 + C * GDN_VAL_DIM * 4 + 2 * C * LANES * 4 + SUBLANES * GDN_CONV_DIM * 4
                     + C * GDN_VAL_DIM * 2 + 2 * GDN_HEADS * GDN_DK * GDN_DV * 4)
                + SUBLANES * GDN_CONV_DIM * 4 + GDN_HEADS * GDN_DK * GDN_DV * 4) + 16 * 2 ** 20
    o, s_new = pl.pallas_call(
        functools.partial(_gdn_kernel, G=G, C=C, nlev=nlev, nc=nc),
        grid=(nb // G, nc),
        in_specs=[pl.BlockSpec((G, C, GDN_CONV_DIM), lambda b, c: (b, c, COL_QKV // GDN_CONV_DIM)),
                  pl.BlockSpec((G, C, GDN_VAL_DIM), lambda b, c: (b, c, COL_Z // GDN_VAL_DIM)),
                  pl.BlockSpec((G, C, LANES), lambda b, c: (b, c, 0)),
                  pl.BlockSpec((G, C, LANES), lambda b, c: (b, c, 1)),
                  pl.BlockSpec((G, SUBLANES, GDN_CONV_DIM), lambda b, c: (b, 0, 0)),
                  pl.BlockSpec((GDN_CONV, GDN_CONV_DIM), fixed),
                  pl.BlockSpec((1, LANES), fixed),
                  pl.BlockSpec((1, LANES), fixed),
                  pl.BlockSpec((1, GDN_DV), fixed),
                  pl.BlockSpec((G, GDN_HEADS, GDN_DK, GDN_DV), lambda b, c: (b, 0, 0, 0))],
        out_specs=[pl.BlockSpec((G, C, GDN_VAL_DIM), lambda b, c: (b, c, 0)),
                   pl.BlockSpec((G, GDN_HEADS, GDN_DK, GDN_DV), lambda b, c: (b, 0, 0, 0))],
        out_shape=[jax.ShapeDtypeStruct((nb, t, GDN_VAL_DIM), BF16),
                   jax.ShapeDtypeStruct((nb, GDN_HEADS, GDN_DK, GDN_DV), F32)],
        scratch_shapes=[pltpu.VMEM((G, SUBLANES, GDN_CONV_DIM), F32),
                        pltpu.VMEM((G, GDN_HEADS, GDN_DK, GDN_DV), F32)],
        compiler_params=_params(("parallel", "arbitrary"), vmem),
        name="gdn",
    )(pg, pg, pm, pm, cst, conv_w, alog, dtb, nw, ssm0)
    return o.reshape(nb * t, GDN_VAL_DIM), s_new


def _mla_prep_kernel(hm_ref, w_ref, inv_ref, wuk_ref, kvn_ref,
                     ckv_o, kr_o, qcat_o, kcat_o, ba_o, *maybe_vt_o, tm, pos0):
    j = pl.program_id(1)
    proj = _dot(hm_ref[...], w_ref[...])
    ba_o[...] = proj[:, COL_B:COL_A + LANES]
    pos = (pos0 + j * tm + lax.broadcasted_iota(jnp.int32, (tm, LANES), 0)).astype(F32)
    lane = lax.broadcasted_iota(jnp.int32, (tm, LANES), 1)
    ang = pos * inv_ref[...]
    cos = jnp.cos(ang)
    sin = jnp.sin(ang)
    half = MLA_ROPE // 2
    cs = jnp.where(lane < MLA_ROPE, cos, jnp.where(lane < MLA_ROPE + half, -sin, sin))
    keep = lane < MLA_ROPE

    def rope(pair):
        prod = pair * cs
        return jnp.where(keep, prod + pltpu.roll(prod, MLA_ROPE, axis=1), 0.0)

    ckv = _rms(proj[:, COL_CKV:COL_CKV + MLA_KV_RANK], kvn_ref[...])
    ckv_o[0] = ckv
    kr = rope(proj[:, COL_KR:COL_KR + LANES])
    kr_o[0] = kr[:, :MLA_ROPE]
    kcat_o[0, :, :MLA_KV_RANK] = ckv.astype(BF16)
    kcat_o[0, :, MLA_KV_RANK:] = kr.astype(BF16)
    if maybe_vt_o:
        maybe_vt_o[0][0] = ckv.T.astype(BF16)
    for h in range(MLA_HEADS):
        qn = proj[:, COL_QNOPE + h * MLA_NOPE:COL_QNOPE + (h + 1) * MLA_NOPE].astype(BF16)
        qlat = _dot(qn, wuk_ref[h]) * MLA_SCALE
        qcat_o[0, h, :, :MLA_KV_RANK] = qlat.astype(BF16)
        qr = rope(proj[:, COL_QROPE + h * LANES:COL_QROPE + (h + 1) * LANES]) * MLA_SCALE
        qcat_o[0, h, :, MLA_KV_RANK:] = qr.astype(BF16)


def _mla_prep(hm, w_mla, nb, t, pos0, inv128, wuk_t, kvn, emit_vt):
    d = hm.shape[1]
    tm = _tile(t, 512)
    nt = t // tm
    rows = lambda b, j: b * nt + j
    fixed2 = lambda b, j: (0, 0)
    vmem = (d * N_MLA * 2 + 2 * (tm * d * 2 + MLA_HEADS * MLA_NOPE * MLA_KV_RANK * 2
                                 + tm * (MLA_KV_RANK + 3 * LANES) * 4 + (MLA_HEADS + 2) * tm * MLA_QK * 2)
            + 2 * tm * N_MLA * 4 + 8 * 2 ** 20)
    out_specs = [pl.BlockSpec((1, tm, MLA_KV_RANK), lambda b, j: (b, j, 0)),
                 pl.BlockSpec((1, tm, MLA_ROPE), lambda b, j: (b, j, 0)),
                 pl.BlockSpec((1, MLA_HEADS, tm, MLA_QK), lambda b, j: (b, 0, j, 0)),
                 pl.BlockSpec((1, tm, MLA_QK), lambda b, j: (b, j, 0)),
                 pl.BlockSpec((tm, 2 * LANES), lambda b, j: (rows(b, j), 0))]
    out_shape = [jax.ShapeDtypeStruct((nb, t, MLA_KV_RANK), F32),
                 jax.ShapeDtypeStruct((nb, t, MLA_ROPE), F32),
                 jax.ShapeDtypeStruct((nb, MLA_HEADS, t, MLA_QK), BF16),
                 jax.ShapeDtypeStruct((nb, t, MLA_QK), BF16),
                 jax.ShapeDtypeStruct((nb * t, 2 * LANES), F32)]
    if emit_vt:
        out_specs.append(pl.BlockSpec((1, MLA_KV_RANK, tm), lambda b, j: (b, 0, j)))
        out_shape.append(jax.ShapeDtypeStruct((nb, MLA_KV_RANK, t), BF16))
    return pl.pallas_call(
        functools.partial(_mla_prep_kernel, tm=tm, pos0=pos0),
        grid=(nb, nt),
        in_specs=[pl.BlockSpec((tm, d), lambda b, j: (rows(b, j), 0)),
                  pl.BlockSpec((d, N_MLA), fixed2, pipeline_mode=pl.Buffered(1)),
                  pl.BlockSpec((1, LANES), fixed2),
                  pl.BlockSpec((MLA_HEADS, MLA_NOPE, MLA_KV_RANK), lambda b, j: (0, 0, 0)),
                  pl.BlockSpec((1, MLA_KV_RANK), fixed2)],
        out_specs=out_specs,
        out_shape=out_shape,
        compiler_params=_params(("parallel", "parallel"), vmem),
        name="mla_proj_prep",
    )(hm, w_mla, inv128, wuk_t, kvn)


ATTN_GROUPS = 2


def _attn_kernel(q_ref, k_ref, vt_ref, wuv_ref, o_ref, s0_ref, s1_ref, m_ref, l_ref, acc_ref,
                 *, tq, tk, n_valid):
    i = pl.program_id(1)
    cols = MLA_HEADS * tq
    cg = cols // ATTN_GROUPS
    q_first = i * tq
    c = lax.broadcasted_iota(jnp.int32, (1, cg), 1)
    qchunk = jnp.right_shift(q_first + jnp.bitwise_and(c, tq - 1), LOG2_CHUNK)
    k_all = jnp.minimum((q_first // CHUNK + 1) * CHUNK, n_valid)
    k_any = jnp.minimum(((q_first + tq - 1) // CHUNK + 1) * CHUNK, n_valid)
    nk = (k_any + tk - 1) // tk
    m_ref[...] = jnp.full(m_ref.shape, -jnp.inf, F32)
    l_ref[...] = jnp.zeros(l_ref.shape, F32)
    acc_ref[...] = jnp.zeros(acc_ref.shape, F32)

    def scores(j, buf):
        start = pl.multiple_of(j * tk, tk)
        q = q_ref[0].reshape(cols, MLA_QK)
        buf[...] = _dot_nt(k_ref[0, pl.ds(start, tk), :], q)

    def fold(x, op):
        while x.shape[0] > SUBLANES:
            half = x.shape[0] // 2
            x = op(x[:half], x[half:])
        return x

    def softmax_values(j, buf, masked):
        start = pl.multiple_of(j * tk, tk)
        vt = vt_ref[0, :, pl.ds(start, tk)]
        if masked:
            kpos = start + lax.broadcasted_iota(jnp.int32, (tk, 1), 0)
            mask = jnp.logical_and(jnp.right_shift(kpos, LOG2_CHUNK) <= qchunk, kpos < n_valid)
        for g in range(ATTN_GROUPS):
            cs = slice(g * cg, (g + 1) * cg)
            s = buf[:, cs]
            if masked:
                s = jnp.where(mask, s, -jnp.inf)
            m_old = m_ref[:, cs]
            m_new = jnp.maximum(m_old, jnp.max(fold(s, jnp.maximum), axis=0, keepdims=True))
            alpha = jnp.exp(m_old - m_new)
            p = jnp.exp(s - m_new)
            l_ref[:, cs] = alpha * l_ref[:, cs] + jnp.sum(fold(p, jnp.add), axis=0, keepdims=True)
            acc_ref[:, cs] = alpha * acc_ref[:, cs] + _dot(vt, p.astype(BF16))
            m_ref[:, cs] = m_new

    def by_parity(j, fn):
        @pl.when(jnp.bitwise_and(j, 1) == 0)
        def _():
            fn(s0_ref, s1_ref)

        @pl.when(jnp.bitwise_and(j, 1) == 1)
        def _():
            fn(s1_ref, s0_ref)

    scores(0, s0_ref)

    def body(j, carry, masked):
        def step(cur, nxt):
            scores(j + 1, nxt)
            softmax_values(j, cur, masked)
        by_parity(j, step)
        return carry

    n_open = jnp.minimum(k_all // tk, nk - 1)
    lax.fori_loop(0, n_open, functools.partial(body, masked=False), 0)
    lax.fori_loop(n_open, nk - 1, functools.partial(body, masked=True), 0)
    by_parity(nk - 1, lambda cur, nxt: softmax_values(nk - 1, cur, True))
    inv_l = 1.0 / l_ref[...]
    for h in range(MLA_HEADS):
        hs = slice(h * tq, (h + 1) * tq)
        lat_t = (acc_ref[:, hs] * inv_l[:, hs]).astype(BF16)
        o_ref[:, h * MLA_V:(h + 1) * MLA_V] = _dot_tn(lat_t, wuv_ref[h]).astype(BF16)


def _attn_cache_kernel(q_ref, ckv_ref, kr_ref, knew_ref, wuv_ref, o_ref, m_ref, l_ref, acc_ref,
                       *, tq, tk, nkb, past):
    j = pl.program_id(1)
    rows = MLA_HEADS * tq
    q = q_ref[0].reshape(rows, MLA_QK)
    r = lax.broadcasted_iota(jnp.int32, (rows, 1), 0)
    qchunk = jnp.right_shift(past + jnp.bitwise_and(r, tq - 1), LOG2_CHUNK)

    @pl.when(j == 0)
    def _():
        m_ref[...] = jnp.full(m_ref.shape, -jnp.inf, F32)
        l_ref[...] = jnp.zeros(l_ref.shape, F32)
        acc_ref[...] = jnp.zeros(acc_ref.shape, F32)

    def fold(x, op):
        while x.shape[1] > LANES:
            half = x.shape[1] // 2
            x = op(x[:, :half], x[:, half:])
        return x

    def update(s, kpos, vals, masked):
        if masked:
            s = jnp.where(jnp.right_shift(kpos, LOG2_CHUNK) <= qchunk, s, -jnp.inf)
        m_old = m_ref[...]
        m_new = jnp.maximum(m_old, jnp.max(fold(s, jnp.maximum), axis=-1, keepdims=True))
        alpha = jnp.exp(m_old - m_new)
        p = jnp.exp(s - m_new)
        l_ref[...] = alpha * l_ref[...] + jnp.sum(fold(p, jnp.add), axis=-1, keepdims=True)
        acc_ref[...] = alpha * acc_ref[...] + _dot(p.astype(BF16), vals)
        m_ref[...] = m_new

    ck = ckv_ref[0].astype(BF16)
    s = (_dot_nt(q[:, :MLA_KV_RANK], ck)
         + _dot_nt(q[:, MLA_KV_RANK:MLA_KV_RANK + MLA_ROPE], kr_ref[0].astype(BF16)))
    update(s, j * tk + lax.broadcasted_iota(jnp.int32, (1, tk), 1), ck, False)

    @pl.when(j == nkb - 1)
    def _():
        kn = knew_ref[0]
        update(_dot_nt(q, kn), past + lax.broadcasted_iota(jnp.int32, (1, tq), 1), kn[:, :MLA_KV_RANK], True)
        inv_l = 1.0 / l_ref[...]
        for h in range(MLA_HEADS):
            hs = slice(h * tq, (h + 1) * tq)
            lat = (acc_ref[hs, :] * inv_l[hs, :]).astype(BF16)
            o_ref[:, h * MLA_V:(h + 1) * MLA_V] = _dot(lat, wuv_ref[h]).astype(BF16)


def _attention(qcat, kcat, vt, wuv_t):
    nb, _, t, _ = qcat.shape
    tq = _tile(t, 128)
    tk = _tile(t, 512)
    assert tq & (tq - 1) == 0
    nq = t // tq
    cols = MLA_HEADS * tq
    vmem = (2 * (cols * MLA_QK * 2 + t * MLA_QK * 2 + MLA_KV_RANK * t * 2
                 + MLA_HEADS * MLA_KV_RANK * MLA_V * 2 + tq * MLA_HEADS * MLA_V * 2)
            + 2 * tk * cols * 4 + 2 * SUBLANES * cols * 4 + MLA_KV_RANK * cols * 4
            + 3 * tk * cols * 4 // ATTN_GROUPS + 4 * 2 ** 20)
    return pl.pallas_call(
        functools.partial(_attn_kernel, tq=tq, tk=tk, n_valid=t),
        grid=(nb, nq),
        in_specs=[pl.BlockSpec((1, MLA_HEADS, tq, MLA_QK), lambda b, i: (b, 0, i, 0)),
                  pl.BlockSpec((1, t, MLA_QK), lambda b, i: (b, 0, 0)),
                  pl.BlockSpec((1, MLA_KV_RANK, t), lambda b, i: (b, 0, 0)),
                  pl.BlockSpec((MLA_HEADS, MLA_KV_RANK, MLA_V), lambda b, i: (0, 0, 0))],
        out_specs=pl.BlockSpec((tq, MLA_HEADS * MLA_V), lambda b, i: (b * nq + i, 0)),
        out_shape=jax.ShapeDtypeStruct((nb * t, MLA_HEADS * MLA_V), BF16),
        scratch_shapes=[pltpu.VMEM((tk, cols), F32), pltpu.VMEM((tk, cols), F32), pltpu.VMEM((1, cols), F32),
                        pltpu.VMEM((1, cols), F32), pltpu.VMEM((MLA_KV_RANK, cols), F32)],
        compiler_params=_params(("parallel", "arbitrary"), vmem),
        name="mla_attention",
    )(qcat, kcat, vt, wuv_t)


def _attention_cached(qcat, ckv_past, krope_past, kcat_new, wuv_t):
    nb, _, t, _ = qcat.shape
    past = ckv_past.shape[1]
    tk = _tile(past, 1024)
    nkb = past // tk
    assert t & (t - 1) == 0
    rows = MLA_HEADS * t
    vmem = (2 * (rows * MLA_QK * 2 + tk * MLA_KV_RANK * 4 + tk * LANES * 4 + t * MLA_QK * 2
                 + MLA_HEADS * MLA_KV_RANK * MLA_V * 2 + t * MLA_HEADS * MLA_V * 2)
            + 2 * rows * LANES * 4 + rows * MLA_KV_RANK * 4 + tk * MLA_QK * 2 + 4 * rows * tk * 4 + 4 * 2 ** 20)
    return pl.pallas_call(
        functools.partial(_attn_cache_kernel, tq=t, tk=tk, nkb=nkb, past=past),
        grid=(nb, nkb),
        in_specs=[pl.BlockSpec((1, MLA_HEADS, t, MLA_QK), lambda b, j: (b, 0, 0, 0)),
                  pl.BlockSpec((1, tk, MLA_KV_RANK), lambda b, j: (b, j, 0)),
                  pl.BlockSpec((1, tk, MLA_ROPE), lambda b, j: (b, j, 0)),
                  pl.BlockSpec((1, t, MLA_QK), lambda b, j: (b, 0, 0)),
                  pl.BlockSpec((MLA_HEADS, MLA_KV_RANK, MLA_V), lambda b, j: (0, 0, 0))],
        out_specs=pl.BlockSpec((t, MLA_HEADS * MLA_V), lambda b, j: (b, 0)),
        out_shape=jax.ShapeDtypeStruct((nb * t, MLA_HEADS * MLA_V), BF16),
        scratch_shapes=[pltpu.VMEM((rows, 1), F32), pltpu.VMEM((rows, 1), F32),
                        pltpu.VMEM((rows, MLA_KV_RANK), F32)],
        compiler_params=_params(("parallel", "arbitrary"), vmem),
        name="mla_attention_cached",
    )(qcat, ckv_past, krope_past, kcat_new, wuv_t)


def _merge_out_kernel(og_ref, om_ref, wg_ref, wm_ref, gg_ref, gm_ref, wo_ref, x_ref, gpost_ref, gnext_ref,
                      xo_ref, xno_ref):
    tg = _dot(og_ref[...], wg_ref[...])
    tm_ = _dot(om_ref[...], wm_ref[...])
    merged = (jax.nn.sigmoid(gg_ref[...].astype(F32)) * tg
              + jax.nn.sigmoid(gm_ref[...].astype(F32)) * tm_).astype(BF16)
    y = _dot(merged, wo_ref[...])
    xnew = x_ref[...] + _rms(y, gpost_ref[...])
    xo_ref[...] = xnew
    xno_ref[...] = _rms(xnew, gnext_ref[...]).astype(BF16)


def _merge_out(og, om, wbg, wbm, proj_gate, wout, x, gpost, gnext):
    m, kg = og.shape
    d = wbg.shape[1]
    tm = _tile(m, 256)
    gsz = proj_gate.dtype.itemsize
    row = lambda i: (i, 0)
    fixed = lambda i: (0, 0)
    resident = dict(pipeline_mode=pl.Buffered(1))
    vmem = ((2 * kg * d + d * d) * 2
            + 2 * (2 * tm * kg * 2 + 2 * tm * d * gsz + 2 * tm * d * 4 + tm * d * 2)
            + 4 * tm * d * 4 + 2 * 2 ** 20)
    return pl.pallas_call(
        _merge_out_kernel,
        grid=(m // tm,),
        in_specs=[pl.BlockSpec((tm, kg), row),
                  pl.BlockSpec((tm, kg), row),
                  pl.BlockSpec((kg, d), fixed, **resident),
                  pl.BlockSpec((kg, d), fixed, **resident),
                  pl.BlockSpec((tm, d), lambda i: (i, 0)),
                  pl.BlockSpec((tm, d), lambda i: (i, 1)),
                  pl.BlockSpec((d, d), fixed, **resident),
                  pl.BlockSpec((tm, d), row),
                  pl.BlockSpec((1, d), fixed),
                  pl.BlockSpec((1, d), fixed)],
        out_specs=[pl.BlockSpec((tm, d), row), pl.BlockSpec((tm, d), row)],
        out_shape=[jax.ShapeDtypeStruct((m, d), F32), jax.ShapeDtypeStruct((m, d), BF16)],
        compiler_params=_params(("parallel",), vmem),
        name="merge_out",
    )(og, om, wbg, wbm, proj_gate, proj_gate, wout, x, gpost, gnext)


def _layer(x3, pos0, conv_state, ssm0, ckv_past, krope_past, w):
    nb, t, d = x3.shape
    m = nb * t
    x = x3.reshape(m, d)
    x1, hm, ffn1_w = _ffn(x, *w["ffn1_w"], w["ffn1_norm_post"],
                          gpre=w["ffn1_norm_pre"], gnext=w["mix_norm_pre"])
    proj_gdn, w_in_gdn = _matmul(hm, w["w_in_gdn"], F32, ncols=N_GDN)
    proj_gate, w_in_gate = _matmul(hm, w["w_in_gate"], BF16)
    ckv, krope, qcat, kcat, ba, *vt = _mla_prep(hm, w["w_in_mla"], nb, t, pos0, w["inv128"], w["wuk_t"],
                                                w["kv_norm"], ckv_past is None)

    o_gdn, ssm_new = _gdn(proj_gdn, ba, nb, t, conv_state, ssm0, w["conv_w"], w["alog"], w["dtb"],
                          w["gdn_nw"])
    conv_new = proj_gdn.reshape(nb, t, -1)[:, t - (GDN_CONV - 1):, COL_QKV:COL_QKV + GDN_CONV_DIM]

    if ckv_past is None:
        o_mla = _attention(qcat, kcat, vt[0], w["wuv_t"])
    else:
        o_mla = _attention_cached(qcat, ckv_past, krope_past, kcat, w["wuv_t"])

    x2, xn2 = _merge_out(o_gdn, o_mla, w["w_br_gdn"], w["w_br_mla"], proj_gate, w["w_out"], x1,
                         w["mix_norm_post"], w["ffn2_norm_pre"])
    x3_, _, ffn2_w = _ffn(x2, *w["ffn2_w"], w["ffn2_norm_post"], xn=xn2)
    w_bf16 = dict(ffn1_w=ffn1_w, ffn2_w=ffn2_w, w_in_gdn=w_in_gdn, w_in_gate=w_in_gate)
    return (x3_.reshape(nb, t, d), conv_new, ssm_new, ckv, krope), w_bf16


def _prep_weights(l, d_model, **p):
    row = lambda v: v[l].reshape(1, -1).astype(F32)
    pad_lanes = lambda v: jnp.pad(v[l].reshape(1, -1).astype(F32), ((0, 0), (0, LANES - v.shape[1])))
    inv = ROPE_THETA ** (-jnp.arange(0, MLA_ROPE, 2, dtype=F32) / MLA_ROPE)
    w_in_gdn, w_in_gate, w_in_mla = _build_w_in(p["w_in"][l], d_model)
    w = dict(
        w_in_gdn=w_in_gdn, w_in_gate=w_in_gate, w_in_mla=w_in_mla,
        ffn1_norm_pre=row(p["ffn1_norm_pre"]), ffn1_norm_post=row(p["ffn1_norm_post"]),
        mix_norm_pre=row(p["mix_norm_pre"]), mix_norm_post=row(p["mix_norm_post"]),
        ffn2_norm_pre=row(p["ffn2_norm_pre"]), ffn2_norm_post=row(p["ffn2_norm_post"]),
        ffn1_w=(p["ffn1_w_gate"][l], p["ffn1_w_up"][l], p["ffn1_w_down"][l]),
        ffn2_w=(p["ffn2_w_gate"][l], p["ffn2_w_up"][l], p["ffn2_w_down"][l]),
        conv_w=p["gdn_conv_w"][l].astype(F32),
        alog=pad_lanes(p["gdn_a_log"]), dtb=pad_lanes(p["gdn_dt_bias"]),
        gdn_nw=row(p["gdn_norm_w"]), kv_norm=row(p["mla_kv_norm"]),
        inv128=jnp.tile(inv, LANES // inv.shape[0]).reshape(1, LANES),
        wuk_t=jnp.transpose(p["mla_w_uk"][l], (1, 2, 0)).astype(BF16),
        wuv_t=jnp.transpose(p["mla_w_uv"][l], (1, 0, 2)).astype(BF16),
        w_br_gdn=p["w_br_gdn"][l].astype(BF16), w_br_mla=p["w_br_mla"][l].astype(BF16),
        w_out=p["w_out"][l].astype(BF16),
    )
    return w


def kernel(x_prompt, x_sample, state_gdn_conv, state_gdn_ssm, cache_mla_ckv, cache_mla_krope, ffn1_norm_pre, ffn1_w_gate, ffn1_w_up, ffn1_w_down, ffn1_norm_post, mix_norm_pre, w_in, gdn_conv_w, gdn_a_log, gdn_dt_bias, gdn_norm_w, mla_kv_norm, mla_w_uk, mla_w_uv, w_br_gdn, w_br_mla, w_out, mix_norm_post, ffn2_norm_pre, ffn2_w_gate, ffn2_w_up, ffn2_w_down, ffn2_norm_post):
    params = dict(
        ffn1_norm_pre=ffn1_norm_pre, ffn1_w_gate=ffn1_w_gate, ffn1_w_up=ffn1_w_up,
        ffn1_w_down=ffn1_w_down, ffn1_norm_post=ffn1_norm_post, mix_norm_pre=mix_norm_pre,
        w_in=w_in, gdn_conv_w=gdn_conv_w, gdn_a_log=gdn_a_log, gdn_dt_bias=gdn_dt_bias,
        gdn_norm_w=gdn_norm_w, mla_kv_norm=mla_kv_norm, mla_w_uk=mla_w_uk, mla_w_uv=mla_w_uv,
        w_br_gdn=w_br_gdn, w_br_mla=w_br_mla, w_out=w_out, mix_norm_post=mix_norm_post,
        ffn2_norm_pre=ffn2_norm_pre, ffn2_w_gate=ffn2_w_gate, ffn2_w_up=ffn2_w_up,
        ffn2_w_down=ffn2_w_down, ffn2_norm_post=ffn2_norm_post)
    depth = w_in.shape[0]
    d_model = x_prompt.shape[-1]
    b_p = x_prompt.shape[0]
    past = cache_mla_ckv.shape[2]
    yp, ys = x_prompt, x_sample
    outs_p, outs_s = [], []
    for l in range(depth):
        w = _prep_weights(l, d_model, **params)
        (ys, *rest_s), w_bf16 = _layer(
            ys, past, state_gdn_conv[l], state_gdn_ssm[l], cache_mla_ckv[l], cache_mla_krope[l], w)
        (yp, *rest_p), _ = _layer(
            yp, 0, jnp.zeros((b_p, GDN_CONV - 1, GDN_CONV_DIM), F32),
            jnp.zeros((b_p, GDN_HEADS, GDN_DK, GDN_DV), F32), None, None, {**w, **w_bf16})
        outs_p.append(rest_p)
        outs_s.append(rest_s)
    stack = lambda outs, i: jnp.stack([o[i] for o in outs])
    return (yp, ys,
            stack(outs_p, 0), stack(outs_p, 1), stack(outs_p, 2), stack(outs_p, 3),
            stack(outs_s, 0), stack(outs_s, 1), stack(outs_s, 2), stack(outs_s, 3))
```

```python
import functools

import jax
import jax.numpy as jnp
from jax import lax
from jax.experimental import pallas as pl
from jax.experimental.pallas import tpu as pltpu

F32 = jnp.float32
BF16 = jnp.bfloat16

CHUNK = 64
LOG2_CHUNK = 6
NORM_EPS = 1e-6
GDN_HEADS = 8
GDN_DK = 128
GDN_DV = 128
GDN_CONV = 4
GDN_KEY_DIM = GDN_HEADS * GDN_DK
GDN_VAL_DIM = GDN_HEADS * GDN_DV
GDN_CONV_DIM = 2 * GDN_KEY_DIM + GDN_VAL_DIM
MLA_HEADS = 8
MLA_NOPE = 128
MLA_ROPE = 64
MLA_V = 128
MLA_KV_RANK = 512
MLA_SCALE = (MLA_NOPE + MLA_ROPE) ** -0.5
ROPE_THETA = 10000.0

LANES = 128
SUBLANES = 8
VMEM_CAP_BYTES = 56 * 2 ** 20
MLA_QK = MLA_KV_RANK + LANES
HI = lax.Precision.HIGHEST


def _params(semantics, vmem_bytes):
    limit = int(min(max(vmem_bytes, 16 * 2 ** 20), VMEM_CAP_BYTES))
    return pltpu.CompilerParams(dimension_semantics=semantics, vmem_limit_bytes=limit)


def _dot(a, b, prec=None):
    return lax.dot_general(a, b, (((1,), (0,)), ((), ())), precision=prec,
                           preferred_element_type=F32)


def _dot_nt(a, b, prec=None):
    return lax.dot_general(a, b, (((1,), (1,)), ((), ())), precision=prec,
                           preferred_element_type=F32)


def _dot_tn(a, b, prec=None):
    return lax.dot_general(a, b, (((0,), (0,)), ((), ())), precision=prec,
                           preferred_element_type=F32)


def _rms(y, g):
    return y * lax.rsqrt(jnp.mean(y * y, axis=-1, keepdims=True) + NORM_EPS) * g


def _silu(x):
    return x * jax.nn.sigmoid(x)


def _tile(n, pref):
    t = min(n, pref)
    assert n % t == 0, (n, pref)
    return t


def _ffn_kernel(*refs, nf, norm_in, emit_next, cast_w):
    refs = list(refs)
    xn_ref = None if norm_in else refs.pop(0)
    x_ref, wg_ref, wu_ref, wd_ref = refs[:4]
    del refs[:4]
    gpre_ref = refs.pop(0) if norm_in else None
    gpost_ref = refs.pop(0)
    gnext_ref = refs.pop(0) if emit_next else None
    xo_ref = refs.pop(0)
    xno_ref = refs.pop(0) if emit_next else None
    if cast_w:
        wgo_ref, wuo_ref, wdo_ref = refs[:3]
        del refs[:3]
    acc_ref = refs.pop(0)
    if norm_in:
        xn_ref = refs.pop(0)
    f = pl.program_id(1)

    @pl.when(f == 0)
    def _():
        acc_ref[...] = jnp.zeros(acc_ref.shape, F32)
        if norm_in:
            xn_ref[...] = _rms(x_ref[...], gpre_ref[...]).astype(BF16)

    xn = xn_ref[...]
    wg, wu, wd = wg_ref[...], wu_ref[...], wd_ref[...]
    if cast_w:
        wg, wu, wd = wg.astype(BF16), wu.astype(BF16), wd.astype(BF16)
        wgo_ref[...] = wg
        wuo_ref[...] = wu
        wdo_ref[...] = wd
    gate = _dot(xn, wg)
    up = _dot(xn, wu)
    h = (_silu(gate) * up).astype(BF16)
    acc_ref[...] += _dot(h, wd)

    @pl.when(f == nf - 1)
    def _():
        xnew = x_ref[...] + 0.5 * _rms(acc_ref[...], gpost_ref[...])
        xo_ref[...] = xnew
        if emit_next:
            xno_ref[...] = _rms(xnew, gnext_ref[...]).astype(BF16)


def _ffn(x, wg, wu, wd, gpost, xn=None, gpre=None, gnext=None):
    norm_in = xn is None
    emit_next = gnext is not None
    cast_w = wg.dtype == F32
    assert norm_in == (gpre is not None)
    m, d = x.shape
    dff = wg.shape[1]
    tm = _tile(m, 512)
    tf = _tile(dff, 256 if cast_w else 512)
    nf = dff // tf
    assert not cast_w or m == tm
    wsz = wg.dtype.itemsize
    row = lambda i, f: (i, 0)
    vec = pl.BlockSpec((1, d), lambda i, f: (0, 0))
    w_specs = [pl.BlockSpec((d, tf), lambda i, f: (0, f)),
               pl.BlockSpec((d, tf), lambda i, f: (0, f)),
               pl.BlockSpec((tf, d), lambda i, f: (f, 0))]
    ins = [] if norm_in else [xn]
    in_specs = [] if norm_in else [pl.BlockSpec((tm, d), row)]
    ins += [x, wg, wu, wd]
    in_specs += [pl.BlockSpec((tm, d), row)] + w_specs
    for g in (gpre, gpost, gnext):
        if g is not None:
            ins.append(g)
            in_specs.append(vec)
    out_specs = [pl.BlockSpec((tm, d), row)]
    out_shape = [jax.ShapeDtypeStruct((m, d), F32)]
    scratch = [pltpu.VMEM((tm, d), F32)]
    if emit_next:
        out_specs.append(pl.BlockSpec((tm, d), row))
        out_shape.append(jax.ShapeDtypeStruct((m, d), BF16))
    if cast_w:
        out_specs += w_specs
        out_shape += [jax.ShapeDtypeStruct(w.shape, BF16) for w in (wg, wu, wd)]
    if norm_in:
        scratch.append(pltpu.VMEM((tm, d), BF16))
    vmem = (2 * (tm * d * 2 + tm * d * 4 + 3 * d * tf * wsz + tm * d * 4 + tm * d * 2)
            + (2 * 3 * d * tf * 2 + 3 * d * tf * 2 if cast_w else 0)
            + tm * d * 4 + 3 * tm * tf * 4 + 2 * tm * d * 4)
    res = pl.pallas_call(
        functools.partial(_ffn_kernel, nf=nf, norm_in=norm_in, emit_next=emit_next, cast_w=cast_w),
        grid=(m // tm, nf),
        in_specs=in_specs,
        out_specs=out_specs,
        out_shape=out_shape,
        scratch_shapes=scratch,
        compiler_params=_params(("parallel", "arbitrary"), vmem),
        name="ffn_cast" if cast_w else "ffn",
    )(*ins)
    res = list(res)
    x_new = res.pop(0)
    xn_next = res.pop(0) if emit_next else None
    return x_new, xn_next, (tuple(res) if cast_w else (wg, wu, wd))


def _mm_kernel(a_ref, w_ref, o_ref):
    o_ref[...] = _dot(a_ref[...], w_ref[...]).astype(o_ref.dtype)


def _matmul(a, w, out_dtype):
    m, k = a.shape
    n = w.shape[1]
    tm = _tile(m, 1024)
    tn = _tile(n, 1024)
    osz = jnp.dtype(out_dtype).itemsize
    vmem = 2 * (tm * k * 2 + k * tn * 2 + tm * tn * osz) + tm * tn * 4
    return pl.pallas_call(
        _mm_kernel,
        grid=(m // tm, n // tn),
        in_specs=[pl.BlockSpec((tm, k), lambda i, j: (i, 0)),
                  pl.BlockSpec((k, tn), lambda i, j: (0, j))],
        out_specs=pl.BlockSpec((tm, tn), lambda i, j: (i, j)),
        out_shape=jax.ShapeDtypeStruct((m, n), out_dtype),
        compiler_params=_params(("parallel", "arbitrary"), vmem),
        name="in_proj",
    )(a, w)


COL_QKV = 0
COL_Z = COL_QKV + GDN_CONV_DIM
COL_QNOPE = 0
COL_QROPE = COL_QNOPE + MLA_HEADS * MLA_NOPE
COL_CKV = COL_QROPE + MLA_HEADS * LANES
COL_KR = COL_CKV + MLA_KV_RANK
COL_B = COL_KR + LANES
COL_A = COL_B + LANES
N_MLA = 3072


def _build_w_in(w_in, d_model):
    sizes = (GDN_CONV_DIM, GDN_VAL_DIM, GDN_HEADS, GDN_HEADS, MLA_HEADS * (MLA_NOPE + MLA_ROPE),
             MLA_KV_RANK, MLA_ROPE, d_model, d_model)
    offs = [0]
    for s in sizes:
        offs.append(offs[-1] + s)
    w_gdn = w_in[:, :offs[2]].astype(BF16)
    w_gate = w_in[:, offs[7]:].astype(BF16)
    part = lambda i: w_in[:, offs[i]:offs[i + 1]].astype(BF16)
    b, a, qm, ckv, kr = (part(i) for i in range(2, 7))
    d = w_in.shape[0]
    half = MLA_ROPE // 2
    qm = qm.reshape(d, MLA_HEADS, MLA_NOPE + MLA_ROPE)
    qn = qm[:, :, :MLA_NOPE].reshape(d, MLA_HEADS * MLA_NOPE)
    qr = qm[:, :, MLA_NOPE:]
    qr_pair = jnp.concatenate([qr, qr[:, :, half:], qr[:, :, :half]], axis=2).reshape(d, MLA_HEADS * LANES)
    kr_pair = jnp.concatenate([kr, kr[:, half:], kr[:, :half]], axis=1)
    pad8 = lambda w: jnp.pad(w, ((0, 0), (0, LANES - w.shape[1])))
    w_mla = jnp.concatenate([qn, qr_pair, ckv, kr_pair, pad8(b), pad8(a)], axis=1)
    w_mla = jnp.pad(w_mla, ((0, 0), (0, N_MLA - w_mla.shape[1])))
    return w_gdn, w_gate, w_mla


_NN = (((1,), (0,)), ((), ()))
_NT = (((1,), (1,)), ((), ()))
_TN = (((0,), (0,)), ((), ()))
INV_BLOCK = 16


def _mm(a, b, dims=_NN):
    return lax.dot_general(a.astype(BF16), b.astype(BF16), dims, preferred_element_type=F32)


def _gdn_kernel(qkv_ref, z_ref, beta_ref, g_ref, cst_ref, cw_ref, nw_ref, s0_ref,
                o_ref, sout_ref, past_ref, s_ref, *, G, C, nlev, nc):
    c = pl.program_id(1)

    @pl.when(c == 0)
    def _():
        past_ref[...] = cst_ref[...]
        s_ref[...] = s0_ref[...]

    ri = lax.broadcasted_iota(jnp.int32, (C, C), 0)
    ci = lax.broadcasted_iota(jnp.int32, (C, C), 1)
    incl = ri >= ci
    strict = ri > ci
    eye = (ri == ci).astype(F32)
    log2_blk = INV_BLOCK.bit_length() - 1
    bdiag = jnp.right_shift(ri, log2_blk) == jnp.right_shift(ci, log2_blk)

    first_row = lax.broadcasted_iota(jnp.int32, (C, 1), 0) == 0

    def conv(s, col):
        cols = slice(col, col + LANES)
        x = qkv_ref[s, :, cols]
        past = past_ref[s, :, cols]
        w = [cw_ref[j:j + 1, cols] for j in range(GDN_CONV)]
        acc = w[0] * x
        for j in range(1, GDN_CONV):
            top = sum(w[i] * past[SUBLANES - 1 - (j - 1 - i):SUBLANES - (j - 1 - i)] for i in range(j))
            acc = w[j] * x + jnp.where(first_row, top, pltpu.roll(acc, 1, axis=0))
        return _silu(acc)

    units = [(s, h) for s in range(G) for h in range(GDN_HEADS)]
    every = lambda fn, *lists: [fn(*args) for args in zip(*lists)]
    mm_inv = _mm
    beta_all, gcum, gcum_t = [], [], []
    for s in range(G):
        beta_all.append(beta_ref[s])
        g_all = g_ref[s]
        gcum.append(_dot(incl.astype(F32), g_all, HI))
        gcum_t.append(_dot_tn(g_all, (ci >= ri).astype(F32), HI))

    def l2n(x):
        return x * lax.rsqrt(jnp.sum(x * x, axis=-1, keepdims=True) + 1e-6)

    q = [l2n(conv(s, h * GDN_DK)) * (GDN_DK ** -0.5) for s, h in units]
    k = [l2n(conv(s, GDN_KEY_DIM + h * GDN_DK)) for s, h in units]
    v = [conv(s, 2 * GDN_KEY_DIM + h * GDN_DV) for s, h in units]
    bcol = [beta_all[s][:, h:h + 1] for s, h in units]
    gcol = [gcum[s][:, h:h + 1] for s, h in units]
    glast = [gcum[s][C - 1:C, h:h + 1] for s, h in units]
    decay = [jnp.where(incl, jnp.exp(jnp.where(incl, gcum[s][:, h:h + 1] - gcum_t[s][h:h + 1, :], 0.0)), 0.0)
             for s, h in units]
    qkk = every(lambda q_, k_: _mm(jnp.concatenate([q_, k_], axis=0), k_, _NT), q, k)
    qk = every(lambda x, d: x[:C] * d, qkk, decay)
    nmat = every(lambda x, b_, d: -jnp.where(strict, b_ * x[C:] * d, 0.0), qkk, bcol, decay)
    ndiag = every(lambda n_: jnp.where(bdiag, n_, 0.0), nmat)
    tinv = every(lambda n_: eye + n_, ndiag)
    npow = ndiag
    for _ in range(INV_BLOCK.bit_length() - 2):
        npow = every(mm_inv, npow, npow)
        tinv = every(lambda t_, p_: t_ + mm_inv(t_, p_), tinv, npow)
    mpow = every(lambda t_, n_, d_: mm_inv(t_, n_ - d_), tinv, nmat, ndiag)
    for lev in range(nlev):
        if lev:
            mpow = every(mm_inv, mpow, mpow)
        tinv = every(lambda t_, m_: t_ + mm_inv(m_, t_), tinv, mpow)
    egc = every(jnp.exp, gcol)
    sol = every(lambda t_, v_, k_, b_, e_: _mm(t_, jnp.concatenate([v_ * b_, k_ * (b_ * e_)], axis=1)),
                tinv, v, k, bcol, egc)
    s_old = [s_ref[s, h] for s, h in units]
    ws_qs = every(lambda x, q_, e_, s_: _mm(jnp.concatenate([x[:, GDN_DV:], q_ * e_], axis=0), s_),
                  sol, q, egc, s_old)
    v_new = every(lambda x, y: x[:, :GDN_DV] - y[:C], sol, ws_qs)
    o = every(lambda y, a_, vn: y[C:] + _mm(a_, vn), ws_qs, qk, v_new)
    s_new = every(lambda s_, gl, k_, gc, vn: s_ * jnp.exp(gl) + _mm(k_ * jnp.exp(gl - gc), vn, _TN),
                  s_old, glast, k, gcol, v_new)
    for (s, h), sn, o_ in zip(units, s_new, o):
        s_ref[s, h] = sn
        zz = z_ref[s, :, h * GDN_DV:(h + 1) * GDN_DV]
        o_ref[s, :, h * GDN_DV:(h + 1) * GDN_DV] = (_rms(o_, nw_ref[...]) * _silu(zz)).astype(BF16)

    past_ref[...] = qkv_ref[:, C - SUBLANES:C, :]

    @pl.when(c == nc - 1)
    def _():
        sout_ref[...] = s_ref[...]


def _gdn(proj_gdn, bg, nb, t, conv_state, ssm0, conv_w, nw):
    C = min(CHUNK, t)
    G = 2 if nb % 2 == 0 else 1
    assert t % C == 0 and C & (C - 1) == 0 and C % INV_BLOCK == 0
    nc = t // C
    nlev = (C // INV_BLOCK).bit_length() - 1
    pg = proj_gdn.reshape(nb, t, proj_gdn.shape[1])
    pm = bg.reshape(nb, t, bg.shape[1])
    fixed = lambda b, c: (0, 0)
    cst = jnp.pad(conv_state, ((0, 0), (SUBLANES - (GDN_CONV - 1), 0), (0, 0)))
    vmem = G * (2 * (C * GDN_CONV_DIM * 4 + C * GDN_VAL_DIM * 4 + 2 * C * LANES * 4 + SUBLANES * GDN_CONV_DIM * 4
                     + C * GDN_VAL_DIM * 2 + 2 * GDN_HEADS * GDN_DK * GDN_DV * 4)
                + SUBLANES * GDN_CONV_DIM * 4 + GDN_HEADS * GDN_DK * GDN_DV * 4) + 16 * 2 ** 20
    o, s_new = pl.pallas_call(
        functools.partial(_gdn_kernel, G=G, C=C, nlev=nlev, nc=nc),
        grid=(nb // G, nc),
        in_specs=[pl.BlockSpec((G, C, GDN_CONV_DIM), lambda b, c: (b, c, COL_QKV // GDN_CONV_DIM)),
                  pl.BlockSpec((G, C, GDN_VAL_DIM), lambda b, c: (b, c, COL_Z // GDN_VAL_DIM)),
                  pl.BlockSpec((G, C, LANES), lambda b, c: (b, c, 0)),
                  pl.BlockSpec((G, C, LANES), lambda b, c: (b, c, 1)),
                  pl.BlockSpec((G, SUBLANES, GDN_CONV_DIM), lambda b, c: (b, 0, 0)),
                  pl.BlockSpec((GDN_CONV, GDN_CONV_DIM), fixed),
                  pl.BlockSpec((1, GDN_DV), fixed),
                  pl.BlockSpec((G, GDN_HEADS, GDN_DK, GDN_DV), lambda b, c: (b, 0, 0, 0))],
        out_specs=[pl.BlockSpec((G, C, GDN_VAL_DIM), lambda b, c: (b, c, 0)),
                   pl.BlockSpec((G, GDN_HEADS, GDN_DK, GDN_DV), lambda b, c: (b, 0, 0, 0))],
        out_shape=[jax.ShapeDtypeStruct((nb, t, GDN_VAL_DIM), BF16),
                   jax.ShapeDtypeStruct((nb, GDN_HEADS, GDN_DK, GDN_DV), F32)],
        scratch_shapes=[pltpu.VMEM((G, SUBLANES, GDN_CONV_DIM), F32),
                        pltpu.VMEM((G, GDN_HEADS, GDN_DK, GDN_DV), F32)],
        compiler_params=_params(("parallel", "arbitrary"), vmem),
        name="gdn",
    )(pg, pg, pm, pm, cst, conv_w, nw, ssm0)
    return o.reshape(nb * t, GDN_VAL_DIM), s_new


def _mla_prep_kernel(hm_ref, w_ref, inv_ref, wuk_ref, kvn_ref, alog_ref, dtb_ref,
                     ckv_o, kr_o, qcat_o, kcat_o, bg_o, *maybe_vt_o, tm, pos0):
    j = pl.program_id(1)
    proj = _dot(hm_ref[...], w_ref[...])
    a_all = proj[:, COL_A:COL_A + LANES] + dtb_ref[...]
    softplus = jnp.maximum(a_all, 0.0) + jnp.log(1.0 + jnp.exp(-jnp.abs(a_all)))
    bg_o[:, :LANES] = jax.nn.sigmoid(proj[:, COL_B:COL_B + LANES])
    bg_o[:, LANES:] = -jnp.exp(alog_ref[...]) * softplus
    pos = (pos0 + j * tm + lax.broadcasted_iota(jnp.int32, (tm, LANES), 0)).astype(F32)
    lane = lax.broadcasted_iota(jnp.int32, (tm, LANES), 1)
    ang = pos * inv_ref[...]
    cos = jnp.cos(ang)
    sin = jnp.sin(ang)
    half = MLA_ROPE // 2
    cs = jnp.where(lane < MLA_ROPE, cos, jnp.where(lane < MLA_ROPE + half, -sin, sin))
    keep = lane < MLA_ROPE

    def rope(pair):
        prod = pair * cs
        return jnp.where(keep, prod + pltpu.roll(prod, MLA_ROPE, axis=1), 0.0)

    ckv = _rms(proj[:, COL_CKV:COL_CKV + MLA_KV_RANK], kvn_ref[...])
    ckv_o[0] = ckv
    kr = rope(proj[:, COL_KR:COL_KR + LANES])
    kr_o[0] = kr[:, :MLA_ROPE]
    kcat_o[0, :, :MLA_KV_RANK] = ckv.astype(BF16)
    kcat_o[0, :, MLA_KV_RANK:] = kr.astype(BF16)
    if maybe_vt_o:
        maybe_vt_o[0][0] = ckv.T.astype(BF16)
    for h in range(MLA_HEADS):
        qn = proj[:, COL_QNOPE + h * MLA_NOPE:COL_QNOPE + (h + 1) * MLA_NOPE].astype(BF16)
        qlat = _dot(qn, wuk_ref[h]) * MLA_SCALE
        qcat_o[0, h, :, :MLA_KV_RANK] = qlat.astype(BF16)
        qr = rope(proj[:, COL_QROPE + h * LANES:COL_QROPE + (h + 1) * LANES]) * MLA_SCALE
        qcat_o[0, h, :, MLA_KV_RANK:] = qr.astype(BF16)


def _mla_prep(hm, w_mla, nb, t, pos0, inv128, wuk_t, kvn, alog, dtb, emit_vt):
    d = hm.shape[1]
    tm = _tile(t, 512)
    nt = t // tm
    rows = lambda b, j: b * nt + j
    fixed2 = lambda b, j: (0, 0)
    vmem = (d * N_MLA * 2 + 2 * (tm * d * 2 + MLA_HEADS * MLA_NOPE * MLA_KV_RANK * 2
                                 + tm * (MLA_KV_RANK + 3 * LANES) * 4 + (MLA_HEADS + 2) * tm * MLA_QK * 2)
            + 2 * tm * N_MLA * 4 + 8 * 2 ** 20)
    out_specs = [pl.BlockSpec((1, tm, MLA_KV_RANK), lambda b, j: (b, j, 0)),
                 pl.BlockSpec((1, tm, MLA_ROPE), lambda b, j: (b, j, 0)),
                 pl.BlockSpec((1, MLA_HEADS, tm, MLA_QK), lambda b, j: (b, 0, j, 0)),
                 pl.BlockSpec((1, tm, MLA_QK), lambda b, j: (b, j, 0)),
                 pl.BlockSpec((tm, 2 * LANES), lambda b, j: (rows(b, j), 0))]
    out_shape = [jax.ShapeDtypeStruct((nb, t, MLA_KV_RANK), F32),
                 jax.ShapeDtypeStruct((nb, t, MLA_ROPE), F32),
                 jax.ShapeDtypeStruct((nb, MLA_HEADS, t, MLA_QK), BF16),
                 jax.ShapeDtypeStruct((nb, t, MLA_QK), BF16),
                 jax.ShapeDtypeStruct((nb * t, 2 * LANES), F32)]
    if emit_vt:
        out_specs.append(pl.BlockSpec((1, MLA_KV_RANK, tm), lambda b, j: (b, 0, j)))
        out_shape.append(jax.ShapeDtypeStruct((nb, MLA_KV_RANK, t), BF16))
    return pl.pallas_call(
        functools.partial(_mla_prep_kernel, tm=tm, pos0=pos0),
        grid=(nb, nt),
        in_specs=[pl.BlockSpec((tm, d), lambda b, j: (rows(b, j), 0)),
                  pl.BlockSpec((d, N_MLA), fixed2, pipeline_mode=pl.Buffered(1)),
                  pl.BlockSpec((1, LANES), fixed2),
                  pl.BlockSpec((MLA_HEADS, MLA_NOPE, MLA_KV_RANK), lambda b, j: (0, 0, 0)),
                  pl.BlockSpec((1, MLA_KV_RANK), fixed2),
                  pl.BlockSpec((1, LANES), fixed2),
                  pl.BlockSpec((1, LANES), fixed2)],
        out_specs=out_specs,
        out_shape=out_shape,
        compiler_params=_params(("parallel", "parallel"), vmem),
        name="mla_proj_prep",
    )(hm, w_mla, inv128, wuk_t, kvn, alog, dtb)


ATTN_GROUPS = 2


def _attn_kernel(q_ref, k_ref, vt_ref, wuv_ref, o_ref, s0_ref, s1_ref, m_ref, l_ref, acc_ref,
                 *, tq, tk, n_valid):
    i = pl.program_id(1)
    cols = MLA_HEADS * tq
    cg = cols // ATTN_GROUPS
    q_first = i * tq
    c = lax.broadcasted_iota(jnp.int32, (1, cg), 1)
    qchunk = jnp.right_shift(q_first + jnp.bitwise_and(c, tq - 1), LOG2_CHUNK)
    k_all = jnp.minimum((q_first // CHUNK + 1) * CHUNK, n_valid)
    k_any = jnp.minimum(((q_first + tq - 1) // CHUNK + 1) * CHUNK, n_valid)
    nk = (k_any + tk - 1) // tk
    m_ref[...] = jnp.full(m_ref.shape, -jnp.inf, F32)
    l_ref[...] = jnp.zeros(l_ref.shape, F32)
    acc_ref[...] = jnp.zeros(acc_ref.shape, F32)

    def scores(j, buf):
        start = pl.multiple_of(j * tk, tk)
        q = q_ref[0].reshape(cols, MLA_QK)
        buf[...] = _dot_nt(k_ref[0, pl.ds(start, tk), :], q)

    def fold(x, op):
        while x.shape[0] > SUBLANES:
            half = x.shape[0] // 2
            x = op(x[:half], x[half:])
        return x

    def softmax_values(j, buf, masked):
        start = pl.multiple_of(j * tk, tk)
        vt = vt_ref[0, :, pl.ds(start, tk)]
        if masked:
            kpos = start + lax.broadcasted_iota(jnp.int32, (tk, 1), 0)
            mask = jnp.logical_and(jnp.right_shift(kpos, LOG2_CHUNK) <= qchunk, kpos < n_valid)
        for g in range(ATTN_GROUPS):
            cs = slice(g * cg, (g + 1) * cg)
            s = buf[:, cs]
            if masked:
                s = jnp.where(mask, s, -jnp.inf)
            m_old = m_ref[:, cs]
            m_new = jnp.maximum(m_old, jnp.max(fold(s, jnp.maximum), axis=0, keepdims=True))
            alpha = jnp.exp(m_old - m_new)
            p = jnp.exp(s - m_new)
            l_ref[:, cs] = alpha * l_ref[:, cs] + jnp.sum(fold(p, jnp.add), axis=0, keepdims=True)
            acc_ref[:, cs] = alpha * acc_ref[:, cs] + _dot(vt, p.astype(BF16))
            m_ref[:, cs] = m_new

    def by_parity(j, fn):
        @pl.when(jnp.bitwise_and(j, 1) == 0)
        def _():
            fn(s0_ref, s1_ref)

        @pl.when(jnp.bitwise_and(j, 1) == 1)
        def _():
            fn(s1_ref, s0_ref)

    scores(0, s0_ref)

    def body(j, carry, masked):
        def step(cur, nxt):
            scores(j + 1, nxt)
            softmax_values(j, cur, masked)
        by_parity(j, step)
        return carry

    n_open = jnp.minimum(k_all // tk, nk - 1)
    lax.fori_loop(0, n_open, functools.partial(body, masked=False), 0)
    lax.fori_loop(n_open, nk - 1, functools.partial(body, masked=True), 0)
    by_parity(nk - 1, lambda cur, nxt: softmax_values(nk - 1, cur, True))
    inv_l = 1.0 / l_ref[...]
    for h in range(MLA_HEADS):
        hs = slice(h * tq, (h + 1) * tq)
        lat_t = (acc_ref[:, hs] * inv_l[:, hs]).astype(BF16)
        o_ref[:, h * MLA_V:(h + 1) * MLA_V] = _dot_tn(lat_t, wuv_ref[h]).astype(BF16)


def _attn_cache_kernel(q_ref, ckv_ref, kr_ref, knew_ref, wuv_ref, o_ref, m_ref, l_ref, acc_ref,
                       *, tq, tk, nkb, past):
    j = pl.program_id(1)
    rows = MLA_HEADS * tq
    q = q_ref[0].reshape(rows, MLA_QK)
    r = lax.broadcasted_iota(jnp.int32, (rows, 1), 0)
    qchunk = jnp.right_shift(past + jnp.bitwise_and(r, tq - 1), LOG2_CHUNK)

    @pl.when(j == 0)
    def _():
        m_ref[...] = jnp.full(m_ref.shape, -jnp.inf, F32)
        l_ref[...] = jnp.zeros(l_ref.shape, F32)
        acc_ref[...] = jnp.zeros(acc_ref.shape, F32)

    def fold(x, op):
        while x.shape[1] > LANES:
            half = x.shape[1] // 2
            x = op(x[:, :half], x[:, half:])
        return x

    def update(s, kpos, vals):
        s = jnp.where(jnp.right_shift(kpos, LOG2_CHUNK) <= qchunk, s, -jnp.inf)
        m_old = m_ref[...]
        m_new = jnp.maximum(m_old, jnp.max(fold(s, jnp.maximum), axis=-1, keepdims=True))
        alpha = jnp.exp(m_old - m_new)
        p = jnp.exp(s - m_new)
        l_ref[...] = alpha * l_ref[...] + jnp.sum(fold(p, jnp.add), axis=-1, keepdims=True)
        acc_ref[...] = alpha * acc_ref[...] + _dot(p.astype(BF16), vals)
        m_ref[...] = m_new

    ck = ckv_ref[0].astype(BF16)
    s = (_dot_nt(q[:, :MLA_KV_RANK], ck)
         + _dot_nt(q[:, MLA_KV_RANK:MLA_KV_RANK + MLA_ROPE], kr_ref[0].astype(BF16)))
    update(s, j * tk + lax.broadcasted_iota(jnp.int32, (1, tk), 1), ck)

    @pl.when(j == nkb - 1)
    def _():
        kn = knew_ref[0]
        update(_dot_nt(q, kn), past + lax.broadcasted_iota(jnp.int32, (1, tq), 1), kn[:, :MLA_KV_RANK])
        inv_l = 1.0 / l_ref[...]
        for h in range(MLA_HEADS):
            hs = slice(h * tq, (h + 1) * tq)
            lat = (acc_ref[hs, :] * inv_l[hs, :]).astype(BF16)
            o_ref[:, h * MLA_V:(h + 1) * MLA_V] = _dot(lat, wuv_ref[h]).astype(BF16)


def _attention(qcat, kcat, vt, wuv_t):
    nb, _, t, _ = qcat.shape
    tq = _tile(t, 128)
    tk = _tile(t, 512)
    assert tq & (tq - 1) == 0
    nq = t // tq
    cols = MLA_HEADS * tq
    vmem = (2 * (cols * MLA_QK * 2 + t * MLA_QK * 2 + MLA_KV_RANK * t * 2
                 + MLA_HEADS * MLA_KV_RANK * MLA_V * 2 + tq * MLA_HEADS * MLA_V * 2)
            + 2 * tk * cols * 4 + 2 * SUBLANES * cols * 4 + MLA_KV_RANK * cols * 4
            + 3 * tk * cols * 4 // ATTN_GROUPS + 4 * 2 ** 20)
    return pl.pallas_call(
        functools.partial(_attn_kernel, tq=tq, tk=tk, n_valid=t),
        grid=(nb, nq),
        in_specs=[pl.BlockSpec((1, MLA_HEADS, tq, MLA_QK), lambda b, i: (b, 0, i, 0)),
                  pl.BlockSpec((1, t, MLA_QK), lambda b, i: (b, 0, 0)),
                  pl.BlockSpec((1, MLA_KV_RANK, t), lambda b, i: (b, 0, 0)),
                  pl.BlockSpec((MLA_HEADS, MLA_KV_RANK, MLA_V), lambda b, i: (0, 0, 0))],
        out_specs=pl.BlockSpec((tq, MLA_HEADS * MLA_V), lambda b, i: (b * nq + i, 0)),
        out_shape=jax.ShapeDtypeStruct((nb * t, MLA_HEADS * MLA_V), BF16),
        scratch_shapes=[pltpu.VMEM((tk, cols), F32), pltpu.VMEM((tk, cols), F32), pltpu.VMEM((1, cols), F32),
                        pltpu.VMEM((1, cols), F32), pltpu.VMEM((MLA_KV_RANK, cols), F32)],
        compiler_params=_params(("parallel", "arbitrary"), vmem),
        name="mla_attention",
    )(qcat, kcat, vt, wuv_t)


def _attention_cached(qcat, ckv_past, krope_past, kcat_new, wuv_t):
    nb, _, t, _ = qcat.shape
    past = ckv_past.shape[1]
    tk = _tile(past, 1024)
    nkb = past // tk
    assert t & (t - 1) == 0
    rows = MLA_HEADS * t
    vmem = (2 * (rows * MLA_QK * 2 + tk * MLA_KV_RANK * 4 + tk * LANES * 4 + t * MLA_QK * 2
                 + MLA_HEADS * MLA_KV_RANK * MLA_V * 2 + t * MLA_HEADS * MLA_V * 2)
            + 2 * rows * LANES * 4 + rows * MLA_KV_RANK * 4 + tk * MLA_QK * 2 + 4 * rows * tk * 4 + 4 * 2 ** 20)
    return pl.pallas_call(
        functools.partial(_attn_cache_kernel, tq=t, tk=tk, nkb=nkb, past=past),
        grid=(nb, nkb),
        in_specs=[pl.BlockSpec((1, MLA_HEADS, t, MLA_QK), lambda b, j: (b, 0, 0, 0)),
                  pl.BlockSpec((1, tk, MLA_KV_RANK), lambda b, j: (b, j, 0)),
                  pl.BlockSpec((1, tk, MLA_ROPE), lambda b, j: (b, j, 0)),
                  pl.BlockSpec((1, t, MLA_QK), lambda b, j: (b, 0, 0)),
                  pl.BlockSpec((MLA_HEADS, MLA_KV_RANK, MLA_V), lambda b, j: (0, 0, 0))],
        out_specs=pl.BlockSpec((t, MLA_HEADS * MLA_V), lambda b, j: (b, 0)),
        out_shape=jax.ShapeDtypeStruct((nb * t, MLA_HEADS * MLA_V), BF16),
        scratch_shapes=[pltpu.VMEM((rows, 1), F32), pltpu.VMEM((rows, 1), F32),
                        pltpu.VMEM((rows, MLA_KV_RANK), F32)],
        compiler_params=_params(("parallel", "arbitrary"), vmem),
        name="mla_attention_cached",
    )(qcat, ckv_past, krope_past, kcat_new, wuv_t)


def _merge_out_kernel(og_ref, om_ref, wg_ref, wm_ref, gg_ref, gm_ref, wo_ref, x_ref, gpost_ref, gnext_ref,
                      xo_ref, xno_ref):
    tg = _dot(og_ref[...], wg_ref[...])
    tm_ = _dot(om_ref[...], wm_ref[...])
    merged = (jax.nn.sigmoid(gg_ref[...].astype(F32)) * tg
              + jax.nn.sigmoid(gm_ref[...].astype(F32)) * tm_).astype(BF16)
    y = _dot(merged, wo_ref[...])
    xnew = x_ref[...] + _rms(y, gpost_ref[...])
    xo_ref[...] = xnew
    xno_ref[...] = _rms(xnew, gnext_ref[...]).astype(BF16)


def _merge_out(og, om, wbg, wbm, proj_gate, wout, x, gpost, gnext):
    m, kg = og.shape
    d = wbg.shape[1]
    tm = _tile(m, 256)
    gsz = proj_gate.dtype.itemsize
    row = lambda i: (i, 0)
    fixed = lambda i: (0, 0)
    resident = dict(pipeline_mode=pl.Buffered(1))
    vmem = ((2 * kg * d + d * d) * 2
            + 2 * (2 * tm * kg * 2 + 2 * tm * d * gsz + 2 * tm * d * 4 + tm * d * 2)
            + 4 * tm * d * 4 + 2 * 2 ** 20)
    return pl.pallas_call(
        _merge_out_kernel,
        grid=(m // tm,),
        in_specs=[pl.BlockSpec((tm, kg), row),
                  pl.BlockSpec((tm, kg), row),
                  pl.BlockSpec((kg, d), fixed, **resident),
                  pl.BlockSpec((kg, d), fixed, **resident),
                  pl.BlockSpec((tm, d), lambda i: (i, 0)),
                  pl.BlockSpec((tm, d), lambda i: (i, 1)),
                  pl.BlockSpec((d, d), fixed, **resident),
                  pl.BlockSpec((tm, d), row),
                  pl.BlockSpec((1, d), fixed),
                  pl.BlockSpec((1, d), fixed)],
        out_specs=[pl.BlockSpec((tm, d), row), pl.BlockSpec((tm, d), row)],
        out_shape=[jax.ShapeDtypeStruct((m, d), F32), jax.ShapeDtypeStruct((m, d), BF16)],
        compiler_params=_params(("parallel",), vmem),
        name="merge_out",
    )(og, om, wbg, wbm, proj_gate, proj_gate, wout, x, gpost, gnext)


def _layer(x3, pos0, conv_state, ssm0, ckv_past, krope_past, w):
    nb, t, d = x3.shape
    m = nb * t
    x = x3.reshape(m, d)
    x1, hm, ffn1_w = _ffn(x, *w["ffn1_w"], w["ffn1_norm_post"],
                          gpre=w["ffn1_norm_pre"], gnext=w["mix_norm_pre"])
    proj_gdn = _matmul(hm, w["w_in_gdn"], F32)
    proj_gate = _matmul(hm, w["w_in_gate"], BF16)
    ckv, krope, qcat, kcat, bg, *vt = _mla_prep(hm, w["w_in_mla"], nb, t, pos0, w["inv128"], w["wuk_t"],
                                                w["kv_norm"], w["alog"], w["dtb"], ckv_past is None)

    o_gdn, ssm_new = _gdn(proj_gdn, bg, nb, t, conv_state, ssm0, w["conv_w"], w["gdn_nw"])
    conv_new = proj_gdn.reshape(nb, t, -1)[:, t - (GDN_CONV - 1):, COL_QKV:COL_QKV + GDN_CONV_DIM]

    if ckv_past is None:
        o_mla = _attention(qcat, kcat, vt[0], w["wuv_t"])
    else:
        o_mla = _attention_cached(qcat, ckv_past, krope_past, kcat, w["wuv_t"])

    x2, xn2 = _merge_out(o_gdn, o_mla, w["w_br_gdn"], w["w_br_mla"], proj_gate, w["w_out"], x1,
                         w["mix_norm_post"], w["ffn2_norm_pre"])
    x3_, _, ffn2_w = _ffn(x2, *w["ffn2_w"], w["ffn2_norm_post"], xn=xn2)
    return (x3_.reshape(nb, t, d), conv_new, ssm_new, ckv, krope), dict(ffn1_w=ffn1_w, ffn2_w=ffn2_w)


def _prep_weights(l, d_model, **p):
    row = lambda v: v[l].reshape(1, -1).astype(F32)
    pad_lanes = lambda v: jnp.pad(v[l].reshape(1, -1).astype(F32), ((0, 0), (0, LANES - v.shape[1])))
    inv = ROPE_THETA ** (-jnp.arange(0, MLA_ROPE, 2, dtype=F32) / MLA_ROPE)
    w_in_gdn, w_in_gate, w_in_mla = _build_w_in(p["w_in"][l], d_model)
    w = dict(
        w_in_gdn=w_in_gdn, w_in_gate=w_in_gate, w_in_mla=w_in_mla,
        ffn1_norm_pre=row(p["ffn1_norm_pre"]), ffn1_norm_post=row(p["ffn1_norm_post"]),
        mix_norm_pre=row(p["mix_norm_pre"]), mix_norm_post=row(p["mix_norm_post"]),
        ffn2_norm_pre=row(p["ffn2_norm_pre"]), ffn2_norm_post=row(p["ffn2_norm_post"]),
        ffn1_w=(p["ffn1_w_gate"][l], p["ffn1_w_up"][l], p["ffn1_w_down"][l]),
        ffn2_w=(p["ffn2_w_gate"][l], p["ffn2_w_up"][l], p["ffn2_w_down"][l]),
        conv_w=p["gdn_conv_w"][l].astype(F32),
        alog=pad_lanes(p["gdn_a_log"]), dtb=pad_lanes(p["gdn_dt_bias"]),
        gdn_nw=row(p["gdn_norm_w"]), kv_norm=row(p["mla_kv_norm"]),
        inv128=jnp.tile(inv, LANES // inv.shape[0]).reshape(1, LANES),
        wuk_t=jnp.transpose(p["mla_w_uk"][l], (1, 2, 0)).astype(BF16),
        wuv_t=jnp.transpose(p["mla_w_uv"][l], (1, 0, 2)).astype(BF16),
        w_br_gdn=p["w_br_gdn"][l].astype(BF16), w_br_mla=p["w_br_mla"][l].astype(BF16),
        w_out=p["w_out"][l].astype(BF16),
    )
    return w


def kernel(x_prompt, x_sample, state_gdn_conv, state_gdn_ssm, cache_mla_ckv, cache_mla_krope, ffn1_norm_pre, ffn1_w_gate, ffn1_w_up, ffn1_w_down, ffn1_norm_post, mix_norm_pre, w_in, gdn_conv_w, gdn_a_log, gdn_dt_bias, gdn_norm_w, mla_kv_norm, mla_w_uk, mla_w_uv, w_br_gdn, w_br_mla, w_out, mix_norm_post, ffn2_norm_pre, ffn2_w_gate, ffn2_w_up, ffn2_w_down, ffn2_norm_post):
    params = dict(
        ffn1_norm_pre=ffn1_norm_pre, ffn1_w_gate=ffn1_w_gate, ffn1_w_up=ffn1_w_up,
        ffn1_w_down=ffn1_w_down, ffn1_norm_post=ffn1_norm_post, mix_norm_pre=mix_norm_pre,
        w_in=w_in, gdn_conv_w=gdn_conv_w, gdn_a_log=gdn_a_log, gdn_dt_bias=gdn_dt_bias,
        gdn_norm_w=gdn_norm_w, mla_kv_norm=mla_kv_norm, mla_w_uk=mla_w_uk, mla_w_uv=mla_w_uv,
        w_br_gdn=w_br_gdn, w_br_mla=w_br_mla, w_out=w_out, mix_norm_post=mix_norm_post,
        ffn2_norm_pre=ffn2_norm_pre, ffn2_w_gate=ffn2_w_gate, ffn2_w_up=ffn2_w_up,
        ffn2_w_down=ffn2_w_down, ffn2_norm_post=ffn2_norm_post)
    depth = w_in.shape[0]
    d_model = x_prompt.shape[-1]
    b_p = x_prompt.shape[0]
    past = cache_mla_ckv.shape[2]
    yp, ys = x_prompt, x_sample
    outs_p, outs_s = [], []
    for l in range(depth):
        w = _prep_weights(l, d_model, **params)
        (ys, *rest_s), w_bf16 = _layer(
            ys, past, state_gdn_conv[l], state_gdn_ssm[l], cache_mla_ckv[l], cache_mla_krope[l], w)
        (yp, *rest_p), _ = _layer(
            yp, 0, jnp.zeros((b_p, GDN_CONV - 1, GDN_CONV_DIM), F32),
            jnp.zeros((b_p, GDN_HEADS, GDN_DK, GDN_DV), F32), None, None, {**w, **w_bf16})
        outs_p.append(rest_p)
        outs_s.append(rest_s)
    stack = lambda outs, i: jnp.stack([o[i] for o in outs])
    return (yp, ys,
            stack(outs_p, 0), stack(outs_p, 1), stack(outs_p, 2), stack(outs_p, 3),
            stack(outs_s, 0), stack(outs_s, 1), stack(outs_s, 2), stack(outs_s, 3))
```

```python
import functools

import jax
import jax.numpy as jnp
from jax import lax
from jax.experimental import pallas as pl
from jax.experimental.pallas import tpu as pltpu

F32 = jnp.float32
BF16 = jnp.bfloat16

CHUNK = 64
LOG2_CHUNK = 6
NORM_EPS = 1e-6
GDN_HEADS = 8
GDN_DK = 128
GDN_DV = 128
GDN_CONV = 4
GDN_KEY_DIM = GDN_HEADS * GDN_DK
GDN_VAL_DIM = GDN_HEADS * GDN_DV
GDN_CONV_DIM = 2 * GDN_KEY_DIM + GDN_VAL_DIM
MLA_HEADS = 8
MLA_NOPE = 128
MLA_ROPE = 64
MLA_V = 128
MLA_KV_RANK = 512
MLA_SCALE = (MLA_NOPE + MLA_ROPE) ** -0.5
ROPE_THETA = 10000.0

LANES = 128
SUBLANES = 8
VMEM_CAP_BYTES = 56 * 2 ** 20
MLA_QK = MLA_KV_RANK + LANES
HI = lax.Precision.HIGHEST


def _params(semantics, vmem_bytes):
    limit = int(min(max(vmem_bytes, 16 * 2 ** 20), VMEM_CAP_BYTES))
    return pltpu.CompilerParams(dimension_semantics=semantics, vmem_limit_bytes=limit)


def _dot(a, b, prec=None):
    return lax.dot_general(a, b, (((1,), (0,)), ((), ())), precision=prec,
                           preferred_element_type=F32)


def _dot_nt(a, b, prec=None):
    return lax.dot_general(a, b, (((1,), (1,)), ((), ())), precision=prec,
                           preferred_element_type=F32)


def _dot_tn(a, b, prec=None):
    return lax.dot_general(a, b, (((0,), (0,)), ((), ())), precision=prec,
                           preferred_element_type=F32)


def _rms(y, g):
    return y * lax.rsqrt(jnp.mean(y * y, axis=-1, keepdims=True) + NORM_EPS) * g


def _silu(x):
    return x * jax.nn.sigmoid(x)


def _tile(n, pref):
    t = min(n, pref)
    assert n % t == 0, (n, pref)
    return t


def _ffn_kernel(*refs, nf, norm_in, emit_next, cast_w):
    refs = list(refs)
    xn_ref = None if norm_in else refs.pop(0)
    x_ref, wg_ref, wu_ref, wd_ref = refs[:4]
    del refs[:4]
    gpre_ref = refs.pop(0) if norm_in else None
    gpost_ref = refs.pop(0)
    gnext_ref = refs.pop(0) if emit_next else None
    xo_ref = refs.pop(0)
    xno_ref = refs.pop(0) if emit_next else None
    if cast_w:
        wgo_ref, wuo_ref, wdo_ref = refs[:3]
        del refs[:3]
    acc_ref = refs.pop(0)
    if norm_in:
        xn_ref = refs.pop(0)
    f = pl.program_id(1)

    @pl.when(f == 0)
    def _():
        acc_ref[...] = jnp.zeros(acc_ref.shape, F32)
        if norm_in:
            xn_ref[...] = _rms(x_ref[...], gpre_ref[...]).astype(BF16)

    xn = xn_ref[...]
    wg, wu, wd = wg_ref[...], wu_ref[...], wd_ref[...]
    if cast_w:
        wg, wu, wd = wg.astype(BF16), wu.astype(BF16), wd.astype(BF16)
        wgo_ref[...] = wg
        wuo_ref[...] = wu
        wdo_ref[...] = wd
    gate = _dot(xn, wg)
    up = _dot(xn, wu)
    h = (_silu(gate) * up).astype(BF16)
    acc_ref[...] += _dot(h, wd)

    @pl.when(f == nf - 1)
    def _():
        xnew = x_ref[...] + 0.5 * _rms(acc_ref[...], gpost_ref[...])
        xo_ref[...] = xnew
        if emit_next:
            xno_ref[...] = _rms(xnew, gnext_ref[...]).astype(BF16)


def _ffn(x, wg, wu, wd, gpost, xn=None, gpre=None, gnext=None):
    norm_in = xn is None
    emit_next = gnext is not None
    cast_w = wg.dtype == F32
    assert norm_in == (gpre is not None)
    m, d = x.shape
    dff = wg.shape[1]
    tm = _tile(m, 512)
    tf = _tile(dff, 256 if cast_w else 512)
    nf = dff // tf
    assert not cast_w or m == tm
    wsz = wg.dtype.itemsize
    row = lambda i, f: (i, 0)
    vec = pl.BlockSpec((1, d), lambda i, f: (0, 0))
    w_specs = [pl.BlockSpec((d, tf), lambda i, f: (0, f)),
               pl.BlockSpec((d, tf), lambda i, f: (0, f)),
               pl.BlockSpec((tf, d), lambda i, f: (f, 0))]
    ins = [] if norm_in else [xn]
    in_specs = [] if norm_in else [pl.BlockSpec((tm, d), row)]
    ins += [x, wg, wu, wd]
    in_specs += [pl.BlockSpec((tm, d), row)] + w_specs
    for g in (gpre, gpost, gnext):
        if g is not None:
            ins.append(g)
            in_specs.append(vec)
    out_specs = [pl.BlockSpec((tm, d), row)]
    out_shape = [jax.ShapeDtypeStruct((m, d), F32)]
    scratch = [pltpu.VMEM((tm, d), F32)]
    if emit_next:
        out_specs.append(pl.BlockSpec((tm, d), row))
        out_shape.append(jax.ShapeDtypeStruct((m, d), BF16))
    if cast_w:
        out_specs += w_specs
        out_shape += [jax.ShapeDtypeStruct(w.shape, BF16) for w in (wg, wu, wd)]
    if norm_in:
        scratch.append(pltpu.VMEM((tm, d), BF16))
    vmem = (2 * (tm * d * 2 + tm * d * 4 + 3 * d * tf * wsz + tm * d * 4 + tm * d * 2)
            + (2 * 3 * d * tf * 2 + 3 * d * tf * 2 if cast_w else 0)
            + tm * d * 4 + 3 * tm * tf * 4 + 2 * tm * d * 4)
    res = pl.pallas_call(
        functools.partial(_ffn_kernel, nf=nf, norm_in=norm_in, emit_next=emit_next, cast_w=cast_w),
        grid=(m // tm, nf),
        in_specs=in_specs,
        out_specs=out_specs,
        out_shape=out_shape,
        scratch_shapes=scratch,
        compiler_params=_params(("parallel", "arbitrary"), vmem),
        name="ffn_cast" if cast_w else "ffn",
    )(*ins)
    res = list(res)
    x_new = res.pop(0)
    xn_next = res.pop(0) if emit_next else None
    return x_new, xn_next, (tuple(res) if cast_w else (wg, wu, wd))


def _mm_kernel(a_ref, w_ref, o_ref):
    o_ref[...] = _dot(a_ref[...], w_ref[...]).astype(o_ref.dtype)


def _matmul(a, w, out_dtype):
    m, k = a.shape
    n = w.shape[1]
    tm = _tile(m, 1024)
    tn = _tile(n, 1024)
    osz = jnp.dtype(out_dtype).itemsize
    vmem = 2 * (tm * k * 2 + k * tn * 2 + tm * tn * osz) + tm * tn * 4
    return pl.pallas_call(
        _mm_kernel,
        grid=(m // tm, n // tn),
        in_specs=[pl.BlockSpec((tm, k), lambda i, j: (i, 0)),
                  pl.BlockSpec((k, tn), lambda i, j: (0, j))],
        out_specs=pl.BlockSpec((tm, tn), lambda i, j: (i, j)),
        out_shape=jax.ShapeDtypeStruct((m, n), out_dtype),
        compiler_params=_params(("parallel", "arbitrary"), vmem),
        name="in_proj",
    )(a, w)


COL_QKV = 0
COL_Z = COL_QKV + GDN_CONV_DIM
COL_QNOPE = 0
COL_QROPE = COL_QNOPE + MLA_HEADS * MLA_NOPE
COL_CKV = COL_QROPE + MLA_HEADS * LANES
COL_KR = COL_CKV + MLA_KV_RANK
COL_B = COL_KR + LANES
COL_A = COL_B + LANES
N_MLA = 3072


def _build_w_in(w_in, d_model):
    sizes = (GDN_CONV_DIM, GDN_VAL_DIM, GDN_HEADS, GDN_HEADS, MLA_HEADS * (MLA_NOPE + MLA_ROPE),
             MLA_KV_RANK, MLA_ROPE, d_model, d_model)
    offs = [0]
    for s in sizes:
        offs.append(offs[-1] + s)
    w_gdn = w_in[:, :offs[2]].astype(BF16)
    w_gate = w_in[:, offs[7]:].astype(BF16)
    part = lambda i: w_in[:, offs[i]:offs[i + 1]].astype(BF16)
    b, a, qm, ckv, kr = (part(i) for i in range(2, 7))
    d = w_in.shape[0]
    half = MLA_ROPE // 2
    qm = qm.reshape(d, MLA_HEADS, MLA_NOPE + MLA_ROPE)
    qn = qm[:, :, :MLA_NOPE].reshape(d, MLA_HEADS * MLA_NOPE)
    qr = qm[:, :, MLA_NOPE:]
    qr_pair = jnp.concatenate([qr, qr[:, :, half:], qr[:, :, :half]], axis=2).reshape(d, MLA_HEADS * LANES)
    kr_pair = jnp.concatenate([kr, kr[:, half:], kr[:, :half]], axis=1)
    pad8 = lambda w: jnp.pad(w, ((0, 0), (0, LANES - w.shape[1])))
    w_mla = jnp.concatenate([qn, qr_pair, ckv, kr_pair, pad8(b), pad8(a)], axis=1)
    w_mla = jnp.pad(w_mla, ((0, 0), (0, N_MLA - w_mla.shape[1])))
    return w_gdn, w_gate, w_mla


_NN = (((1,), (0,)), ((), ()))
_NT = (((1,), (1,)), ((), ()))
_TN = (((0,), (0,)), ((), ()))
INV_BLOCK = 16


def _mm(a, b, dims=_NN):
    return lax.dot_general(a.astype(BF16), b.astype(BF16), dims, preferred_element_type=F32)


def _gdn_kernel(qkv_ref, z_ref, beta_ref, g_ref, cst_ref, cw_ref, nw_ref, s0_ref,
                o_ref, sout_ref, past_ref, s_ref, *, G, C, nlev, nc):
    c = pl.program_id(1)

    @pl.when(c == 0)
    def _():
        past_ref[...] = cst_ref[...]
        s_ref[...] = s0_ref[...]

    ri = lax.broadcasted_iota(jnp.int32, (C, C), 0)
    ci = lax.broadcasted_iota(jnp.int32, (C, C), 1)
    incl = ri >= ci
    strict = ri > ci
    eye = (ri == ci).astype(F32)
    log2_blk = INV_BLOCK.bit_length() - 1
    bdiag = jnp.right_shift(ri, log2_blk) == jnp.right_shift(ci, log2_blk)

    first_row = lax.broadcasted_iota(jnp.int32, (C, 1), 0) == 0

    def conv(s, col):
        cols = slice(col, col + LANES)
        x = qkv_ref[s, :, cols]
        past = past_ref[s, :, cols]
        w = [cw_ref[j:j + 1, cols] for j in range(GDN_CONV)]
        acc = w[0] * x
        for j in range(1, GDN_CONV):
            top = sum(w[i] * past[SUBLANES - 1 - (j - 1 - i):SUBLANES - (j - 1 - i)] for i in range(j))
            acc = w[j] * x + jnp.where(first_row, top, pltpu.roll(acc, 1, axis=0))
        return _silu(acc)

    units = [(s, h) for s in range(G) for h in range(GDN_HEADS)]
    every = lambda fn, *lists: [fn(*args) for args in zip(*lists)]
    mm_inv = _mm
    beta_all, gcum, gcum_t = [], [], []
    for s in range(G):
        beta_all.append(beta_ref[s])
        g_all = g_ref[s]
        gcum.append(_dot(incl.astype(F32), g_all, HI))
        gcum_t.append(_dot_tn(g_all, (ci >= ri).astype(F32), HI))

    def l2n(x):
        return x * lax.rsqrt(jnp.sum(x * x, axis=-1, keepdims=True) + 1e-6)

    q = [l2n(conv(s, h * GDN_DK)) * (GDN_DK ** -0.5) for s, h in units]
    k = [l2n(conv(s, GDN_KEY_DIM + h * GDN_DK)) for s, h in units]
    v = [conv(s, 2 * GDN_KEY_DIM + h * GDN_DV) for s, h in units]
    bcol = [beta_all[s][:, h:h + 1] for s, h in units]
    gcol = [gcum[s][:, h:h + 1] for s, h in units]
    glast = [gcum[s][C - 1:C, h:h + 1] for s, h in units]
    decay = [jnp.where(incl, jnp.exp(jnp.where(incl, gcum[s][:, h:h + 1] - gcum_t[s][h:h + 1, :], 0.0)), 0.0)
             for s, h in units]
    qkk = every(lambda q_, k_: _mm(jnp.concatenate([q_, k_], axis=0), k_, _NT), q, k)
    qk = every(lambda x, d: x[:C] * d, qkk, decay)
    nmat = every(lambda x, b_, d: -jnp.where(strict, b_ * x[C:] * d, 0.0), qkk, bcol, decay)
    ndiag = every(lambda n_: jnp.where(bdiag, n_, 0.0), nmat)
    tinv = every(lambda n_: eye + n_, ndiag)
    npow = ndiag
    for _ in range(INV_BLOCK.bit_length() - 2):
        npow = every(mm_inv, npow, npow)
        tinv = every(lambda t_, p_: t_ + mm_inv(t_, p_), tinv, npow)
    mpow = every(lambda t_, n_, d_: mm_inv(t_, n_ - d_), tinv, nmat, ndiag)
    for lev in range(nlev):
        if lev:
            mpow = every(mm_inv, mpow, mpow)
        tinv = every(lambda t_, m_: t_ + mm_inv(m_, t_), tinv, mpow)
    egc = every(jnp.exp, gcol)
    sol = every(lambda t_, v_, k_, b_, e_: _mm(t_, jnp.concatenate([v_ * b_, k_ * (b_ * e_)], axis=1)),
                tinv, v, k, bcol, egc)
    s_old = [s_ref[s, h] for s, h in units]
    ws_qs = every(lambda x, q_, e_, s_: _mm(jnp.concatenate([x[:, GDN_DV:], q_ * e_], axis=0), s_),
                  sol, q, egc, s_old)
    v_new = every(lambda x, y: x[:, :GDN_DV] - y[:C], sol, ws_qs)
    o = every(lambda y, a_, vn: y[C:] + _mm(a_, vn), ws_qs, qk, v_new)
    s_new = every(lambda s_, gl, k_, gc, vn: s_ * jnp.exp(gl) + _mm(k_ * jnp.exp(gl - gc), vn, _TN),
                  s_old, glast, k, gcol, v_new)
    for (s, h), sn, o_ in zip(units, s_new, o):
        s_ref[s, h] = sn
        zz = z_ref[s, :, h * GDN_DV:(h + 1) * GDN_DV]
        o_ref[s, :, h * GDN_DV:(h + 1) * GDN_DV] = (_rms(o_, nw_ref[...]) * _silu(zz)).astype(BF16)

    past_ref[...] = qkv_ref[:, C - SUBLANES:C, :]

    @pl.when(c == nc - 1)
    def _():
        sout_ref[...] = s_ref[...]


def _gdn(proj_gdn, bg, nb, t, conv_state, ssm0, conv_w, nw):
    C = min(CHUNK, t)
    G = 2 if nb % 2 == 0 else 1
    assert t % C == 0 and C & (C - 1) == 0 and C % INV_BLOCK == 0
    nc = t // C
    nlev = (C // INV_BLOCK).bit_length() - 1
    pg = proj_gdn.reshape(nb, t, proj_gdn.shape[1])
    pm = bg.reshape(nb, t, bg.shape[1])
    fixed = lambda b, c: (0, 0)
    cst = jnp.pad(conv_state, ((0, 0), (SUBLANES - (GDN_CONV - 1), 0), (0, 0)))
    vmem = G * (2 * (C * GDN_CONV_DIM * 4 + C * GDN_VAL_DIM * 4 + 2 * C * LANES * 4 + SUBLANES * GDN_CONV_DIM * 4
                     + C * GDN_VAL_DIM * 2 + 2 * GDN_HEADS * GDN_DK * GDN_DV * 4)
                + SUBLANES * GDN_CONV_DIM * 4 + GDN_HEADS * GDN_DK * GDN_DV * 4) + 16 * 2 ** 20
    o, s_new = pl.pallas_call(
        functools.partial(_gdn_kernel, G=G, C=C, nlev=nlev, nc=nc),
        grid=(nb // G, nc),
        in_specs=[pl.BlockSpec((G, C, GDN_CONV_DIM), lambda b, c: (b, c, COL_QKV // GDN_CONV_DIM)),
                  pl.BlockSpec((G, C, GDN_VAL_DIM), lambda b, c: (b, c, COL_Z // GDN_VAL_DIM)),
                  pl.BlockSpec((G, C, LANES), lambda b, c: (b, c, 0)),
                  pl.BlockSpec((G, C, LANES), lambda b, c: (b, c, 1)),
                  pl.BlockSpec((G, SUBLANES, GDN_CONV_DIM), lambda b, c: (b, 0, 0)),
                  pl.BlockSpec((GDN_CONV, GDN_CONV_DIM), fixed),
                  pl.BlockSpec((1, GDN_DV), fixed),
                  pl.BlockSpec((G, GDN_HEADS, GDN_DK, GDN_DV), lambda b, c: (b, 0, 0, 0))],
        out_specs=[pl.BlockSpec((G, C, GDN_VAL_DIM), lambda b, c: (b, c, 0)),
                   pl.BlockSpec((G, GDN_HEADS, GDN_DK, GDN_DV), lambda b, c: (b, 0, 0, 0))],
        out_shape=[jax.ShapeDtypeStruct((nb, t, GDN_VAL_DIM), BF16),
                   jax.ShapeDtypeStruct((nb, GDN_HEADS, GDN_DK, GDN_DV), F32)],
        scratch_shapes=[pltpu.VMEM((G, SUBLANES, GDN_CONV_DIM), F32),
                        pltpu.VMEM((G, GDN_HEADS, GDN_DK, GDN_DV), F32)],
        compiler_params=_params(("parallel", "arbitrary"), vmem),
        name="gdn",
    )(pg, pg, pm, pm, cst, conv_w, nw, ssm0)
    return o.reshape(nb * t, GDN_VAL_DIM), s_new


def _mla_prep_kernel(hm_ref, w_ref, inv_ref, wuk_ref, kvn_ref, alog_ref, dtb_ref,
                     ckv_o, kr_o, qcat_o, kcat_o, bg_o, *maybe_vt_o, tm, pos0):
    j = pl.program_id(1)
    proj = _dot(hm_ref[...], w_ref[...])
    a_all = proj[:, COL_A:COL_A + LANES] + dtb_ref[...]
    softplus = jnp.maximum(a_all, 0.0) + jnp.log(1.0 + jnp.exp(-jnp.abs(a_all)))
    bg_o[:, :LANES] = jax.nn.sigmoid(proj[:, COL_B:COL_B + LANES])
    bg_o[:, LANES:] = -jnp.exp(alog_ref[...]) * softplus
    pos = (pos0 + j * tm + lax.broadcasted_iota(jnp.int32, (tm, LANES), 0)).astype(F32)
    lane = lax.broadcasted_iota(jnp.int32, (tm, LANES), 1)
    ang = pos * inv_ref[...]
    cos = jnp.cos(ang)
    sin = jnp.sin(ang)
    half = MLA_ROPE // 2
    cs = jnp.where(lane < MLA_ROPE, cos, jnp.where(lane < MLA_ROPE + half, -sin, sin))
    keep = lane < MLA_ROPE

    def rope(pair):
        prod = pair * cs
        return jnp.where(keep, prod + pltpu.roll(prod, MLA_ROPE, axis=1), 0.0)

    ckv = _rms(proj[:, COL_CKV:COL_CKV + MLA_KV_RANK], kvn_ref[...])
    ckv_o[0] = ckv
    kr = rope(proj[:, COL_KR:COL_KR + LANES])
    kr_o[0] = kr[:, :MLA_ROPE]
    kcat_o[0, :, :MLA_KV_RANK] = ckv.astype(BF16)
    kcat_o[0, :, MLA_KV_RANK:] = kr.astype(BF16)
    if maybe_vt_o:
        maybe_vt_o[0][0] = ckv.T.astype(BF16)
    for h in range(MLA_HEADS):
        qn = proj[:, COL_QNOPE + h * MLA_NOPE:COL_QNOPE + (h + 1) * MLA_NOPE].astype(BF16)
        qlat = _dot(qn, wuk_ref[h]) * MLA_SCALE
        qcat_o[0, h, :, :MLA_KV_RANK] = qlat.astype(BF16)
        qr = rope(proj[:, COL_QROPE + h * LANES:COL_QROPE + (h + 1) * LANES]) * MLA_SCALE
        qcat_o[0, h, :, MLA_KV_RANK:] = qr.astype(BF16)


def _mla_prep(hm, w_mla, nb, t, pos0, inv128, wuk_t, kvn, alog, dtb, emit_vt):
    d = hm.shape[1]
    tm = _tile(t, 512)
    nt = t // tm
    rows = lambda b, j: b * nt + j
    fixed2 = lambda b, j: (0, 0)
    vmem = (d * N_MLA * 2 + 2 * (tm * d * 2 + MLA_HEADS * MLA_NOPE * MLA_KV_RANK * 2
                                 + tm * (MLA_KV_RANK + 3 * LANES) * 4 + (MLA_HEADS + 2) * tm * MLA_QK * 2)
            + 2 * tm * N_MLA * 4 + 8 * 2 ** 20)
    out_specs = [pl.BlockSpec((1, tm, MLA_KV_RANK), lambda b, j: (b, j, 0)),
                 pl.BlockSpec((1, tm, MLA_ROPE), lambda b, j: (b, j, 0)),
                 pl.BlockSpec((1, MLA_HEADS, tm, MLA_QK), lambda b, j: (b, 0, j, 0)),
                 pl.BlockSpec((1, tm, MLA_QK), lambda b, j: (b, j, 0)),
                 pl.BlockSpec((tm, 2 * LANES), lambda b, j: (rows(b, j), 0))]
    out_shape = [jax.ShapeDtypeStruct((nb, t, MLA_KV_RANK), F32),
                 jax.ShapeDtypeStruct((nb, t, MLA_ROPE), F32),
                 jax.ShapeDtypeStruct((nb, MLA_HEADS, t, MLA_QK), BF16),
                 jax.ShapeDtypeStruct((nb, t, MLA_QK), BF16),
                 jax.ShapeDtypeStruct((nb * t, 2 * LANES), F32)]
    if emit_vt:
        out_specs.append(pl.BlockSpec((1, MLA_KV_RANK, tm), lambda b, j: (b, 0, j)))
        out_shape.append(jax.ShapeDtypeStruct((nb, MLA_KV_RANK, t), BF16))
    return pl.pallas_call(
        functools.partial(_mla_prep_kernel, tm=tm, pos0=pos0),
        grid=(nb, nt),
        in_specs=[pl.BlockSpec((tm, d), lambda b, j: (rows(b, j), 0)),
                  pl.BlockSpec((d, N_MLA), fixed2, pipeline_mode=pl.Buffered(1)),
                  pl.BlockSpec((1, LANES), fixed2),
                  pl.BlockSpec((MLA_HEADS, MLA_NOPE, MLA_KV_RANK), lambda b, j: (0, 0, 0)),
                  pl.BlockSpec((1, MLA_KV_RANK), fixed2),
                  pl.BlockSpec((1, LANES), fixed2),
                  pl.BlockSpec((1, LANES), fixed2)],
        out_specs=out_specs,
        out_shape=out_shape,
        compiler_params=_params(("parallel", "parallel"), vmem),
        name="mla_proj_prep",
    )(hm, w_mla, inv128, wuk_t, kvn, alog, dtb)


ATTN_GROUPS = 2
ATTN_CHAIN = 4


def _attn_kernel(q_ref, k_ref, vt_ref, wuv_ref, o_ref, s0_ref, s1_ref, m_ref, l_ref, acc_ref,
                 *, tq, tk, nq, n_valid):
    i = pl.program_id(1)
    cols = MLA_HEADS * tq
    cg = cols // ATTN_GROUPS
    c = lax.broadcasted_iota(jnp.int32, (1, cg), 1)
    tok = jnp.bitwise_and(c, tq - 1)

    def scores(sub, j, buf):
        start = pl.multiple_of(j * tk, tk)
        q = q_ref[0, :, sub * tq:(sub + 1) * tq, :].reshape(cols, MLA_QK)
        buf[...] = _dot_nt(k_ref[0, pl.ds(start, tk), :], q)

    def fold(x, op):
        while x.shape[0] > SUBLANES:
            half = x.shape[0] // 2
            x = op(x[:half], x[half:])
        return x

    def softmax_values(qchunk, j, buf, masked):
        start = pl.multiple_of(j * tk, tk)
        vt = vt_ref[0, :, pl.ds(start, tk)]
        if masked:
            kpos = start + lax.broadcasted_iota(jnp.int32, (tk, 1), 0)
            mask = jnp.logical_and(jnp.right_shift(kpos, LOG2_CHUNK) <= qchunk, kpos < n_valid)
        for g in range(ATTN_GROUPS):
            cs = slice(g * cg, (g + 1) * cg)
            s = buf[:, cs]
            if masked:
                s = jnp.where(mask, s, -jnp.inf)
            m_old = m_ref[:, cs]
            m_new = jnp.maximum(m_old, jnp.max(fold(s, jnp.maximum), axis=0, keepdims=True))
            alpha = jnp.exp(m_old - m_new)
            p = jnp.exp(s - m_new)
            l_ref[:, cs] = alpha * l_ref[:, cs] + jnp.sum(fold(p, jnp.add), axis=0, keepdims=True)
            acc_ref[:, cs] = alpha * acc_ref[:, cs] + _dot(vt, p.astype(BF16))
            m_ref[:, cs] = m_new

    def by_parity(j, fn):
        @pl.when(jnp.bitwise_and(j, 1) == 0)
        def _():
            fn(s0_ref, s1_ref)

        @pl.when(jnp.bitwise_and(j, 1) == 1)
        def _():
            fn(s1_ref, s0_ref)

    scores(0, 0, s0_ref)
    g0 = 0
    for sub in range(nq):
        q_first = (i * nq + sub) * tq
        qchunk = jnp.right_shift(q_first + tok, LOG2_CHUNK)
        k_all = jnp.minimum((q_first // CHUNK + 1) * CHUNK, n_valid)
        k_any = jnp.minimum(((q_first + tq - 1) // CHUNK + 1) * CHUNK, n_valid)
        nk = (k_any + tk - 1) // tk
        m_ref[...] = jnp.full(m_ref.shape, -jnp.inf, F32)
        l_ref[...] = jnp.zeros(l_ref.shape, F32)
        acc_ref[...] = jnp.zeros(acc_ref.shape, F32)

        def body(j, carry, masked, sub=sub, qchunk=qchunk, g0=g0):
            def step(cur, nxt):
                scores(sub, j + 1, nxt)
                softmax_values(qchunk, j, cur, masked)
            by_parity(g0 + j, step)
            return carry

        n_open = jnp.minimum(k_all // tk, nk - 1)
        lax.fori_loop(0, n_open, functools.partial(body, masked=False), 0)
        lax.fori_loop(n_open, nk - 1, functools.partial(body, masked=True), 0)

        def last(cur, nxt, sub=sub, qchunk=qchunk, nk=nk):
            if sub + 1 < nq:
                scores(sub + 1, 0, nxt)
            softmax_values(qchunk, nk - 1, cur, True)

        by_parity(g0 + nk - 1, last)
        g0 = g0 + nk
        inv_l = 1.0 / l_ref[...]
        for h in range(MLA_HEADS):
            hs = slice(h * tq, (h + 1) * tq)
            lat_t = (acc_ref[:, hs] * inv_l[:, hs]).astype(BF16)
            o_ref[sub * tq:(sub + 1) * tq, h * MLA_V:(h + 1) * MLA_V] = _dot_tn(lat_t, wuv_ref[h]).astype(BF16)


def _attn_cache_kernel(q_ref, ckv_ref, kr_ref, knew_ref, wuv_ref, o_ref, m_ref, l_ref, acc_ref,
                       *, tq, tk, nkb, past):
    j = pl.program_id(1)
    rows = MLA_HEADS * tq
    q = q_ref[0].reshape(rows, MLA_QK)
    r = lax.broadcasted_iota(jnp.int32, (rows, 1), 0)
    qchunk = jnp.right_shift(past + jnp.bitwise_and(r, tq - 1), LOG2_CHUNK)

    @pl.when(j == 0)
    def _():
        m_ref[...] = jnp.full(m_ref.shape, -jnp.inf, F32)
        l_ref[...] = jnp.zeros(l_ref.shape, F32)
        acc_ref[...] = jnp.zeros(acc_ref.shape, F32)

    def fold(x, op):
        while x.shape[1] > LANES:
            half = x.shape[1] // 2
            x = op(x[:, :half], x[:, half:])
        return x

    def update(s, kpos, vals):
        s = jnp.where(jnp.right_shift(kpos, LOG2_CHUNK) <= qchunk, s, -jnp.inf)
        m_old = m_ref[...]
        m_new = jnp.maximum(m_old, jnp.max(fold(s, jnp.maximum), axis=-1, keepdims=True))
        alpha = jnp.exp(m_old - m_new)
        p = jnp.exp(s - m_new)
        l_ref[...] = alpha * l_ref[...] + jnp.sum(fold(p, jnp.add), axis=-1, keepdims=True)
        acc_ref[...] = alpha * acc_ref[...] + _dot(p.astype(BF16), vals)
        m_ref[...] = m_new

    ck = ckv_ref[0].astype(BF16)
    s = (_dot_nt(q[:, :MLA_KV_RANK], ck)
         + _dot_nt(q[:, MLA_KV_RANK:MLA_KV_RANK + MLA_ROPE], kr_ref[0].astype(BF16)))
    update(s, j * tk + lax.broadcasted_iota(jnp.int32, (1, tk), 1), ck)

    @pl.when(j == nkb - 1)
    def _():
        kn = knew_ref[0]
        update(_dot_nt(q, kn), past + lax.broadcasted_iota(jnp.int32, (1, tq), 1), kn[:, :MLA_KV_RANK])
        inv_l = 1.0 / l_ref[...]
        for h in range(MLA_HEADS):
            hs = slice(h * tq, (h + 1) * tq)
            lat = (acc_ref[hs, :] * inv_l[hs, :]).astype(BF16)
            o_ref[:, h * MLA_V:(h + 1) * MLA_V] = _dot(lat, wuv_ref[h]).astype(BF16)


def _attention(qcat, kcat, vt, wuv_t):
    nb, _, t, _ = qcat.shape
    tq = _tile(t, 128)
    tk = _tile(t, 512)
    assert tq & (tq - 1) == 0
    nq = ATTN_CHAIN if (t // tq) % ATTN_CHAIN == 0 else 1
    steps = t // (tq * nq)
    cols = MLA_HEADS * tq
    vmem = (2 * (nq * cols * MLA_QK * 2 + t * MLA_QK * 2 + MLA_KV_RANK * t * 2
                 + MLA_HEADS * MLA_KV_RANK * MLA_V * 2 + nq * tq * MLA_HEADS * MLA_V * 2)
            + 2 * tk * cols * 4 + 2 * SUBLANES * cols * 4 + MLA_KV_RANK * cols * 4
            + 3 * tk * cols * 4 // ATTN_GROUPS + 4 * 2 ** 20)
    return pl.pallas_call(
        functools.partial(_attn_kernel, tq=tq, tk=tk, nq=nq, n_valid=t),
        grid=(nb, steps),
        in_specs=[pl.BlockSpec((1, MLA_HEADS, nq * tq, MLA_QK), lambda b, i: (b, 0, i, 0)),
                  pl.BlockSpec((1, t, MLA_QK), lambda b, i: (b, 0, 0)),
                  pl.BlockSpec((1, MLA_KV_RANK, t), lambda b, i: (b, 0, 0)),
                  pl.BlockSpec((MLA_HEADS, MLA_KV_RANK, MLA_V), lambda b, i: (0, 0, 0))],
        out_specs=pl.BlockSpec((nq * tq, MLA_HEADS * MLA_V), lambda b, i: (b * steps + i, 0)),
        out_shape=jax.ShapeDtypeStruct((nb * t, MLA_HEADS * MLA_V), BF16),
        scratch_shapes=[pltpu.VMEM((tk, cols), F32), pltpu.VMEM((tk, cols), F32), pltpu.VMEM((1, cols), F32),
                        pltpu.VMEM((1, cols), F32), pltpu.VMEM((MLA_KV_RANK, cols), F32)],
        compiler_params=_params(("parallel", "arbitrary"), vmem),
        name="mla_attention",
    )(qcat, kcat, vt, wuv_t)


def _attention_cached(qcat, ckv_past, krope_past, kcat_new, wuv_t):
    nb, _, t, _ = qcat.shape
    past = ckv_past.shape[1]
    tk = _tile(past, 1024)
    nkb = past // tk
    assert t & (t - 1) == 0
    rows = MLA_HEADS * t
    vmem = (2 * (rows * MLA_QK * 2 + tk * MLA_KV_RANK * 4 + tk * LANES * 4 + t * MLA_QK * 2
                 + MLA_HEADS * MLA_KV_RANK * MLA_V * 2 + t * MLA_HEADS * MLA_V * 2)
            + 2 * rows * LANES * 4 + rows * MLA_KV_RANK * 4 + tk * MLA_QK * 2 + 4 * rows * tk * 4 + 4 * 2 ** 20)
    return pl.pallas_call(
        functools.partial(_attn_cache_kernel, tq=t, tk=tk, nkb=nkb, past=past),
        grid=(nb, nkb),
        in_specs=[pl.BlockSpec((1, MLA_HEADS, t, MLA_QK), lambda b, j: (b, 0, 0, 0)),
                  pl.BlockSpec((1, tk, MLA_KV_RANK), lambda b, j: (b, j, 0)),
                  pl.BlockSpec((1, tk, MLA_ROPE), lambda b, j: (b, j, 0)),
                  pl.BlockSpec((1, t, MLA_QK), lambda b, j: (b, 0, 0)),
                  pl.BlockSpec((MLA_HEADS, MLA_KV_RANK, MLA_V), lambda b, j: (0, 0, 0))],
        out_specs=pl.BlockSpec((t, MLA_HEADS * MLA_V), lambda b, j: (b, 0)),
        out_shape=jax.ShapeDtypeStruct((nb * t, MLA_HEADS * MLA_V), BF16),
        scratch_shapes=[pltpu.VMEM((rows, 1), F32), pltpu.VMEM((rows, 1), F32),
                        pltpu.VMEM((rows, MLA_KV_RANK), F32)],
        compiler_params=_params(("parallel", "arbitrary"), vmem),
        name="mla_attention_cached",
    )(qcat, ckv_past, krope_past, kcat_new, wuv_t)


def _merge_out_kernel(og_ref, om_ref, wg_ref, wm_ref, gg_ref, gm_ref, wo_ref, x_ref, gpost_ref, gnext_ref,
                      xo_ref, xno_ref):
    tg = _dot(og_ref[...], wg_ref[...])
    tm_ = _dot(om_ref[...], wm_ref[...])
    merged = (jax.nn.sigmoid(gg_ref[...].astype(F32)) * tg
              + jax.nn.sigmoid(gm_ref[...].astype(F32)) * tm_).astype(BF16)
    y = _dot(merged, wo_ref[...])
    xnew = x_ref[...] + _rms(y, gpost_ref[...])
    xo_ref[...] = xnew
    xno_ref[...] = _rms(xnew, gnext_ref[...]).astype(BF16)


def _merge_out(og, om, wbg, wbm, proj_gate, wout, x, gpost, gnext):
    m, kg = og.shape
    d = wbg.shape[1]
    tm = _tile(m, 256)
    gsz = proj_gate.dtype.itemsize
    row = lambda i: (i, 0)
    fixed = lambda i: (0, 0)
    resident = dict(pipeline_mode=pl.Buffered(1))
    vmem = ((2 * kg * d + d * d) * 2
            + 2 * (2 * tm * kg * 2 + 2 * tm * d * gsz + 2 * tm * d * 4 + tm * d * 2)
            + 4 * tm * d * 4 + 2 * 2 ** 20)
    return pl.pallas_call(
        _merge_out_kernel,
        grid=(m // tm,),
        in_specs=[pl.BlockSpec((tm, kg), row),
                  pl.BlockSpec((tm, kg), row),
                  pl.BlockSpec((kg, d), fixed, **resident),
                  pl.BlockSpec((kg, d), fixed, **resident),
                  pl.BlockSpec((tm, d), lambda i: (i, 0)),
                  pl.BlockSpec((tm, d), lambda i: (i, 1)),
                  pl.BlockSpec((d, d), fixed, **resident),
                  pl.BlockSpec((tm, d), row),
                  pl.BlockSpec((1, d), fixed),
                  pl.BlockSpec((1, d), fixed)],
        out_specs=[pl.BlockSpec((tm, d), row), pl.BlockSpec((tm, d), row)],
        out_shape=[jax.ShapeDtypeStruct((m, d), F32), jax.ShapeDtypeStruct((m, d), BF16)],
        compiler_params=_params(("parallel",), vmem),
        name="merge_out",
    )(og, om, wbg, wbm, proj_gate, proj_gate, wout, x, gpost, gnext)


def _layer(x3, pos0, conv_state, ssm0, ckv_past, krope_past, w):
    nb, t, d = x3.shape
    m = nb * t
    x = x3.reshape(m, d)
    x1, hm, ffn1_w = _ffn(x, *w["ffn1_w"], w["ffn1_norm_post"],
                          gpre=w["ffn1_norm_pre"], gnext=w["mix_norm_pre"])
    proj_gdn = _matmul(hm, w["w_in_gdn"], F32)
    proj_gate = _matmul(hm, w["w_in_gate"], BF16)
    ckv, krope, qcat, kcat, bg, *vt = _mla_prep(hm, w["w_in_mla"], nb, t, pos0, w["inv128"], w["wuk_t"],
                                                w["kv_norm"], w["alog"], w["dtb"], ckv_past is None)

    o_gdn, ssm_new = _gdn(proj_gdn, bg, nb, t, conv_state, ssm0, w["conv_w"], w["gdn_nw"])
    conv_new = proj_gdn.reshape(nb, t, -1)[:, t - (GDN_CONV - 1):, COL_QKV:COL_QKV + GDN_CONV_DIM]

    if ckv_past is None:
        o_mla = _attention(qcat, kcat, vt[0], w["wuv_t"])
    else:
        o_mla = _attention_cached(qcat, ckv_past, krope_past, kcat, w["wuv_t"])

    x2, xn2 = _merge_out(o_gdn, o_mla, w["w_br_gdn"], w["w_br_mla"], proj_gate, w["w_out"], x1,
                         w["mix_norm_post"], w["ffn2_norm_pre"])
    x3_, _, ffn2_w = _ffn(x2, *w["ffn2_w"], w["ffn2_norm_post"], xn=xn2)
    return (x3_.reshape(nb, t, d), conv_new, ssm_new, ckv, krope), dict(ffn1_w=ffn1_w, ffn2_w=ffn2_w)


def _prep_weights(l, d_model, **p):
    row = lambda v: v[l].reshape(1, -1).astype(F32)
    pad_lanes = lambda v: jnp.pad(v[l].reshape(1, -1).astype(F32), ((0, 0), (0, LANES - v.shape[1])))
    inv = ROPE_THETA ** (-jnp.arange(0, MLA_ROPE, 2, dtype=F32) / MLA_ROPE)
    w_in_gdn, w_in_gate, w_in_mla = _build_w_in(p["w_in"][l], d_model)
    w = dict(
        w_in_gdn=w_in_gdn, w_in_gate=w_in_gate, w_in_mla=w_in_mla,
        ffn1_norm_pre=row(p["ffn1_norm_pre"]), ffn1_norm_post=row(p["ffn1_norm_post"]),
        mix_norm_pre=row(p["mix_norm_pre"]), mix_norm_post=row(p["mix_norm_post"]),
        ffn2_norm_pre=row(p["ffn2_norm_pre"]), ffn2_norm_post=row(p["ffn2_norm_post"]),
        ffn1_w=(p["ffn1_w_gate"][l], p["ffn1_w_up"][l], p["ffn1_w_down"][l]),
        ffn2_w=(p["ffn2_w_gate"][l], p["ffn2_w_up"][l], p["ffn2_w_down"][l]),
        conv_w=p["gdn_conv_w"][l].astype(F32),
        alog=pad_lanes(p["gdn_a_log"]), dtb=pad_lanes(p["gdn_dt_bias"]),
        gdn_nw=row(p["gdn_norm_w"]), kv_norm=row(p["mla_kv_norm"]),
        inv128=jnp.tile(inv, LANES // inv.shape[0]).reshape(1, LANES),
        wuk_t=jnp.transpose(p["mla_w_uk"][l], (1, 2, 0)).astype(BF16),
        wuv_t=jnp.transpose(p["mla_w_uv"][l], (1, 0, 2)).astype(BF16),
        w_br_gdn=p["w_br_gdn"][l].astype(BF16), w_br_mla=p["w_br_mla"][l].astype(BF16),
        w_out=p["w_out"][l].astype(BF16),
    )
    return w


def kernel(x_prompt, x_sample, state_gdn_conv, state_gdn_ssm, cache_mla_ckv, cache_mla_krope, ffn1_norm_pre, ffn1_w_gate, ffn1_w_up, ffn1_w_down, ffn1_norm_post, mix_norm_pre, w_in, gdn_conv_w, gdn_a_log, gdn_dt_bias, gdn_norm_w, mla_kv_norm, mla_w_uk, mla_w_uv, w_br_gdn, w_br_mla, w_out, mix_norm_post, ffn2_norm_pre, ffn2_w_gate, ffn2_w_up, ffn2_w_down, ffn2_norm_post):
    params = dict(
        ffn1_norm_pre=ffn1_norm_pre, ffn1_w_gate=ffn1_w_gate, ffn1_w_up=ffn1_w_up,
        ffn1_w_down=ffn1_w_down, ffn1_norm_post=ffn1_norm_post, mix_norm_pre=mix_norm_pre,
        w_in=w_in, gdn_conv_w=gdn_conv_w, gdn_a_log=gdn_a_log, gdn_dt_bias=gdn_dt_bias,
        gdn_norm_w=gdn_norm_w, mla_kv_norm=mla_kv_norm, mla_w_uk=mla_w_uk, mla_w_uv=mla_w_uv,
        w_br_gdn=w_br_gdn, w_br_mla=w_br_mla, w_out=w_out, mix_norm_post=mix_norm_post,
        ffn2_norm_pre=ffn2_norm_pre, ffn2_w_gate=ffn2_w_gate, ffn2_w_up=ffn2_w_up,
        ffn2_w_down=ffn2_w_down, ffn2_norm_post=ffn2_norm_post)
    depth = w_in.shape[0]
    d_model = x_prompt.shape[-1]
    b_p = x_prompt.shape[0]
    past = cache_mla_ckv.shape[2]
    yp, ys = x_prompt, x_sample
    outs_p, outs_s = [], []
    for l in range(depth):
        w = _prep_weights(l, d_model, **params)
        (ys, *rest_s), w_bf16 = _layer(
            ys, past, state_gdn_conv[l], state_gdn_ssm[l], cache_mla_ckv[l], cache_mla_krope[l], w)
        (yp, *rest_p), _ = _layer(
            yp, 0, jnp.zeros((b_p, GDN_CONV - 1, GDN_CONV_DIM), F32),
            jnp.zeros((b_p, GDN_HEADS, GDN_DK, GDN_DV), F32), None, None, {**w, **w_bf16})
        outs_p.append(rest_p)
        outs_s.append(rest_s)
    stack = lambda outs, i: jnp.stack([o[i] for o in outs])
    return (yp, ys,
            stack(outs_p, 0), stack(outs_p, 1), stack(outs_p, 2), stack(outs_p, 3),
            stack(outs_s, 0), stack(outs_s, 1), stack(outs_s, 2), stack(outs_s, 3))
```

```python
import functools

import jax
import jax.numpy as jnp
from jax import lax
from jax.experimental import pallas as pl
from jax.experimental.pallas import tpu as pltpu

F32 = jnp.float32
BF16 = jnp.bfloat16

CHUNK = 64
LOG2_CHUNK = 6
NORM_EPS = 1e-6
GDN_HEADS = 8
GDN_DK = 128
GDN_DV = 128
GDN_CONV = 4
GDN_KEY_DIM = GDN_HEADS * GDN_DK
GDN_VAL_DIM = GDN_HEADS * GDN_DV
GDN_CONV_DIM = 2 * GDN_KEY_DIM + GDN_VAL_DIM
MLA_HEADS = 8
MLA_NOPE = 128
MLA_ROPE = 64
MLA_V = 128
MLA_KV_RANK = 512
MLA_SCALE = (MLA_NOPE + MLA_ROPE) ** -0.5
ROPE_THETA = 10000.0

LANES = 128
SUBLANES = 8
VMEM_CAP_BYTES = 56 * 2 ** 20
MLA_QK = MLA_KV_RANK + LANES
HI = lax.Precision.HIGHEST


def _params(semantics, vmem_bytes):
    limit = int(min(max(vmem_bytes, 16 * 2 ** 20), VMEM_CAP_BYTES))
    return pltpu.CompilerParams(dimension_semantics=semantics, vmem_limit_bytes=limit)


def _dot(a, b, prec=None):
    return lax.dot_general(a, b, (((1,), (0,)), ((), ())), precision=prec,
                           preferred_element_type=F32)


def _dot_nt(a, b, prec=None):
    return lax.dot_general(a, b, (((1,), (1,)), ((), ())), precision=prec,
                           preferred_element_type=F32)


def _dot_tn(a, b, prec=None):
    return lax.dot_general(a, b, (((0,), (0,)), ((), ())), precision=prec,
                           preferred_element_type=F32)


def _rms(y, g):
    return y * lax.rsqrt(jnp.mean(y * y, axis=-1, keepdims=True) + NORM_EPS) * g


def _silu(x):
    return x * jax.nn.sigmoid(x)


def _tile(n, pref):
    t = min(n, pref)
    assert n % t == 0, (n, pref)
    return t


def _ffn_kernel(*refs, nf, norm_in, emit_next, cast_w):
    refs = list(refs)
    xn_ref = None if norm_in else refs.pop(0)
    x_ref, wg_ref, wu_ref, wd_ref = refs[:4]
    del refs[:4]
    gpre_ref = refs.pop(0) if norm_in else None
    gpost_ref = refs.pop(0)
    gnext_ref = refs.pop(0) if emit_next else None
    xo_ref = refs.pop(0)
    xno_ref = refs.pop(0) if emit_next else None
    if cast_w:
        wgo_ref, wuo_ref, wdo_ref = refs[:3]
        del refs[:3]
    acc_ref = refs.pop(0)
    if norm_in:
        xn_ref = refs.pop(0)
    f = pl.program_id(1)

    @pl.when(f == 0)
    def _():
        acc_ref[...] = jnp.zeros(acc_ref.shape, F32)
        if norm_in:
            xn_ref[...] = _rms(x_ref[...], gpre_ref[...]).astype(BF16)

    xn = xn_ref[...]
    wg, wu, wd = wg_ref[...], wu_ref[...], wd_ref[...]
    if cast_w:
        wg, wu, wd = wg.astype(BF16), wu.astype(BF16), wd.astype(BF16)
        wgo_ref[...] = wg
        wuo_ref[...] = wu
        wdo_ref[...] = wd
    gate = _dot(xn, wg)
    up = _dot(xn, wu)
    h = (_silu(gate) * up).astype(BF16)
    acc_ref[...] += _dot(h, wd)

    @pl.when(f == nf - 1)
    def _():
        xnew = x_ref[...] + 0.5 * _rms(acc_ref[...], gpost_ref[...])
        xo_ref[...] = xnew
        if emit_next:
            xno_ref[...] = _rms(xnew, gnext_ref[...]).astype(BF16)


def _ffn(x, wg, wu, wd, gpost, xn=None, gpre=None, gnext=None):
    norm_in = xn is None
    emit_next = gnext is not None
    cast_w = wg.dtype == F32
    assert norm_in == (gpre is not None)
    m, d = x.shape
    dff = wg.shape[1]
    tm = _tile(m, 512)
    tf = _tile(dff, 256 if cast_w else 512)
    nf = dff // tf
    assert not cast_w or m == tm
    wsz = wg.dtype.itemsize
    row = lambda i, f: (i, 0)
    vec = pl.BlockSpec((1, d), lambda i, f: (0, 0))
    w_specs = [pl.BlockSpec((d, tf), lambda i, f: (0, f)),
               pl.BlockSpec((d, tf), lambda i, f: (0, f)),
               pl.BlockSpec((tf, d), lambda i, f: (f, 0))]
    ins = [] if norm_in else [xn]
    in_specs = [] if norm_in else [pl.BlockSpec((tm, d), row)]
    ins += [x, wg, wu, wd]
    in_specs += [pl.BlockSpec((tm, d), row)] + w_specs
    for g in (gpre, gpost, gnext):
        if g is not None:
            ins.append(g)
            in_specs.append(vec)
    out_specs = [pl.BlockSpec((tm, d), row)]
    out_shape = [jax.ShapeDtypeStruct((m, d), F32)]
    scratch = [pltpu.VMEM((tm, d), F32)]
    if emit_next:
        out_specs.append(pl.BlockSpec((tm, d), row))
        out_shape.append(jax.ShapeDtypeStruct((m, d), BF16))
    if cast_w:
        out_specs += w_specs
        out_shape += [jax.ShapeDtypeStruct(w.shape, BF16) for w in (wg, wu, wd)]
    if norm_in:
        scratch.append(pltpu.VMEM((tm, d), BF16))
    vmem = (2 * (tm * d * 2 + tm * d * 4 + 3 * d * tf * wsz + tm * d * 4 + tm * d * 2)
            + (2 * 3 * d * tf * 2 + 3 * d * tf * 2 if cast_w else 0)
            + tm * d * 4 + 3 * tm * tf * 4 + 2 * tm * d * 4)
    res = pl.pallas_call(
        functools.partial(_ffn_kernel, nf=nf, norm_in=norm_in, emit_next=emit_next, cast_w=cast_w),
        grid=(m // tm, nf),
        in_specs=in_specs,
        out_specs=out_specs,
        out_shape=out_shape,
        scratch_shapes=scratch,
        compiler_params=_params(("parallel", "arbitrary"), vmem),
        name="ffn_cast" if cast_w else "ffn",
    )(*ins)
    res = list(res)
    x_new = res.pop(0)
    xn_next = res.pop(0) if emit_next else None
    return x_new, xn_next, (tuple(res) if cast_w else (wg, wu, wd))


def _mm_kernel(a_ref, w_ref, o_ref):
    o_ref[...] = _dot(a_ref[...], w_ref[...]).astype(o_ref.dtype)


def _matmul(a, w, out_dtype):
    m, k = a.shape
    n = w.shape[1]
    tm = _tile(m, 1024)
    tn = _tile(n, 1024)
    osz = jnp.dtype(out_dtype).itemsize
    vmem = 2 * (tm * k * 2 + k * tn * 2 + tm * tn * osz) + tm * tn * 4
    return pl.pallas_call(
        _mm_kernel,
        grid=(m // tm, n // tn),
        in_specs=[pl.BlockSpec((tm, k), lambda i, j: (i, 0)),
                  pl.BlockSpec((k, tn), lambda i, j: (0, j))],
        out_specs=pl.BlockSpec((tm, tn), lambda i, j: (i, j)),
        out_shape=jax.ShapeDtypeStruct((m, n), out_dtype),
        compiler_params=_params(("parallel", "arbitrary"), vmem),
        name="in_proj",
    )(a, w)


COL_QKV = 0
COL_Z = COL_QKV + GDN_CONV_DIM
COL_QNOPE = 0
COL_QROPE = COL_QNOPE + MLA_HEADS * MLA_NOPE
COL_CKV = COL_QROPE + MLA_HEADS * LANES
COL_KR = COL_CKV + MLA_KV_RANK
COL_B = COL_KR + LANES
COL_A = COL_B + LANES
N_MLA = 3072


def _build_w_in(w_in, d_model):
    sizes = (GDN_CONV_DIM, GDN_VAL_DIM, GDN_HEADS, GDN_HEADS, MLA_HEADS * (MLA_NOPE + MLA_ROPE),
             MLA_KV_RANK, MLA_ROPE, d_model, d_model)
    offs = [0]
    for s in sizes:
        offs.append(offs[-1] + s)
    w_gdn = w_in[:, :offs[2]].astype(BF16)
    w_gate = w_in[:, offs[7]:].astype(BF16)
    part = lambda i: w_in[:, offs[i]:offs[i + 1]].astype(BF16)
    b, a, qm, ckv, kr = (part(i) for i in range(2, 7))
    d = w_in.shape[0]
    half = MLA_ROPE // 2
    qm = qm.reshape(d, MLA_HEADS, MLA_NOPE + MLA_ROPE)
    qn = qm[:, :, :MLA_NOPE].reshape(d, MLA_HEADS * MLA_NOPE)
    qr = qm[:, :, MLA_NOPE:]
    qr_pair = jnp.concatenate([qr, qr[:, :, half:], qr[:, :, :half]], axis=2).reshape(d, MLA_HEADS * LANES)
    kr_pair = jnp.concatenate([kr, kr[:, half:], kr[:, :half]], axis=1)
    zeros = lambda n: jnp.zeros((d, n), BF16)
    w_mla = jnp.concatenate([qn, qr_pair, ckv, kr_pair, b, zeros(LANES - GDN_HEADS), a, zeros(LANES - GDN_HEADS),
                             zeros(N_MLA - COL_A - LANES)], axis=1)
    return w_gdn, w_gate, w_mla


_NN = (((1,), (0,)), ((), ()))
_NT = (((1,), (1,)), ((), ()))
_TN = (((0,), (0,)), ((), ()))
INV_BLOCK = 16


def _mm(a, b, dims=_NN):
    return lax.dot_general(a.astype(BF16), b.astype(BF16), dims, preferred_element_type=F32)


def _gdn_kernel(qkv_ref, z_ref, beta_ref, g_ref, cst_ref, cw_ref, nw_ref, s0_ref,
                o_ref, sout_ref, past_ref, s_ref, *, G, C, nlev, nc):
    c = pl.program_id(1)

    @pl.when(c == 0)
    def _():
        past_ref[...] = cst_ref[...]
        s_ref[...] = s0_ref[...]

    ri = lax.broadcasted_iota(jnp.int32, (C, C), 0)
    ci = lax.broadcasted_iota(jnp.int32, (C, C), 1)
    incl = ri >= ci
    strict = ri > ci
    eye = (ri == ci).astype(F32)
    log2_blk = INV_BLOCK.bit_length() - 1
    bdiag = jnp.right_shift(ri, log2_blk) == jnp.right_shift(ci, log2_blk)

    first_row = lax.broadcasted_iota(jnp.int32, (C, 1), 0) == 0

    def conv(s, col):
        cols = slice(col, col + LANES)
        x = qkv_ref[s, :, cols]
        past = past_ref[s, :, cols]
        w = [cw_ref[j:j + 1, cols] for j in range(GDN_CONV)]
        acc = w[0] * x
        for j in range(1, GDN_CONV):
            top = sum(w[i] * past[SUBLANES - 1 - (j - 1 - i):SUBLANES - (j - 1 - i)] for i in range(j))
            acc = w[j] * x + jnp.where(first_row, top, pltpu.roll(acc, 1, axis=0))
        return _silu(acc)

    units = [(s, h) for s in range(G) for h in range(GDN_HEADS)]
    every = lambda fn, *lists: [fn(*args) for args in zip(*lists)]
    mm_inv = _mm
    beta_all, gcum, gcum_t = [], [], []
    for s in range(G):
        beta_all.append(beta_ref[s])
        g_all = g_ref[s]
        gcum.append(_dot(incl.astype(F32), g_all, HI))
        gcum_t.append(_dot_tn(g_all, (ci >= ri).astype(F32), HI))

    def l2n(x):
        return x * lax.rsqrt(jnp.sum(x * x, axis=-1, keepdims=True) + 1e-6)

    q = [l2n(conv(s, h * GDN_DK)) * (GDN_DK ** -0.5) for s, h in units]
    k = [l2n(conv(s, GDN_KEY_DIM + h * GDN_DK)) for s, h in units]
    v = [conv(s, 2 * GDN_KEY_DIM + h * GDN_DV) for s, h in units]
    bcol = [beta_all[s][:, h:h + 1] for s, h in units]
    gcol = [gcum[s][:, h:h + 1] for s, h in units]
    glast = [gcum[s][C - 1:C, h:h + 1] for s, h in units]
    decay = [jnp.where(incl, jnp.exp(jnp.where(incl, gcum[s][:, h:h + 1] - gcum_t[s][h:h + 1, :], 0.0)), 0.0)
             for s, h in units]
    qkk = every(lambda q_, k_: _mm(jnp.concatenate([q_, k_], axis=0), k_, _NT), q, k)
    qk = every(lambda x, d: x[:C] * d, qkk, decay)
    nmat = every(lambda x, b_, d: -jnp.where(strict, b_ * x[C:] * d, 0.0), qkk, bcol, decay)
    ndiag = every(lambda n_: jnp.where(bdiag, n_, 0.0), nmat)
    tinv = every(lambda n_: eye + n_, ndiag)
    npow = ndiag
    for _ in range(INV_BLOCK.bit_length() - 2):
        npow = every(mm_inv, npow, npow)
        tinv = every(lambda t_, p_: t_ + mm_inv(t_, p_), tinv, npow)
    mpow = every(lambda t_, n_, d_: mm_inv(t_, n_ - d_), tinv, nmat, ndiag)
    for lev in range(nlev):
        if lev:
            mpow = every(mm_inv, mpow, mpow)
        tinv = every(lambda t_, m_: t_ + mm_inv(m_, t_), tinv, mpow)
    egc = every(jnp.exp, gcol)
    sol = every(lambda t_, v_, k_, b_, e_: _mm(t_, jnp.concatenate([v_ * b_, k_ * (b_ * e_)], axis=1)),
                tinv, v, k, bcol, egc)
    s_old = [s_ref[s, h] for s, h in units]
    ws_qs = every(lambda x, q_, e_, s_: _mm(jnp.concatenate([x[:, GDN_DV:], q_ * e_], axis=0), s_),
                  sol, q, egc, s_old)
    v_new = every(lambda x, y: x[:, :GDN_DV] - y[:C], sol, ws_qs)
    o = every(lambda y, a_, vn: y[C:] + _mm(a_, vn), ws_qs, qk, v_new)
    s_new = every(lambda s_, gl, k_, gc, vn: s_ * jnp.exp(gl) + _mm(k_ * jnp.exp(gl - gc), vn, _TN),
                  s_old, glast, k, gcol, v_new)
    for (s, h), sn, o_ in zip(units, s_new, o):
        s_ref[s, h] = sn
        zz = z_ref[s, :, h * GDN_DV:(h + 1) * GDN_DV]
        o_ref[s, :, h * GDN_DV:(h + 1) * GDN_DV] = (_rms(o_, nw_ref[...]) * _silu(zz)).astype(BF16)

    past_ref[...] = qkv_ref[:, C - SUBLANES:C, :]

    @pl.when(c == nc - 1)
    def _():
        sout_ref[...] = s_ref[...]


def _gdn(proj_gdn, bg, nb, t, conv_state, ssm0, conv_w, nw):
    C = min(CHUNK, t)
    G = 2 if nb % 2 == 0 else 1
    assert t % C == 0 and C & (C - 1) == 0 and C % INV_BLOCK == 0
    nc = t // C
    nlev = (C // INV_BLOCK).bit_length() - 1
    pg = proj_gdn.reshape(nb, t, proj_gdn.shape[1])
    pm = bg.reshape(nb, t, bg.shape[1])
    fixed = lambda b, c: (0, 0)
    cst = jnp.pad(conv_state, ((0, 0), (SUBLANES - (GDN_CONV - 1), 0), (0, 0)))
    vmem = G * (2 * (C * GDN_CONV_DIM * 4 + C * GDN_VAL_DIM * 4 + 2 * C * LANES * 4 + SUBLANES * GDN_CONV_DIM * 4
                     + C * GDN_VAL_DIM * 2 + 2 * GDN_HEADS * GDN_DK * GDN_DV * 4)
                + SUBLANES * GDN_CONV_DIM * 4 + GDN_HEADS * GDN_DK * GDN_DV * 4) + 16 * 2 ** 20
    o, s_new = pl.pallas_call(
        functools.partial(_gdn_kernel, G=G, C=C, nlev=nlev, nc=nc),
        grid=(nb // G, nc),
        in_specs=[pl.BlockSpec((G, C, GDN_CONV_DIM), lambda b, c: (b, c, COL_QKV // GDN_CONV_DIM)),
                  pl.BlockSpec((G, C, GDN_VAL_DIM), lambda b, c: (b, c, COL_Z // GDN_VAL_DIM)),
                  pl.BlockSpec((G, C, LANES), lambda b, c: (b, c, 0)),
                  pl.BlockSpec((G, C, LANES), lambda b, c: (b, c, 1)),
                  pl.BlockSpec((G, SUBLANES, GDN_CONV_DIM), lambda b, c: (b, 0, 0)),
                  pl.BlockSpec((GDN_CONV, GDN_CONV_DIM), fixed),
                  pl.BlockSpec((1, GDN_DV), fixed),
                  pl.BlockSpec((G, GDN_HEADS, GDN_DK, GDN_DV), lambda b, c: (b, 0, 0, 0))],
        out_specs=[pl.BlockSpec((G, C, GDN_VAL_DIM), lambda b, c: (b, c, 0)),
                   pl.BlockSpec((G, GDN_HEADS, GDN_DK, GDN_DV), lambda b, c: (b, 0, 0, 0))],
        out_shape=[jax.ShapeDtypeStruct((nb, t, GDN_VAL_DIM), BF16),
                   jax.ShapeDtypeStruct((nb, GDN_HEADS, GDN_DK, GDN_DV), F32)],
        scratch_shapes=[pltpu.VMEM((G, SUBLANES, GDN_CONV_DIM), F32),
                        pltpu.VMEM((G, GDN_HEADS, GDN_DK, GDN_DV), F32)],
        compiler_params=_params(("parallel", "arbitrary"), vmem),
        name="gdn",
    )(pg, pg, pm, pm, cst, conv_w, nw, ssm0)
    return o.reshape(nb * t, GDN_VAL_DIM), s_new


def _mla_prep_kernel(hm_ref, w_ref, inv_ref, wuk_ref, kvn_ref, alog_ref, dtb_ref,
                     ckv_o, kr_o, qcat_o, kcat_o, bg_o, *maybe_vt_o, tm, pos0):
    j = pl.program_id(1)
    proj = _dot(hm_ref[...], w_ref[...])
    a_all = proj[:, COL_A:COL_A + LANES] + dtb_ref[...]
    softplus = jnp.maximum(a_all, 0.0) + jnp.log(1.0 + jnp.exp(-jnp.abs(a_all)))
    bg_o[:, :LANES] = jax.nn.sigmoid(proj[:, COL_B:COL_B + LANES])
    bg_o[:, LANES:] = -jnp.exp(alog_ref[...]) * softplus
    pos = (pos0 + j * tm + lax.broadcasted_iota(jnp.int32, (tm, LANES), 0)).astype(F32)
    lane = lax.broadcasted_iota(jnp.int32, (tm, LANES), 1)
    ang = pos * inv_ref[...]
    cos = jnp.cos(ang)
    sin = jnp.sin(ang)
    half = MLA_ROPE // 2
    cs = jnp.where(lane < MLA_ROPE, cos, jnp.where(lane < MLA_ROPE + half, -sin, sin))
    keep = lane < MLA_ROPE

    def rope(pair):
        prod = pair * cs
        return jnp.where(keep, prod + pltpu.roll(prod, MLA_ROPE, axis=1), 0.0)

    ckv = _rms(proj[:, COL_CKV:COL_CKV + MLA_KV_RANK], kvn_ref[...])
    ckv_o[0] = ckv
    kr = rope(proj[:, COL_KR:COL_KR + LANES])
    kr_o[0] = kr[:, :MLA_ROPE]
    kcat_o[0, :, :MLA_KV_RANK] = ckv.astype(BF16)
    kcat_o[0, :, MLA_KV_RANK:] = kr.astype(BF16)
    if maybe_vt_o:
        maybe_vt_o[0][0] = ckv.T.astype(BF16)
    for h in range(MLA_HEADS):
        qn = proj[:, COL_QNOPE + h * MLA_NOPE:COL_QNOPE + (h + 1) * MLA_NOPE].astype(BF16)
        qlat = _dot(qn, wuk_ref[h]) * MLA_SCALE
        qcat_o[0, h, :, :MLA_KV_RANK] = qlat.astype(BF16)
        qr = rope(proj[:, COL_QROPE + h * LANES:COL_QROPE + (h + 1) * LANES]) * MLA_SCALE
        qcat_o[0, h, :, MLA_KV_RANK:] = qr.astype(BF16)


def _mla_prep(hm, w_mla, nb, t, pos0, inv128, wuk_t, kvn, alog, dtb, emit_vt):
    d = hm.shape[1]
    tm = _tile(t, 512)
    nt = t // tm
    rows = lambda b, j: b * nt + j
    fixed2 = lambda b, j: (0, 0)
    vmem = (d * N_MLA * 2 + 2 * (tm * d * 2 + MLA_HEADS * MLA_NOPE * MLA_KV_RANK * 2
                                 + tm * (MLA_KV_RANK + 3 * LANES) * 4 + (MLA_HEADS + 2) * tm * MLA_QK * 2)
            + 2 * tm * N_MLA * 4 + 8 * 2 ** 20)
    out_specs = [pl.BlockSpec((1, tm, MLA_KV_RANK), lambda b, j: (b, j, 0)),
                 pl.BlockSpec((1, tm, MLA_ROPE), lambda b, j: (b, j, 0)),
                 pl.BlockSpec((1, MLA_HEADS, tm, MLA_QK), lambda b, j: (b, 0, j, 0)),
                 pl.BlockSpec((1, tm, MLA_QK), lambda b, j: (b, j, 0)),
                 pl.BlockSpec((tm, 2 * LANES), lambda b, j: (rows(b, j), 0))]
    out_shape = [jax.ShapeDtypeStruct((nb, t, MLA_KV_RANK), F32),
                 jax.ShapeDtypeStruct((nb, t, MLA_ROPE), F32),
                 jax.ShapeDtypeStruct((nb, MLA_HEADS, t, MLA_QK), BF16),
                 jax.ShapeDtypeStruct((nb, t, MLA_QK), BF16),
                 jax.ShapeDtypeStruct((nb * t, 2 * LANES), F32)]
    if emit_vt:
        out_specs.append(pl.BlockSpec((1, MLA_KV_RANK, tm), lambda b, j: (b, 0, j)))
        out_shape.append(jax.ShapeDtypeStruct((nb, MLA_KV_RANK, t), BF16))
    return pl.pallas_call(
        functools.partial(_mla_prep_kernel, tm=tm, pos0=pos0),
        grid=(nb, nt),
        in_specs=[pl.BlockSpec((tm, d), lambda b, j: (rows(b, j), 0)),
                  pl.BlockSpec((d, N_MLA), fixed2, pipeline_mode=pl.Buffered(1)),
                  pl.BlockSpec((1, LANES), fixed2),
                  pl.BlockSpec((MLA_HEADS, MLA_NOPE, MLA_KV_RANK), lambda b, j: (0, 0, 0)),
                  pl.BlockSpec((1, MLA_KV_RANK), fixed2),
                  pl.BlockSpec((1, LANES), fixed2),
                  pl.BlockSpec((1, LANES), fixed2)],
        out_specs=out_specs,
        out_shape=out_shape,
        compiler_params=_params(("parallel", "parallel"), vmem),
        name="mla_proj_prep",
    )(hm, w_mla, inv128, wuk_t, kvn, alog, dtb)


ATTN_GROUPS = 2


def _attn_kernel(q_ref, k_ref, vt_ref, wuv_ref, o_ref, s0_ref, s1_ref, m_ref, l_ref, acc_ref,
                 *, tq, tk, n_valid):
    i = pl.program_id(1)
    cols = MLA_HEADS * tq
    cg = cols // ATTN_GROUPS
    q_first = i * tq
    c = lax.broadcasted_iota(jnp.int32, (1, cg), 1)
    qchunk = jnp.right_shift(q_first + jnp.bitwise_and(c, tq - 1), LOG2_CHUNK)
    k_all = jnp.minimum((q_first // CHUNK + 1) * CHUNK, n_valid)
    k_any = jnp.minimum(((q_first + tq - 1) // CHUNK + 1) * CHUNK, n_valid)
    nk = (k_any + tk - 1) // tk
    m_ref[...] = jnp.full(m_ref.shape, -jnp.inf, F32)
    l_ref[...] = jnp.zeros(l_ref.shape, F32)
    acc_ref[...] = jnp.zeros(acc_ref.shape, F32)

    def scores(j, buf):
        start = pl.multiple_of(j * tk, tk)
        q = q_ref[0].reshape(cols, MLA_QK)
        buf[...] = _dot_nt(k_ref[0, pl.ds(start, tk), :], q)

    def fold(x, op):
        while x.shape[0] > SUBLANES:
            half = x.shape[0] // 2
            x = op(x[:half], x[half:])
        return x

    def softmax_values(j, buf, masked):
        start = pl.multiple_of(j * tk, tk)
        vt = vt_ref[0, :, pl.ds(start, tk)]
        if masked:
            kpos = start + lax.broadcasted_iota(jnp.int32, (tk, 1), 0)
            mask = jnp.logical_and(jnp.right_shift(kpos, LOG2_CHUNK) <= qchunk, kpos < n_valid)
        for g in range(ATTN_GROUPS):
            cs = slice(g * cg, (g + 1) * cg)
            s = buf[:, cs]
            if masked:
                s = jnp.where(mask, s, -jnp.inf)
            m_old = m_ref[:, cs]
            m_new = jnp.maximum(m_old, jnp.max(fold(s, jnp.maximum), axis=0, keepdims=True))
            alpha = jnp.exp(m_old - m_new)
            p = jnp.exp(s - m_new)
            l_ref[:, cs] = alpha * l_ref[:, cs] + jnp.sum(fold(p, jnp.add), axis=0, keepdims=True)
            acc_ref[:, cs] = alpha * acc_ref[:, cs] + _dot(vt, p.astype(BF16))
            m_ref[:, cs] = m_new

    def by_parity(j, fn):
        @pl.when(jnp.bitwise_and(j, 1) == 0)
        def _():
            fn(s0_ref, s1_ref)

        @pl.when(jnp.bitwise_and(j, 1) == 1)
        def _():
            fn(s1_ref, s0_ref)

    scores(0, s0_ref)

    def body(j, carry, masked):
        def step(cur, nxt):
            scores(j + 1, nxt)
            softmax_values(j, cur, masked)
        by_parity(j, step)
        return carry

    n_open = jnp.minimum(k_all // tk, nk - 1)
    lax.fori_loop(0, n_open, functools.partial(body, masked=False), 0)
    lax.fori_loop(n_open, nk - 1, functools.partial(body, masked=True), 0)
    by_parity(nk - 1, lambda cur, nxt: softmax_values(nk - 1, cur, True))
    inv_l = 1.0 / l_ref[...]
    for h in range(MLA_HEADS):
        hs = slice(h * tq, (h + 1) * tq)
        lat_t = (acc_ref[:, hs] * inv_l[:, hs]).astype(BF16)
        o_ref[:, h * MLA_V:(h + 1) * MLA_V] = _dot_tn(lat_t, wuv_ref[h]).astype(BF16)


def _attn_cache_kernel(q_ref, ckv_ref, kr_ref, knew_ref, wuv_ref, o_ref, m_ref, l_ref, acc_ref,
                       *, tq, tk, nkb, past):
    j = pl.program_id(1)
    rows = MLA_HEADS * tq
    q = q_ref[0].reshape(rows, MLA_QK)
    r = lax.broadcasted_iota(jnp.int32, (rows, 1), 0)
    qchunk = jnp.right_shift(past + jnp.bitwise_and(r, tq - 1), LOG2_CHUNK)

    @pl.when(j == 0)
    def _():
        m_ref[...] = jnp.full(m_ref.shape, -jnp.inf, F32)
        l_ref[...] = jnp.zeros(l_ref.shape, F32)
        acc_ref[...] = jnp.zeros(acc_ref.shape, F32)

    def fold(x, op):
        while x.shape[1] > LANES:
            half = x.shape[1] // 2
            x = op(x[:, :half], x[:, half:])
        return x

    def update(s, kpos, vals):
        s = jnp.where(jnp.right_shift(kpos, LOG2_CHUNK) <= qchunk, s, -jnp.inf)
        m_old = m_ref[...]
        m_new = jnp.maximum(m_old, jnp.max(fold(s, jnp.maximum), axis=-1, keepdims=True))
        alpha = jnp.exp(m_old - m_new)
        p = jnp.exp(s - m_new)
        l_ref[...] = alpha * l_ref[...] + jnp.sum(fold(p, jnp.add), axis=-1, keepdims=True)
        acc_ref[...] = alpha * acc_ref[...] + _dot(p.astype(BF16), vals)
        m_ref[...] = m_new

    ck = ckv_ref[0].astype(BF16)
    s = (_dot_nt(q[:, :MLA_KV_RANK], ck)
         + _dot_nt(q[:, MLA_KV_RANK:MLA_KV_RANK + MLA_ROPE], kr_ref[0].astype(BF16)))
    update(s, j * tk + lax.broadcasted_iota(jnp.int32, (1, tk), 1), ck)

    @pl.when(j == nkb - 1)
    def _():
        kn = knew_ref[0]
        update(_dot_nt(q, kn), past + lax.broadcasted_iota(jnp.int32, (1, tq), 1), kn[:, :MLA_KV_RANK])
        inv_l = 1.0 / l_ref[...]
        for h in range(MLA_HEADS):
            hs = slice(h * tq, (h + 1) * tq)
            lat = (acc_ref[hs, :] * inv_l[hs, :]).astype(BF16)
            o_ref[:, h * MLA_V:(h + 1) * MLA_V] = _dot(lat, wuv_ref[h]).astype(BF16)


def _attention(qcat, kcat, vt, wuv_t):
    nb, _, t, _ = qcat.shape
    tq = _tile(t, 128)
    tk = _tile(t, 512)
    assert tq & (tq - 1) == 0
    nq = t // tq
    cols = MLA_HEADS * tq
    vmem = (2 * (cols * MLA_QK * 2 + t * MLA_QK * 2 + MLA_KV_RANK * t * 2
                 + MLA_HEADS * MLA_KV_RANK * MLA_V * 2 + tq * MLA_HEADS * MLA_V * 2)
            + 2 * tk * cols * 4 + 2 * SUBLANES * cols * 4 + MLA_KV_RANK * cols * 4
            + 3 * tk * cols * 4 // ATTN_GROUPS + 4 * 2 ** 20)
    return pl.pallas_call(
        functools.partial(_attn_kernel, tq=tq, tk=tk, n_valid=t),
        grid=(nb, nq),
        in_specs=[pl.BlockSpec((1, MLA_HEADS, tq, MLA_QK), lambda b, i: (b, 0, i, 0)),
                  pl.BlockSpec((1, t, MLA_QK), lambda b, i: (b, 0, 0)),
                  pl.BlockSpec((1, MLA_KV_RANK, t), lambda b, i: (b, 0, 0)),
                  pl.BlockSpec((MLA_HEADS, MLA_KV_RANK, MLA_V), lambda b, i: (0, 0, 0))],
        out_specs=pl.BlockSpec((tq, MLA_HEADS * MLA_V), lambda b, i: (b * nq + i, 0)),
        out_shape=jax.ShapeDtypeStruct((nb * t, MLA_HEADS * MLA_V), BF16),
        scratch_shapes=[pltpu.VMEM((tk, cols), F32), pltpu.VMEM((tk, cols), F32), pltpu.VMEM((1, cols), F32),
                        pltpu.VMEM((1, cols), F32), pltpu.VMEM((MLA_KV_RANK, cols), F32)],
        compiler_params=_params(("parallel", "arbitrary"), vmem),
        name="mla_attention",
    )(qcat, kcat, vt, wuv_t)


def _attention_cached(qcat, ckv_past, krope_past, kcat_new, wuv_t):
    nb, _, t, _ = qcat.shape
    past = ckv_past.shape[1]
    tk = _tile(past, 2048)
    nkb = past // tk
    assert t & (t - 1) == 0
    rows = MLA_HEADS * t
    vmem = (2 * (rows * MLA_QK * 2 + tk * MLA_KV_RANK * 4 + tk * LANES * 4 + t * MLA_QK * 2
                 + MLA_HEADS * MLA_KV_RANK * MLA_V * 2 + t * MLA_HEADS * MLA_V * 2)
            + 2 * rows * LANES * 4 + rows * MLA_KV_RANK * 4 + tk * MLA_QK * 2 + 4 * rows * tk * 4 + 4 * 2 ** 20)
    return pl.pallas_call(
        functools.partial(_attn_cache_kernel, tq=t, tk=tk, nkb=nkb, past=past),
        grid=(nb, nkb),
        in_specs=[pl.BlockSpec((1, MLA_HEADS, t, MLA_QK), lambda b, j: (b, 0, 0, 0)),
                  pl.BlockSpec((1, tk, MLA_KV_RANK), lambda b, j: (b, j, 0)),
                  pl.BlockSpec((1, tk, MLA_ROPE), lambda b, j: (b, j, 0)),
                  pl.BlockSpec((1, t, MLA_QK), lambda b, j: (b, 0, 0)),
                  pl.BlockSpec((MLA_HEADS, MLA_KV_RANK, MLA_V), lambda b, j: (0, 0, 0))],
        out_specs=pl.BlockSpec((t, MLA_HEADS * MLA_V), lambda b, j: (b, 0)),
        out_shape=jax.ShapeDtypeStruct((nb * t, MLA_HEADS * MLA_V), BF16),
        scratch_shapes=[pltpu.VMEM((rows, 1), F32), pltpu.VMEM((rows, 1), F32),
                        pltpu.VMEM((rows, MLA_KV_RANK), F32)],
        compiler_params=_params(("parallel", "arbitrary"), vmem),
        name="mla_attention_cached",
    )(qcat, ckv_past, krope_past, kcat_new, wuv_t)


def _merge_out_kernel(og_ref, om_ref, wg_ref, wm_ref, gg_ref, gm_ref, wo_ref, x_ref, gpost_ref, gnext_ref,
                      xo_ref, xno_ref):
    tg = _dot(og_ref[...], wg_ref[...])
    tm_ = _dot(om_ref[...], wm_ref[...])
    merged = (jax.nn.sigmoid(gg_ref[...].astype(F32)) * tg
              + jax.nn.sigmoid(gm_ref[...].astype(F32)) * tm_).astype(BF16)
    y = _dot(merged, wo_ref[...])
    xnew = x_ref[...] + _rms(y, gpost_ref[...])
    xo_ref[...] = xnew
    xno_ref[...] = _rms(xnew, gnext_ref[...]).astype(BF16)


def _merge_out(og, om, wbg, wbm, proj_gate, wout, x, gpost, gnext):
    m, kg = og.shape
    d = wbg.shape[1]
    tm = _tile(m, 256)
    gsz = proj_gate.dtype.itemsize
    row = lambda i: (i, 0)
    fixed = lambda i: (0, 0)
    resident = dict(pipeline_mode=pl.Buffered(1))
    vmem = ((2 * kg * d + d * d) * 2
            + 2 * (2 * tm * kg * 2 + 2 * tm * d * gsz + 2 * tm * d * 4 + tm * d * 2)
            + 4 * tm * d * 4 + 2 * 2 ** 20)
    return pl.pallas_call(
        _merge_out_kernel,
        grid=(m // tm,),
        in_specs=[pl.BlockSpec((tm, kg), row),
                  pl.BlockSpec((tm, kg), row),
                  pl.BlockSpec((kg, d), fixed, **resident),
                  pl.BlockSpec((kg, d), fixed, **resident),
                  pl.BlockSpec((tm, d), lambda i: (i, 0)),
                  pl.BlockSpec((tm, d), lambda i: (i, 1)),
                  pl.BlockSpec((d, d), fixed, **resident),
                  pl.BlockSpec((tm, d), row),
                  pl.BlockSpec((1, d), fixed),
                  pl.BlockSpec((1, d), fixed)],
        out_specs=[pl.BlockSpec((tm, d), row), pl.BlockSpec((tm, d), row)],
        out_shape=[jax.ShapeDtypeStruct((m, d), F32), jax.ShapeDtypeStruct((m, d), BF16)],
        compiler_params=_params(("parallel",), vmem),
        name="merge_out",
    )(og, om, wbg, wbm, proj_gate, proj_gate, wout, x, gpost, gnext)


def _layer(x3, pos0, conv_state, ssm0, ckv_past, krope_past, w):
    nb, t, d = x3.shape
    m = nb * t
    x = x3.reshape(m, d)
    x1, hm, ffn1_w = _ffn(x, *w["ffn1_w"], w["ffn1_norm_post"],
                          gpre=w["ffn1_norm_pre"], gnext=w["mix_norm_pre"])
    proj_gdn = _matmul(hm, w["w_in_gdn"], F32)
    proj_gate = _matmul(hm, w["w_in_gate"], BF16)
    ckv, krope, qcat, kcat, bg, *vt = _mla_prep(hm, w["w_in_mla"], nb, t, pos0, w["inv128"], w["wuk_t"],
                                                w["kv_norm"], w["alog"], w["dtb"], ckv_past is None)

    o_gdn, ssm_new = _gdn(proj_gdn, bg, nb, t, conv_state, ssm0, w["conv_w"], w["gdn_nw"])
    conv_new = proj_gdn.reshape(nb, t, -1)[:, t - (GDN_CONV - 1):, COL_QKV:COL_QKV + GDN_CONV_DIM]

    if ckv_past is None:
        o_mla = _attention(qcat, kcat, vt[0], w["wuv_t"])
    else:
        o_mla = _attention_cached(qcat, ckv_past, krope_past, kcat, w["wuv_t"])

    x2, xn2 = _merge_out(o_gdn, o_mla, w["w_br_gdn"], w["w_br_mla"], proj_gate, w["w_out"], x1,
                         w["mix_norm_post"], w["ffn2_norm_pre"])
    x3_, _, ffn2_w = _ffn(x2, *w["ffn2_w"], w["ffn2_norm_post"], xn=xn2)
    return (x3_.reshape(nb, t, d), conv_new, ssm_new, ckv, krope), dict(ffn1_w=ffn1_w, ffn2_w=ffn2_w)


def _prep_weights(l, d_model, **p):
    row = lambda v: v[l].reshape(1, -1).astype(F32)
    pad_lanes = lambda v: jnp.pad(v[l].reshape(1, -1).astype(F32), ((0, 0), (0, LANES - v.shape[1])))
    inv = ROPE_THETA ** (-jnp.arange(0, MLA_ROPE, 2, dtype=F32) / MLA_ROPE)
    w_in_gdn, w_in_gate, w_in_mla = _build_w_in(p["w_in"][l], d_model)
    w = dict(
        w_in_gdn=w_in_gdn, w_in_gate=w_in_gate, w_in_mla=w_in_mla,
        ffn1_norm_pre=row(p["ffn1_norm_pre"]), ffn1_norm_post=row(p["ffn1_norm_post"]),
        mix_norm_pre=row(p["mix_norm_pre"]), mix_norm_post=row(p["mix_norm_post"]),
        ffn2_norm_pre=row(p["ffn2_norm_pre"]), ffn2_norm_post=row(p["ffn2_norm_post"]),
        ffn1_w=(p["ffn1_w_gate"][l], p["ffn1_w_up"][l], p["ffn1_w_down"][l]),
        ffn2_w=(p["ffn2_w_gate"][l], p["ffn2_w_up"][l], p["ffn2_w_down"][l]),
        conv_w=p["gdn_conv_w"][l].astype(F32),
        alog=pad_lanes(p["gdn_a_log"]), dtb=pad_lanes(p["gdn_dt_bias"]),
        gdn_nw=row(p["gdn_norm_w"]), kv_norm=row(p["mla_kv_norm"]),
        inv128=jnp.tile(inv, LANES // inv.shape[0]).reshape(1, LANES),
        wuk_t=jnp.transpose(p["mla_w_uk"][l], (1, 2, 0)).astype(BF16),
        wuv_t=jnp.transpose(p["mla_w_uv"][l], (1, 0, 2)).astype(BF16),
        w_br_gdn=p["w_br_gdn"][l].astype(BF16), w_br_mla=p["w_br_mla"][l].astype(BF16),
        w_out=p["w_out"][l].astype(BF16),
    )
    return w


def kernel(x_prompt, x_sample, state_gdn_conv, state_gdn_ssm, cache_mla_ckv, cache_mla_krope, ffn1_norm_pre, ffn1_w_gate, ffn1_w_up, ffn1_w_down, ffn1_norm_post, mix_norm_pre, w_in, gdn_conv_w, gdn_a_log, gdn_dt_bias, gdn_norm_w, mla_kv_norm, mla_w_uk, mla_w_uv, w_br_gdn, w_br_mla, w_out, mix_norm_post, ffn2_norm_pre, ffn2_w_gate, ffn2_w_up, ffn2_w_down, ffn2_norm_post):
    params = dict(
        ffn1_norm_pre=ffn1_norm_pre, ffn1_w_gate=ffn1_w_gate, ffn1_w_up=ffn1_w_up,
        ffn1_w_down=ffn1_w_down, ffn1_norm_post=ffn1_norm_post, mix_norm_pre=mix_norm_pre,
        w_in=w_in, gdn_conv_w=gdn_conv_w, gdn_a_log=gdn_a_log, gdn_dt_bias=gdn_dt_bias,
        gdn_norm_w=gdn_norm_w, mla_kv_norm=mla_kv_norm, mla_w_uk=mla_w_uk, mla_w_uv=mla_w_uv,
        w_br_gdn=w_br_gdn, w_br_mla=w_br_mla, w_out=w_out, mix_norm_post=mix_norm_post,
        ffn2_norm_pre=ffn2_norm_pre, ffn2_w_gate=ffn2_w_gate, ffn2_w_up=ffn2_w_up,
        ffn2_w_down=ffn2_w_down, ffn2_norm_post=ffn2_norm_post)
    depth = w_in.shape[0]
    d_model = x_prompt.shape[-1]
    b_p = x_prompt.shape[0]
    past = cache_mla_ckv.shape[2]
    yp, ys = x_prompt, x_sample
    outs_p, outs_s = [], []
    for l in range(depth):
        w = _prep_weights(l, d_model, **params)
        (ys, *rest_s), w_bf16 = _layer(
            ys, past, state_gdn_conv[l], state_gdn_ssm[l], cache_mla_ckv[l], cache_mla_krope[l], w)
        (yp, *rest_p), _ = _layer(
            yp, 0, jnp.zeros((b_p, GDN_CONV - 1, GDN_CONV_DIM), F32),
            jnp.zeros((b_p, GDN_HEADS, GDN_DK, GDN_DV), F32), None, None, {**w, **w_bf16})
        outs_p.append(rest_p)
        outs_s.append(rest_s)
    stack = lambda outs, i: jnp.stack([o[i] for o in outs])
    return (yp, ys,
            stack(outs_p, 0), stack(outs_p, 1), stack(outs_p, 2), stack(outs_p, 3),
            stack(outs_s, 0), stack(outs_s, 1), stack(outs_s, 2), stack(outs_s, 3))
```

```python
import functools

import jax
import jax.numpy as jnp
from jax import lax
from jax.experimental import pallas as pl
from jax.experimental.pallas import tpu as pltpu

F32 = jnp.float32
BF16 = jnp.bfloat16

CHUNK = 64
LOG2_CHUNK = 6
NORM_EPS = 1e-6
GDN_HEADS = 8
GDN_DK = 128
GDN_DV = 128
GDN_CONV = 4
GDN_KEY_DIM = GDN_HEADS * GDN_DK
GDN_VAL_DIM = GDN_HEADS * GDN_DV
GDN_CONV_DIM = 2 * GDN_KEY_DIM + GDN_VAL_DIM
MLA_HEADS = 8
MLA_NOPE = 128
MLA_ROPE = 64
MLA_V = 128
MLA_KV_RANK = 512
MLA_SCALE = (MLA_NOPE + MLA_ROPE) ** -0.5
ROPE_THETA = 10000.0

LANES = 128
SUBLANES = 8
VMEM_CAP_BYTES = 56 * 2 ** 20
MLA_QK = MLA_KV_RANK + LANES
HI = lax.Precision.HIGHEST


def _params(semantics, vmem_bytes):
    limit = int(min(max(vmem_bytes, 16 * 2 ** 20), VMEM_CAP_BYTES))
    return pltpu.CompilerParams(dimension_semantics=semantics, vmem_limit_bytes=limit)


def _dot(a, b, prec=None):
    return lax.dot_general(a, b, (((1,), (0,)), ((), ())), precision=prec,
                           preferred_element_type=F32)


def _dot_nt(a, b, prec=None):
    return lax.dot_general(a, b, (((1,), (1,)), ((), ())), precision=prec,
                           preferred_element_type=F32)


def _dot_tn(a, b, prec=None):
    return lax.dot_general(a, b, (((0,), (0,)), ((), ())), precision=prec,
                           preferred_element_type=F32)


def _rms(y, g):
    return y * lax.rsqrt(jnp.mean(y * y, axis=-1, keepdims=True) + NORM_EPS) * g


def _silu(x):
    return x * jax.nn.sigmoid(x)


def _tile(n, pref):
    t = min(n, pref)
    assert n % t == 0, (n, pref)
    return t


def _ffn_kernel(*refs, nf, norm_in, emit_next, cast_w):
    refs = list(refs)
    xn_ref = None if norm_in else refs.pop(0)
    x_ref, wg_ref, wu_ref, wd_ref = refs[:4]
    del refs[:4]
    gpre_ref = refs.pop(0) if norm_in else None
    gpost_ref = refs.pop(0)
    gnext_ref = refs.pop(0) if emit_next else None
    xo_ref = refs.pop(0)
    xno_ref = refs.pop(0) if emit_next else None
    if cast_w:
        wgo_ref, wuo_ref, wdo_ref = refs[:3]
        del refs[:3]
    acc_ref = refs.pop(0)
    if norm_in:
        xn_ref = refs.pop(0)
    f = pl.program_id(1)

    @pl.when(f == 0)
    def _():
        acc_ref[...] = jnp.zeros(acc_ref.shape, F32)
        if norm_in:
            xn_ref[...] = _rms(x_ref[...], gpre_ref[...]).astype(BF16)

    xn = xn_ref[...]
    wg, wu, wd = wg_ref[...], wu_ref[...], wd_ref[...]
    if cast_w:
        wg, wu, wd = wg.astype(BF16), wu.astype(BF16), wd.astype(BF16)
        wgo_ref[...] = wg
        wuo_ref[...] = wu
        wdo_ref[...] = wd
    gate = _dot(xn, wg)
    up = _dot(xn, wu)
    h = (_silu(gate) * up).astype(BF16)
    acc_ref[...] += _dot(h, wd)

    @pl.when(f == nf - 1)
    def _():
        xnew = x_ref[...] + 0.5 * _rms(acc_ref[...], gpost_ref[...])
        xo_ref[...] = xnew
        if emit_next:
            xno_ref[...] = _rms(xnew, gnext_ref[...]).astype(BF16)


def _ffn(x, wg, wu, wd, gpost, xn=None, gpre=None, gnext=None):
    norm_in = xn is None
    emit_next = gnext is not None
    cast_w = wg.dtype == F32
    assert norm_in == (gpre is not None)
    m, d = x.shape
    dff = wg.shape[1]
    tm = _tile(m, 512)
    tf = _tile(dff, 256 if cast_w else 512)
    nf = dff // tf
    assert not cast_w or m == tm
    wsz = wg.dtype.itemsize
    row = lambda i, f: (i, 0)
    vec = pl.BlockSpec((1, d), lambda i, f: (0, 0))
    w_specs = [pl.BlockSpec((d, tf), lambda i, f: (0, f)),
               pl.BlockSpec((d, tf), lambda i, f: (0, f)),
               pl.BlockSpec((tf, d), lambda i, f: (f, 0))]
    ins = [] if norm_in else [xn]
    in_specs = [] if norm_in else [pl.BlockSpec((tm, d), row)]
    ins += [x, wg, wu, wd]
    in_specs += [pl.BlockSpec((tm, d), row)] + w_specs
    for g in (gpre, gpost, gnext):
        if g is not None:
            ins.append(g)
            in_specs.append(vec)
    out_specs = [pl.BlockSpec((tm, d), row)]
    out_shape = [jax.ShapeDtypeStruct((m, d), F32)]
    scratch = [pltpu.VMEM((tm, d), F32)]
    if emit_next:
        out_specs.append(pl.BlockSpec((tm, d), row))
        out_shape.append(jax.ShapeDtypeStruct((m, d), BF16))
    if cast_w:
        out_specs += w_specs
        out_shape += [jax.ShapeDtypeStruct(w.shape, BF16) for w in (wg, wu, wd)]
    if norm_in:
        scratch.append(pltpu.VMEM((tm, d), BF16))
    vmem = (2 * (tm * d * 2 + tm * d * 4 + 3 * d * tf * wsz + tm * d * 4 + tm * d * 2)
            + (2 * 3 * d * tf * 2 + 3 * d * tf * 2 if cast_w else 0)
            + tm * d * 4 + 3 * tm * tf * 4 + 2 * tm * d * 4)
    res = pl.pallas_call(
        functools.partial(_ffn_kernel, nf=nf, norm_in=norm_in, emit_next=emit_next, cast_w=cast_w),
        grid=(m // tm, nf),
        in_specs=in_specs,
        out_specs=out_specs,
        out_shape=out_shape,
        scratch_shapes=scratch,
        compiler_params=_params(("parallel", "arbitrary"), vmem),
        name="ffn_cast" if cast_w else "ffn",
    )(*ins)
    res = list(res)
    x_new = res.pop(0)
    xn_next = res.pop(0) if emit_next else None
    return x_new, xn_next, (tuple(res) if cast_w else (wg, wu, wd))


def _mm_kernel(a_ref, w_ref, o_ref):
    o_ref[...] = _dot(a_ref[...], w_ref[...]).astype(o_ref.dtype)


def _matmul(a, w, out_dtype):
    m, k = a.shape
    n = w.shape[1]
    tm = _tile(m, 1024)
    tn = _tile(n, 1024)
    osz = jnp.dtype(out_dtype).itemsize
    vmem = 2 * (tm * k * 2 + k * tn * 2 + tm * tn * osz) + tm * tn * 4
    return pl.pallas_call(
        _mm_kernel,
        grid=(m // tm, n // tn),
        in_specs=[pl.BlockSpec((tm, k), lambda i, j: (i, 0)),
                  pl.BlockSpec((k, tn), lambda i, j: (0, j))],
        out_specs=pl.BlockSpec((tm, tn), lambda i, j: (i, j)),
        out_shape=jax.ShapeDtypeStruct((m, n), out_dtype),
        compiler_params=_params(("parallel", "arbitrary"), vmem),
        name="in_proj",
    )(a, w)


COL_QKV = 0
COL_Z = COL_QKV + GDN_CONV_DIM
COL_QNOPE = 0
COL_QROPE = COL_QNOPE + MLA_HEADS * MLA_NOPE
COL_CKV = COL_QROPE + MLA_HEADS * LANES
COL_KR = COL_CKV + MLA_KV_RANK
COL_B = COL_KR + LANES
COL_A = COL_B + LANES
N_MLA = 3072


def _build_w_in(w_in, d_model):
    sizes = (GDN_CONV_DIM, GDN_VAL_DIM, GDN_HEADS, GDN_HEADS, MLA_HEADS * (MLA_NOPE + MLA_ROPE),
             MLA_KV_RANK, MLA_ROPE, d_model, d_model)
    offs = [0]
    for s in sizes:
        offs.append(offs[-1] + s)
    w_gdn = w_in[:, :offs[2]].astype(BF16)
    w_gate = w_in[:, offs[7]:].astype(BF16)
    part = lambda i: w_in[:, offs[i]:offs[i + 1]].astype(BF16)
    b, a, qm, ckv, kr = (part(i) for i in range(2, 7))
    d = w_in.shape[0]
    half = MLA_ROPE // 2
    qm = qm.reshape(d, MLA_HEADS, MLA_NOPE + MLA_ROPE)
    qn = qm[:, :, :MLA_NOPE].reshape(d, MLA_HEADS * MLA_NOPE)
    qr = qm[:, :, MLA_NOPE:]
    qr_pair = jnp.concatenate([qr, qr[:, :, half:], qr[:, :, :half]], axis=2).reshape(d, MLA_HEADS * LANES)
    kr_pair = jnp.concatenate([kr, kr[:, half:], kr[:, :half]], axis=1)
    zeros = lambda n: jnp.zeros((d, n), BF16)
    w_mla = jnp.concatenate([qn, qr_pair, ckv, kr_pair, b, zeros(LANES - GDN_HEADS), a, zeros(LANES - GDN_HEADS),
                             zeros(N_MLA - COL_A - LANES)], axis=1)
    return w_gdn, w_gate, w_mla


_NN = (((1,), (0,)), ((), ()))
_NT = (((1,), (1,)), ((), ()))
_TN = (((0,), (0,)), ((), ()))
INV_BLOCK = 16


def _mm(a, b, dims=_NN):
    return lax.dot_general(a.astype(BF16), b.astype(BF16), dims, preferred_element_type=F32)


def _gdn_kernel(qkv_ref, z_ref, beta_ref, g_ref, cst_ref, cw_ref, nw_ref, s0_ref,
                o_ref, sout_ref, past_ref, s_ref, *, G, C, nlev, nc):
    c = pl.program_id(1)

    @pl.when(c == 0)
    def _():
        past_ref[...] = cst_ref[...]
        s_ref[...] = s0_ref[...]

    ri = lax.broadcasted_iota(jnp.int32, (C, C), 0)
    ci = lax.broadcasted_iota(jnp.int32, (C, C), 1)
    incl = ri >= ci
    strict = ri > ci
    eye = (ri == ci).astype(F32)
    log2_blk = INV_BLOCK.bit_length() - 1
    bdiag = jnp.right_shift(ri, log2_blk) == jnp.right_shift(ci, log2_blk)

    first_row = lax.broadcasted_iota(jnp.int32, (C, 1), 0) == 0

    def conv(s, col):
        cols = slice(col, col + LANES)
        x = qkv_ref[s, :, cols]
        past = past_ref[s, :, cols]
        w = [cw_ref[j:j + 1, cols] for j in range(GDN_CONV)]
        acc = w[0] * x
        for j in range(1, GDN_CONV):
            top = sum(w[i] * past[SUBLANES - 1 - (j - 1 - i):SUBLANES - (j - 1 - i)] for i in range(j))
            acc = w[j] * x + jnp.where(first_row, top, pltpu.roll(acc, 1, axis=0))
        return _silu(acc)

    units = [(s, h) for s in range(G) for h in range(GDN_HEADS)]
    every = lambda fn, *lists: [fn(*args) for args in zip(*lists)]
    mm_inv = _mm
    beta_all, gcum, gcum_t = [], [], []
    for s in range(G):
        beta_all.append(beta_ref[s])
        g_all = g_ref[s]
        gcum.append(_dot(incl.astype(F32), g_all, HI))
        gcum_t.append(_dot_tn(g_all, (ci >= ri).astype(F32), HI))

    def l2n(x):
        return x * lax.rsqrt(jnp.sum(x * x, axis=-1, keepdims=True) + 1e-6)

    q = [l2n(conv(s, h * GDN_DK)) * (GDN_DK ** -0.5) for s, h in units]
    k = [l2n(conv(s, GDN_KEY_DIM + h * GDN_DK)) for s, h in units]
    v = [conv(s, 2 * GDN_KEY_DIM + h * GDN_DV) for s, h in units]
    bcol = [beta_all[s][:, h:h + 1] for s, h in units]
    gcol = [gcum[s][:, h:h + 1] for s, h in units]
    glast = [gcum[s][C - 1:C, h:h + 1] for s, h in units]
    decay = [jnp.where(incl, jnp.exp(jnp.where(incl, gcum[s][:, h:h + 1] - gcum_t[s][h:h + 1, :], 0.0)), 0.0)
             for s, h in units]
    qkk = every(lambda q_, k_: _mm(jnp.concatenate([q_, k_], axis=0), k_, _NT), q, k)
    qk = every(lambda x, d: x[:C] * d, qkk, decay)
    nmat = every(lambda x, b_, d: -jnp.where(strict, b_ * x[C:] * d, 0.0), qkk, bcol, decay)
    ndiag = every(lambda n_: jnp.where(bdiag, n_, 0.0), nmat)
    tinv = every(lambda n_: eye + n_, ndiag)
    npow = ndiag
    for _ in range(INV_BLOCK.bit_length() - 2):
        npow = every(mm_inv, npow, npow)
        tinv = every(lambda t_, p_: t_ + mm_inv(t_, p_), tinv, npow)
    mpow = every(lambda t_, n_, d_: mm_inv(t_, n_ - d_), tinv, nmat, ndiag)
    for lev in range(nlev):
        if lev:
            mpow = every(mm_inv, mpow, mpow)
        tinv = every(lambda t_, m_: t_ + mm_inv(m_, t_), tinv, mpow)
    egc = every(jnp.exp, gcol)
    sol = every(lambda t_, v_, k_, b_, e_: _mm(t_, jnp.concatenate([v_ * b_, k_ * (b_ * e_)], axis=1)),
                tinv, v, k, bcol, egc)
    s_old = [s_ref[s, h] for s, h in units]
    ws_qs = every(lambda x, q_, e_, s_: _mm(jnp.concatenate([x[:, GDN_DV:], q_ * e_], axis=0), s_),
                  sol, q, egc, s_old)
    v_new = every(lambda x, y: x[:, :GDN_DV] - y[:C], sol, ws_qs)
    o = every(lambda y, a_, vn: y[C:] + _mm(a_, vn), ws_qs, qk, v_new)
    s_new = every(lambda s_, gl, k_, gc, vn: s_ * jnp.exp(gl) + _mm(k_ * jnp.exp(gl - gc), vn, _TN),
                  s_old, glast, k, gcol, v_new)
    for (s, h), sn, o_ in zip(units, s_new, o):
        s_ref[s, h] = sn
        zz = z_ref[s, :, h * GDN_DV:(h + 1) * GDN_DV]
        o_ref[s, :, h * GDN_DV:(h + 1) * GDN_DV] = (_rms(o_, nw_ref[...]) * _silu(zz)).astype(BF16)

    past_ref[...] = qkv_ref[:, C - SUBLANES:C, :]

    @pl.when(c == nc - 1)
    def _():
        sout_ref[...] = s_ref[...]


def _gdn(proj_gdn, bg, nb, t, conv_state, ssm0, conv_w, nw):
    C = min(CHUNK, t)
    G = 2 if nb % 2 == 0 else 1
    assert t % C == 0 and C & (C - 1) == 0 and C % INV_BLOCK == 0
    nc = t // C
    nlev = (C // INV_BLOCK).bit_length() - 1
    pg = proj_gdn.reshape(nb, t, proj_gdn.shape[1])
    pm = bg.reshape(nb, t, bg.shape[1])
    fixed = lambda b, c: (0, 0)
    cst = jnp.pad(conv_state, ((0, 0), (SUBLANES - (GDN_CONV - 1), 0), (0, 0)))
    vmem = G * (2 * (C * GDN_CONV_DIM * 4 + C * GDN_VAL_DIM * 4 + 2 * C * LANES * 4 + SUBLANES * GDN_CONV_DIM * 4
                     + C * GDN_VAL_DIM * 2 + 2 * GDN_HEADS * GDN_DK * GDN_DV * 4)
                + SUBLANES * GDN_CONV_DIM * 4 + GDN_HEADS * GDN_DK * GDN_DV * 4) + 16 * 2 ** 20
    o, s_new = pl.pallas_call(
        functools.partial(_gdn_kernel, G=G, C=C, nlev=nlev, nc=nc),
        grid=(nb // G, nc),
        in_specs=[pl.BlockSpec((G, C, GDN_CONV_DIM), lambda b, c: (b, c, COL_QKV // GDN_CONV_DIM)),
                  pl.BlockSpec((G, C, GDN_VAL_DIM), lambda b, c: (b, c, COL_Z // GDN_VAL_DIM)),
                  pl.BlockSpec((G, C, LANES), lambda b, c: (b, c, 0)),
                  pl.BlockSpec((G, C, LANES), lambda b, c: (b, c, 1)),
                  pl.BlockSpec((G, SUBLANES, GDN_CONV_DIM), lambda b, c: (b, 0, 0)),
                  pl.BlockSpec((GDN_CONV, GDN_CONV_DIM), fixed),
                  pl.BlockSpec((1, GDN_DV), fixed),
                  pl.BlockSpec((G, GDN_HEADS, GDN_DK, GDN_DV), lambda b, c: (b, 0, 0, 0))],
        out_specs=[pl.BlockSpec((G, C, GDN_VAL_DIM), lambda b, c: (b, c, 0)),
                   pl.BlockSpec((G, GDN_HEADS, GDN_DK, GDN_DV), lambda b, c: (b, 0, 0, 0))],
        out_shape=[jax.ShapeDtypeStruct((nb, t, GDN_VAL_DIM), BF16),
                   jax.ShapeDtypeStruct((nb, GDN_HEADS, GDN_DK, GDN_DV), F32)],
        scratch_shapes=[pltpu.VMEM((G, SUBLANES, GDN_CONV_DIM), F32),
                        pltpu.VMEM((G, GDN_HEADS, GDN_DK, GDN_DV), F32)],
        compiler_params=_params(("parallel", "arbitrary"), vmem),
        name="gdn",
    )(pg, pg, pm, pm, cst, conv_w, nw, ssm0)
    return o.reshape(nb * t, GDN_VAL_DIM), s_new


def _mla_prep_kernel(hm_ref, w_ref, inv_ref, wuk_ref, kvn_ref, alog_ref, dtb_ref,
                     ckv_o, kr_o, qcat_o, kcat_o, bg_o, *maybe_vt_o, tm, pos0):
    j = pl.program_id(1)
    proj = _dot(hm_ref[...], w_ref[...])
    a_all = proj[:, COL_A:COL_A + LANES] + dtb_ref[...]
    softplus = jnp.maximum(a_all, 0.0) + jnp.log(1.0 + jnp.exp(-jnp.abs(a_all)))
    bg_o[:, :LANES] = jax.nn.sigmoid(proj[:, COL_B:COL_B + LANES])
    bg_o[:, LANES:] = -jnp.exp(alog_ref[...]) * softplus
    pos = (pos0 + j * tm + lax.broadcasted_iota(jnp.int32, (tm, LANES), 0)).astype(F32)
    lane = lax.broadcasted_iota(jnp.int32, (tm, LANES), 1)
    ang = pos * inv_ref[...]
    cos = jnp.cos(ang)
    sin = jnp.sin(ang)
    half = MLA_ROPE // 2
    cs = jnp.where(lane < MLA_ROPE, cos, jnp.where(lane < MLA_ROPE + half, -sin, sin))
    keep = lane < MLA_ROPE

    def rope(pair):
        prod = pair * cs
        return jnp.where(keep, prod + pltpu.roll(prod, MLA_ROPE, axis=1), 0.0)

    ckv = _rms(proj[:, COL_CKV:COL_CKV + MLA_KV_RANK], kvn_ref[...])
    ckv_o[0] = ckv
    kr = rope(proj[:, COL_KR:COL_KR + LANES])
    kr_o[0] = kr[:, :MLA_ROPE]
    kcat_o[0, :, :MLA_KV_RANK] = ckv.astype(BF16)
    kcat_o[0, :, MLA_KV_RANK:] = kr.astype(BF16)
    if maybe_vt_o:
        maybe_vt_o[0][0] = ckv.T.astype(BF16)
    for h in range(MLA_HEADS):
        qn = proj[:, COL_QNOPE + h * MLA_NOPE:COL_QNOPE + (h + 1) * MLA_NOPE].astype(BF16)
        qlat = _dot(qn, wuk_ref[h]) * MLA_SCALE
        qcat_o[0, h, :, :MLA_KV_RANK] = qlat.astype(BF16)
        qr = rope(proj[:, COL_QROPE + h * LANES:COL_QROPE + (h + 1) * LANES]) * MLA_SCALE
        qcat_o[0, h, :, MLA_KV_RANK:] = qr.astype(BF16)


def _mla_prep(hm, w_mla, nb, t, pos0, inv128, wuk_t, kvn, alog, dtb, emit_vt):
    d = hm.shape[1]
    tm = _tile(t, 512)
    nt = t // tm
    rows = lambda b, j: b * nt + j
    fixed2 = lambda b, j: (0, 0)
    vmem = (d * N_MLA * 2 + 2 * (tm * d * 2 + MLA_HEADS * MLA_NOPE * MLA_KV_RANK * 2
                                 + tm * (MLA_KV_RANK + 3 * LANES) * 4 + (MLA_HEADS + 2) * tm * MLA_QK * 2)
            + 2 * tm * N_MLA * 4 + 8 * 2 ** 20)
    out_specs = [pl.BlockSpec((1, tm, MLA_KV_RANK), lambda b, j: (b, j, 0)),
                 pl.BlockSpec((1, tm, MLA_ROPE), lambda b, j: (b, j, 0)),
                 pl.BlockSpec((1, MLA_HEADS, tm, MLA_QK), lambda b, j: (b, 0, j, 0)),
                 pl.BlockSpec((1, tm, MLA_QK), lambda b, j: (b, j, 0)),
                 pl.BlockSpec((tm, 2 * LANES), lambda b, j: (rows(b, j), 0))]
    out_shape = [jax.ShapeDtypeStruct((nb, t, MLA_KV_RANK), F32),
                 jax.ShapeDtypeStruct((nb, t, MLA_ROPE), F32),
                 jax.ShapeDtypeStruct((nb, MLA_HEADS, t, MLA_QK), BF16),
                 jax.ShapeDtypeStruct((nb, t, MLA_QK), BF16),
                 jax.ShapeDtypeStruct((nb * t, 2 * LANES), F32)]
    if emit_vt:
        out_specs.append(pl.BlockSpec((1, MLA_KV_RANK, tm), lambda b, j: (b, 0, j)))
        out_shape.append(jax.ShapeDtypeStruct((nb, MLA_KV_RANK, t), BF16))
    return pl.pallas_call(
        functools.partial(_mla_prep_kernel, tm=tm, pos0=pos0),
        grid=(nb, nt),
        in_specs=[pl.BlockSpec((tm, d), lambda b, j: (rows(b, j), 0)),
                  pl.BlockSpec((d, N_MLA), fixed2, pipeline_mode=pl.Buffered(1)),
                  pl.BlockSpec((1, LANES), fixed2),
                  pl.BlockSpec((MLA_HEADS, MLA_NOPE, MLA_KV_RANK), lambda b, j: (0, 0, 0)),
                  pl.BlockSpec((1, MLA_KV_RANK), fixed2),
                  pl.BlockSpec((1, LANES), fixed2),
                  pl.BlockSpec((1, LANES), fixed2)],
        out_specs=out_specs,
        out_shape=out_shape,
        compiler_params=_params(("parallel", "parallel"), vmem),
        name="mla_proj_prep",
    )(hm, w_mla, inv128, wuk_t, kvn, alog, dtb)


ATTN_GROUPS = 2


def _attn_kernel(q_ref, k_ref, vt_ref, wuv_ref, o_ref, s0_ref, s1_ref, m_ref, l_ref, acc_ref,
                 *, tq, tk, n_valid):
    i = pl.program_id(1)
    cols = MLA_HEADS * tq
    cg = cols // ATTN_GROUPS
    q_first = i * tq
    c = lax.broadcasted_iota(jnp.int32, (1, cg), 1)
    qchunk = jnp.right_shift(q_first + jnp.bitwise_and(c, tq - 1), LOG2_CHUNK)
    k_all = jnp.minimum((q_first // CHUNK + 1) * CHUNK, n_valid)
    k_any = jnp.minimum(((q_first + tq - 1) // CHUNK + 1) * CHUNK, n_valid)
    nk = (k_any + tk - 1) // tk
    m_ref[...] = jnp.full(m_ref.shape, -jnp.inf, F32)
    l_ref[...] = jnp.zeros(l_ref.shape, F32)
    acc_ref[...] = jnp.zeros(acc_ref.shape, F32)

    def scores(j, buf):
        start = pl.multiple_of(j * tk, tk)
        q = q_ref[0].reshape(cols, MLA_QK)
        buf[...] = _dot_nt(k_ref[0, pl.ds(start, tk), :], q)

    def fold(x, op):
        while x.shape[0] > SUBLANES:
            half = x.shape[0] // 2
            x = op(x[:half], x[half:])
        return x

    def softmax_values(j, buf, masked):
        start = pl.multiple_of(j * tk, tk)
        vt = vt_ref[0, :, pl.ds(start, tk)]
        if masked:
            kpos = start + lax.broadcasted_iota(jnp.int32, (tk, 1), 0)
            mask = jnp.logical_and(jnp.right_shift(kpos, LOG2_CHUNK) <= qchunk, kpos < n_valid)
        for g in range(ATTN_GROUPS):
            cs = slice(g * cg, (g + 1) * cg)
            s = buf[:, cs]
            if masked:
                s = jnp.where(mask, s, -jnp.inf)
            m_old = m_ref[:, cs]
            m_new = jnp.maximum(m_old, jnp.max(fold(s, jnp.maximum), axis=0, keepdims=True))
            alpha = jnp.exp(m_old - m_new)
            p = jnp.exp(s - m_new)
            l_ref[:, cs] = alpha * l_ref[:, cs] + jnp.sum(fold(p, jnp.add), axis=0, keepdims=True)
            acc_ref[:, cs] = alpha * acc_ref[:, cs] + _dot(vt, p.astype(BF16))
            m_ref[:, cs] = m_new

    def by_parity(j, fn):
        @pl.when(jnp.bitwise_and(j, 1) == 0)
        def _():
            fn(s0_ref, s1_ref)

        @pl.when(jnp.bitwise_and(j, 1) == 1)
        def _():
            fn(s1_ref, s0_ref)

    scores(0, s0_ref)

    def body(j, carry, masked):
        def step(cur, nxt):
            scores(j + 1, nxt)
            softmax_values(j, cur, masked)
        by_parity(j, step)
        return carry

    n_open = jnp.minimum(k_all // tk, nk - 1)
    lax.fori_loop(0, n_open, functools.partial(body, masked=False), 0)
    lax.fori_loop(n_open, nk - 1, functools.partial(body, masked=True), 0)
    by_parity(nk - 1, lambda cur, nxt: softmax_values(nk - 1, cur, True))
    inv_l = 1.0 / l_ref[...]
    for h in range(MLA_HEADS):
        hs = slice(h * tq, (h + 1) * tq)
        lat_t = (acc_ref[:, hs] * inv_l[:, hs]).astype(BF16)
        o_ref[:, h * MLA_V:(h + 1) * MLA_V] = _dot_tn(lat_t, wuv_ref[h]).astype(BF16)


def _attn_cache_kernel(q_ref, ckv_ref, kr_ref, knew_ref, wuv_ref, o_ref, m_ref, l_ref, acc_ref,
                       *, tq, tk, nkb, past):
    j = pl.program_id(1)
    rows = MLA_HEADS * tq
    q = q_ref[0].reshape(rows, MLA_QK)
    r = lax.broadcasted_iota(jnp.int32, (rows, 1), 0)
    qchunk = jnp.right_shift(past + jnp.bitwise_and(r, tq - 1), LOG2_CHUNK)

    @pl.when(j == 0)
    def _():
        m_ref[...] = jnp.full(m_ref.shape, -jnp.inf, F32)
        l_ref[...] = jnp.zeros(l_ref.shape, F32)
        acc_ref[...] = jnp.zeros(acc_ref.shape, F32)

    def fold(x, op):
        while x.shape[1] > LANES:
            half = x.shape[1] // 2
            x = op(x[:, :half], x[:, half:])
        return x

    def update(s, kpos, vals):
        s = jnp.where(jnp.right_shift(kpos, LOG2_CHUNK) <= qchunk, s, -jnp.inf)
        m_old = m_ref[...]
        m_new = jnp.maximum(m_old, jnp.max(fold(s, jnp.maximum), axis=-1, keepdims=True))
        alpha = jnp.exp(m_old - m_new)
        p = jnp.exp(s - m_new)
        l_ref[...] = alpha * l_ref[...] + jnp.sum(fold(p, jnp.add), axis=-1, keepdims=True)
        acc_ref[...] = alpha * acc_ref[...] + _dot(p.astype(BF16), vals)
        m_ref[...] = m_new

    ck = ckv_ref[0].astype(BF16)
    s = (_dot_nt(q[:, :MLA_KV_RANK], ck)
         + _dot_nt(q[:, MLA_KV_RANK:MLA_KV_RANK + MLA_ROPE], kr_ref[0].astype(BF16)))
    update(s, j * tk + lax.broadcasted_iota(jnp.int32, (1, tk), 1), ck)

    @pl.when(j == nkb - 1)
    def _():
        kn = knew_ref[0]
        update(_dot_nt(q, kn), past + lax.broadcasted_iota(jnp.int32, (1, tq), 1), kn[:, :MLA_KV_RANK])
        inv_l = 1.0 / l_ref[...]
        for h in range(MLA_HEADS):
            hs = slice(h * tq, (h + 1) * tq)
            lat = (acc_ref[hs, :] * inv_l[hs, :]).astype(BF16)
            o_ref[:, h * MLA_V:(h + 1) * MLA_V] = _dot(lat, wuv_ref[h]).astype(BF16)


def _attention(qcat, kcat, vt, wuv_t):
    nb, _, t, _ = qcat.shape
    tq = _tile(t, 128)
    tk = _tile(t, 512)
    assert tq & (tq - 1) == 0
    nq = t // tq
    cols = MLA_HEADS * tq
    vmem = (2 * (cols * MLA_QK * 2 + t * MLA_QK * 2 + MLA_KV_RANK * t * 2
                 + MLA_HEADS * MLA_KV_RANK * MLA_V * 2 + tq * MLA_HEADS * MLA_V * 2)
            + 2 * tk * cols * 4 + 2 * SUBLANES * cols * 4 + MLA_KV_RANK * cols * 4
            + 3 * tk * cols * 4 // ATTN_GROUPS + 4 * 2 ** 20)
    return pl.pallas_call(
        functools.partial(_attn_kernel, tq=tq, tk=tk, n_valid=t),
        grid=(nb, nq),
        in_specs=[pl.BlockSpec((1, MLA_HEADS, tq, MLA_QK), lambda b, i: (b, 0, i, 0)),
                  pl.BlockSpec((1, t, MLA_QK), lambda b, i: (b, 0, 0)),
                  pl.BlockSpec((1, MLA_KV_RANK, t), lambda b, i: (b, 0, 0)),
                  pl.BlockSpec((MLA_HEADS, MLA_KV_RANK, MLA_V), lambda b, i: (0, 0, 0))],
        out_specs=pl.BlockSpec((tq, MLA_HEADS * MLA_V), lambda b, i: (b * nq + i, 0)),
        out_shape=jax.ShapeDtypeStruct((nb * t, MLA_HEADS * MLA_V), BF16),
        scratch_shapes=[pltpu.VMEM((tk, cols), F32), pltpu.VMEM((tk, cols), F32), pltpu.VMEM((1, cols), F32),
                        pltpu.VMEM((1, cols), F32), pltpu.VMEM((MLA_KV_RANK, cols), F32)],
        compiler_params=_params(("parallel", "arbitrary"), vmem),
        name="mla_attention",
    )(qcat, kcat, vt, wuv_t)


def _attention_cached(qcat, ckv_past, krope_past, kcat_new, wuv_t):
    nb, _, t, _ = qcat.shape
    past = ckv_past.shape[1]
    tk = _tile(past, 4096)
    nkb = past // tk
    assert t & (t - 1) == 0
    rows = MLA_HEADS * t
    vmem = (2 * (rows * MLA_QK * 2 + tk * MLA_KV_RANK * 4 + tk * LANES * 4 + t * MLA_QK * 2
                 + MLA_HEADS * MLA_KV_RANK * MLA_V * 2 + t * MLA_HEADS * MLA_V * 2)
            + 2 * rows * LANES * 4 + rows * MLA_KV_RANK * 4 + tk * MLA_QK * 2 + 4 * rows * tk * 4 + 4 * 2 ** 20)
    return pl.pallas_call(
        functools.partial(_attn_cache_kernel, tq=t, tk=tk, nkb=nkb, past=past),
        grid=(nb, nkb),
        in_specs=[pl.BlockSpec((1, MLA_HEADS, t, MLA_QK), lambda b, j: (b, 0, 0, 0)),
                  pl.BlockSpec((1, tk, MLA_KV_RANK), lambda b, j: (b, j, 0)),
                  pl.BlockSpec((1, tk, MLA_ROPE), lambda b, j: (b, j, 0)),
                  pl.BlockSpec((1, t, MLA_QK), lambda b, j: (b, 0, 0)),
                  pl.BlockSpec((MLA_HEADS, MLA_KV_RANK, MLA_V), lambda b, j: (0, 0, 0))],
        out_specs=pl.BlockSpec((t, MLA_HEADS * MLA_V), lambda b, j: (b, 0)),
        out_shape=jax.ShapeDtypeStruct((nb * t, MLA_HEADS * MLA_V), BF16),
        scratch_shapes=[pltpu.VMEM((rows, 1), F32), pltpu.VMEM((rows, 1), F32),
                        pltpu.VMEM((rows, MLA_KV_RANK), F32)],
        compiler_params=_params(("parallel", "arbitrary"), vmem),
        name="mla_attention_cached",
    )(qcat, ckv_past, krope_past, kcat_new, wuv_t)


def _merge_out_kernel(og_ref, om_ref, wg_ref, wm_ref, gg_ref, gm_ref, wo_ref, x_ref, gpost_ref, gnext_ref,
                      xo_ref, xno_ref):
    tg = _dot(og_ref[...], wg_ref[...])
    tm_ = _dot(om_ref[...], wm_ref[...])
    merged = (jax.nn.sigmoid(gg_ref[...].astype(F32)) * tg
              + jax.nn.sigmoid(gm_ref[...].astype(F32)) * tm_).astype(BF16)
    y = _dot(merged, wo_ref[...])
    xnew = x_ref[...] + _rms(y, gpost_ref[...])
    xo_ref[...] = xnew
    xno_ref[...] = _rms(xnew, gnext_ref[...]).astype(BF16)


def _merge_out(og, om, wbg, wbm, proj_gate, wout, x, gpost, gnext):
    m, kg = og.shape
    d = wbg.shape[1]
    tm = _tile(m, 256)
    gsz = proj_gate.dtype.itemsize
    row = lambda i: (i, 0)
    fixed = lambda i: (0, 0)
    resident = dict(pipeline_mode=pl.Buffered(1))
    vmem = ((2 * kg * d + d * d) * 2
            + 2 * (2 * tm * kg * 2 + 2 * tm * d * gsz + 2 * tm * d * 4 + tm * d * 2)
            + 4 * tm * d * 4 + 2 * 2 ** 20)
    return pl.pallas_call(
        _merge_out_kernel,
        grid=(m // tm,),
        in_specs=[pl.BlockSpec((tm, kg), row),
                  pl.BlockSpec((tm, kg), row),
                  pl.BlockSpec((kg, d), fixed, **resident),
                  pl.BlockSpec((kg, d), fixed, **resident),
                  pl.BlockSpec((tm, d), lambda i: (i, 0)),
                  pl.BlockSpec((tm, d), lambda i: (i, 1)),
                  pl.BlockSpec((d, d), fixed, **resident),
                  pl.BlockSpec((tm, d), row),
                  pl.BlockSpec((1, d), fixed),
                  pl.BlockSpec((1, d), fixed)],
        out_specs=[pl.BlockSpec((tm, d), row), pl.BlockSpec((tm, d), row)],
        out_shape=[jax.ShapeDtypeStruct((m, d), F32), jax.ShapeDtypeStruct((m, d), BF16)],
        compiler_params=_params(("parallel",), vmem),
        name="merge_out",
    )(og, om, wbg, wbm, proj_gate, proj_gate, wout, x, gpost, gnext)


def _layer(x3, pos0, conv_state, ssm0, ckv_past, krope_past, w):
    nb, t, d = x3.shape
    m = nb * t
    x = x3.reshape(m, d)
    x1, hm, ffn1_w = _ffn(x, *w["ffn1_w"], w["ffn1_norm_post"],
                          gpre=w["ffn1_norm_pre"], gnext=w["mix_norm_pre"])
    proj_gdn = _matmul(hm, w["w_in_gdn"], F32)
    proj_gate = _matmul(hm, w["w_in_gate"], BF16)
    ckv, krope, qcat, kcat, bg, *vt = _mla_prep(hm, w["w_in_mla"], nb, t, pos0, w["inv128"], w["wuk_t"],
                                                w["kv_norm"], w["alog"], w["dtb"], ckv_past is None)

    o_gdn, ssm_new = _gdn(proj_gdn, bg, nb, t, conv_state, ssm0, w["conv_w"], w["gdn_nw"])
    conv_new = proj_gdn.reshape(nb, t, -1)[:, t - (GDN_CONV - 1):, COL_QKV:COL_QKV + GDN_CONV_DIM]

    if ckv_past is None:
        o_mla = _attention(qcat, kcat, vt[0], w["wuv_t"])
    else:
        o_mla = _attention_cached(qcat, ckv_past, krope_past, kcat, w["wuv_t"])

    x2, xn2 = _merge_out(o_gdn, o_mla, w["w_br_gdn"], w["w_br_mla"], proj_gate, w["w_out"], x1,
                         w["mix_norm_post"], w["ffn2_norm_pre"])
    x3_, _, ffn2_w = _ffn(x2, *w["ffn2_w"], w["ffn2_norm_post"], xn=xn2)
    return (x3_.reshape(nb, t, d), conv_new, ssm_new, ckv, krope), dict(ffn1_w=ffn1_w, ffn2_w=ffn2_w)


def _prep_weights(l, d_model, **p):
    row = lambda v: v[l].reshape(1, -1).astype(F32)
    pad_lanes = lambda v: jnp.pad(v[l].reshape(1, -1).astype(F32), ((0, 0), (0, LANES - v.shape[1])))
    inv = ROPE_THETA ** (-jnp.arange(0, MLA_ROPE, 2, dtype=F32) / MLA_ROPE)
    w_in_gdn, w_in_gate, w_in_mla = _build_w_in(p["w_in"][l], d_model)
    w = dict(
        w_in_gdn=w_in_gdn, w_in_gate=w_in_gate, w_in_mla=w_in_mla,
        ffn1_norm_pre=row(p["ffn1_norm_pre"]), ffn1_norm_post=row(p["ffn1_norm_post"]),
        mix_norm_pre=row(p["mix_norm_pre"]), mix_norm_post=row(p["mix_norm_post"]),
        ffn2_norm_pre=row(p["ffn2_norm_pre"]), ffn2_norm_post=row(p["ffn2_norm_post"]),
        ffn1_w=(p["ffn1_w_gate"][l], p["ffn1_w_up"][l], p["ffn1_w_down"][l]),
        ffn2_w=(p["ffn2_w_gate"][l], p["ffn2_w_up"][l], p["ffn2_w_down"][l]),
        conv_w=p["gdn_conv_w"][l].astype(F32),
        alog=pad_lanes(p["gdn_a_log"]), dtb=pad_lanes(p["gdn_dt_bias"]),
        gdn_nw=row(p["gdn_norm_w"]), kv_norm=row(p["mla_kv_norm"]),
        inv128=jnp.tile(inv, LANES // inv.shape[0]).reshape(1, LANES),
        wuk_t=jnp.transpose(p["mla_w_uk"][l], (1, 2, 0)).astype(BF16),
        wuv_t=jnp.transpose(p["mla_w_uv"][l], (1, 0, 2)).astype(BF16),
        w_br_gdn=p["w_br_gdn"][l].astype(BF16), w_br_mla=p["w_br_mla"][l].astype(BF16),
        w_out=p["w_out"][l].astype(BF16),
    )
    return w


def kernel(x_prompt, x_sample, state_gdn_conv, state_gdn_ssm, cache_mla_ckv, cache_mla_krope, ffn1_norm_pre, ffn1_w_gate, ffn1_w_up, ffn1_w_down, ffn1_norm_post, mix_norm_pre, w_in, gdn_conv_w, gdn_a_log, gdn_dt_bias, gdn_norm_w, mla_kv_norm, mla_w_uk, mla_w_uv, w_br_gdn, w_br_mla, w_out, mix_norm_post, ffn2_norm_pre, ffn2_w_gate, ffn2_w_up, ffn2_w_down, ffn2_norm_post):
    params = dict(
        ffn1_norm_pre=ffn1_norm_pre, ffn1_w_gate=ffn1_w_gate, ffn1_w_up=ffn1_w_up,
        ffn1_w_down=ffn1_w_down, ffn1_norm_post=ffn1_norm_post, mix_norm_pre=mix_norm_pre,
        w_in=w_in, gdn_conv_w=gdn_conv_w, gdn_a_log=gdn_a_log, gdn_dt_bias=gdn_dt_bias,
        gdn_norm_w=gdn_norm_w, mla_kv_norm=mla_kv_norm, mla_w_uk=mla_w_uk, mla_w_uv=mla_w_uv,
        w_br_gdn=w_br_gdn, w_br_mla=w_br_mla, w_out=w_out, mix_norm_post=mix_norm_post,
        ffn2_norm_pre=ffn2_norm_pre, ffn2_w_gate=ffn2_w_gate, ffn2_w_up=ffn2_w_up,
        ffn2_w_down=ffn2_w_down, ffn2_norm_post=ffn2_norm_post)
    depth = w_in.shape[0]
    d_model = x_prompt.shape[-1]
    b_p = x_prompt.shape[0]
    past = cache_mla_ckv.shape[2]
    yp, ys = x_prompt, x_sample
    outs_p, outs_s = [], []
    for l in range(depth):
        w = _prep_weights(l, d_model, **params)
        (ys, *rest_s), w_bf16 = _layer(
            ys, past, state_gdn_conv[l], state_gdn_ssm[l], cache_mla_ckv[l], cache_mla_krope[l], w)
        (yp, *rest_p), _ = _layer(
            yp, 0, jnp.zeros((b_p, GDN_CONV - 1, GDN_CONV_DIM), F32),
            jnp.zeros((b_p, GDN_HEADS, GDN_DK, GDN_DV), F32), None, None, {**w, **w_bf16})
        outs_p.append(rest_p)
        outs_s.append(rest_s)
    stack = lambda outs, i: jnp.stack([o[i] for o in outs])
    return (yp, ys,
            stack(outs_p, 0), stack(outs_p, 1), stack(outs_p, 2), stack(outs_p, 3),
            stack(outs_s, 0), stack(outs_s, 1), stack(outs_s, 2), stack(outs_s, 3))
```

```python
import functools

import jax
import jax.numpy as jnp
from jax import lax
from jax.experimental import pallas as pl
from jax.experimental.pallas import tpu as pltpu

F32 = jnp.float32
BF16 = jnp.bfloat16

CHUNK = 64
LOG2_CHUNK = 6
NORM_EPS = 1e-6
GDN_HEADS = 8
GDN_DK = 128
GDN_DV = 128
GDN_CONV = 4
GDN_KEY_DIM = GDN_HEADS * GDN_DK
GDN_VAL_DIM = GDN_HEADS * GDN_DV
GDN_CONV_DIM = 2 * GDN_KEY_DIM + GDN_VAL_DIM
MLA_HEADS = 8
MLA_NOPE = 128
MLA_ROPE = 64
MLA_V = 128
MLA_KV_RANK = 512
MLA_SCALE = (MLA_NOPE + MLA_ROPE) ** -0.5
ROPE_THETA = 10000.0

LANES = 128
SUBLANES = 8
VMEM_CAP_BYTES = 56 * 2 ** 20
MLA_QK = MLA_KV_RANK + LANES
HI = lax.Precision.HIGHEST


def _params(semantics, vmem_bytes):
    limit = int(min(max(vmem_bytes, 16 * 2 ** 20), VMEM_CAP_BYTES))
    return pltpu.CompilerParams(dimension_semantics=semantics, vmem_limit_bytes=limit)


def _dot(a, b, prec=None):
    return lax.dot_general(a, b, (((1,), (0,)), ((), ())), precision=prec,
                           preferred_element_type=F32)


def _dot_nt(a, b, prec=None):
    return lax.dot_general(a, b, (((1,), (1,)), ((), ())), precision=prec,
                           preferred_element_type=F32)


def _dot_tn(a, b, prec=None):
    return lax.dot_general(a, b, (((0,), (0,)), ((), ())), precision=prec,
                           preferred_element_type=F32)


def _rms(y, g):
    return y * lax.rsqrt(jnp.mean(y * y, axis=-1, keepdims=True) + NORM_EPS) * g


def _silu(x):
    return x * jax.nn.sigmoid(x)


def _tile(n, pref):
    t = min(n, pref)
    assert n % t == 0, (n, pref)
    return t


def _ffn_kernel(*refs, nf, norm_in, emit_next, cast_w):
    refs = list(refs)
    xn_ref = None if norm_in else refs.pop(0)
    x_ref, wg_ref, wu_ref, wd_ref = refs[:4]
    del refs[:4]
    gpre_ref = refs.pop(0) if norm_in else None
    gpost_ref = refs.pop(0)
    gnext_ref = refs.pop(0) if emit_next else None
    xo_ref = refs.pop(0)
    xno_ref = refs.pop(0) if emit_next else None
    if cast_w:
        wgo_ref, wuo_ref, wdo_ref = refs[:3]
        del refs[:3]
    acc_ref = refs.pop(0)
    if norm_in:
        xn_ref = refs.pop(0)
    f = pl.program_id(1)

    @pl.when(f == 0)
    def _():
        acc_ref[...] = jnp.zeros(acc_ref.shape, F32)
        if norm_in:
            xn_ref[...] = _rms(x_ref[...], gpre_ref[...]).astype(BF16)

    xn = xn_ref[...]
    wg, wu, wd = wg_ref[...], wu_ref[...], wd_ref[...]
    if cast_w:
        wg, wu, wd = wg.astype(BF16), wu.astype(BF16), wd.astype(BF16)
        wgo_ref[...] = wg
        wuo_ref[...] = wu
        wdo_ref[...] = wd
    gate = _dot(xn, wg)
    up = _dot(xn, wu)
    h = (_silu(gate) * up).astype(BF16)
    acc_ref[...] += _dot(h, wd)

    @pl.when(f == nf - 1)
    def _():
        xnew = x_ref[...] + 0.5 * _rms(acc_ref[...], gpost_ref[...])
        xo_ref[...] = xnew
        if emit_next:
            xno_ref[...] = _rms(xnew, gnext_ref[...]).astype(BF16)


def _ffn(x, wg, wu, wd, gpost, xn=None, gpre=None, gnext=None):
    norm_in = xn is None
    emit_next = gnext is not None
    cast_w = wg.dtype == F32
    assert norm_in == (gpre is not None)
    m, d = x.shape
    dff = wg.shape[1]
    tm = _tile(m, 512)
    tf = _tile(dff, 256 if cast_w else 512)
    nf = dff // tf
    assert not cast_w or m == tm
    wsz = wg.dtype.itemsize
    row = lambda i, f: (i, 0)
    vec = pl.BlockSpec((1, d), lambda i, f: (0, 0))
    w_specs = [pl.BlockSpec((d, tf), lambda i, f: (0, f)),
               pl.BlockSpec((d, tf), lambda i, f: (0, f)),
               pl.BlockSpec((tf, d), lambda i, f: (f, 0))]
    ins = [] if norm_in else [xn]
    in_specs = [] if norm_in else [pl.BlockSpec((tm, d), row)]
    ins += [x, wg, wu, wd]
    in_specs += [pl.BlockSpec((tm, d), row)] + w_specs
    for g in (gpre, gpost, gnext):
        if g is not None:
            ins.append(g)
            in_specs.append(vec)
    out_specs = [pl.BlockSpec((tm, d), row)]
    out_shape = [jax.ShapeDtypeStruct((m, d), F32)]
    scratch = [pltpu.VMEM((tm, d), F32)]
    if emit_next:
        out_specs.append(pl.BlockSpec((tm, d), row))
        out_shape.append(jax.ShapeDtypeStruct((m, d), BF16))
    if cast_w:
        out_specs += w_specs
        out_shape += [jax.ShapeDtypeStruct(w.shape, BF16) for w in (wg, wu, wd)]
    if norm_in:
        scratch.append(pltpu.VMEM((tm, d), BF16))
    vmem = (2 * (tm * d * 2 + tm * d * 4 + 3 * d * tf * wsz + tm * d * 4 + tm * d * 2)
            + (2 * 3 * d * tf * 2 + 3 * d * tf * 2 if cast_w else 0)
            + tm * d * 4 + 3 * tm * tf * 4 + 2 * tm * d * 4)
    res = pl.pallas_call(
        functools.partial(_ffn_kernel, nf=nf, norm_in=norm_in, emit_next=emit_next, cast_w=cast_w),
        grid=(m // tm, nf),
        in_specs=in_specs,
        out_specs=out_specs,
        out_shape=out_shape,
        scratch_shapes=scratch,
        compiler_params=_params(("parallel", "arbitrary"), vmem),
        name="ffn_cast" if cast_w else "ffn",
    )(*ins)
    res = list(res)
    x_new = res.pop(0)
    xn_next = res.pop(0) if emit_next else None
    return x_new, xn_next, (tuple(res) if cast_w else (wg, wu, wd))


def _mm_kernel(a_ref, w_ref, o_ref):
    o_ref[...] = _dot(a_ref[...], w_ref[...]).astype(o_ref.dtype)


def _matmul(a, w, out_dtype):
    m, k = a.shape
    n = w.shape[1]
    tm = _tile(m, 1024)
    tn = _tile(n, 1024)
    osz = jnp.dtype(out_dtype).itemsize
    vmem = 2 * (tm * k * 2 + k * tn * 2 + tm * tn * osz) + tm * tn * 4
    return pl.pallas_call(
        _mm_kernel,
        grid=(m // tm, n // tn),
        in_specs=[pl.BlockSpec((tm, k), lambda i, j: (i, 0)),
                  pl.BlockSpec((k, tn), lambda i, j: (0, j))],
        out_specs=pl.BlockSpec((tm, tn), lambda i, j: (i, j)),
        out_shape=jax.ShapeDtypeStruct((m, n), out_dtype),
        compiler_params=_params(("parallel", "arbitrary"), vmem),
        name="in_proj",
    )(a, w)


COL_QKV = 0
COL_Z = COL_QKV + GDN_CONV_DIM
COL_QNOPE = 0
COL_QROPE = COL_QNOPE + MLA_HEADS * MLA_NOPE
COL_CKV = COL_QROPE + MLA_HEADS * LANES
COL_KR = COL_CKV + MLA_KV_RANK
COL_B = COL_KR + LANES
COL_A = COL_B + LANES
N_MLA = 3072


def _build_w_in(w_in, d_model):
    sizes = (GDN_CONV_DIM, GDN_VAL_DIM, GDN_HEADS, GDN_HEADS, MLA_HEADS * (MLA_NOPE + MLA_ROPE),
             MLA_KV_RANK, MLA_ROPE, d_model, d_model)
    offs = [0]
    for s in sizes:
        offs.append(offs[-1] + s)
    w_gdn = w_in[:, :offs[2]].astype(BF16)
    w_gate = w_in[:, offs[7]:].astype(BF16)
    part = lambda i: w_in[:, offs[i]:offs[i + 1]].astype(BF16)
    b, a, qm, ckv, kr = (part(i) for i in range(2, 7))
    d = w_in.shape[0]
    half = MLA_ROPE // 2
    qm = qm.reshape(d, MLA_HEADS, MLA_NOPE + MLA_ROPE)
    qn = qm[:, :, :MLA_NOPE].reshape(d, MLA_HEADS * MLA_NOPE)
    qr = qm[:, :, MLA_NOPE:]
    qr_pair = jnp.concatenate([qr, qr[:, :, half:], qr[:, :, :half]], axis=2).reshape(d, MLA_HEADS * LANES)
    kr_pair = jnp.concatenate([kr, kr[:, half:], kr[:, :half]], axis=1)
    zeros = lambda n: jnp.zeros((d, n), BF16)
    w_mla = jnp.concatenate([qn, qr_pair, ckv, kr_pair, b, zeros(LANES - GDN_HEADS), a, zeros(LANES - GDN_HEADS),
                             zeros(N_MLA - COL_A - LANES)], axis=1)
    return w_gdn, w_gate, w_mla


_NN = (((1,), (0,)), ((), ()))
_NT = (((1,), (1,)), ((), ()))
_TN = (((0,), (0,)), ((), ()))
INV_BLOCK = 16


def _mm(a, b, dims=_NN):
    return lax.dot_general(a.astype(BF16), b.astype(BF16), dims, preferred_element_type=F32)


def _gdn_kernel(qkv_ref, z_ref, beta_ref, g_ref, cst_ref, cw_ref, nw_ref, s0_ref,
                o_ref, sout_ref, past_ref, s_ref, *, G, C, nlev, nc):
    c = pl.program_id(1)

    @pl.when(c == 0)
    def _():
        past_ref[...] = cst_ref[...]
        s_ref[...] = s0_ref[...]

    ri = lax.broadcasted_iota(jnp.int32, (C, C), 0)
    ci = lax.broadcasted_iota(jnp.int32, (C, C), 1)
    incl = ri >= ci
    strict = ri > ci
    eye = (ri == ci).astype(F32)
    log2_blk = INV_BLOCK.bit_length() - 1
    bdiag = jnp.right_shift(ri, log2_blk) == jnp.right_shift(ci, log2_blk)

    first_row = lax.broadcasted_iota(jnp.int32, (C, 1), 0) == 0

    def conv(s, col):
        cols = slice(col, col + LANES)
        x = qkv_ref[s, :, cols]
        past = past_ref[s, :, cols]
        w = [cw_ref[j:j + 1, cols] for j in range(GDN_CONV)]
        acc = w[0] * x
        for j in range(1, GDN_CONV):
            top = sum(w[i] * past[SUBLANES - 1 - (j - 1 - i):SUBLANES - (j - 1 - i)] for i in range(j))
            acc = w[j] * x + jnp.where(first_row, top, pltpu.roll(acc, 1, axis=0))
        return _silu(acc)

    units = [(s, h) for s in range(G) for h in range(GDN_HEADS)]
    every = lambda fn, *lists: [fn(*args) for args in zip(*lists)]
    mm_inv = _mm
    beta_all, gcum, gcum_t = [], [], []
    for s in range(G):
        beta_all.append(beta_ref[s])
        g_all = g_ref[s]
        gcum.append(_dot(incl.astype(F32), g_all, HI))
        gcum_t.append(_dot_tn(g_all, (ci >= ri).astype(F32), HI))

    def l2n(x):
        return x * lax.rsqrt(jnp.sum(x * x, axis=-1, keepdims=True) + 1e-6)

    q = [l2n(conv(s, h * GDN_DK)) * (GDN_DK ** -0.5) for s, h in units]
    k = [l2n(conv(s, GDN_KEY_DIM + h * GDN_DK)) for s, h in units]
    v = [conv(s, 2 * GDN_KEY_DIM + h * GDN_DV) for s, h in units]
    bcol = [beta_all[s][:, h:h + 1] for s, h in units]
    gcol = [gcum[s][:, h:h + 1] for s, h in units]
    glast = [gcum[s][C - 1:C, h:h + 1] for s, h in units]
    decay = [jnp.where(incl, jnp.exp(jnp.where(incl, gcum[s][:, h:h + 1] - gcum_t[s][h:h + 1, :], 0.0)), 0.0)
             for s, h in units]
    qkk = every(lambda q_, k_: _mm(jnp.concatenate([q_, k_], axis=0), k_, _NT), q, k)
    qk = every(lambda x, d: x[:C] * d, qkk, decay)
    nmat = every(lambda x, b_, d: -jnp.where(strict, b_ * x[C:] * d, 0.0), qkk, bcol, decay)
    ndiag = every(lambda n_: jnp.where(bdiag, n_, 0.0), nmat)
    tinv = every(lambda n_: eye + n_, ndiag)
    npow = ndiag
    for _ in range(INV_BLOCK.bit_length() - 2):
        npow = every(mm_inv, npow, npow)
        tinv = every(lambda t_, p_: t_ + mm_inv(t_, p_), tinv, npow)
    mpow = every(lambda t_, n_, d_: mm_inv(t_, n_ - d_), tinv, nmat, ndiag)
    for lev in range(nlev):
        if lev:
            mpow = every(mm_inv, mpow, mpow)
        tinv = every(lambda t_, m_: t_ + mm_inv(m_, t_), tinv, mpow)
    egc = every(jnp.exp, gcol)
    sol = every(lambda t_, v_, k_, b_, e_: _mm(t_, jnp.concatenate([v_ * b_, k_ * (b_ * e_)], axis=1)),
                tinv, v, k, bcol, egc)
    s_old = [s_ref[s, h] for s, h in units]
    ws_qs = every(lambda x, q_, e_, s_: _mm(jnp.concatenate([x[:, GDN_DV:], q_ * e_], axis=0), s_),
                  sol, q, egc, s_old)
    v_new = every(lambda x, y: x[:, :GDN_DV] - y[:C], sol, ws_qs)
    o = every(lambda y, a_, vn: y[C:] + _mm(a_, vn), ws_qs, qk, v_new)
    s_new = every(lambda s_, gl, k_, gc, vn: s_ * jnp.exp(gl) + _mm(k_ * jnp.exp(gl - gc), vn, _TN),
                  s_old, glast, k, gcol, v_new)
    for (s, h), sn, o_ in zip(units, s_new, o):
        s_ref[s, h] = sn
        zz = z_ref[s, :, h * GDN_DV:(h + 1) * GDN_DV]
        o_ref[s, :, h * GDN_DV:(h + 1) * GDN_DV] = (_rms(o_, nw_ref[...]) * _silu(zz)).astype(BF16)

    past_ref[...] = qkv_ref[:, C - SUBLANES:C, :]

    @pl.when(c == nc - 1)
    def _():
        sout_ref[...] = s_ref[...]


def _gdn(proj_gdn, bg, nb, t, conv_state, ssm0, conv_w, nw):
    C = min(CHUNK, t)
    G = 2 if nb % 2 == 0 else 1
    assert t % C == 0 and C & (C - 1) == 0 and C % INV_BLOCK == 0
    nc = t // C
    nlev = (C // INV_BLOCK).bit_length() - 1
    pg = proj_gdn.reshape(nb, t, proj_gdn.shape[1])
    pm = bg.reshape(nb, t, bg.shape[1])
    fixed = lambda b, c: (0, 0)
    cst = jnp.pad(conv_state, ((0, 0), (SUBLANES - (GDN_CONV - 1), 0), (0, 0)))
    vmem = G * (2 * (C * GDN_CONV_DIM * 4 + C * GDN_VAL_DIM * 4 + 2 * C * LANES * 4 + SUBLANES * GDN_CONV_DIM * 4
                     + C * GDN_VAL_DIM * 2 + 2 * GDN_HEADS * GDN_DK * GDN_DV * 4)
                + SUBLANES * GDN_CONV_DIM * 4 + GDN_HEADS * GDN_DK * GDN_DV * 4) + 16 * 2 ** 20
    o, s_new = pl.pallas_call(
        functools.partial(_gdn_kernel, G=G, C=C, nlev=nlev, nc=nc),
        grid=(nb // G, nc),
        in_specs=[pl.BlockSpec((G, C, GDN_CONV_DIM), lambda b, c: (b, c, COL_QKV // GDN_CONV_DIM)),
                  pl.BlockSpec((G, C, GDN_VAL_DIM), lambda b, c: (b, c, COL_Z // GDN_VAL_DIM)),
                  pl.BlockSpec((G, C, LANES), lambda b, c: (b, c, 0)),
                  pl.BlockSpec((G, C, LANES), lambda b, c: (b, c, 1)),
                  pl.BlockSpec((G, SUBLANES, GDN_CONV_DIM), lambda b, c: (b, 0, 0)),
                  pl.BlockSpec((GDN_CONV, GDN_CONV_DIM), fixed),
                  pl.BlockSpec((1, GDN_DV), fixed),
                  pl.BlockSpec((G, GDN_HEADS, GDN_DK, GDN_DV), lambda b, c: (b, 0, 0, 0))],
        out_specs=[pl.BlockSpec((G, C, GDN_VAL_DIM), lambda b, c: (b, c, 0)),
                   pl.BlockSpec((G, GDN_HEADS, GDN_DK, GDN_DV), lambda b, c: (b, 0, 0, 0))],
        out_shape=[jax.ShapeDtypeStruct((nb, t, GDN_VAL_DIM), BF16),
                   jax.ShapeDtypeStruct((nb, GDN_HEADS, GDN_DK, GDN_DV), F32)],
        scratch_shapes=[pltpu.VMEM((G, SUBLANES, GDN_CONV_DIM), F32),
                        pltpu.VMEM((G, GDN_HEADS, GDN_DK, GDN_DV), F32)],
        compiler_params=_params(("parallel", "arbitrary"), vmem),
        name="gdn",
    )(pg, pg, pm, pm, cst, conv_w, nw, ssm0)
    return o.reshape(nb * t, GDN_VAL_DIM), s_new


def _mla_prep_kernel(hm_ref, w_ref, inv_ref, wuk_ref, kvn_ref, alog_ref, dtb_ref,
                     ckv_o, kr_o, qcat_o, kcat_o, bg_o, *maybe_vt_o, S, tm, pos0):
    j = pl.program_id(1)
    rows = S * tm
    per_seq = lambda x: x.reshape(S, tm, x.shape[-1])
    proj = _dot(hm_ref[...], w_ref[...])
    a_all = proj[:, COL_A:COL_A + LANES] + dtb_ref[...]
    softplus = jnp.maximum(a_all, 0.0) + jnp.log(1.0 + jnp.exp(-jnp.abs(a_all)))
    bg_o[:, :LANES] = jax.nn.sigmoid(proj[:, COL_B:COL_B + LANES])
    bg_o[:, LANES:] = -jnp.exp(alog_ref[...]) * softplus
    token = jnp.bitwise_and(lax.broadcasted_iota(jnp.int32, (rows, LANES), 0), tm - 1)
    pos = (pos0 + j * tm + token).astype(F32)
    lane = lax.broadcasted_iota(jnp.int32, (rows, LANES), 1)
    ang = pos * inv_ref[...]
    cos = jnp.cos(ang)
    sin = jnp.sin(ang)
    half = MLA_ROPE // 2
    cs = jnp.where(lane < MLA_ROPE, cos, jnp.where(lane < MLA_ROPE + half, -sin, sin))
    keep = lane < MLA_ROPE

    def rope(pair):
        prod = pair * cs
        return jnp.where(keep, prod + pltpu.roll(prod, MLA_ROPE, axis=1), 0.0)

    ckv = _rms(proj[:, COL_CKV:COL_CKV + MLA_KV_RANK], kvn_ref[...])
    ckv_o[...] = per_seq(ckv)
    kr = rope(proj[:, COL_KR:COL_KR + LANES])
    kr_o[...] = per_seq(kr[:, :MLA_ROPE])
    kcat_o[:, :, :MLA_KV_RANK] = per_seq(ckv.astype(BF16))
    kcat_o[:, :, MLA_KV_RANK:] = per_seq(kr.astype(BF16))
    if maybe_vt_o:
        maybe_vt_o[0][0] = ckv.T.astype(BF16)
    for h in range(MLA_HEADS):
        qn = proj[:, COL_QNOPE + h * MLA_NOPE:COL_QNOPE + (h + 1) * MLA_NOPE].astype(BF16)
        qlat = _dot(qn, wuk_ref[h]) * MLA_SCALE
        qcat_o[:, h, :, :MLA_KV_RANK] = per_seq(qlat.astype(BF16))
        qr = rope(proj[:, COL_QROPE + h * LANES:COL_QROPE + (h + 1) * LANES]) * MLA_SCALE
        qcat_o[:, h, :, MLA_KV_RANK:] = per_seq(qr.astype(BF16))


def _mla_prep(hm, w_mla, nb, t, pos0, inv128, wuk_t, kvn, alog, dtb, emit_vt):
    d = hm.shape[1]
    tm = _tile(t, 512)
    nt = t // tm
    S = max(1, min(nb, 512 // tm)) if (nt == 1 and not emit_vt) else 1
    assert nb % S == 0 and tm & (tm - 1) == 0
    rt = S * tm
    rows = lambda b, j: b * nt + j
    fixed2 = lambda b, j: (0, 0)
    vmem = (d * N_MLA * 2 + 2 * (rt * d * 2 + MLA_HEADS * MLA_NOPE * MLA_KV_RANK * 2
                                 + rt * (MLA_KV_RANK + 3 * LANES) * 4 + (MLA_HEADS + 2) * rt * MLA_QK * 2)
            + 2 * rt * N_MLA * 4 + 8 * 2 ** 20)
    out_specs = [pl.BlockSpec((S, tm, MLA_KV_RANK), lambda b, j: (b, j, 0)),
                 pl.BlockSpec((S, tm, MLA_ROPE), lambda b, j: (b, j, 0)),
                 pl.BlockSpec((S, MLA_HEADS, tm, MLA_QK), lambda b, j: (b, 0, j, 0)),
                 pl.BlockSpec((S, tm, MLA_QK), lambda b, j: (b, j, 0)),
                 pl.BlockSpec((rt, 2 * LANES), lambda b, j: (rows(b, j), 0))]
    out_shape = [jax.ShapeDtypeStruct((nb, t, MLA_KV_RANK), F32),
                 jax.ShapeDtypeStruct((nb, t, MLA_ROPE), F32),
                 jax.ShapeDtypeStruct((nb, MLA_HEADS, t, MLA_QK), BF16),
                 jax.ShapeDtypeStruct((nb, t, MLA_QK), BF16),
                 jax.ShapeDtypeStruct((nb * t, 2 * LANES), F32)]
    if emit_vt:
        out_specs.append(pl.BlockSpec((1, MLA_KV_RANK, tm), lambda b, j: (b, 0, j)))
        out_shape.append(jax.ShapeDtypeStruct((nb, MLA_KV_RANK, t), BF16))
    return pl.pallas_call(
        functools.partial(_mla_prep_kernel, S=S, tm=tm, pos0=pos0),
        grid=(nb // S, nt),
        in_specs=[pl.BlockSpec((rt, d), lambda b, j: (rows(b, j), 0)),
                  pl.BlockSpec((d, N_MLA), fixed2, pipeline_mode=pl.Buffered(1)),
                  pl.BlockSpec((1, LANES), fixed2),
                  pl.BlockSpec((MLA_HEADS, MLA_NOPE, MLA_KV_RANK), lambda b, j: (0, 0, 0)),
                  pl.BlockSpec((1, MLA_KV_RANK), fixed2),
                  pl.BlockSpec((1, LANES), fixed2),
                  pl.BlockSpec((1, LANES), fixed2)],
        out_specs=out_specs,
        out_shape=out_shape,
        compiler_params=_params(("parallel", "parallel"), vmem),
        name="mla_proj_prep",
    )(hm, w_mla, inv128, wuk_t, kvn, alog, dtb)


ATTN_GROUPS = 2


def _attn_kernel(q_ref, k_ref, vt_ref, wuv_ref, o_ref, s0_ref, s1_ref, m_ref, l_ref, acc_ref,
                 *, tq, tk, n_valid):
    i = pl.program_id(1)
    cols = MLA_HEADS * tq
    cg = cols // ATTN_GROUPS
    q_first = i * tq
    c = lax.broadcasted_iota(jnp.int32, (1, cg), 1)
    qchunk = jnp.right_shift(q_first + jnp.bitwise_and(c, tq - 1), LOG2_CHUNK)
    k_all = jnp.minimum((q_first // CHUNK + 1) * CHUNK, n_valid)
    k_any = jnp.minimum(((q_first + tq - 1) // CHUNK + 1) * CHUNK, n_valid)
    nk = (k_any + tk - 1) // tk
    m_ref[...] = jnp.full(m_ref.shape, -jnp.inf, F32)
    l_ref[...] = jnp.zeros(l_ref.shape, F32)
    acc_ref[...] = jnp.zeros(acc_ref.shape, F32)

    def scores(j, buf):
        start = pl.multiple_of(j * tk, tk)
        q = q_ref[0].reshape(cols, MLA_QK)
        buf[...] = _dot_nt(k_ref[0, pl.ds(start, tk), :], q)

    def fold(x, op):
        while x.shape[0] > SUBLANES:
            half = x.shape[0] // 2
            x = op(x[:half], x[half:])
        return x

    def softmax_values(j, buf, masked):
        start = pl.multiple_of(j * tk, tk)
        vt = vt_ref[0, :, pl.ds(start, tk)]
        if masked:
            kpos = start + lax.broadcasted_iota(jnp.int32, (tk, 1), 0)
            mask = jnp.logical_and(jnp.right_shift(kpos, LOG2_CHUNK) <= qchunk, kpos < n_valid)
        for g in range(ATTN_GROUPS):
            cs = slice(g * cg, (g + 1) * cg)
            s = buf[:, cs]
            if masked:
                s = jnp.where(mask, s, -jnp.inf)
            m_old = m_ref[:, cs]
            m_new = jnp.maximum(m_old, jnp.max(fold(s, jnp.maximum), axis=0, keepdims=True))
            alpha = jnp.exp(m_old - m_new)
            p = jnp.exp(s - m_new)
            l_ref[:, cs] = alpha * l_ref[:, cs] + jnp.sum(fold(p, jnp.add), axis=0, keepdims=True)
            acc_ref[:, cs] = alpha * acc_ref[:, cs] + _dot(vt, p.astype(BF16))
            m_ref[:, cs] = m_new

    def by_parity(j, fn):
        @pl.when(jnp.bitwise_and(j, 1) == 0)
        def _():
            fn(s0_ref, s1_ref)

        @pl.when(jnp.bitwise_and(j, 1) == 1)
        def _():
            fn(s1_ref, s0_ref)

    scores(0, s0_ref)

    def body(j, carry, masked):
        def step(cur, nxt):
            scores(j + 1, nxt)
            softmax_values(j, cur, masked)
        by_parity(j, step)
        return carry

    n_open = jnp.minimum(k_all // tk, nk - 1)
    lax.fori_loop(0, n_open, functools.partial(body, masked=False), 0)
    lax.fori_loop(n_open, nk - 1, functools.partial(body, masked=True), 0)
    by_parity(nk - 1, lambda cur, nxt: softmax_values(nk - 1, cur, True))
    inv_l = 1.0 / l_ref[...]
    for h in range(MLA_HEADS):
        hs = slice(h * tq, (h + 1) * tq)
        lat_t = (acc_ref[:, hs] * inv_l[:, hs]).astype(BF16)
        o_ref[:, h * MLA_V:(h + 1) * MLA_V] = _dot_tn(lat_t, wuv_ref[h]).astype(BF16)


def _attn_cache_kernel(q_ref, ckv_ref, kr_ref, knew_ref, wuv_ref, o_ref, m_ref, l_ref, acc_ref,
                       *, tq, tk, nkb, past):
    j = pl.program_id(1)
    rows = MLA_HEADS * tq
    q = q_ref[0].reshape(rows, MLA_QK)
    r = lax.broadcasted_iota(jnp.int32, (rows, 1), 0)
    qchunk = jnp.right_shift(past + jnp.bitwise_and(r, tq - 1), LOG2_CHUNK)

    @pl.when(j == 0)
    def _():
        m_ref[...] = jnp.full(m_ref.shape, -jnp.inf, F32)
        l_ref[...] = jnp.zeros(l_ref.shape, F32)
        acc_ref[...] = jnp.zeros(acc_ref.shape, F32)

    def fold(x, op):
        while x.shape[1] > LANES:
            half = x.shape[1] // 2
            x = op(x[:, :half], x[:, half:])
        return x

    def update(s, kpos, vals):
        s = jnp.where(jnp.right_shift(kpos, LOG2_CHUNK) <= qchunk, s, -jnp.inf)
        m_old = m_ref[...]
        m_new = jnp.maximum(m_old, jnp.max(fold(s, jnp.maximum), axis=-1, keepdims=True))
        alpha = jnp.exp(m_old - m_new)
        p = jnp.exp(s - m_new)
        l_ref[...] = alpha * l_ref[...] + jnp.sum(fold(p, jnp.add), axis=-1, keepdims=True)
        acc_ref[...] = alpha * acc_ref[...] + _dot(p.astype(BF16), vals)
        m_ref[...] = m_new

    ck = ckv_ref[0].astype(BF16)
    s = (_dot_nt(q[:, :MLA_KV_RANK], ck)
         + _dot_nt(q[:, MLA_KV_RANK:MLA_KV_RANK + MLA_ROPE], kr_ref[0].astype(BF16)))
    update(s, j * tk + lax.broadcasted_iota(jnp.int32, (1, tk), 1), ck)

    @pl.when(j == nkb - 1)
    def _():
        kn = knew_ref[0]
        update(_dot_nt(q, kn), past + lax.broadcasted_iota(jnp.int32, (1, tq), 1), kn[:, :MLA_KV_RANK])
        inv_l = 1.0 / l_ref[...]
        for h in range(MLA_HEADS):
            hs = slice(h * tq, (h + 1) * tq)
            lat = (acc_ref[hs, :] * inv_l[hs, :]).astype(BF16)
            o_ref[:, h * MLA_V:(h + 1) * MLA_V] = _dot(lat, wuv_ref[h]).astype(BF16)


def _attention(qcat, kcat, vt, wuv_t):
    nb, _, t, _ = qcat.shape
    tq = _tile(t, 128)
    tk = _tile(t, 512)
    assert tq & (tq - 1) == 0
    nq = t // tq
    cols = MLA_HEADS * tq
    vmem = (2 * (cols * MLA_QK * 2 + t * MLA_QK * 2 + MLA_KV_RANK * t * 2
                 + MLA_HEADS * MLA_KV_RANK * MLA_V * 2 + tq * MLA_HEADS * MLA_V * 2)
            + 2 * tk * cols * 4 + 2 * SUBLANES * cols * 4 + MLA_KV_RANK * cols * 4
            + 3 * tk * cols * 4 // ATTN_GROUPS + 4 * 2 ** 20)
    return pl.pallas_call(
        functools.partial(_attn_kernel, tq=tq, tk=tk, n_valid=t),
        grid=(nb, nq),
        in_specs=[pl.BlockSpec((1, MLA_HEADS, tq, MLA_QK), lambda b, i: (b, 0, i, 0)),
                  pl.BlockSpec((1, t, MLA_QK), lambda b, i: (b, 0, 0)),
                  pl.BlockSpec((1, MLA_KV_RANK, t), lambda b, i: (b, 0, 0)),
                  pl.BlockSpec((MLA_HEADS, MLA_KV_RANK, MLA_V), lambda b, i: (0, 0, 0))],
        out_specs=pl.BlockSpec((tq, MLA_HEADS * MLA_V), lambda b, i: (b * nq + i, 0)),
        out_shape=jax.ShapeDtypeStruct((nb * t, MLA_HEADS * MLA_V), BF16),
        scratch_shapes=[pltpu.VMEM((tk, cols), F32), pltpu.VMEM((tk, cols), F32), pltpu.VMEM((1, cols), F32),
                        pltpu.VMEM((1, cols), F32), pltpu.VMEM((MLA_KV_RANK, cols), F32)],
        compiler_params=_params(("parallel", "arbitrary"), vmem),
        name="mla_attention",
    )(qcat, kcat, vt, wuv_t)


def _attention_cached(qcat, ckv_past, krope_past, kcat_new, wuv_t):
    nb, _, t, _ = qcat.shape
    past = ckv_past.shape[1]
    tk = _tile(past, 4096)
    nkb = past // tk
    assert t & (t - 1) == 0
    rows = MLA_HEADS * t
    vmem = (2 * (rows * MLA_QK * 2 + tk * MLA_KV_RANK * 4 + tk * LANES * 4 + t * MLA_QK * 2
                 + MLA_HEADS * MLA_KV_RANK * MLA_V * 2 + t * MLA_HEADS * MLA_V * 2)
            + 2 * rows * LANES * 4 + rows * MLA_KV_RANK * 4 + tk * MLA_QK * 2 + 4 * rows * tk * 4 + 4 * 2 ** 20)
    return pl.pallas_call(
        functools.partial(_attn_cache_kernel, tq=t, tk=tk, nkb=nkb, past=past),
        grid=(nb, nkb),
        in_specs=[pl.BlockSpec((1, MLA_HEADS, t, MLA_QK), lambda b, j: (b, 0, 0, 0)),
                  pl.BlockSpec((1, tk, MLA_KV_RANK), lambda b, j: (b, j, 0)),
                  pl.BlockSpec((1, tk, MLA_ROPE), lambda b, j: (b, j, 0)),
                  pl.BlockSpec((1, t, MLA_QK), lambda b, j: (b, 0, 0)),
                  pl.BlockSpec((MLA_HEADS, MLA_KV_RANK, MLA_V), lambda b, j: (0, 0, 0))],
        out_specs=pl.BlockSpec((t, MLA_HEADS * MLA_V), lambda b, j: (b, 0)),
        out_shape=jax.ShapeDtypeStruct((nb * t, MLA_HEADS * MLA_V), BF16),
        scratch_shapes=[pltpu.VMEM((rows, 1), F32), pltpu.VMEM((rows, 1), F32),
                        pltpu.VMEM((rows, MLA_KV_RANK), F32)],
        compiler_params=_params(("parallel", "arbitrary"), vmem),
        name="mla_attention_cached",
    )(qcat, ckv_past, krope_past, kcat_new, wuv_t)


def _merge_out_kernel(og_ref, om_ref, wg_ref, wm_ref, gg_ref, gm_ref, wo_ref, x_ref, gpost_ref, gnext_ref,
                      xo_ref, xno_ref):
    tg = _dot(og_ref[...], wg_ref[...])
    tm_ = _dot(om_ref[...], wm_ref[...])
    merged = (jax.nn.sigmoid(gg_ref[...].astype(F32)) * tg
              + jax.nn.sigmoid(gm_ref[...].astype(F32)) * tm_).astype(BF16)
    y = _dot(merged, wo_ref[...])
    xnew = x_ref[...] + _rms(y, gpost_ref[...])
    xo_ref[...] = xnew
    xno_ref[...] = _rms(xnew, gnext_ref[...]).astype(BF16)


def _merge_out(og, om, wbg, wbm, proj_gate, wout, x, gpost, gnext):
    m, kg = og.shape
    d = wbg.shape[1]
    tm = _tile(m, 256)
    gsz = proj_gate.dtype.itemsize
    row = lambda i: (i, 0)
    fixed = lambda i: (0, 0)
    resident = dict(pipeline_mode=pl.Buffered(1))
    vmem = ((2 * kg * d + d * d) * 2
            + 2 * (2 * tm * kg * 2 + 2 * tm * d * gsz + 2 * tm * d * 4 + tm * d * 2)
            + 4 * tm * d * 4 + 2 * 2 ** 20)
    return pl.pallas_call(
        _merge_out_kernel,
        grid=(m // tm,),
        in_specs=[pl.BlockSpec((tm, kg), row),
                  pl.BlockSpec((tm, kg), row),
                  pl.BlockSpec((kg, d), fixed, **resident),
                  pl.BlockSpec((kg, d), fixed, **resident),
                  pl.BlockSpec((tm, d), lambda i: (i, 0)),
                  pl.BlockSpec((tm, d), lambda i: (i, 1)),
                  pl.BlockSpec((d, d), fixed, **resident),
                  pl.BlockSpec((tm, d), row),
                  pl.BlockSpec((1, d), fixed),
                  pl.BlockSpec((1, d), fixed)],
        out_specs=[pl.BlockSpec((tm, d), row), pl.BlockSpec((tm, d), row)],
        out_shape=[jax.ShapeDtypeStruct((m, d), F32), jax.ShapeDtypeStruct((m, d), BF16)],
        compiler_params=_params(("parallel",), vmem),
        name="merge_out",
    )(og, om, wbg, wbm, proj_gate, proj_gate, wout, x, gpost, gnext)


def _layer(x3, pos0, conv_state, ssm0, ckv_past, krope_past, w):
    nb, t, d = x3.shape
    m = nb * t
    x = x3.reshape(m, d)
    x1, hm, ffn1_w = _ffn(x, *w["ffn1_w"], w["ffn1_norm_post"],
                          gpre=w["ffn1_norm_pre"], gnext=w["mix_norm_pre"])
    proj_gdn = _matmul(hm, w["w_in_gdn"], F32)
    proj_gate = _matmul(hm, w["w_in_gate"], BF16)
    ckv, krope, qcat, kcat, bg, *vt = _mla_prep(hm, w["w_in_mla"], nb, t, pos0, w["inv128"], w["wuk_t"],
                                                w["kv_norm"], w["alog"], w["dtb"], ckv_past is None)

    o_gdn, ssm_new = _gdn(proj_gdn, bg, nb, t, conv_state, ssm0, w["conv_w"], w["gdn_nw"])
    conv_new = proj_gdn.reshape(nb, t, -1)[:, t - (GDN_CONV - 1):, COL_QKV:COL_QKV + GDN_CONV_DIM]

    if ckv_past is None:
        o_mla = _attention(qcat, kcat, vt[0], w["wuv_t"])
    else:
        o_mla = _attention_cached(qcat, ckv_past, krope_past, kcat, w["wuv_t"])

    x2, xn2 = _merge_out(o_gdn, o_mla, w["w_br_gdn"], w["w_br_mla"], proj_gate, w["w_out"], x1,
                         w["mix_norm_post"], w["ffn2_norm_pre"])
    x3_, _, ffn2_w = _ffn(x2, *w["ffn2_w"], w["ffn2_norm_post"], xn=xn2)
    return (x3_.reshape(nb, t, d), conv_new, ssm_new, ckv, krope), dict(ffn1_w=ffn1_w, ffn2_w=ffn2_w)


def _prep_weights(l, d_model, **p):
    row = lambda v: v[l].reshape(1, -1).astype(F32)
    pad_lanes = lambda v: jnp.pad(v[l].reshape(1, -1).astype(F32), ((0, 0), (0, LANES - v.shape[1])))
    inv = ROPE_THETA ** (-jnp.arange(0, MLA_ROPE, 2, dtype=F32) / MLA_ROPE)
    w_in_gdn, w_in_gate, w_in_mla = _build_w_in(p["w_in"][l], d_model)
    w = dict(
        w_in_gdn=w_in_gdn, w_in_gate=w_in_gate, w_in_mla=w_in_mla,
        ffn1_norm_pre=row(p["ffn1_norm_pre"]), ffn1_norm_post=row(p["ffn1_norm_post"]),
        mix_norm_pre=row(p["mix_norm_pre"]), mix_norm_post=row(p["mix_norm_post"]),
        ffn2_norm_pre=row(p["ffn2_norm_pre"]), ffn2_norm_post=row(p["ffn2_norm_post"]),
        ffn1_w=(p["ffn1_w_gate"][l], p["ffn1_w_up"][l], p["ffn1_w_down"][l]),
        ffn2_w=(p["ffn2_w_gate"][l], p["ffn2_w_up"][l], p["ffn2_w_down"][l]),
        conv_w=p["gdn_conv_w"][l].astype(F32),
        alog=pad_lanes(p["gdn_a_log"]), dtb=pad_lanes(p["gdn_dt_bias"]),
        gdn_nw=row(p["gdn_norm_w"]), kv_norm=row(p["mla_kv_norm"]),
        inv128=jnp.tile(inv, LANES // inv.shape[0]).reshape(1, LANES),
        wuk_t=jnp.transpose(p["mla_w_uk"][l], (1, 2, 0)).astype(BF16),
        wuv_t=jnp.transpose(p["mla_w_uv"][l], (1, 0, 2)).astype(BF16),
        w_br_gdn=p["w_br_gdn"][l].astype(BF16), w_br_mla=p["w_br_mla"][l].astype(BF16),
        w_out=p["w_out"][l].astype(BF16),
    )
    return w


def kernel(x_prompt, x_sample, state_gdn_conv, state_gdn_ssm, cache_mla_ckv, cache_mla_krope, ffn1_norm_pre, ffn1_w_gate, ffn1_w_up, ffn1_w_down, ffn1_norm_post, mix_norm_pre, w_in, gdn_conv_w, gdn_a_log, gdn_dt_bias, gdn_norm_w, mla_kv_norm, mla_w_uk, mla_w_uv, w_br_gdn, w_br_mla, w_out, mix_norm_post, ffn2_norm_pre, ffn2_w_gate, ffn2_w_up, ffn2_w_down, ffn2_norm_post):
    params = dict(
        ffn1_norm_pre=ffn1_norm_pre, ffn1_w_gate=ffn1_w_gate, ffn1_w_up=ffn1_w_up,
        ffn1_w_down=ffn1_w_down, ffn1_norm_post=ffn1_norm_post, mix_norm_pre=mix_norm_pre,
        w_in=w_in, gdn_conv_w=gdn_conv_w, gdn_a_log=gdn_a_log, gdn_dt_bias=gdn_dt_bias,
        gdn_norm_w=gdn_norm_w, mla_kv_norm=mla_kv_norm, mla_w_uk=mla_w_uk, mla_w_uv=mla_w_uv,
        w_br_gdn=w_br_gdn, w_br_mla=w_br_mla, w_out=w_out, mix_norm_post=mix_norm_post,
        ffn2_norm_pre=ffn2_norm_pre, ffn2_w_gate=ffn2_w_gate, ffn2_w_up=ffn2_w_up,
        ffn2_w_down=ffn2_w_down, ffn2_norm_post=ffn2_norm_post)
    depth = w_in.shape[0]
    d_model = x_prompt.shape[-1]
    b_p = x_prompt.shape[0]
    past = cache_mla_ckv.shape[2]
    yp, ys = x_prompt, x_sample
    outs_p, outs_s = [], []
    for l in range(depth):
        w = _prep_weights(l, d_model, **params)
        (ys, *rest_s), w_bf16 = _layer(
            ys, past, state_gdn_conv[l], state_gdn_ssm[l], cache_mla_ckv[l], cache_mla_krope[l], w)
        (yp, *rest_p), _ = _layer(
            yp, 0, jnp.zeros((b_p, GDN_CONV - 1, GDN_CONV_DIM), F32),
            jnp.zeros((b_p, GDN_HEADS, GDN_DK, GDN_DV), F32), None, None, {**w, **w_bf16})
        outs_p.append(rest_p)
        outs_s.append(rest_s)
    stack = lambda outs, i: jnp.stack([o[i] for o in outs])
    return (yp, ys,
            stack(outs_p, 0), stack(outs_p, 1), stack(outs_p, 2), stack(outs_p, 3),
            stack(outs_s, 0), stack(outs_s, 1), stack(outs_s, 2), stack(outs_s, 3))
```

```python
import functools

import jax
import jax.numpy as jnp
from jax import lax
from jax.experimental import pallas as pl
from jax.experimental.pallas import tpu as pltpu

F32 = jnp.float32
BF16 = jnp.bfloat16

CHUNK = 64
LOG2_CHUNK = 6
NORM_EPS = 1e-6
GDN_HEADS = 8
GDN_DK = 128
GDN_DV = 128
GDN_CONV = 4
GDN_KEY_DIM = GDN_HEADS * GDN_DK
GDN_VAL_DIM = GDN_HEADS * GDN_DV
GDN_CONV_DIM = 2 * GDN_KEY_DIM + GDN_VAL_DIM
MLA_HEADS = 8
MLA_NOPE = 128
MLA_ROPE = 64
MLA_V = 128
MLA_KV_RANK = 512
MLA_SCALE = (MLA_NOPE + MLA_ROPE) ** -0.5
ROPE_THETA = 10000.0

LANES = 128
SUBLANES = 8
VMEM_CAP_BYTES = 56 * 2 ** 20
MLA_QK = MLA_KV_RANK + LANES
HI = lax.Precision.HIGHEST


def _params(semantics, vmem_bytes):
    limit = int(min(max(vmem_bytes, 16 * 2 ** 20), VMEM_CAP_BYTES))
    return pltpu.CompilerParams(dimension_semantics=semantics, vmem_limit_bytes=limit)


def _dot(a, b, prec=None):
    return lax.dot_general(a, b, (((1,), (0,)), ((), ())), precision=prec,
                           preferred_element_type=F32)


def _dot_nt(a, b, prec=None):
    return lax.dot_general(a, b, (((1,), (1,)), ((), ())), precision=prec,
                           preferred_element_type=F32)


def _dot_tn(a, b, prec=None):
    return lax.dot_general(a, b, (((0,), (0,)), ((), ())), precision=prec,
                           preferred_element_type=F32)


def _rms(y, g):
    return y * lax.rsqrt(jnp.mean(y * y, axis=-1, keepdims=True) + NORM_EPS) * g


def _silu(x):
    return x * jax.nn.sigmoid(x)


def _tile(n, pref):
    t = min(n, pref)
    assert n % t == 0, (n, pref)
    return t


def _ffn_kernel(*refs, nf, norm_in, emit_next, cast_w):
    refs = list(refs)
    xn_ref = None if norm_in else refs.pop(0)
    x_ref, wg_ref, wu_ref, wd_ref = refs[:4]
    del refs[:4]
    gpre_ref = refs.pop(0) if norm_in else None
    gpost_ref = refs.pop(0)
    gnext_ref = refs.pop(0) if emit_next else None
    xo_ref = refs.pop(0)
    xno_ref = refs.pop(0) if emit_next else None
    if cast_w:
        wgo_ref, wuo_ref, wdo_ref = refs[:3]
        del refs[:3]
    acc_ref = refs.pop(0)
    if norm_in:
        xn_ref = refs.pop(0)
    f = pl.program_id(1)

    @pl.when(f == 0)
    def _():
        acc_ref[...] = jnp.zeros(acc_ref.shape, F32)
        if norm_in:
            xn_ref[...] = _rms(x_ref[...], gpre_ref[...]).astype(BF16)

    xn = xn_ref[...]
    wg, wu, wd = wg_ref[...], wu_ref[...], wd_ref[...]
    if cast_w:
        wg, wu, wd = wg.astype(BF16), wu.astype(BF16), wd.astype(BF16)
        wgo_ref[...] = wg
        wuo_ref[...] = wu
        wdo_ref[...] = wd
    gate = _dot(xn, wg)
    up = _dot(xn, wu)
    h = (_silu(gate) * up).astype(BF16)
    acc_ref[...] += _dot(h, wd)

    @pl.when(f == nf - 1)
    def _():
        xnew = x_ref[...] + 0.5 * _rms(acc_ref[...], gpost_ref[...])
        xo_ref[...] = xnew
        if emit_next:
            xno_ref[...] = _rms(xnew, gnext_ref[...]).astype(BF16)


def _ffn(x, wg, wu, wd, gpost, xn=None, gpre=None, gnext=None):
    norm_in = xn is None
    emit_next = gnext is not None
    cast_w = wg.dtype == F32
    assert norm_in == (gpre is not None)
    m, d = x.shape
    dff = wg.shape[1]
    tm = _tile(m, 512)
    tf = _tile(dff, 256 if cast_w else 512)
    nf = dff // tf
    assert not cast_w or m == tm
    wsz = wg.dtype.itemsize
    row = lambda i, f: (i, 0)
    vec = pl.BlockSpec((1, d), lambda i, f: (0, 0))
    w_specs = [pl.BlockSpec((d, tf), lambda i, f: (0, f)),
               pl.BlockSpec((d, tf), lambda i, f: (0, f)),
               pl.BlockSpec((tf, d), lambda i, f: (f, 0))]
    ins = [] if norm_in else [xn]
    in_specs = [] if norm_in else [pl.BlockSpec((tm, d), row)]
    ins += [x, wg, wu, wd]
    in_specs += [pl.BlockSpec((tm, d), row)] + w_specs
    for g in (gpre, gpost, gnext):
        if g is not None:
            ins.append(g)
            in_specs.append(vec)
    out_specs = [pl.BlockSpec((tm, d), row)]
    out_shape = [jax.ShapeDtypeStruct((m, d), F32)]
    scratch = [pltpu.VMEM((tm, d), F32)]
    if emit_next:
        out_specs.append(pl.BlockSpec((tm, d), row))
        out_shape.append(jax.ShapeDtypeStruct((m, d), BF16))
    if cast_w:
        out_specs += w_specs
        out_shape += [jax.ShapeDtypeStruct(w.shape, BF16) for w in (wg, wu, wd)]
    if norm_in:
        scratch.append(pltpu.VMEM((tm, d), BF16))
    vmem = (2 * (tm * d * 2 + tm * d * 4 + 3 * d * tf * wsz + tm * d * 4 + tm * d * 2)
            + (2 * 3 * d * tf * 2 + 3 * d * tf * 2 if cast_w else 0)
            + tm * d * 4 + 3 * tm * tf * 4 + 2 * tm * d * 4)
    res = pl.pallas_call(
        functools.partial(_ffn_kernel, nf=nf, norm_in=norm_in, emit_next=emit_next, cast_w=cast_w),
        grid=(m // tm, nf),
        in_specs=in_specs,
        out_specs=out_specs,
        out_shape=out_shape,
        scratch_shapes=scratch,
        compiler_params=_params(("parallel", "arbitrary"), vmem),
        name="ffn_cast" if cast_w else "ffn",
    )(*ins)
    res = list(res)
    x_new = res.pop(0)
    xn_next = res.pop(0) if emit_next else None
    return x_new, xn_next, (tuple(res) if cast_w else (wg, wu, wd))


def _mm_kernel(a_ref, w_ref, o_ref):
    o_ref[...] = _dot(a_ref[...], w_ref[...]).astype(o_ref.dtype)


def _matmul(a, w, out_dtype):
    m, k = a.shape
    n = w.shape[1]
    tm = _tile(m, 1024)
    tn = _tile(n, 2048)
    osz = jnp.dtype(out_dtype).itemsize
    vmem = 2 * (tm * k * 2 + k * tn * 2 + tm * tn * osz) + tm * tn * 4
    return pl.pallas_call(
        _mm_kernel,
        grid=(m // tm, n // tn),
        in_specs=[pl.BlockSpec((tm, k), lambda i, j: (i, 0)),
                  pl.BlockSpec((k, tn), lambda i, j: (0, j))],
        out_specs=pl.BlockSpec((tm, tn), lambda i, j: (i, j)),
        out_shape=jax.ShapeDtypeStruct((m, n), out_dtype),
        compiler_params=_params(("parallel", "arbitrary"), vmem),
        name="in_proj",
    )(a, w)


COL_QKV = 0
COL_Z = COL_QKV + GDN_CONV_DIM
COL_QNOPE = 0
COL_QROPE = COL_QNOPE + MLA_HEADS * MLA_NOPE
COL_CKV = COL_QROPE + MLA_HEADS * LANES
COL_KR = COL_CKV + MLA_KV_RANK
COL_B = COL_KR + LANES
COL_A = COL_B + LANES
N_MLA = 3072


def _build_w_in(w_in, d_model):
    sizes = (GDN_CONV_DIM, GDN_VAL_DIM, GDN_HEADS, GDN_HEADS, MLA_HEADS * (MLA_NOPE + MLA_ROPE),
             MLA_KV_RANK, MLA_ROPE, d_model, d_model)
    offs = [0]
    for s in sizes:
        offs.append(offs[-1] + s)
    w_gdn = w_in[:, :offs[2]].astype(BF16)
    w_gate = w_in[:, offs[7]:].astype(BF16)
    part = lambda i: w_in[:, offs[i]:offs[i + 1]].astype(BF16)
    b, a, qm, ckv, kr = (part(i) for i in range(2, 7))
    d = w_in.shape[0]
    half = MLA_ROPE // 2
    qm = qm.reshape(d, MLA_HEADS, MLA_NOPE + MLA_ROPE)
    qn = qm[:, :, :MLA_NOPE].reshape(d, MLA_HEADS * MLA_NOPE)
    qr = qm[:, :, MLA_NOPE:]
    qr_pair = jnp.concatenate([qr, qr[:, :, half:], qr[:, :, :half]], axis=2).reshape(d, MLA_HEADS * LANES)
    kr_pair = jnp.concatenate([kr, kr[:, half:], kr[:, :half]], axis=1)
    zeros = lambda n: jnp.zeros((d, n), BF16)
    w_mla = jnp.concatenate([qn, qr_pair, ckv, kr_pair, b, zeros(LANES - GDN_HEADS), a, zeros(LANES - GDN_HEADS),
                             zeros(N_MLA - COL_A - LANES)], axis=1)
    return w_gdn, w_gate, w_mla


_NN = (((1,), (0,)), ((), ()))
_NT = (((1,), (1,)), ((), ()))
_TN = (((0,), (0,)), ((), ()))
INV_BLOCK = 16


def _mm(a, b, dims=_NN):
    return lax.dot_general(a.astype(BF16), b.astype(BF16), dims, preferred_element_type=F32)


def _gdn_kernel(qkv_ref, z_ref, beta_ref, g_ref, cst_ref, cw_ref, nw_ref, s0_ref,
                o_ref, sout_ref, past_ref, s_ref, *, G, C, nlev, nc):
    c = pl.program_id(1)

    @pl.when(c == 0)
    def _():
        past_ref[...] = cst_ref[...]
        s_ref[...] = s0_ref[...]

    ri = lax.broadcasted_iota(jnp.int32, (C, C), 0)
    ci = lax.broadcasted_iota(jnp.int32, (C, C), 1)
    incl = ri >= ci
    strict = ri > ci
    eye = (ri == ci).astype(F32)
    log2_blk = INV_BLOCK.bit_length() - 1
    bdiag = jnp.right_shift(ri, log2_blk) == jnp.right_shift(ci, log2_blk)

    first_row = lax.broadcasted_iota(jnp.int32, (C, 1), 0) == 0

    def conv(s, col):
        cols = slice(col, col + LANES)
        x = qkv_ref[s, :, cols]
        past = past_ref[s, :, cols]
        w = [cw_ref[j:j + 1, cols] for j in range(GDN_CONV)]
        acc = w[0] * x
        for j in range(1, GDN_CONV):
            top = sum(w[i] * past[SUBLANES - 1 - (j - 1 - i):SUBLANES - (j - 1 - i)] for i in range(j))
            acc = w[j] * x + jnp.where(first_row, top, pltpu.roll(acc, 1, axis=0))
        return _silu(acc)

    units = [(s, h) for s in range(G) for h in range(GDN_HEADS)]
    every = lambda fn, *lists: [fn(*args) for args in zip(*lists)]
    mm_inv = _mm
    beta_all, gcum, gcum_t = [], [], []
    for s in range(G):
        beta_all.append(beta_ref[s])
        g_all = g_ref[s]
        gcum.append(_dot(incl.astype(F32), g_all, HI))
        gcum_t.append(_dot_tn(g_all, (ci >= ri).astype(F32), HI))

    def l2n(x):
        return x * lax.rsqrt(jnp.sum(x * x, axis=-1, keepdims=True) + 1e-6)

    q = [l2n(conv(s, h * GDN_DK)) * (GDN_DK ** -0.5) for s, h in units]
    k = [l2n(conv(s, GDN_KEY_DIM + h * GDN_DK)) for s, h in units]
    v = [conv(s, 2 * GDN_KEY_DIM + h * GDN_DV) for s, h in units]
    bcol = [beta_all[s][:, h:h + 1] for s, h in units]
    gcol = [gcum[s][:, h:h + 1] for s, h in units]
    glast = [gcum[s][C - 1:C, h:h + 1] for s, h in units]
    decay = [jnp.where(incl, jnp.exp(jnp.where(incl, gcum[s][:, h:h + 1] - gcum_t[s][h:h + 1, :], 0.0)), 0.0)
             for s, h in units]
    qkk = every(lambda q_, k_: _mm(jnp.concatenate([q_, k_], axis=0), k_, _NT), q, k)
    qk = every(lambda x, d: x[:C] * d, qkk, decay)
    nmat = every(lambda x, b_, d: -jnp.where(strict, b_ * x[C:] * d, 0.0), qkk, bcol, decay)
    ndiag = every(lambda n_: jnp.where(bdiag, n_, 0.0), nmat)
    tinv = every(lambda n_: eye + n_, ndiag)
    npow = ndiag
    for _ in range(INV_BLOCK.bit_length() - 2):
        npow = every(mm_inv, npow, npow)
        tinv = every(lambda t_, p_: t_ + mm_inv(t_, p_), tinv, npow)
    mpow = every(lambda t_, n_, d_: mm_inv(t_, n_ - d_), tinv, nmat, ndiag)
    for lev in range(nlev):
        if lev:
            mpow = every(mm_inv, mpow, mpow)
        tinv = every(lambda t_, m_: t_ + mm_inv(m_, t_), tinv, mpow)
    egc = every(jnp.exp, gcol)
    sol = every(lambda t_, v_, k_, b_, e_: _mm(t_, jnp.concatenate([v_ * b_, k_ * (b_ * e_)], axis=1)),
                tinv, v, k, bcol, egc)
    s_old = [s_ref[s, h] for s, h in units]
    ws_qs = every(lambda x, q_, e_, s_: _mm(jnp.concatenate([x[:, GDN_DV:], q_ * e_], axis=0), s_),
                  sol, q, egc, s_old)
    v_new = every(lambda x, y: x[:, :GDN_DV] - y[:C], sol, ws_qs)
    o = every(lambda y, a_, vn: y[C:] + _mm(a_, vn), ws_qs, qk, v_new)
    s_new = every(lambda s_, gl, k_, gc, vn: s_ * jnp.exp(gl) + _mm(k_ * jnp.exp(gl - gc), vn, _TN),
                  s_old, glast, k, gcol, v_new)
    for (s, h), sn, o_ in zip(units, s_new, o):
        s_ref[s, h] = sn
        zz = z_ref[s, :, h * GDN_DV:(h + 1) * GDN_DV]
        o_ref[s, :, h * GDN_DV:(h + 1) * GDN_DV] = (_rms(o_, nw_ref[...]) * _silu(zz)).astype(BF16)

    past_ref[...] = qkv_ref[:, C - SUBLANES:C, :]

    @pl.when(c == nc - 1)
    def _():
        sout_ref[...] = s_ref[...]


def _gdn(proj_gdn, bg, nb, t, conv_state, ssm0, conv_w, nw):
    C = min(CHUNK, t)
    G = 2 if nb % 2 == 0 else 1
    assert t % C == 0 and C & (C - 1) == 0 and C % INV_BLOCK == 0
    nc = t // C
    nlev = (C // INV_BLOCK).bit_length() - 1
    pg = proj_gdn.reshape(nb, t, proj_gdn.shape[1])
    pm = bg.reshape(nb, t, bg.shape[1])
    fixed = lambda b, c: (0, 0)
    cst = jnp.pad(conv_state, ((0, 0), (SUBLANES - (GDN_CONV - 1), 0), (0, 0)))
    vmem = G * (2 * (C * GDN_CONV_DIM * 4 + C * GDN_VAL_DIM * 4 + 2 * C * LANES * 4 + SUBLANES * GDN_CONV_DIM * 4
                     + C * GDN_VAL_DIM * 2 + 2 * GDN_HEADS * GDN_DK * GDN_DV * 4)
                + SUBLANES * GDN_CONV_DIM * 4 + GDN_HEADS * GDN_DK * GDN_DV * 4) + 16 * 2 ** 20
    o, s_new = pl.pallas_call(
        functools.partial(_gdn_kernel, G=G, C=C, nlev=nlev, nc=nc),
        grid=(nb // G, nc),
        in_specs=[pl.BlockSpec((G, C, GDN_CONV_DIM), lambda b, c: (b, c, COL_QKV // GDN_CONV_DIM)),
                  pl.BlockSpec((G, C, GDN_VAL_DIM), lambda b, c: (b, c, COL_Z // GDN_VAL_DIM)),
                  pl.BlockSpec((G, C, LANES), lambda b, c: (b, c, 0)),
                  pl.BlockSpec((G, C, LANES), lambda b, c: (b, c, 1)),
                  pl.BlockSpec((G, SUBLANES, GDN_CONV_DIM), lambda b, c: (b, 0, 0)),
                  pl.BlockSpec((GDN_CONV, GDN_CONV_DIM), fixed),
                  pl.BlockSpec((1, GDN_DV), fixed),
                  pl.BlockSpec((G, GDN_HEADS, GDN_DK, GDN_DV), lambda b, c: (b, 0, 0, 0))],
        out_specs=[pl.BlockSpec((G, C, GDN_VAL_DIM), lambda b, c: (b, c, 0)),
                   pl.BlockSpec((G, GDN_HEADS, GDN_DK, GDN_DV), lambda b, c: (b, 0, 0, 0))],
        out_shape=[jax.ShapeDtypeStruct((nb, t, GDN_VAL_DIM), BF16),
                   jax.ShapeDtypeStruct((nb, GDN_HEADS, GDN_DK, GDN_DV), F32)],
        scratch_shapes=[pltpu.VMEM((G, SUBLANES, GDN_CONV_DIM), F32),
                        pltpu.VMEM((G, GDN_HEADS, GDN_DK, GDN_DV), F32)],
        compiler_params=_params(("parallel", "arbitrary"), vmem),
        name="gdn",
    )(pg, pg, pm, pm, cst, conv_w, nw, ssm0)
    return o.reshape(nb * t, GDN_VAL_DIM), s_new


def _mla_prep_kernel(hm_ref, w_ref, inv_ref, wuk_ref, kvn_ref, alog_ref, dtb_ref,
                     ckv_o, kr_o, qcat_o, kcat_o, bg_o, *maybe_vt_o, S, tm, pos0):
    j = pl.program_id(1)
    rows = S * tm
    per_seq = lambda x: x.reshape(S, tm, x.shape[-1])
    proj = _dot(hm_ref[...], w_ref[...])
    a_all = proj[:, COL_A:COL_A + LANES] + dtb_ref[...]
    softplus = jnp.maximum(a_all, 0.0) + jnp.log(1.0 + jnp.exp(-jnp.abs(a_all)))
    bg_o[:, :LANES] = jax.nn.sigmoid(proj[:, COL_B:COL_B + LANES])
    bg_o[:, LANES:] = -jnp.exp(alog_ref[...]) * softplus
    token = jnp.bitwise_and(lax.broadcasted_iota(jnp.int32, (rows, LANES), 0), tm - 1)
    pos = (pos0 + j * tm + token).astype(F32)
    lane = lax.broadcasted_iota(jnp.int32, (rows, LANES), 1)
    ang = pos * inv_ref[...]
    cos = jnp.cos(ang)
    sin = jnp.sin(ang)
    half = MLA_ROPE // 2
    cs = jnp.where(lane < MLA_ROPE, cos, jnp.where(lane < MLA_ROPE + half, -sin, sin))
    keep = lane < MLA_ROPE

    def rope(pair):
        prod = pair * cs
        return jnp.where(keep, prod + pltpu.roll(prod, MLA_ROPE, axis=1), 0.0)

    ckv = _rms(proj[:, COL_CKV:COL_CKV + MLA_KV_RANK], kvn_ref[...])
    ckv_o[...] = per_seq(ckv)
    kr = rope(proj[:, COL_KR:COL_KR + LANES])
    kr_o[...] = per_seq(kr[:, :MLA_ROPE])
    kcat_o[:, :, :MLA_KV_RANK] = per_seq(ckv.astype(BF16))
    kcat_o[:, :, MLA_KV_RANK:] = per_seq(kr.astype(BF16))
    if maybe_vt_o:
        maybe_vt_o[0][0] = ckv.T.astype(BF16)
    for h in range(MLA_HEADS):
        qn = proj[:, COL_QNOPE + h * MLA_NOPE:COL_QNOPE + (h + 1) * MLA_NOPE].astype(BF16)
        qlat = _dot(qn, wuk_ref[h]) * MLA_SCALE
        qcat_o[:, h, :, :MLA_KV_RANK] = per_seq(qlat.astype(BF16))
        qr = rope(proj[:, COL_QROPE + h * LANES:COL_QROPE + (h + 1) * LANES]) * MLA_SCALE
        qcat_o[:, h, :, MLA_KV_RANK:] = per_seq(qr.astype(BF16))


def _mla_prep(hm, w_mla, nb, t, pos0, inv128, wuk_t, kvn, alog, dtb, emit_vt):
    d = hm.shape[1]
    tm = _tile(t, 512)
    nt = t // tm
    S = max(1, min(nb, 512 // tm)) if (nt == 1 and not emit_vt) else 1
    assert nb % S == 0 and tm & (tm - 1) == 0
    rt = S * tm
    rows = lambda b, j: b * nt + j
    fixed2 = lambda b, j: (0, 0)
    vmem = (d * N_MLA * 2 + 2 * (rt * d * 2 + MLA_HEADS * MLA_NOPE * MLA_KV_RANK * 2
                                 + rt * (MLA_KV_RANK + 3 * LANES) * 4 + (MLA_HEADS + 2) * rt * MLA_QK * 2)
            + 2 * rt * N_MLA * 4 + 8 * 2 ** 20)
    out_specs = [pl.BlockSpec((S, tm, MLA_KV_RANK), lambda b, j: (b, j, 0)),
                 pl.BlockSpec((S, tm, MLA_ROPE), lambda b, j: (b, j, 0)),
                 pl.BlockSpec((S, MLA_HEADS, tm, MLA_QK), lambda b, j: (b, 0, j, 0)),
                 pl.BlockSpec((S, tm, MLA_QK), lambda b, j: (b, j, 0)),
                 pl.BlockSpec((rt, 2 * LANES), lambda b, j: (rows(b, j), 0))]
    out_shape = [jax.ShapeDtypeStruct((nb, t, MLA_KV_RANK), F32),
                 jax.ShapeDtypeStruct((nb, t, MLA_ROPE), F32),
                 jax.ShapeDtypeStruct((nb, MLA_HEADS, t, MLA_QK), BF16),
                 jax.ShapeDtypeStruct((nb, t, MLA_QK), BF16),
                 jax.ShapeDtypeStruct((nb * t, 2 * LANES), F32)]
    if emit_vt:
        out_specs.append(pl.BlockSpec((1, MLA_KV_RANK, tm), lambda b, j: (b, 0, j)))
        out_shape.append(jax.ShapeDtypeStruct((nb, MLA_KV_RANK, t), BF16))
    return pl.pallas_call(
        functools.partial(_mla_prep_kernel, S=S, tm=tm, pos0=pos0),
        grid=(nb // S, nt),
        in_specs=[pl.BlockSpec((rt, d), lambda b, j: (rows(b, j), 0)),
                  pl.BlockSpec((d, N_MLA), fixed2, pipeline_mode=pl.Buffered(1)),
                  pl.BlockSpec((1, LANES), fixed2),
                  pl.BlockSpec((MLA_HEADS, MLA_NOPE, MLA_KV_RANK), lambda b, j: (0, 0, 0)),
                  pl.BlockSpec((1, MLA_KV_RANK), fixed2),
                  pl.BlockSpec((1, LANES), fixed2),
                  pl.BlockSpec((1, LANES), fixed2)],
        out_specs=out_specs,
        out_shape=out_shape,
        compiler_params=_params(("parallel", "parallel"), vmem),
        name="mla_proj_prep",
    )(hm, w_mla, inv128, wuk_t, kvn, alog, dtb)


ATTN_GROUPS = 2


def _attn_kernel(q_ref, k_ref, vt_ref, wuv_ref, o_ref, s0_ref, s1_ref, m_ref, l_ref, acc_ref,
                 *, tq, tk, n_valid):
    i = pl.program_id(1)
    cols = MLA_HEADS * tq
    cg = cols // ATTN_GROUPS
    q_first = i * tq
    c = lax.broadcasted_iota(jnp.int32, (1, cg), 1)
    qchunk = jnp.right_shift(q_first + jnp.bitwise_and(c, tq - 1), LOG2_CHUNK)
    k_all = jnp.minimum((q_first // CHUNK + 1) * CHUNK, n_valid)
    k_any = jnp.minimum(((q_first + tq - 1) // CHUNK + 1) * CHUNK, n_valid)
    nk = (k_any + tk - 1) // tk
    m_ref[...] = jnp.full(m_ref.shape, -jnp.inf, F32)
    l_ref[...] = jnp.zeros(l_ref.shape, F32)
    acc_ref[...] = jnp.zeros(acc_ref.shape, F32)

    def scores(j, buf):
        start = pl.multiple_of(j * tk, tk)
        q = q_ref[0].reshape(cols, MLA_QK)
        buf[...] = _dot_nt(k_ref[0, pl.ds(start, tk), :], q)

    def fold(x, op):
        while x.shape[0] > SUBLANES:
            half = x.shape[0] // 2
            x = op(x[:half], x[half:])
        return x

    def softmax_values(j, buf, masked):
        start = pl.multiple_of(j * tk, tk)
        vt = vt_ref[0, :, pl.ds(start, tk)]
        if masked:
            kpos = start + lax.broadcasted_iota(jnp.int32, (tk, 1), 0)
            mask = jnp.logical_and(jnp.right_shift(kpos, LOG2_CHUNK) <= qchunk, kpos < n_valid)
        for g in range(ATTN_GROUPS):
            cs = slice(g * cg, (g + 1) * cg)
            s = buf[:, cs]
            if masked:
                s = jnp.where(mask, s, -jnp.inf)
            m_old = m_ref[:, cs]
            m_new = jnp.maximum(m_old, jnp.max(fold(s, jnp.maximum), axis=0, keepdims=True))
            alpha = jnp.exp(m_old - m_new)
            p = jnp.exp(s - m_new)
            l_ref[:, cs] = alpha * l_ref[:, cs] + jnp.sum(fold(p, jnp.add), axis=0, keepdims=True)
            acc_ref[:, cs] = alpha * acc_ref[:, cs] + _dot(vt, p.astype(BF16))
            m_ref[:, cs] = m_new

    def by_parity(j, fn):
        @pl.when(jnp.bitwise_and(j, 1) == 0)
        def _():
            fn(s0_ref, s1_ref)

        @pl.when(jnp.bitwise_and(j, 1) == 1)
        def _():
            fn(s1_ref, s0_ref)

    scores(0, s0_ref)

    def body(j, carry, masked):
        def step(cur, nxt):
            scores(j + 1, nxt)
            softmax_values(j, cur, masked)
        by_parity(j, step)
        return carry

    n_open = jnp.minimum(k_all // tk, nk - 1)
    lax.fori_loop(0, n_open, functools.partial(body, masked=False), 0)
    lax.fori_loop(n_open, nk - 1, functools.partial(body, masked=True), 0)
    by_parity(nk - 1, lambda cur, nxt: softmax_values(nk - 1, cur, True))
    inv_l = 1.0 / l_ref[...]
    for h in range(MLA_HEADS):
        hs = slice(h * tq, (h + 1) * tq)
        lat_t = (acc_ref[:, hs] * inv_l[:, hs]).astype(BF16)
        o_ref[:, h * MLA_V:(h + 1) * MLA_V] = _dot_tn(lat_t, wuv_ref[h]).astype(BF16)


def _attn_cache_kernel(q_ref, ckv_ref, kr_ref, knew_ref, wuv_ref, o_ref, m_ref, l_ref, acc_ref,
                       *, tq, tk, nkb, past):
    j = pl.program_id(1)
    rows = MLA_HEADS * tq
    q = q_ref[0].reshape(rows, MLA_QK)
    r = lax.broadcasted_iota(jnp.int32, (rows, 1), 0)
    qchunk = jnp.right_shift(past + jnp.bitwise_and(r, tq - 1), LOG2_CHUNK)

    @pl.when(j == 0)
    def _():
        m_ref[...] = jnp.full(m_ref.shape, -jnp.inf, F32)
        l_ref[...] = jnp.zeros(l_ref.shape, F32)
        acc_ref[...] = jnp.zeros(acc_ref.shape, F32)

    def fold(x, op):
        while x.shape[1] > LANES:
            half = x.shape[1] // 2
            x = op(x[:, :half], x[:, half:])
        return x

    def update(s, kpos, vals):
        s = jnp.where(jnp.right_shift(kpos, LOG2_CHUNK) <= qchunk, s, -jnp.inf)
        m_old = m_ref[...]
        m_new = jnp.maximum(m_old, jnp.max(fold(s, jnp.maximum), axis=-1, keepdims=True))
        alpha = jnp.exp(m_old - m_new)
        p = jnp.exp(s - m_new)
        l_ref[...] = alpha * l_ref[...] + jnp.sum(fold(p, jnp.add), axis=-1, keepdims=True)
        acc_ref[...] = alpha * acc_ref[...] + _dot(p.astype(BF16), vals)
        m_ref[...] = m_new

    ck = ckv_ref[0].astype(BF16)
    s = (_dot_nt(q[:, :MLA_KV_RANK], ck)
         + _dot_nt(q[:, MLA_KV_RANK:MLA_KV_RANK + MLA_ROPE], kr_ref[0].astype(BF16)))
    update(s, j * tk + lax.broadcasted_iota(jnp.int32, (1, tk), 1), ck)

    @pl.when(j == nkb - 1)
    def _():
        kn = knew_ref[0]
        update(_dot_nt(q, kn), past + lax.broadcasted_iota(jnp.int32, (1, tq), 1), kn[:, :MLA_KV_RANK])
        inv_l = 1.0 / l_ref[...]
        for h in range(MLA_HEADS):
            hs = slice(h * tq, (h + 1) * tq)
            lat = (acc_ref[hs, :] * inv_l[hs, :]).astype(BF16)
            o_ref[:, h * MLA_V:(h + 1) * MLA_V] = _dot(lat, wuv_ref[h]).astype(BF16)


def _attention(qcat, kcat, vt, wuv_t):
    nb, _, t, _ = qcat.shape
    tq = _tile(t, 128)
    tk = _tile(t, 512)
    assert tq & (tq - 1) == 0
    nq = t // tq
    cols = MLA_HEADS * tq
    vmem = (2 * (cols * MLA_QK * 2 + t * MLA_QK * 2 + MLA_KV_RANK * t * 2
                 + MLA_HEADS * MLA_KV_RANK * MLA_V * 2 + tq * MLA_HEADS * MLA_V * 2)
            + 2 * tk * cols * 4 + 2 * SUBLANES * cols * 4 + MLA_KV_RANK * cols * 4
            + 3 * tk * cols * 4 // ATTN_GROUPS + 4 * 2 ** 20)
    return pl.pallas_call(
        functools.partial(_attn_kernel, tq=tq, tk=tk, n_valid=t),
        grid=(nb, nq),
        in_specs=[pl.BlockSpec((1, MLA_HEADS, tq, MLA_QK), lambda b, i: (b, 0, i, 0)),
                  pl.BlockSpec((1, t, MLA_QK), lambda b, i: (b, 0, 0)),
                  pl.BlockSpec((1, MLA_KV_RANK, t), lambda b, i: (b, 0, 0)),
                  pl.BlockSpec((MLA_HEADS, MLA_KV_RANK, MLA_V), lambda b, i: (0, 0, 0))],
        out_specs=pl.BlockSpec((tq, MLA_HEADS * MLA_V), lambda b, i: (b * nq + i, 0)),
        out_shape=jax.ShapeDtypeStruct((nb * t, MLA_HEADS * MLA_V), BF16),
        scratch_shapes=[pltpu.VMEM((tk, cols), F32), pltpu.VMEM((tk, cols), F32), pltpu.VMEM((1, cols), F32),
                        pltpu.VMEM((1, cols), F32), pltpu.VMEM((MLA_KV_RANK, cols), F32)],
        compiler_params=_params(("parallel", "arbitrary"), vmem),
        name="mla_attention",
    )(qcat, kcat, vt, wuv_t)


def _attention_cached(qcat, ckv_past, krope_past, kcat_new, wuv_t):
    nb, _, t, _ = qcat.shape
    past = ckv_past.shape[1]
    tk = _tile(past, 4096)
    nkb = past // tk
    assert t & (t - 1) == 0
    rows = MLA_HEADS * t
    vmem = (2 * (rows * MLA_QK * 2 + tk * MLA_KV_RANK * 4 + tk * LANES * 4 + t * MLA_QK * 2
                 + MLA_HEADS * MLA_KV_RANK * MLA_V * 2 + t * MLA_HEADS * MLA_V * 2)
            + 2 * rows * LANES * 4 + rows * MLA_KV_RANK * 4 + tk * MLA_QK * 2 + 4 * rows * tk * 4 + 4 * 2 ** 20)
    return pl.pallas_call(
        functools.partial(_attn_cache_kernel, tq=t, tk=tk, nkb=nkb, past=past),
        grid=(nb, nkb),
        in_specs=[pl.BlockSpec((1, MLA_HEADS, t, MLA_QK), lambda b, j: (b, 0, 0, 0)),
                  pl.BlockSpec((1, tk, MLA_KV_RANK), lambda b, j: (b, j, 0)),
                  pl.BlockSpec((1, tk, MLA_ROPE), lambda b, j: (b, j, 0)),
                  pl.BlockSpec((1, t, MLA_QK), lambda b, j: (b, 0, 0)),
                  pl.BlockSpec((MLA_HEADS, MLA_KV_RANK, MLA_V), lambda b, j: (0, 0, 0))],
        out_specs=pl.BlockSpec((t, MLA_HEADS * MLA_V), lambda b, j: (b, 0)),
        out_shape=jax.ShapeDtypeStruct((nb * t, MLA_HEADS * MLA_V), BF16),
        scratch_shapes=[pltpu.VMEM((rows, 1), F32), pltpu.VMEM((rows, 1), F32),
                        pltpu.VMEM((rows, MLA_KV_RANK), F32)],
        compiler_params=_params(("parallel", "arbitrary"), vmem),
        name="mla_attention_cached",
    )(qcat, ckv_past, krope_past, kcat_new, wuv_t)


def _merge_out_kernel(og_ref, om_ref, wg_ref, wm_ref, gg_ref, gm_ref, wo_ref, x_ref, gpost_ref, gnext_ref,
                      xo_ref, xno_ref):
    tg = _dot(og_ref[...], wg_ref[...])
    tm_ = _dot(om_ref[...], wm_ref[...])
    merged = (jax.nn.sigmoid(gg_ref[...].astype(F32)) * tg
              + jax.nn.sigmoid(gm_ref[...].astype(F32)) * tm_).astype(BF16)
    y = _dot(merged, wo_ref[...])
    xnew = x_ref[...] + _rms(y, gpost_ref[...])
    xo_ref[...] = xnew
    xno_ref[...] = _rms(xnew, gnext_ref[...]).astype(BF16)


def _merge_out(og, om, wbg, wbm, proj_gate, wout, x, gpost, gnext):
    m, kg = og.shape
    d = wbg.shape[1]
    tm = _tile(m, 256)
    gsz = proj_gate.dtype.itemsize
    row = lambda i: (i, 0)
    fixed = lambda i: (0, 0)
    resident = dict(pipeline_mode=pl.Buffered(1))
    vmem = ((2 * kg * d + d * d) * 2
            + 2 * (2 * tm * kg * 2 + 2 * tm * d * gsz + 2 * tm * d * 4 + tm * d * 2)
            + 4 * tm * d * 4 + 2 * 2 ** 20)
    return pl.pallas_call(
        _merge_out_kernel,
        grid=(m // tm,),
        in_specs=[pl.BlockSpec((tm, kg), row),
                  pl.BlockSpec((tm, kg), row),
                  pl.BlockSpec((kg, d), fixed, **resident),
                  pl.BlockSpec((kg, d), fixed, **resident),
                  pl.BlockSpec((tm, d), lambda i: (i, 0)),
                  pl.BlockSpec((tm, d), lambda i: (i, 1)),
                  pl.BlockSpec((d, d), fixed, **resident),
                  pl.BlockSpec((tm, d), row),
                  pl.BlockSpec((1, d), fixed),
                  pl.BlockSpec((1, d), fixed)],
        out_specs=[pl.BlockSpec((tm, d), row), pl.BlockSpec((tm, d), row)],
        out_shape=[jax.ShapeDtypeStruct((m, d), F32), jax.ShapeDtypeStruct((m, d), BF16)],
        compiler_params=_params(("parallel",), vmem),
        name="merge_out",
    )(og, om, wbg, wbm, proj_gate, proj_gate, wout, x, gpost, gnext)


def _layer(x3, pos0, conv_state, ssm0, ckv_past, krope_past, w):
    nb, t, d = x3.shape
    m = nb * t
    x = x3.reshape(m, d)
    x1, hm, ffn1_w = _ffn(x, *w["ffn1_w"], w["ffn1_norm_post"],
                          gpre=w["ffn1_norm_pre"], gnext=w["mix_norm_pre"])
    proj_gdn = _matmul(hm, w["w_in_gdn"], F32)
    proj_gate = _matmul(hm, w["w_in_gate"], BF16)
    ckv, krope, qcat, kcat, bg, *vt = _mla_prep(hm, w["w_in_mla"], nb, t, pos0, w["inv128"], w["wuk_t"],
                                                w["kv_norm"], w["alog"], w["dtb"], ckv_past is None)

    o_gdn, ssm_new = _gdn(proj_gdn, bg, nb, t, conv_state, ssm0, w["conv_w"], w["gdn_nw"])
    conv_new = proj_gdn.reshape(nb, t, -1)[:, t - (GDN_CONV - 1):, COL_QKV:COL_QKV + GDN_CONV_DIM]

    if ckv_past is None:
        o_mla = _attention(qcat, kcat, vt[0], w["wuv_t"])
    else:
        o_mla = _attention_cached(qcat, ckv_past, krope_past, kcat, w["wuv_t"])

    x2, xn2 = _merge_out(o_gdn, o_mla, w["w_br_gdn"], w["w_br_mla"], proj_gate, w["w_out"], x1,
                         w["mix_norm_post"], w["ffn2_norm_pre"])
    x3_, _, ffn2_w = _ffn(x2, *w["ffn2_w"], w["ffn2_norm_post"], xn=xn2)
    return (x3_.reshape(nb, t, d), conv_new, ssm_new, ckv, krope), dict(ffn1_w=ffn1_w, ffn2_w=ffn2_w)


def _prep_weights(l, d_model, **p):
    row = lambda v: v[l].reshape(1, -1).astype(F32)
    pad_lanes = lambda v: jnp.pad(v[l].reshape(1, -1).astype(F32), ((0, 0), (0, LANES - v.shape[1])))
    inv = ROPE_THETA ** (-jnp.arange(0, MLA_ROPE, 2, dtype=F32) / MLA_ROPE)
    w_in_gdn, w_in_gate, w_in_mla = _build_w_in(p["w_in"][l], d_model)
    w = dict(
        w_in_gdn=w_in_gdn, w_in_gate=w_in_gate, w_in_mla=w_in_mla,
        ffn1_norm_pre=row(p["ffn1_norm_pre"]), ffn1_norm_post=row(p["ffn1_norm_post"]),
        mix_norm_pre=row(p["mix_norm_pre"]), mix_norm_post=row(p["mix_norm_post"]),
        ffn2_norm_pre=row(p["ffn2_norm_pre"]), ffn2_norm_post=row(p["ffn2_norm_post"]),
        ffn1_w=(p["ffn1_w_gate"][l], p["ffn1_w_up"][l], p["ffn1_w_down"][l]),
        ffn2_w=(p["ffn2_w_gate"][l], p["ffn2_w_up"][l], p["ffn2_w_down"][l]),
        conv_w=p["gdn_conv_w"][l].astype(F32),
        alog=pad_lanes(p["gdn_a_log"]), dtb=pad_lanes(p["gdn_dt_bias"]),
        gdn_nw=row(p["gdn_norm_w"]), kv_norm=row(p["mla_kv_norm"]),
        inv128=jnp.tile(inv, LANES // inv.shape[0]).reshape(1, LANES),
        wuk_t=jnp.transpose(p["mla_w_uk"][l], (1, 2, 0)).astype(BF16),
        wuv_t=jnp.transpose(p["mla_w_uv"][l], (1, 0, 2)).astype(BF16),
        w_br_gdn=p["w_br_gdn"][l].astype(BF16), w_br_mla=p["w_br_mla"][l].astype(BF16),
        w_out=p["w_out"][l].astype(BF16),
    )
    return w


def kernel(x_prompt, x_sample, state_gdn_conv, state_gdn_ssm, cache_mla_ckv, cache_mla_krope, ffn1_norm_pre, ffn1_w_gate, ffn1_w_up, ffn1_w_down, ffn1_norm_post, mix_norm_pre, w_in, gdn_conv_w, gdn_a_log, gdn_dt_bias, gdn_norm_w, mla_kv_norm, mla_w_uk, mla_w_uv, w_br_gdn, w_br_mla, w_out, mix_norm_post, ffn2_norm_pre, ffn2_w_gate, ffn2_w_up, ffn2_w_down, ffn2_norm_post):
    params = dict(
        ffn1_norm_pre=ffn1_norm_pre, ffn1_w_gate=ffn1_w_gate, ffn1_w_up=ffn1_w_up,
        ffn1_w_down=ffn1_w_down, ffn1_norm_post=ffn1_norm_post, mix_norm_pre=mix_norm_pre,
        w_in=w_in, gdn_conv_w=gdn_conv_w, gdn_a_log=gdn_a_log, gdn_dt_bias=gdn_dt_bias,
        gdn_norm_w=gdn_norm_w, mla_kv_norm=mla_kv_norm, mla_w_uk=mla_w_uk, mla_w_uv=mla_w_uv,
        w_br_gdn=w_br_gdn, w_br_mla=w_br_mla, w_out=w_out, mix_norm_post=mix_norm_post,
        ffn2_norm_pre=ffn2_norm_pre, ffn2_w_gate=ffn2_w_gate, ffn2_w_up=ffn2_w_up,
        ffn2_w_down=ffn2_w_down, ffn2_norm_post=ffn2_norm_post)
    depth = w_in.shape[0]
    d_model = x_prompt.shape[-1]
    b_p = x_prompt.shape[0]
    past = cache_mla_ckv.shape[2]
    yp, ys = x_prompt, x_sample
    outs_p, outs_s = [], []
    for l in range(depth):
        w = _prep_weights(l, d_model, **params)
        (ys, *rest_s), w_bf16 = _layer(
            ys, past, state_gdn_conv[l], state_gdn_ssm[l], cache_mla_ckv[l], cache_mla_krope[l], w)
        (yp, *rest_p), _ = _layer(
            yp, 0, jnp.zeros((b_p, GDN_CONV - 1, GDN_CONV_DIM), F32),
            jnp.zeros((b_p, GDN_HEADS, GDN_DK, GDN_DV), F32), None, None, {**w, **w_bf16})
        outs_p.append(rest_p)
        outs_s.append(rest_s)
    stack = lambda outs, i: jnp.stack([o[i] for o in outs])
    return (yp, ys,
            stack(outs_p, 0), stack(outs_p, 1), stack(outs_p, 2), stack(outs_p, 3),
            stack(outs_s, 0), stack(outs_s, 1), stack(outs_s, 2), stack(outs_s, 3))
```

```python
import functools

import jax
import jax.numpy as jnp
from jax import lax
from jax.experimental import pallas as pl
from jax.experimental.pallas import tpu as pltpu

F32 = jnp.float32
BF16 = jnp.bfloat16

CHUNK = 64
LOG2_CHUNK = 6
NORM_EPS = 1e-6
GDN_HEADS = 8
GDN_DK = 128
GDN_DV = 128
GDN_CONV = 4
GDN_KEY_DIM = GDN_HEADS * GDN_DK
GDN_VAL_DIM = GDN_HEADS * GDN_DV
GDN_CONV_DIM = 2 * GDN_KEY_DIM + GDN_VAL_DIM
MLA_HEADS = 8
MLA_NOPE = 128
MLA_ROPE = 64
MLA_V = 128
MLA_KV_RANK = 512
MLA_SCALE = (MLA_NOPE + MLA_ROPE) ** -0.5
ROPE_THETA = 10000.0

LANES = 128
SUBLANES = 8
VMEM_CAP_BYTES = 56 * 2 ** 20
MLA_QK = MLA_KV_RANK + LANES
HI = lax.Precision.HIGHEST


def _params(semantics, vmem_bytes):
    limit = int(min(max(vmem_bytes, 16 * 2 ** 20), VMEM_CAP_BYTES))
    return pltpu.CompilerParams(dimension_semantics=semantics, vmem_limit_bytes=limit)


def _dot(a, b, prec=None):
    return lax.dot_general(a, b, (((1,), (0,)), ((), ())), precision=prec,
                           preferred_element_type=F32)


def _dot_nt(a, b, prec=None):
    return lax.dot_general(a, b, (((1,), (1,)), ((), ())), precision=prec,
                           preferred_element_type=F32)


def _dot_tn(a, b, prec=None):
    return lax.dot_general(a, b, (((0,), (0,)), ((), ())), precision=prec,
                           preferred_element_type=F32)


def _rms(y, g):
    return y * lax.rsqrt(jnp.mean(y * y, axis=-1, keepdims=True) + NORM_EPS) * g


def _silu(x):
    return x * jax.nn.sigmoid(x)


def _tile(n, pref):
    t = min(n, pref)
    assert n % t == 0, (n, pref)
    return t


def _ffn_kernel(*refs, nf, norm_in, emit_next, cast_w):
    refs = list(refs)
    xn_ref = None if norm_in else refs.pop(0)
    x_ref, wg_ref, wu_ref, wd_ref = refs[:4]
    del refs[:4]
    gpre_ref = refs.pop(0) if norm_in else None
    gpost_ref = refs.pop(0)
    gnext_ref = refs.pop(0) if emit_next else None
    xo_ref = refs.pop(0)
    xno_ref = refs.pop(0) if emit_next else None
    if cast_w:
        wgo_ref, wuo_ref, wdo_ref = refs[:3]
        del refs[:3]
    acc_ref = refs.pop(0)
    if norm_in:
        xn_ref = refs.pop(0)
    f = pl.program_id(1)

    @pl.when(f == 0)
    def _():
        acc_ref[...] = jnp.zeros(acc_ref.shape, F32)
        if norm_in:
            xn_ref[...] = _rms(x_ref[...], gpre_ref[...]).astype(BF16)

    xn = xn_ref[...]
    wg, wu, wd = wg_ref[...], wu_ref[...], wd_ref[...]
    if cast_w:
        wg, wu, wd = wg.astype(BF16), wu.astype(BF16), wd.astype(BF16)
        wgo_ref[...] = wg
        wuo_ref[...] = wu
        wdo_ref[...] = wd
    gate = _dot(xn, wg)
    up = _dot(xn, wu)
    h = (_silu(gate) * up).astype(BF16)
    acc_ref[...] += _dot(h, wd)

    @pl.when(f == nf - 1)
    def _():
        xnew = x_ref[...] + 0.5 * _rms(acc_ref[...], gpost_ref[...])
        xo_ref[...] = xnew
        if emit_next:
            xno_ref[...] = _rms(xnew, gnext_ref[...]).astype(BF16)


def _ffn(x, wg, wu, wd, gpost, xn=None, gpre=None, gnext=None):
    norm_in = xn is None
    emit_next = gnext is not None
    cast_w = wg.dtype == F32
    assert norm_in == (gpre is not None)
    m, d = x.shape
    dff = wg.shape[1]
    tm = _tile(m, 512)
    tf = _tile(dff, 256 if cast_w else 512)
    nf = dff // tf
    assert not cast_w or m == tm
    wsz = wg.dtype.itemsize
    row = lambda i, f: (i, 0)
    vec = pl.BlockSpec((1, d), lambda i, f: (0, 0))
    w_specs = [pl.BlockSpec((d, tf), lambda i, f: (0, f)),
               pl.BlockSpec((d, tf), lambda i, f: (0, f)),
               pl.BlockSpec((tf, d), lambda i, f: (f, 0))]
    ins = [] if norm_in else [xn]
    in_specs = [] if norm_in else [pl.BlockSpec((tm, d), row)]
    ins += [x, wg, wu, wd]
    in_specs += [pl.BlockSpec((tm, d), row)] + w_specs
    for g in (gpre, gpost, gnext):
        if g is not None:
            ins.append(g)
            in_specs.append(vec)
    out_specs = [pl.BlockSpec((tm, d), row)]
    out_shape = [jax.ShapeDtypeStruct((m, d), F32)]
    scratch = [pltpu.VMEM((tm, d), F32)]
    if emit_next:
        out_specs.append(pl.BlockSpec((tm, d), row))
        out_shape.append(jax.ShapeDtypeStruct((m, d), BF16))
    if cast_w:
        out_specs += w_specs
        out_shape += [jax.ShapeDtypeStruct(w.shape, BF16) for w in (wg, wu, wd)]
    if norm_in:
        scratch.append(pltpu.VMEM((tm, d), BF16))
    vmem = (2 * (tm * d * 2 + tm * d * 4 + 3 * d * tf * wsz + tm * d * 4 + tm * d * 2)
            + (2 * 3 * d * tf * 2 + 3 * d * tf * 2 if cast_w else 0)
            + tm * d * 4 + 3 * tm * tf * 4 + 2 * tm * d * 4)
    res = pl.pallas_call(
        functools.partial(_ffn_kernel, nf=nf, norm_in=norm_in, emit_next=emit_next, cast_w=cast_w),
        grid=(m // tm, nf),
        in_specs=in_specs,
        out_specs=out_specs,
        out_shape=out_shape,
        scratch_shapes=scratch,
        compiler_params=_params(("parallel", "arbitrary"), vmem),
        name="ffn_cast" if cast_w else "ffn",
    )(*ins)
    res = list(res)
    x_new = res.pop(0)
    xn_next = res.pop(0) if emit_next else None
    return x_new, xn_next, (tuple(res) if cast_w else (wg, wu, wd))


def _mm_kernel(a_ref, w_ref, o_ref):
    o_ref[...] = _dot(a_ref[...], w_ref[...]).astype(o_ref.dtype)


def _matmul(a, w, out_dtype):
    m, k = a.shape
    n = w.shape[1]
    tm = _tile(m, 1024)
    tn = _tile(n, 2048)
    osz = jnp.dtype(out_dtype).itemsize
    vmem = 2 * (tm * k * 2 + k * tn * 2 + tm * tn * osz) + tm * tn * 4
    return pl.pallas_call(
        _mm_kernel,
        grid=(m // tm, n // tn),
        in_specs=[pl.BlockSpec((tm, k), lambda i, j: (i, 0)),
                  pl.BlockSpec((k, tn), lambda i, j: (0, j))],
        out_specs=pl.BlockSpec((tm, tn), lambda i, j: (i, j)),
        out_shape=jax.ShapeDtypeStruct((m, n), out_dtype),
        compiler_params=_params(("parallel", "arbitrary"), vmem),
        name="in_proj",
    )(a, w)


COL_QKV = 0
COL_Z = COL_QKV + GDN_CONV_DIM
COL_QNOPE = 0
COL_QROPE = COL_QNOPE + MLA_HEADS * MLA_NOPE
COL_CKV = COL_QROPE + MLA_HEADS * LANES
COL_KR = COL_CKV + MLA_KV_RANK
COL_B = COL_KR + LANES
COL_A = COL_B + LANES
N_MLA = 3072


def _build_w_in(w_in, d_model):
    sizes = (GDN_CONV_DIM, GDN_VAL_DIM, GDN_HEADS, GDN_HEADS, MLA_HEADS * (MLA_NOPE + MLA_ROPE),
             MLA_KV_RANK, MLA_ROPE, d_model, d_model)
    offs = [0]
    for s in sizes:
        offs.append(offs[-1] + s)
    w_gdn = w_in[:, :offs[2]].astype(BF16)
    w_gate = w_in[:, offs[7]:].astype(BF16)
    part = lambda i: w_in[:, offs[i]:offs[i + 1]].astype(BF16)
    b, a, qm, ckv, kr = (part(i) for i in range(2, 7))
    d = w_in.shape[0]
    half = MLA_ROPE // 2
    qm = qm.reshape(d, MLA_HEADS, MLA_NOPE + MLA_ROPE)
    qn = qm[:, :, :MLA_NOPE].reshape(d, MLA_HEADS * MLA_NOPE)
    qr = qm[:, :, MLA_NOPE:]
    qr_pair = jnp.concatenate([qr, qr[:, :, half:], qr[:, :, :half]], axis=2).reshape(d, MLA_HEADS * LANES)
    kr_pair = jnp.concatenate([kr, kr[:, half:], kr[:, :half]], axis=1)
    zeros = lambda n: jnp.zeros((d, n), BF16)
    w_mla = jnp.concatenate([qn, qr_pair, ckv, kr_pair, b, zeros(LANES - GDN_HEADS), a, zeros(LANES - GDN_HEADS),
                             zeros(N_MLA - COL_A - LANES)], axis=1)
    return w_gdn, w_gate, w_mla


_NN = (((1,), (0,)), ((), ()))
_NT = (((1,), (1,)), ((), ()))
_TN = (((0,), (0,)), ((), ()))
INV_BLOCK = 16


def _mm(a, b, dims=_NN):
    return lax.dot_general(a.astype(BF16), b.astype(BF16), dims, preferred_element_type=F32)


def _gdn_kernel(qkv_ref, z_ref, beta_ref, g_ref, cst_ref, cw_ref, nw_ref, s0_ref,
                o_ref, sout_ref, past_ref, s_ref, *, G, C, nlev, nc):
    c = pl.program_id(1)

    @pl.when(c == 0)
    def _():
        past_ref[...] = cst_ref[...]
        s_ref[...] = s0_ref[...]

    ri = lax.broadcasted_iota(jnp.int32, (C, C), 0)
    ci = lax.broadcasted_iota(jnp.int32, (C, C), 1)
    incl = ri >= ci
    strict = ri > ci
    eye = (ri == ci).astype(F32)
    log2_blk = INV_BLOCK.bit_length() - 1
    bdiag = jnp.right_shift(ri, log2_blk) == jnp.right_shift(ci, log2_blk)

    first_row = lax.broadcasted_iota(jnp.int32, (C, 1), 0) == 0

    def conv(s, col):
        cols = slice(col, col + LANES)
        x = qkv_ref[s, :, cols]
        past = past_ref[s, :, cols]
        w = [cw_ref[j:j + 1, cols] for j in range(GDN_CONV)]
        acc = w[0] * x
        for j in range(1, GDN_CONV):
            top = sum(w[i] * past[SUBLANES - 1 - (j - 1 - i):SUBLANES - (j - 1 - i)] for i in range(j))
            acc = w[j] * x + jnp.where(first_row, top, pltpu.roll(acc, 1, axis=0))
        return _silu(acc)

    units = [(s, h) for s in range(G) for h in range(GDN_HEADS)]
    every = lambda fn, *lists: [fn(*args) for args in zip(*lists)]
    mm_inv = _mm
    beta_all, gcum, gcum_t = [], [], []
    for s in range(G):
        beta_all.append(beta_ref[s])
        g_all = g_ref[s]
        gcum.append(_dot(incl.astype(F32), g_all, HI))
        gcum_t.append(_dot_tn(g_all, (ci >= ri).astype(F32), HI))

    def l2n(x):
        return x * lax.rsqrt(jnp.sum(x * x, axis=-1, keepdims=True) + 1e-6)

    q = [l2n(conv(s, h * GDN_DK)) * (GDN_DK ** -0.5) for s, h in units]
    k = [l2n(conv(s, GDN_KEY_DIM + h * GDN_DK)) for s, h in units]
    v = [conv(s, 2 * GDN_KEY_DIM + h * GDN_DV) for s, h in units]
    bcol = [beta_all[s][:, h:h + 1] for s, h in units]
    gcol = [gcum[s][:, h:h + 1] for s, h in units]
    glast = [gcum[s][C - 1:C, h:h + 1] for s, h in units]
    decay = [jnp.where(incl, jnp.exp(jnp.where(incl, gcum[s][:, h:h + 1] - gcum_t[s][h:h + 1, :], 0.0)), 0.0)
             for s, h in units]
    qkk = every(lambda q_, k_: _mm(jnp.concatenate([q_, k_], axis=0), k_, _NT), q, k)
    qk = every(lambda x, d: x[:C] * d, qkk, decay)
    nmat = every(lambda x, b_, d: -jnp.where(strict, b_ * x[C:] * d, 0.0), qkk, bcol, decay)
    ndiag = every(lambda n_: jnp.where(bdiag, n_, 0.0), nmat)
    tinv = every(lambda n_: eye + n_, ndiag)
    npow = ndiag
    for _ in range(INV_BLOCK.bit_length() - 2):
        npow = every(mm_inv, npow, npow)
        tinv = every(lambda t_, p_: t_ + mm_inv(t_, p_), tinv, npow)
    mpow = every(lambda t_, n_, d_: mm_inv(t_, n_ - d_), tinv, nmat, ndiag)
    for lev in range(nlev):
        if lev:
            mpow = every(mm_inv, mpow, mpow)
        tinv = every(lambda t_, m_: t_ + mm_inv(m_, t_), tinv, mpow)
    egc = every(jnp.exp, gcol)
    sol = every(lambda t_, v_, k_, b_, e_: _mm(t_, jnp.concatenate([v_ * b_, k_ * (b_ * e_)], axis=1)),
                tinv, v, k, bcol, egc)
    s_old = [s_ref[s, h] for s, h in units]
    ws_qs = every(lambda x, q_, e_, s_: _mm(jnp.concatenate([x[:, GDN_DV:], q_ * e_], axis=0), s_),
                  sol, q, egc, s_old)
    v_new = every(lambda x, y: x[:, :GDN_DV] - y[:C], sol, ws_qs)
    o = every(lambda y, a_, vn: y[C:] + _mm(a_, vn), ws_qs, qk, v_new)
    s_new = every(lambda s_, gl, k_, gc, vn: s_ * jnp.exp(gl) + _mm(k_ * jnp.exp(gl - gc), vn, _TN),
                  s_old, glast, k, gcol, v_new)
    for (s, h), sn, o_ in zip(units, s_new, o):
        s_ref[s, h] = sn
        zz = z_ref[s, :, h * GDN_DV:(h + 1) * GDN_DV]
        o_ref[s, :, h * GDN_DV:(h + 1) * GDN_DV] = (_rms(o_, nw_ref[...]) * _silu(zz)).astype(BF16)

    past_ref[...] = qkv_ref[:, C - SUBLANES:C, :]

    @pl.when(c == nc - 1)
    def _():
        sout_ref[...] = s_ref[...]


def _gdn(proj_gdn, bg, nb, t, conv_state, ssm0, conv_w, nw):
    C = min(CHUNK, t)
    G = 2 if nb % 2 == 0 else 1
    assert t % C == 0 and C & (C - 1) == 0 and C % INV_BLOCK == 0
    nc = t // C
    nlev = (C // INV_BLOCK).bit_length() - 1
    pg = proj_gdn.reshape(nb, t, proj_gdn.shape[1])
    pm = bg.reshape(nb, t, bg.shape[1])
    fixed = lambda b, c: (0, 0)
    cst = jnp.pad(conv_state, ((0, 0), (SUBLANES - (GDN_CONV - 1), 0), (0, 0)))
    vmem = G * (2 * (C * GDN_CONV_DIM * 4 + C * GDN_VAL_DIM * 4 + 2 * C * LANES * 4 + SUBLANES * GDN_CONV_DIM * 4
                     + C * GDN_VAL_DIM * 2 + 2 * GDN_HEADS * GDN_DK * GDN_DV * 4)
                + SUBLANES * GDN_CONV_DIM * 4 + GDN_HEADS * GDN_DK * GDN_DV * 4) + 16 * 2 ** 20
    o, s_new = pl.pallas_call(
        functools.partial(_gdn_kernel, G=G, C=C, nlev=nlev, nc=nc),
        grid=(nb // G, nc),
        in_specs=[pl.BlockSpec((G, C, GDN_CONV_DIM), lambda b, c: (b, c, COL_QKV // GDN_CONV_DIM)),
                  pl.BlockSpec((G, C, GDN_VAL_DIM), lambda b, c: (b, c, COL_Z // GDN_VAL_DIM)),
                  pl.BlockSpec((G, C, LANES), lambda b, c: (b, c, 0)),
                  pl.BlockSpec((G, C, LANES), lambda b, c: (b, c, 1)),
                  pl.BlockSpec((G, SUBLANES, GDN_CONV_DIM), lambda b, c: (b, 0, 0)),
                  pl.BlockSpec((GDN_CONV, GDN_CONV_DIM), fixed),
                  pl.BlockSpec((1, GDN_DV), fixed),
                  pl.BlockSpec((G, GDN_HEADS, GDN_DK, GDN_DV), lambda b, c: (b, 0, 0, 0))],
        out_specs=[pl.BlockSpec((G, C, GDN_VAL_DIM), lambda b, c: (b, c, 0)),
                   pl.BlockSpec((G, GDN_HEADS, GDN_DK, GDN_DV), lambda b, c: (b, 0, 0, 0))],
        out_shape=[jax.ShapeDtypeStruct((nb, t, GDN_VAL_DIM), BF16),
                   jax.ShapeDtypeStruct((nb, GDN_HEADS, GDN_DK, GDN_DV), F32)],
        scratch_shapes=[pltpu.VMEM((G, SUBLANES, GDN_CONV_DIM), F32),
                        pltpu.VMEM((G, GDN_HEADS, GDN_DK, GDN_DV), F32)],
        compiler_params=_params(("parallel", "arbitrary"), vmem),
        name="gdn",
    )(pg, pg, pm, pm, cst, conv_w, nw, ssm0)
    return o.reshape(nb * t, GDN_VAL_DIM), s_new


def _mla_prep_kernel(hm_ref, w_ref, inv_ref, wuk_ref, kvn_ref, alog_ref, dtb_ref,
                     ckv_o, kr_o, qcat_o, kcat_o, bg_o, *maybe_vt_o, S, tm, pos0):
    j = pl.program_id(1)
    rows = S * tm
    per_seq = lambda x: x.reshape(S, tm, x.shape[-1])
    proj = _dot(hm_ref[...], w_ref[...])
    a_all = proj[:, COL_A:COL_A + LANES] + dtb_ref[...]
    softplus = jnp.maximum(a_all, 0.0) + jnp.log(1.0 + jnp.exp(-jnp.abs(a_all)))
    bg_o[:, :LANES] = jax.nn.sigmoid(proj[:, COL_B:COL_B + LANES])
    bg_o[:, LANES:] = -jnp.exp(alog_ref[...]) * softplus
    token = jnp.bitwise_and(lax.broadcasted_iota(jnp.int32, (rows, LANES), 0), tm - 1)
    pos = (pos0 + j * tm + token).astype(F32)
    lane = lax.broadcasted_iota(jnp.int32, (rows, LANES), 1)
    ang = pos * inv_ref[...]
    cos = jnp.cos(ang)
    sin = jnp.sin(ang)
    half = MLA_ROPE // 2
    cs = jnp.where(lane < MLA_ROPE, cos, jnp.where(lane < MLA_ROPE + half, -sin, sin))
    keep = lane < MLA_ROPE

    def rope(pair):
        prod = pair * cs
        return jnp.where(keep, prod + pltpu.roll(prod, MLA_ROPE, axis=1), 0.0)

    ckv = _rms(proj[:, COL_CKV:COL_CKV + MLA_KV_RANK], kvn_ref[...])
    ckv_o[...] = per_seq(ckv)
    kr = rope(proj[:, COL_KR:COL_KR + LANES])
    kr_o[...] = per_seq(kr[:, :MLA_ROPE])
    kcat_o[:, :, :MLA_KV_RANK] = per_seq(ckv.astype(BF16))
    kcat_o[:, :, MLA_KV_RANK:] = per_seq(kr.astype(BF16))
    if maybe_vt_o:
        maybe_vt_o[0][0] = ckv.T.astype(BF16)
    for h in range(MLA_HEADS):
        qn = proj[:, COL_QNOPE + h * MLA_NOPE:COL_QNOPE + (h + 1) * MLA_NOPE].astype(BF16)
        qlat = _dot(qn, wuk_ref[h]) * MLA_SCALE
        qcat_o[:, h, :, :MLA_KV_RANK] = per_seq(qlat.astype(BF16))
        qr = rope(proj[:, COL_QROPE + h * LANES:COL_QROPE + (h + 1) * LANES]) * MLA_SCALE
        qcat_o[:, h, :, MLA_KV_RANK:] = per_seq(qr.astype(BF16))


def _mla_prep(hm, w_mla, nb, t, pos0, inv128, wuk_t, kvn, alog, dtb, emit_vt):
    d = hm.shape[1]
    tm = _tile(t, 512)
    nt = t // tm
    S = max(1, min(nb, 512 // tm)) if (nt == 1 and not emit_vt) else 1
    assert nb % S == 0 and tm & (tm - 1) == 0
    rt = S * tm
    rows = lambda b, j: b * nt + j
    fixed2 = lambda b, j: (0, 0)
    vmem = (d * N_MLA * 2 + 2 * (rt * d * 2 + MLA_HEADS * MLA_NOPE * MLA_KV_RANK * 2
                                 + rt * (MLA_KV_RANK + 3 * LANES) * 4 + (MLA_HEADS + 2) * rt * MLA_QK * 2)
            + 2 * rt * N_MLA * 4 + 8 * 2 ** 20)
    out_specs = [pl.BlockSpec((S, tm, MLA_KV_RANK), lambda b, j: (b, j, 0)),
                 pl.BlockSpec((S, tm, MLA_ROPE), lambda b, j: (b, j, 0)),
                 pl.BlockSpec((S, MLA_HEADS, tm, MLA_QK), lambda b, j: (b, 0, j, 0)),
                 pl.BlockSpec((S, tm, MLA_QK), lambda b, j: (b, j, 0)),
                 pl.BlockSpec((rt, 2 * LANES), lambda b, j: (rows(b, j), 0))]
    out_shape = [jax.ShapeDtypeStruct((nb, t, MLA_KV_RANK), F32),
                 jax.ShapeDtypeStruct((nb, t, MLA_ROPE), F32),
                 jax.ShapeDtypeStruct((nb, MLA_HEADS, t, MLA_QK), BF16),
                 jax.ShapeDtypeStruct((nb, t, MLA_QK), BF16),
                 jax.ShapeDtypeStruct((nb * t, 2 * LANES), F32)]
    if emit_vt:
        out_specs.append(pl.BlockSpec((1, MLA_KV_RANK, tm), lambda b, j: (b, 0, j)))
        out_shape.append(jax.ShapeDtypeStruct((nb, MLA_KV_RANK, t), BF16))
    return pl.pallas_call(
        functools.partial(_mla_prep_kernel, S=S, tm=tm, pos0=pos0),
        grid=(nb // S, nt),
        in_specs=[pl.BlockSpec((rt, d), lambda b, j: (rows(b, j), 0)),
                  pl.BlockSpec((d, N_MLA), fixed2, pipeline_mode=pl.Buffered(1)),
                  pl.BlockSpec((1, LANES), fixed2),
                  pl.BlockSpec((MLA_HEADS, MLA_NOPE, MLA_KV_RANK), lambda b, j: (0, 0, 0)),
                  pl.BlockSpec((1, MLA_KV_RANK), fixed2),
                  pl.BlockSpec((1, LANES), fixed2),
                  pl.BlockSpec((1, LANES), fixed2)],
        out_specs=out_specs,
        out_shape=out_shape,
        compiler_params=_params(("parallel", "parallel"), vmem),
        name="mla_proj_prep",
    )(hm, w_mla, inv128, wuk_t, kvn, alog, dtb)


ATTN_GROUPS = 2


def _attn_kernel(q_ref, k_ref, vt_ref, wuv_ref, o_ref, s0_ref, s1_ref, m_ref, l_ref, acc_ref,
                 *, tq, tk, n_valid):
    i = pl.program_id(1)
    cols = MLA_HEADS * tq
    cg = cols // ATTN_GROUPS
    q_first = i * tq
    c = lax.broadcasted_iota(jnp.int32, (1, cg), 1)
    qchunk = jnp.right_shift(q_first + jnp.bitwise_and(c, tq - 1), LOG2_CHUNK)
    k_all = jnp.minimum((q_first // CHUNK + 1) * CHUNK, n_valid)
    k_any = jnp.minimum(((q_first + tq - 1) // CHUNK + 1) * CHUNK, n_valid)
    nk = (k_any + tk - 1) // tk
    m_ref[...] = jnp.full(m_ref.shape, -jnp.inf, F32)
    l_ref[...] = jnp.zeros(l_ref.shape, F32)
    acc_ref[...] = jnp.zeros(acc_ref.shape, F32)

    def scores(j, buf):
        start = pl.multiple_of(j * tk, tk)
        q = q_ref[0].reshape(cols, MLA_QK)
        buf[...] = _dot_nt(k_ref[0, pl.ds(start, tk), :], q)

    def fold(x, op):
        while x.shape[0] > SUBLANES:
            half = x.shape[0] // 2
            x = op(x[:half], x[half:])
        return x

    def softmax_values(j, buf, masked):
        start = pl.multiple_of(j * tk, tk)
        vt = vt_ref[0, :, pl.ds(start, tk)]
        if masked:
            kpos = start + lax.broadcasted_iota(jnp.int32, (tk, 1), 0)
            mask = jnp.logical_and(jnp.right_shift(kpos, LOG2_CHUNK) <= qchunk, kpos < n_valid)
        for g in range(ATTN_GROUPS):
            cs = slice(g * cg, (g + 1) * cg)
            s = buf[:, cs]
            if masked:
                s = jnp.where(mask, s, -jnp.inf)
            m_old = m_ref[:, cs]
            m_new = jnp.maximum(m_old, jnp.max(fold(s, jnp.maximum), axis=0, keepdims=True))
            alpha = jnp.exp(m_old - m_new)
            p = jnp.exp(s - m_new)
            l_ref[:, cs] = alpha * l_ref[:, cs] + jnp.sum(fold(p, jnp.add), axis=0, keepdims=True)
            acc_ref[:, cs] = alpha * acc_ref[:, cs] + _dot(vt, p.astype(BF16))
            m_ref[:, cs] = m_new

    def by_parity(j, fn):
        @pl.when(jnp.bitwise_and(j, 1) == 0)
        def _():
            fn(s0_ref, s1_ref)

        @pl.when(jnp.bitwise_and(j, 1) == 1)
        def _():
            fn(s1_ref, s0_ref)

    scores(0, s0_ref)

    def body(j, carry, masked):
        def step(cur, nxt):
            scores(j + 1, nxt)
            softmax_values(j, cur, masked)
        by_parity(j, step)
        return carry

    n_open = jnp.minimum(k_all // tk, nk - 1)
    lax.fori_loop(0, n_open, functools.partial(body, masked=False), 0)
    lax.fori_loop(n_open, nk - 1, functools.partial(body, masked=True), 0)
    by_parity(nk - 1, lambda cur, nxt: softmax_values(nk - 1, cur, True))
    inv_l = 1.0 / l_ref[...]
    for h in range(MLA_HEADS):
        hs = slice(h * tq, (h + 1) * tq)
        lat_t = (acc_ref[:, hs] * inv_l[:, hs]).astype(BF16)
        o_ref[:, h * MLA_V:(h + 1) * MLA_V] = _dot_tn(lat_t, wuv_ref[h]).astype(BF16)


def _attn_cache_kernel(q_ref, ckv_ref, kr_ref, knew_ref, wuv_ref, o_ref, m_ref, l_ref, acc_ref,
                       *, tq, tk, nkb, past):
    j = pl.program_id(1)
    rows = MLA_HEADS * tq
    q = q_ref[0].reshape(rows, MLA_QK)
    r = lax.broadcasted_iota(jnp.int32, (rows, 1), 0)
    qchunk = jnp.right_shift(past + jnp.bitwise_and(r, tq - 1), LOG2_CHUNK)

    @pl.when(j == 0)
    def _():
        m_ref[...] = jnp.full(m_ref.shape, -jnp.inf, F32)
        l_ref[...] = jnp.zeros(l_ref.shape, F32)
        acc_ref[...] = jnp.zeros(acc_ref.shape, F32)

    def fold(x, op):
        while x.shape[1] > LANES:
            half = x.shape[1] // 2
            x = op(x[:, :half], x[:, half:])
        return x

    def update(s, kpos, vals):
        s = jnp.where(jnp.right_shift(kpos, LOG2_CHUNK) <= qchunk, s, -jnp.inf)
        m_old = m_ref[...]
        m_new = jnp.maximum(m_old, jnp.max(fold(s, jnp.maximum), axis=-1, keepdims=True))
        alpha = jnp.exp(m_old - m_new)
        p = jnp.exp(s - m_new)
        l_ref[...] = alpha * l_ref[...] + jnp.sum(fold(p, jnp.add), axis=-1, keepdims=True)
        acc_ref[...] = alpha * acc_ref[...] + _dot(p.astype(BF16), vals)
        m_ref[...] = m_new

    ck = ckv_ref[0].astype(BF16)
    s = (_dot_nt(q[:, :MLA_KV_RANK], ck)
         + _dot_nt(q[:, MLA_KV_RANK:MLA_KV_RANK + MLA_ROPE], kr_ref[0].astype(BF16)))
    update(s, j * tk + lax.broadcasted_iota(jnp.int32, (1, tk), 1), ck)

    @pl.when(j == nkb - 1)
    def _():
        kn = knew_ref[0]
        update(_dot_nt(q, kn), past + lax.broadcasted_iota(jnp.int32, (1, tq), 1), kn[:, :MLA_KV_RANK])
        inv_l = 1.0 / l_ref[...]
        for h in range(MLA_HEADS):
            hs = slice(h * tq, (h + 1) * tq)
            lat = (acc_ref[hs, :] * inv_l[hs, :]).astype(BF16)
            o_ref[:, h * MLA_V:(h + 1) * MLA_V] = _dot(lat, wuv_ref[h]).astype(BF16)


def _attention(qcat, kcat, vt, wuv_t):
    nb, _, t, _ = qcat.shape
    tq = _tile(t, 256)
    tk = _tile(t, 512)
    assert tq & (tq - 1) == 0
    nq = t // tq
    cols = MLA_HEADS * tq
    vmem = (2 * (cols * MLA_QK * 2 + t * MLA_QK * 2 + MLA_KV_RANK * t * 2
                 + MLA_HEADS * MLA_KV_RANK * MLA_V * 2 + tq * MLA_HEADS * MLA_V * 2)
            + 2 * tk * cols * 4 + 2 * SUBLANES * cols * 4 + MLA_KV_RANK * cols * 4
            + 3 * tk * cols * 4 // ATTN_GROUPS + 4 * 2 ** 20)
    return pl.pallas_call(
        functools.partial(_attn_kernel, tq=tq, tk=tk, n_valid=t),
        grid=(nb, nq),
        in_specs=[pl.BlockSpec((1, MLA_HEADS, tq, MLA_QK), lambda b, i: (b, 0, i, 0)),
                  pl.BlockSpec((1, t, MLA_QK), lambda b, i: (b, 0, 0)),
                  pl.BlockSpec((1, MLA_KV_RANK, t), lambda b, i: (b, 0, 0)),
                  pl.BlockSpec((MLA_HEADS, MLA_KV_RANK, MLA_V), lambda b, i: (0, 0, 0))],
        out_specs=pl.BlockSpec((tq, MLA_HEADS * MLA_V), lambda b, i: (b * nq + i, 0)),
        out_shape=jax.ShapeDtypeStruct((nb * t, MLA_HEADS * MLA_V), BF16),
        scratch_shapes=[pltpu.VMEM((tk, cols), F32), pltpu.VMEM((tk, cols), F32), pltpu.VMEM((1, cols), F32),
                        pltpu.VMEM((1, cols), F32), pltpu.VMEM((MLA_KV_RANK, cols), F32)],
        compiler_params=_params(("parallel", "arbitrary"), vmem),
        name="mla_attention",
    )(qcat, kcat, vt, wuv_t)


def _attention_cached(qcat, ckv_past, krope_past, kcat_new, wuv_t):
    nb, _, t, _ = qcat.shape
    past = ckv_past.shape[1]
    tk = _tile(past, 4096)
    nkb = past // tk
    assert t & (t - 1) == 0
    rows = MLA_HEADS * t
    vmem = (2 * (rows * MLA_QK * 2 + tk * MLA_KV_RANK * 4 + tk * LANES * 4 + t * MLA_QK * 2
                 + MLA_HEADS * MLA_KV_RANK * MLA_V * 2 + t * MLA_HEADS * MLA_V * 2)
            + 2 * rows * LANES * 4 + rows * MLA_KV_RANK * 4 + tk * MLA_QK * 2 + 4 * rows * tk * 4 + 4 * 2 ** 20)
    return pl.pallas_call(
        functools.partial(_attn_cache_kernel, tq=t, tk=tk, nkb=nkb, past=past),
        grid=(nb, nkb),
        in_specs=[pl.BlockSpec((1, MLA_HEADS, t, MLA_QK), lambda b, j: (b, 0, 0, 0)),
                  pl.BlockSpec((1, tk, MLA_KV_RANK), lambda b, j: (b, j, 0)),
                  pl.BlockSpec((1, tk, MLA_ROPE), lambda b, j: (b, j, 0)),
                  pl.BlockSpec((1, t, MLA_QK), lambda b, j: (b, 0, 0)),
                  pl.BlockSpec((MLA_HEADS, MLA_KV_RANK, MLA_V), lambda b, j: (0, 0, 0))],
        out_specs=pl.BlockSpec((t, MLA_HEADS * MLA_V), lambda b, j: (b, 0)),
        out_shape=jax.ShapeDtypeStruct((nb * t, MLA_HEADS * MLA_V), BF16),
        scratch_shapes=[pltpu.VMEM((rows, 1), F32), pltpu.VMEM((rows, 1), F32),
                        pltpu.VMEM((rows, MLA_KV_RANK), F32)],
        compiler_params=_params(("parallel", "arbitrary"), vmem),
        name="mla_attention_cached",
    )(qcat, ckv_past, krope_past, kcat_new, wuv_t)


def _merge_out_kernel(og_ref, om_ref, wg_ref, wm_ref, gg_ref, gm_ref, wo_ref, x_ref, gpost_ref, gnext_ref,
                      xo_ref, xno_ref):
    tg = _dot(og_ref[...], wg_ref[...])
    tm_ = _dot(om_ref[...], wm_ref[...])
    merged = (jax.nn.sigmoid(gg_ref[...].astype(F32)) * tg
              + jax.nn.sigmoid(gm_ref[...].astype(F32)) * tm_).astype(BF16)
    y = _dot(merged, wo_ref[...])
    xnew = x_ref[...] + _rms(y, gpost_ref[...])
    xo_ref[...] = xnew
    xno_ref[...] = _rms(xnew, gnext_ref[...]).astype(BF16)


def _merge_out(og, om, wbg, wbm, proj_gate, wout, x, gpost, gnext):
    m, kg = og.shape
    d = wbg.shape[1]
    tm = _tile(m, 256)
    gsz = proj_gate.dtype.itemsize
    row = lambda i: (i, 0)
    fixed = lambda i: (0, 0)
    resident = dict(pipeline_mode=pl.Buffered(1))
    vmem = ((2 * kg * d + d * d) * 2
            + 2 * (2 * tm * kg * 2 + 2 * tm * d * gsz + 2 * tm * d * 4 + tm * d * 2)
            + 4 * tm * d * 4 + 2 * 2 ** 20)
    return pl.pallas_call(
        _merge_out_kernel,
        grid=(m // tm,),
        in_specs=[pl.BlockSpec((tm, kg), row),
                  pl.BlockSpec((tm, kg), row),
                  pl.BlockSpec((kg, d), fixed, **resident),
                  pl.BlockSpec((kg, d), fixed, **resident),
                  pl.BlockSpec((tm, d), lambda i: (i, 0)),
                  pl.BlockSpec((tm, d), lambda i: (i, 1)),
                  pl.BlockSpec((d, d), fixed, **resident),
                  pl.BlockSpec((tm, d), row),
                  pl.BlockSpec((1, d), fixed),
                  pl.BlockSpec((1, d), fixed)],
        out_specs=[pl.BlockSpec((tm, d), row), pl.BlockSpec((tm, d), row)],
        out_shape=[jax.ShapeDtypeStruct((m, d), F32), jax.ShapeDtypeStruct((m, d), BF16)],
        compiler_params=_params(("parallel",), vmem),
        name="merge_out",
    )(og, om, wbg, wbm, proj_gate, proj_gate, wout, x, gpost, gnext)


def _layer(x3, pos0, conv_state, ssm0, ckv_past, krope_past, w):
    nb, t, d = x3.shape
    m = nb * t
    x = x3.reshape(m, d)
    x1, hm, ffn1_w = _ffn(x, *w["ffn1_w"], w["ffn1_norm_post"],
                          gpre=w["ffn1_norm_pre"], gnext=w["mix_norm_pre"])
    proj_gdn = _matmul(hm, w["w_in_gdn"], F32)
    proj_gate = _matmul(hm, w["w_in_gate"], BF16)
    ckv, krope, qcat, kcat, bg, *vt = _mla_prep(hm, w["w_in_mla"], nb, t, pos0, w["inv128"], w["wuk_t"],
                                                w["kv_norm"], w["alog"], w["dtb"], ckv_past is None)

    o_gdn, ssm_new = _gdn(proj_gdn, bg, nb, t, conv_state, ssm0, w["conv_w"], w["gdn_nw"])
    conv_new = proj_gdn.reshape(nb, t, -1)[:, t - (GDN_CONV - 1):, COL_QKV:COL_QKV + GDN_CONV_DIM]

    if ckv_past is None:
        o_mla = _attention(qcat, kcat, vt[0], w["wuv_t"])
    else:
        o_mla = _attention_cached(qcat, ckv_past, krope_past, kcat, w["wuv_t"])

    x2, xn2 = _merge_out(o_gdn, o_mla, w["w_br_gdn"], w["w_br_mla"], proj_gate, w["w_out"], x1,
                         w["mix_norm_post"], w["ffn2_norm_pre"])
    x3_, _, ffn2_w = _ffn(x2, *w["ffn2_w"], w["ffn2_norm_post"], xn=xn2)
    return (x3_.reshape(nb, t, d), conv_new, ssm_new, ckv, krope), dict(ffn1_w=ffn1_w, ffn2_w=ffn2_w)


def _prep_weights(l, d_model, **p):
    row = lambda v: v[l].reshape(1, -1).astype(F32)
    pad_lanes = lambda v: jnp.pad(v[l].reshape(1, -1).astype(F32), ((0, 0), (0, LANES - v.shape[1])))
    inv = ROPE_THETA ** (-jnp.arange(0, MLA_ROPE, 2, dtype=F32) / MLA_ROPE)
    w_in_gdn, w_in_gate, w_in_mla = _build_w_in(p["w_in"][l], d_model)
    w = dict(
        w_in_gdn=w_in_gdn, w_in_gate=w_in_gate, w_in_mla=w_in_mla,
        ffn1_norm_pre=row(p["ffn1_norm_pre"]), ffn1_norm_post=row(p["ffn1_norm_post"]),
        mix_norm_pre=row(p["mix_norm_pre"]), mix_norm_post=row(p["mix_norm_post"]),
        ffn2_norm_pre=row(p["ffn2_norm_pre"]), ffn2_norm_post=row(p["ffn2_norm_post"]),
        ffn1_w=(p["ffn1_w_gate"][l], p["ffn1_w_up"][l], p["ffn1_w_down"][l]),
        ffn2_w=(p["ffn2_w_gate"][l], p["ffn2_w_up"][l], p["ffn2_w_down"][l]),
        conv_w=p["gdn_conv_w"][l].astype(F32),
        alog=pad_lanes(p["gdn_a_log"]), dtb=pad_lanes(p["gdn_dt_bias"]),
        gdn_nw=row(p["gdn_norm_w"]), kv_norm=row(p["mla_kv_norm"]),
        inv128=jnp.tile(inv, LANES // inv.shape[0]).reshape(1, LANES),
        wuk_t=jnp.transpose(p["mla_w_uk"][l], (1, 2, 0)).astype(BF16),
        wuv_t=jnp.transpose(p["mla_w_uv"][l], (1, 0, 2)).astype(BF16),
        w_br_gdn=p["w_br_gdn"][l].astype(BF16), w_br_mla=p["w_br_mla"][l].astype(BF16),
        w_out=p["w_out"][l].astype(BF16),
    )
    return w


def kernel(x_prompt, x_sample, state_gdn_conv, state_gdn_ssm, cache_mla_ckv, cache_mla_krope, ffn1_norm_pre, ffn1_w_gate, ffn1_w_up, ffn1_w_down, ffn1_norm_post, mix_norm_pre, w_in, gdn_conv_w, gdn_a_log, gdn_dt_bias, gdn_norm_w, mla_kv_norm, mla_w_uk, mla_w_uv, w_br_gdn, w_br_mla, w_out, mix_norm_post, ffn2_norm_pre, ffn2_w_gate, ffn2_w_up, ffn2_w_down, ffn2_norm_post):
    params = dict(
        ffn1_norm_pre=ffn1_norm_pre, ffn1_w_gate=ffn1_w_gate, ffn1_w_up=ffn1_w_up,
        ffn1_w_down=ffn1_w_down, ffn1_norm_post=ffn1_norm_post, mix_norm_pre=mix_norm_pre,
        w_in=w_in, gdn_conv_w=gdn_conv_w, gdn_a_log=gdn_a_log, gdn_dt_bias=gdn_dt_bias,
        gdn_norm_w=gdn_norm_w, mla_kv_norm=mla_kv_norm, mla_w_uk=mla_w_uk, mla_w_uv=mla_w_uv,
        w_br_gdn=w_br_gdn, w_br_mla=w_br_mla, w_out=w_out, mix_norm_post=mix_norm_post,
        ffn2_norm_pre=ffn2_norm_pre, ffn2_w_gate=ffn2_w_gate, ffn2_w_up=ffn2_w_up,
        ffn2_w_down=ffn2_w_down, ffn2_norm_post=ffn2_norm_post)
    depth = w_in.shape[0]
    d_model = x_prompt.shape[-1]
    b_p = x_prompt.shape[0]
    past = cache_mla_ckv.shape[2]
    yp, ys = x_prompt, x_sample
    outs_p, outs_s = [], []
    for l in range(depth):
        w = _prep_weights(l, d_model, **params)
        (ys, *rest_s), w_bf16 = _layer(
            ys, past, state_gdn_conv[l], state_gdn_ssm[l], cache_mla_ckv[l], cache_mla_krope[l], w)
        (yp, *rest_p), _ = _layer(
            yp, 0, jnp.zeros((b_p, GDN_CONV - 1, GDN_CONV_DIM), F32),
            jnp.zeros((b_p, GDN_HEADS, GDN_DK, GDN_DV), F32), None, None, {**w, **w_bf16})
        outs_p.append(rest_p)
        outs_s.append(rest_s)
    stack = lambda outs, i: jnp.stack([o[i] for o in outs])
    return (yp, ys,
            stack(outs_p, 0), stack(outs_p, 1), stack(outs_p, 2), stack(outs_p, 3),
            stack(outs_s, 0), stack(outs_s, 1), stack(outs_s, 2), stack(outs_s, 3))
```

```python
import functools

import jax
import jax.numpy as jnp
from jax import lax
from jax.experimental import pallas as pl
from jax.experimental.pallas import tpu as pltpu

F32 = jnp.float32
BF16 = jnp.bfloat16

CHUNK = 64
LOG2_CHUNK = 6
NORM_EPS = 1e-6
GDN_HEADS = 8
GDN_DK = 128
GDN_DV = 128
GDN_CONV = 4
GDN_KEY_DIM = GDN_HEADS * GDN_DK
GDN_VAL_DIM = GDN_HEADS * GDN_DV
GDN_CONV_DIM = 2 * GDN_KEY_DIM + GDN_VAL_DIM
MLA_HEADS = 8
MLA_NOPE = 128
MLA_ROPE = 64
MLA_V = 128
MLA_KV_RANK = 512
MLA_SCALE = (MLA_NOPE + MLA_ROPE) ** -0.5
ROPE_THETA = 10000.0

LANES = 128
SUBLANES = 8
VMEM_CAP_BYTES = 56 * 2 ** 20
MLA_QK = MLA_KV_RANK + LANES
HI = lax.Precision.HIGHEST


def _params(semantics, vmem_bytes):
    limit = int(min(max(vmem_bytes, 16 * 2 ** 20), VMEM_CAP_BYTES))
    return pltpu.CompilerParams(dimension_semantics=semantics, vmem_limit_bytes=limit)


def _dot(a, b, prec=None):
    return lax.dot_general(a, b, (((1,), (0,)), ((), ())), precision=prec,
                           preferred_element_type=F32)


def _dot_nt(a, b, prec=None):
    return lax.dot_general(a, b, (((1,), (1,)), ((), ())), precision=prec,
                           preferred_element_type=F32)


def _dot_tn(a, b, prec=None):
    return lax.dot_general(a, b, (((0,), (0,)), ((), ())), precision=prec,
                           preferred_element_type=F32)


def _rms(y, g):
    return y * lax.rsqrt(jnp.mean(y * y, axis=-1, keepdims=True) + NORM_EPS) * g


def _silu(x):
    return x * jax.nn.sigmoid(x)


def _tile(n, pref):
    t = min(n, pref)
    assert n % t == 0, (n, pref)
    return t


def _ffn_kernel(*refs, nf, norm_in, emit_next, cast_w):
    refs = list(refs)
    xn_ref = None if norm_in else refs.pop(0)
    x_ref, wg_ref, wu_ref, wd_ref = refs[:4]
    del refs[:4]
    gpre_ref = refs.pop(0) if norm_in else None
    gpost_ref = refs.pop(0)
    gnext_ref = refs.pop(0) if emit_next else None
    xo_ref = refs.pop(0)
    xno_ref = refs.pop(0) if emit_next else None
    if cast_w:
        wgo_ref, wuo_ref, wdo_ref = refs[:3]
        del refs[:3]
    acc_ref = refs.pop(0)
    if norm_in:
        xn_ref = refs.pop(0)
    f = pl.program_id(1)

    @pl.when(f == 0)
    def _():
        acc_ref[...] = jnp.zeros(acc_ref.shape, F32)
        if norm_in:
            xn_ref[...] = _rms(x_ref[...], gpre_ref[...]).astype(BF16)

    xn = xn_ref[...]
    wg, wu, wd = wg_ref[...], wu_ref[...], wd_ref[...]
    if cast_w:
        wg, wu, wd = wg.astype(BF16), wu.astype(BF16), wd.astype(BF16)
        wgo_ref[...] = wg
        wuo_ref[...] = wu
        wdo_ref[...] = wd
    gate = _dot(xn, wg)
    up = _dot(xn, wu)
    h = (_silu(gate) * up).astype(BF16)
    acc_ref[...] += _dot(h, wd)

    @pl.when(f == nf - 1)
    def _():
        xnew = x_ref[...] + 0.5 * _rms(acc_ref[...], gpost_ref[...])
        xo_ref[...] = xnew
        if emit_next:
            xno_ref[...] = _rms(xnew, gnext_ref[...]).astype(BF16)


def _ffn(x, wg, wu, wd, gpost, xn=None, gpre=None, gnext=None):
    norm_in = xn is None
    emit_next = gnext is not None
    cast_w = wg.dtype == F32
    assert norm_in == (gpre is not None)
    m, d = x.shape
    dff = wg.shape[1]
    tm = _tile(m, 512)
    tf = _tile(dff, 256 if cast_w else 512)
    nf = dff // tf
    assert not cast_w or m == tm
    wsz = wg.dtype.itemsize
    row = lambda i, f: (i, 0)
    vec = pl.BlockSpec((1, d), lambda i, f: (0, 0))
    w_specs = [pl.BlockSpec((d, tf), lambda i, f: (0, f)),
               pl.BlockSpec((d, tf), lambda i, f: (0, f)),
               pl.BlockSpec((tf, d), lambda i, f: (f, 0))]
    ins = [] if norm_in else [xn]
    in_specs = [] if norm_in else [pl.BlockSpec((tm, d), row)]
    ins += [x, wg, wu, wd]
    in_specs += [pl.BlockSpec((tm, d), row)] + w_specs
    for g in (gpre, gpost, gnext):
        if g is not None:
            ins.append(g)
            in_specs.append(vec)
    out_specs = [pl.BlockSpec((tm, d), row)]
    out_shape = [jax.ShapeDtypeStruct((m, d), F32)]
    scratch = [pltpu.VMEM((tm, d), F32)]
    if emit_next:
        out_specs.append(pl.BlockSpec((tm, d), row))
        out_shape.append(jax.ShapeDtypeStruct((m, d), BF16))
    if cast_w:
        out_specs += w_specs
        out_shape += [jax.ShapeDtypeStruct(w.shape, BF16) for w in (wg, wu, wd)]
    if norm_in:
        scratch.append(pltpu.VMEM((tm, d), BF16))
    vmem = (2 * (tm * d * 2 + tm * d * 4 + 3 * d * tf * wsz + tm * d * 4 + tm * d * 2)
            + (2 * 3 * d * tf * 2 + 3 * d * tf * 2 if cast_w else 0)
            + tm * d * 4 + 3 * tm * tf * 4 + 2 * tm * d * 4)
    res = pl.pallas_call(
        functools.partial(_ffn_kernel, nf=nf, norm_in=norm_in, emit_next=emit_next, cast_w=cast_w),
        grid=(m // tm, nf),
        in_specs=in_specs,
        out_specs=out_specs,
        out_shape=out_shape,
        scratch_shapes=scratch,
        compiler_params=_params(("parallel", "arbitrary"), vmem),
        name="ffn_cast" if cast_w else "ffn",
    )(*ins)
    res = list(res)
    x_new = res.pop(0)
    xn_next = res.pop(0) if emit_next else None
    return x_new, xn_next, (tuple(res) if cast_w else (wg, wu, wd))


def _mm_kernel(a_ref, w_ref, o_ref):
    o_ref[...] = _dot(a_ref[...], w_ref[...]).astype(o_ref.dtype)


def _matmul(a, w, out_dtype):
    m, k = a.shape
    n = w.shape[1]
    tm = _tile(m, 1024)
    tn = _tile(n, 2048)
    osz = jnp.dtype(out_dtype).itemsize
    vmem = 2 * (tm * k * 2 + k * tn * 2 + tm * tn * osz) + tm * tn * 4
    return pl.pallas_call(
        _mm_kernel,
        grid=(m // tm, n // tn),
        in_specs=[pl.BlockSpec((tm, k), lambda i, j: (i, 0)),
                  pl.BlockSpec((k, tn), lambda i, j: (0, j))],
        out_specs=pl.BlockSpec((tm, tn), lambda i, j: (i, j)),
        out_shape=jax.ShapeDtypeStruct((m, n), out_dtype),
        compiler_params=_params(("parallel", "arbitrary"), vmem),
        name="in_proj",
    )(a, w)


COL_QKV = 0
COL_Z = COL_QKV + GDN_CONV_DIM
COL_QNOPE = 0
COL_QROPE = COL_QNOPE + MLA_HEADS * MLA_NOPE
COL_CKV = COL_QROPE + MLA_HEADS * LANES
COL_KR = COL_CKV + MLA_KV_RANK
COL_B = COL_KR + LANES
COL_A = COL_B + LANES
N_MLA = 3072


def _build_w_in(w_in, d_model):
    sizes = (GDN_CONV_DIM, GDN_VAL_DIM, GDN_HEADS, GDN_HEADS, MLA_HEADS * (MLA_NOPE + MLA_ROPE),
             MLA_KV_RANK, MLA_ROPE, d_model, d_model)
    offs = [0]
    for s in sizes:
        offs.append(offs[-1] + s)
    w_gdn = w_in[:, :offs[2]].astype(BF16)
    w_gate = w_in[:, offs[7]:].astype(BF16)
    part = lambda i: w_in[:, offs[i]:offs[i + 1]].astype(BF16)
    b, a, qm, ckv, kr = (part(i) for i in range(2, 7))
    d = w_in.shape[0]
    half = MLA_ROPE // 2
    qm = qm.reshape(d, MLA_HEADS, MLA_NOPE + MLA_ROPE)
    qn = qm[:, :, :MLA_NOPE].reshape(d, MLA_HEADS * MLA_NOPE)
    qr = qm[:, :, MLA_NOPE:]
    qr_pair = jnp.concatenate([qr, qr[:, :, half:], qr[:, :, :half]], axis=2).reshape(d, MLA_HEADS * LANES)
    kr_pair = jnp.concatenate([kr, kr[:, half:], kr[:, :half]], axis=1)
    zeros = lambda n: jnp.zeros((d, n), BF16)
    w_mla = jnp.concatenate([qn, qr_pair, ckv, kr_pair, b, zeros(LANES - GDN_HEADS), a, zeros(LANES - GDN_HEADS),
                             zeros(N_MLA - COL_A - LANES)], axis=1)
    return w_gdn, w_gate, w_mla


_NN = (((1,), (0,)), ((), ()))
_NT = (((1,), (1,)), ((), ()))
_TN = (((0,), (0,)), ((), ()))
INV_BLOCK = 16


def _mm(a, b, dims=_NN):
    return lax.dot_general(a.astype(BF16), b.astype(BF16), dims, preferred_element_type=F32)


def _gdn_kernel(qkv_ref, z_ref, beta_ref, g_ref, cst_ref, cw_ref, nw_ref, s0_ref,
                o_ref, sout_ref, past_ref, s_ref, *, G, C, nlev, nc):
    c = pl.program_id(1)

    @pl.when(c == 0)
    def _():
        past_ref[...] = cst_ref[...]
        s_ref[...] = s0_ref[...]

    ri = lax.broadcasted_iota(jnp.int32, (C, C), 0)
    ci = lax.broadcasted_iota(jnp.int32, (C, C), 1)
    incl = ri >= ci
    strict = ri > ci
    eye = (ri == ci).astype(F32)
    log2_blk = INV_BLOCK.bit_length() - 1
    bdiag = jnp.right_shift(ri, log2_blk) == jnp.right_shift(ci, log2_blk)

    first_row = lax.broadcasted_iota(jnp.int32, (C, 1), 0) == 0

    def conv(s, col):
        cols = slice(col, col + LANES)
        x = qkv_ref[s, :, cols]
        past = past_ref[s, :, cols]
        w = [cw_ref[j:j + 1, cols] for j in range(GDN_CONV)]
        acc = w[0] * x
        for j in range(1, GDN_CONV):
            top = sum(w[i] * past[SUBLANES - 1 - (j - 1 - i):SUBLANES - (j - 1 - i)] for i in range(j))
            acc = w[j] * x + jnp.where(first_row, top, pltpu.roll(acc, 1, axis=0))
        return _silu(acc)

    units = [(s, h) for s in range(G) for h in range(GDN_HEADS)]
    every = lambda fn, *lists: [fn(*args) for args in zip(*lists)]
    mm_inv = _mm
    beta_all, gcum, gcum_t = [], [], []
    for s in range(G):
        beta_all.append(beta_ref[s])
        g_all = g_ref[s]
        gcum.append(_dot(incl.astype(F32), g_all, HI))
        gcum_t.append(_dot_tn(g_all, (ci >= ri).astype(F32), HI))

    def l2n(x):
        return x * lax.rsqrt(jnp.sum(x * x, axis=-1, keepdims=True) + 1e-6)

    q = [l2n(conv(s, h * GDN_DK)) * (GDN_DK ** -0.5) for s, h in units]
    k = [l2n(conv(s, GDN_KEY_DIM + h * GDN_DK)) for s, h in units]
    v = [conv(s, 2 * GDN_KEY_DIM + h * GDN_DV) for s, h in units]
    bcol = [beta_all[s][:, h:h + 1] for s, h in units]
    gcol = [gcum[s][:, h:h + 1] for s, h in units]
    glast = [gcum[s][C - 1:C, h:h + 1] for s, h in units]
    decay = [jnp.where(incl, jnp.exp(jnp.where(incl, gcum[s][:, h:h + 1] - gcum_t[s][h:h + 1, :], 0.0)), 0.0)
             for s, h in units]
    qkk = every(lambda q_, k_: _mm(jnp.concatenate([q_, k_], axis=0), k_, _NT), q, k)
    qk = every(lambda x, d: x[:C] * d, qkk, decay)
    nmat = every(lambda x, b_, d: -jnp.where(strict, b_ * x[C:] * d, 0.0), qkk, bcol, decay)
    ndiag = every(lambda n_: jnp.where(bdiag, n_, 0.0), nmat)
    tinv = every(lambda n_: eye + n_, ndiag)
    npow = ndiag
    for _ in range(INV_BLOCK.bit_length() - 2):
        npow = every(mm_inv, npow, npow)
        tinv = every(lambda t_, p_: t_ + mm_inv(t_, p_), tinv, npow)
    mpow = every(lambda t_, n_, d_: mm_inv(t_, n_ - d_), tinv, nmat, ndiag)
    for lev in range(nlev):
        if lev:
            mpow = every(mm_inv, mpow, mpow)
        tinv = every(lambda t_, m_: t_ + mm_inv(m_, t_), tinv, mpow)
    egc = every(jnp.exp, gcol)
    sol = every(lambda t_, v_, k_, b_, e_: _mm(t_, jnp.concatenate([v_ * b_, k_ * (b_ * e_)], axis=1)),
                tinv, v, k, bcol, egc)
    s_old = [s_ref[s, h] for s, h in units]
    ws_qs = every(lambda x, q_, e_, s_: _mm(jnp.concatenate([x[:, GDN_DV:], q_ * e_], axis=0), s_),
                  sol, q, egc, s_old)
    v_new = every(lambda x, y: x[:, :GDN_DV] - y[:C], sol, ws_qs)
    o = every(lambda y, a_, vn: y[C:] + _mm(a_, vn), ws_qs, qk, v_new)
    s_new = every(lambda s_, gl, k_, gc, vn: s_ * jnp.exp(gl) + _mm(k_ * jnp.exp(gl - gc), vn, _TN),
                  s_old, glast, k, gcol, v_new)
    for (s, h), sn, o_ in zip(units, s_new, o):
        s_ref[s, h] = sn
        zz = z_ref[s, :, h * GDN_DV:(h + 1) * GDN_DV]
        o_ref[s, :, h * GDN_DV:(h + 1) * GDN_DV] = (_rms(o_, nw_ref[...]) * _silu(zz)).astype(BF16)

    past_ref[...] = qkv_ref[:, C - SUBLANES:C, :]

    @pl.when(c == nc - 1)
    def _():
        sout_ref[...] = s_ref[...]


def _gdn(proj_gdn, bg, nb, t, conv_state, ssm0, conv_w, nw):
    C = min(CHUNK, t)
    G = 2 if nb % 2 == 0 else 1
    assert t % C == 0 and C & (C - 1) == 0 and C % INV_BLOCK == 0
    nc = t // C
    nlev = (C // INV_BLOCK).bit_length() - 1
    pg = proj_gdn.reshape(nb, t, proj_gdn.shape[1])
    pm = bg.reshape(nb, t, bg.shape[1])
    fixed = lambda b, c: (0, 0)
    cst = jnp.pad(conv_state, ((0, 0), (SUBLANES - (GDN_CONV - 1), 0), (0, 0)))
    vmem = G * (2 * (C * GDN_CONV_DIM * 4 + C * GDN_VAL_DIM * 4 + 2 * C * LANES * 4 + SUBLANES * GDN_CONV_DIM * 4
                     + C * GDN_VAL_DIM * 2 + 2 * GDN_HEADS * GDN_DK * GDN_DV * 4)
                + SUBLANES * GDN_CONV_DIM * 4 + GDN_HEADS * GDN_DK * GDN_DV * 4) + 16 * 2 ** 20
    o, s_new = pl.pallas_call(
        functools.partial(_gdn_kernel, G=G, C=C, nlev=nlev, nc=nc),
        grid=(nb // G, nc),
        in_specs=[pl.BlockSpec((G, C, GDN_CONV_DIM), lambda b, c: (b, c, COL_QKV // GDN_CONV_DIM)),
                  pl.BlockSpec((G, C, GDN_VAL_DIM), lambda b, c: (b, c, COL_Z // GDN_VAL_DIM)),
                  pl.BlockSpec((G, C, LANES), lambda b, c: (b, c, 0)),
                  pl.BlockSpec((G, C, LANES), lambda b, c: (b, c, 1)),
                  pl.BlockSpec((G, SUBLANES, GDN_CONV_DIM), lambda b, c: (b, 0, 0)),
                  pl.BlockSpec((GDN_CONV, GDN_CONV_DIM), fixed),
                  pl.BlockSpec((1, GDN_DV), fixed),
                  pl.BlockSpec((G, GDN_HEADS, GDN_DK, GDN_DV), lambda b, c: (b, 0, 0, 0))],
        out_specs=[pl.BlockSpec((G, C, GDN_VAL_DIM), lambda b, c: (b, c, 0)),
                   pl.BlockSpec((G, GDN_HEADS, GDN_DK, GDN_DV), lambda b, c: (b, 0, 0, 0))],
        out_shape=[jax.ShapeDtypeStruct((nb, t, GDN_VAL_DIM), BF16),
                   jax.ShapeDtypeStruct((nb, GDN_HEADS, GDN_DK, GDN_DV), F32)],
        scratch_shapes=[pltpu.VMEM((G, SUBLANES, GDN_CONV_DIM), F32),
                        pltpu.VMEM((G, GDN_HEADS, GDN_DK, GDN_DV), F32)],
        compiler_params=_params(("parallel", "arbitrary"), vmem),
        name="gdn",
    )(pg, pg, pm, pm, cst, conv_w, nw, ssm0)
    return o.reshape(nb * t, GDN_VAL_DIM), s_new


def _mla_prep_kernel(hm_ref, w_ref, inv_ref, wuk_ref, kvn_ref, alog_ref, dtb_ref,
                     ckv_o, kr_o, qcat_o, kcat_o, bg_o, *maybe_vt_o, S, tm, pos0):
    j = pl.program_id(1)
    rows = S * tm
    per_seq = lambda x: x.reshape(S, tm, x.shape[-1])
    proj = _dot(hm_ref[...], w_ref[...])
    a_all = proj[:, COL_A:COL_A + LANES] + dtb_ref[...]
    softplus = jnp.maximum(a_all, 0.0) + jnp.log(1.0 + jnp.exp(-jnp.abs(a_all)))
    bg_o[:, :LANES] = jax.nn.sigmoid(proj[:, COL_B:COL_B + LANES])
    bg_o[:, LANES:] = -jnp.exp(alog_ref[...]) * softplus
    token = jnp.bitwise_and(lax.broadcasted_iota(jnp.int32, (rows, LANES), 0), tm - 1)
    pos = (pos0 + j * tm + token).astype(F32)
    lane = lax.broadcasted_iota(jnp.int32, (rows, LANES), 1)
    ang = pos * inv_ref[...]
    cos = jnp.cos(ang)
    sin = jnp.sin(ang)
    half = MLA_ROPE // 2
    cs = jnp.where(lane < MLA_ROPE, cos, jnp.where(lane < MLA_ROPE + half, -sin, sin))
    keep = lane < MLA_ROPE

    def rope(pair):
        prod = pair * cs
        return jnp.where(keep, prod + pltpu.roll(prod, MLA_ROPE, axis=1), 0.0)

    ckv = _rms(proj[:, COL_CKV:COL_CKV + MLA_KV_RANK], kvn_ref[...])
    ckv_o[...] = per_seq(ckv)
    kr = rope(proj[:, COL_KR:COL_KR + LANES])
    kr_o[...] = per_seq(kr[:, :MLA_ROPE])
    kcat_o[:, :, :MLA_KV_RANK] = per_seq(ckv.astype(BF16))
    kcat_o[:, :, MLA_KV_RANK:] = per_seq(kr.astype(BF16))
    if maybe_vt_o:
        maybe_vt_o[0][0] = ckv.T.astype(BF16)
    for h in range(MLA_HEADS):
        qn = proj[:, COL_QNOPE + h * MLA_NOPE:COL_QNOPE + (h + 1) * MLA_NOPE].astype(BF16)
        qlat = _dot(qn, wuk_ref[h]) * MLA_SCALE
        qcat_o[:, h, :, :MLA_KV_RANK] = per_seq(qlat.astype(BF16))
        qr = rope(proj[:, COL_QROPE + h * LANES:COL_QROPE + (h + 1) * LANES]) * MLA_SCALE
        qcat_o[:, h, :, MLA_KV_RANK:] = per_seq(qr.astype(BF16))


def _mla_prep(hm, w_mla, nb, t, pos0, inv128, wuk_t, kvn, alog, dtb, emit_vt):
    d = hm.shape[1]
    tm = _tile(t, 512)
    nt = t // tm
    S = max(1, min(nb, 512 // tm)) if (nt == 1 and not emit_vt) else 1
    assert nb % S == 0 and tm & (tm - 1) == 0
    rt = S * tm
    rows = lambda b, j: b * nt + j
    fixed2 = lambda b, j: (0, 0)
    vmem = (d * N_MLA * 2 + 2 * (rt * d * 2 + MLA_HEADS * MLA_NOPE * MLA_KV_RANK * 2
                                 + rt * (MLA_KV_RANK + 3 * LANES) * 4 + (MLA_HEADS + 2) * rt * MLA_QK * 2)
            + 2 * rt * N_MLA * 4 + 8 * 2 ** 20)
    out_specs = [pl.BlockSpec((S, tm, MLA_KV_RANK), lambda b, j: (b, j, 0)),
                 pl.BlockSpec((S, tm, MLA_ROPE), lambda b, j: (b, j, 0)),
                 pl.BlockSpec((S, MLA_HEADS, tm, MLA_QK), lambda b, j: (b, 0, j, 0)),
                 pl.BlockSpec((S, tm, MLA_QK), lambda b, j: (b, j, 0)),
                 pl.BlockSpec((rt, 2 * LANES), lambda b, j: (rows(b, j), 0))]
    out_shape = [jax.ShapeDtypeStruct((nb, t, MLA_KV_RANK), F32),
                 jax.ShapeDtypeStruct((nb, t, MLA_ROPE), F32),
                 jax.ShapeDtypeStruct((nb, MLA_HEADS, t, MLA_QK), BF16),
                 jax.ShapeDtypeStruct((nb, t, MLA_QK), BF16),
                 jax.ShapeDtypeStruct((nb * t, 2 * LANES), F32)]
    if emit_vt:
        out_specs.append(pl.BlockSpec((1, MLA_KV_RANK, tm), lambda b, j: (b, 0, j)))
        out_shape.append(jax.ShapeDtypeStruct((nb, MLA_KV_RANK, t), BF16))
    return pl.pallas_call(
        functools.partial(_mla_prep_kernel, S=S, tm=tm, pos0=pos0),
        grid=(nb // S, nt),
        in_specs=[pl.BlockSpec((rt, d), lambda b, j: (rows(b, j), 0)),
                  pl.BlockSpec((d, N_MLA), fixed2, pipeline_mode=pl.Buffered(1)),
                  pl.BlockSpec((1, LANES), fixed2),
                  pl.BlockSpec((MLA_HEADS, MLA_NOPE, MLA_KV_RANK), lambda b, j: (0, 0, 0)),
                  pl.BlockSpec((1, MLA_KV_RANK), fixed2),
                  pl.BlockSpec((1, LANES), fixed2),
                  pl.BlockSpec((1, LANES), fixed2)],
        out_specs=out_specs,
        out_shape=out_shape,
        compiler_params=_params(("parallel", "parallel"), vmem),
        name="mla_proj_prep",
    )(hm, w_mla, inv128, wuk_t, kvn, alog, dtb)


ATTN_GROUPS = 4


def _attn_kernel(q_ref, k_ref, vt_ref, wuv_ref, o_ref, s0_ref, s1_ref, m_ref, l_ref, acc_ref,
                 *, tq, tk, n_valid):
    i = pl.program_id(1)
    cols = MLA_HEADS * tq
    cg = cols // ATTN_GROUPS
    q_first = i * tq
    c = lax.broadcasted_iota(jnp.int32, (1, cg), 1)
    qchunk = jnp.right_shift(q_first + jnp.bitwise_and(c, tq - 1), LOG2_CHUNK)
    k_all = jnp.minimum((q_first // CHUNK + 1) * CHUNK, n_valid)
    k_any = jnp.minimum(((q_first + tq - 1) // CHUNK + 1) * CHUNK, n_valid)
    nk = (k_any + tk - 1) // tk
    m_ref[...] = jnp.full(m_ref.shape, -jnp.inf, F32)
    l_ref[...] = jnp.zeros(l_ref.shape, F32)
    acc_ref[...] = jnp.zeros(acc_ref.shape, F32)

    def scores(j, buf):
        start = pl.multiple_of(j * tk, tk)
        q = q_ref[0].reshape(cols, MLA_QK)
        buf[...] = _dot_nt(k_ref[0, pl.ds(start, tk), :], q)

    def fold(x, op):
        while x.shape[0] > SUBLANES:
            half = x.shape[0] // 2
            x = op(x[:half], x[half:])
        return x

    def softmax_values(j, buf, masked):
        start = pl.multiple_of(j * tk, tk)
        vt = vt_ref[0, :, pl.ds(start, tk)]
        if masked:
            kpos = start + lax.broadcasted_iota(jnp.int32, (tk, 1), 0)
            mask = jnp.logical_and(jnp.right_shift(kpos, LOG2_CHUNK) <= qchunk, kpos < n_valid)
        for g in range(ATTN_GROUPS):
            cs = slice(g * cg, (g + 1) * cg)
            s = buf[:, cs]
            if masked:
                s = jnp.where(mask, s, -jnp.inf)
            m_old = m_ref[:, cs]
            m_new = jnp.maximum(m_old, jnp.max(fold(s, jnp.maximum), axis=0, keepdims=True))
            alpha = jnp.exp(m_old - m_new)
            p = jnp.exp(s - m_new)
            l_ref[:, cs] = alpha * l_ref[:, cs] + jnp.sum(fold(p, jnp.add), axis=0, keepdims=True)
            acc_ref[:, cs] = alpha * acc_ref[:, cs] + _dot(vt, p.astype(BF16))
            m_ref[:, cs] = m_new

    def by_parity(j, fn):
        @pl.when(jnp.bitwise_and(j, 1) == 0)
        def _():
            fn(s0_ref, s1_ref)

        @pl.when(jnp.bitwise_and(j, 1) == 1)
        def _():
            fn(s1_ref, s0_ref)

    scores(0, s0_ref)

    def body(j, carry, masked):
        def step(cur, nxt):
            scores(j + 1, nxt)
            softmax_values(j, cur, masked)
        by_parity(j, step)
        return carry

    n_open = jnp.minimum(k_all // tk, nk - 1)
    lax.fori_loop(0, n_open, functools.partial(body, masked=False), 0)
    lax.fori_loop(n_open, nk - 1, functools.partial(body, masked=True), 0)
    by_parity(nk - 1, lambda cur, nxt: softmax_values(nk - 1, cur, True))
    inv_l = 1.0 / l_ref[...]
    for h in range(MLA_HEADS):
        hs = slice(h * tq, (h + 1) * tq)
        lat_t = (acc_ref[:, hs] * inv_l[:, hs]).astype(BF16)
        o_ref[:, h * MLA_V:(h + 1) * MLA_V] = _dot_tn(lat_t, wuv_ref[h]).astype(BF16)


def _attn_cache_kernel(q_ref, ckv_ref, kr_ref, knew_ref, wuv_ref, o_ref, m_ref, l_ref, acc_ref,
                       *, tq, tk, nkb, past):
    j = pl.program_id(1)
    rows = MLA_HEADS * tq
    q = q_ref[0].reshape(rows, MLA_QK)
    r = lax.broadcasted_iota(jnp.int32, (rows, 1), 0)
    qchunk = jnp.right_shift(past + jnp.bitwise_and(r, tq - 1), LOG2_CHUNK)

    @pl.when(j == 0)
    def _():
        m_ref[...] = jnp.full(m_ref.shape, -jnp.inf, F32)
        l_ref[...] = jnp.zeros(l_ref.shape, F32)
        acc_ref[...] = jnp.zeros(acc_ref.shape, F32)

    def fold(x, op):
        while x.shape[1] > LANES:
            half = x.shape[1] // 2
            x = op(x[:, :half], x[:, half:])
        return x

    def update(s, kpos, vals):
        s = jnp.where(jnp.right_shift(kpos, LOG2_CHUNK) <= qchunk, s, -jnp.inf)
        m_old = m_ref[...]
        m_new = jnp.maximum(m_old, jnp.max(fold(s, jnp.maximum), axis=-1, keepdims=True))
        alpha = jnp.exp(m_old - m_new)
        p = jnp.exp(s - m_new)
        l_ref[...] = alpha * l_ref[...] + jnp.sum(fold(p, jnp.add), axis=-1, keepdims=True)
        acc_ref[...] = alpha * acc_ref[...] + _dot(p.astype(BF16), vals)
        m_ref[...] = m_new

    nsub = max(1, tk // 1024)
    ts = tk // nsub
    cks = [ckv_ref[0, i * ts:(i + 1) * ts, :].astype(BF16) for i in range(nsub)]
    ss = [_dot_nt(q[:, :MLA_KV_RANK], cks[i])
          + _dot_nt(q[:, MLA_KV_RANK:MLA_KV_RANK + MLA_ROPE], kr_ref[0, i * ts:(i + 1) * ts, :].astype(BF16))
          for i in range(nsub)]
    for i in range(nsub):
        update(ss[i], j * tk + i * ts + lax.broadcasted_iota(jnp.int32, (1, ts), 1), cks[i])

    @pl.when(j == nkb - 1)
    def _():
        kn = knew_ref[0]
        update(_dot_nt(q, kn), past + lax.broadcasted_iota(jnp.int32, (1, tq), 1), kn[:, :MLA_KV_RANK])
        inv_l = 1.0 / l_ref[...]
        for h in range(MLA_HEADS):
            hs = slice(h * tq, (h + 1) * tq)
            lat = (acc_ref[hs, :] * inv_l[hs, :]).astype(BF16)
            o_ref[:, h * MLA_V:(h + 1) * MLA_V] = _dot(lat, wuv_ref[h]).astype(BF16)


def _attention(qcat, kcat, vt, wuv_t):
    nb, _, t, _ = qcat.shape
    tq = _tile(t, 256)
    tk = _tile(t, 512)
    assert tq & (tq - 1) == 0
    nq = t // tq
    cols = MLA_HEADS * tq
    vmem = (2 * (cols * MLA_QK * 2 + t * MLA_QK * 2 + MLA_KV_RANK * t * 2
                 + MLA_HEADS * MLA_KV_RANK * MLA_V * 2 + tq * MLA_HEADS * MLA_V * 2)
            + 2 * tk * cols * 4 + 2 * SUBLANES * cols * 4 + MLA_KV_RANK * cols * 4
            + 3 * tk * cols * 4 // ATTN_GROUPS + 4 * 2 ** 20)
    return pl.pallas_call(
        functools.partial(_attn_kernel, tq=tq, tk=tk, n_valid=t),
        grid=(nb, nq),
        in_specs=[pl.BlockSpec((1, MLA_HEADS, tq, MLA_QK), lambda b, i: (b, 0, i, 0)),
                  pl.BlockSpec((1, t, MLA_QK), lambda b, i: (b, 0, 0)),
                  pl.BlockSpec((1, MLA_KV_RANK, t), lambda b, i: (b, 0, 0)),
                  pl.BlockSpec((MLA_HEADS, MLA_KV_RANK, MLA_V), lambda b, i: (0, 0, 0))],
        out_specs=pl.BlockSpec((tq, MLA_HEADS * MLA_V), lambda b, i: (b * nq + i, 0)),
        out_shape=jax.ShapeDtypeStruct((nb * t, MLA_HEADS * MLA_V), BF16),
        scratch_shapes=[pltpu.VMEM((tk, cols), F32), pltpu.VMEM((tk, cols), F32), pltpu.VMEM((1, cols), F32),
                        pltpu.VMEM((1, cols), F32), pltpu.VMEM((MLA_KV_RANK, cols), F32)],
        compiler_params=_params(("parallel", "arbitrary"), vmem),
        name="mla_attention",
    )(qcat, kcat, vt, wuv_t)


def _attention_cached(qcat, ckv_past, krope_past, kcat_new, wuv_t):
    nb, _, t, _ = qcat.shape
    past = ckv_past.shape[1]
    tk = _tile(past, 4096)
    nkb = past // tk
    assert t & (t - 1) == 0
    rows = MLA_HEADS * t
    vmem = (2 * (rows * MLA_QK * 2 + tk * MLA_KV_RANK * 4 + tk * LANES * 4 + t * MLA_QK * 2
                 + MLA_HEADS * MLA_KV_RANK * MLA_V * 2 + t * MLA_HEADS * MLA_V * 2)
            + 2 * rows * LANES * 4 + rows * MLA_KV_RANK * 4 + tk * MLA_QK * 2 + 4 * rows * tk * 4 + 4 * 2 ** 20)
    return pl.pallas_call(
        functools.partial(_attn_cache_kernel, tq=t, tk=tk, nkb=nkb, past=past),
        grid=(nb, nkb),
        in_specs=[pl.BlockSpec((1, MLA_HEADS, t, MLA_QK), lambda b, j: (b, 0, 0, 0)),
                  pl.BlockSpec((1, tk, MLA_KV_RANK), lambda b, j: (b, j, 0)),
                  pl.BlockSpec((1, tk, MLA_ROPE), lambda b, j: (b, j, 0)),
                  pl.BlockSpec((1, t, MLA_QK), lambda b, j: (b, 0, 0)),
                  pl.BlockSpec((MLA_HEADS, MLA_KV_RANK, MLA_V), lambda b, j: (0, 0, 0))],
        out_specs=pl.BlockSpec((t, MLA_HEADS * MLA_V), lambda b, j: (b, 0)),
        out_shape=jax.ShapeDtypeStruct((nb * t, MLA_HEADS * MLA_V), BF16),
        scratch_shapes=[pltpu.VMEM((rows, 1), F32), pltpu.VMEM((rows, 1), F32),
                        pltpu.VMEM((rows, MLA_KV_RANK), F32)],
        compiler_params=_params(("parallel", "arbitrary"), vmem),
        name="mla_attention_cached",
    )(qcat, ckv_past, krope_past, kcat_new, wuv_t)


def _merge_out_kernel(og_ref, om_ref, wg_ref, wm_ref, gg_ref, gm_ref, wo_ref, x_ref, gpost_ref, gnext_ref,
                      xo_ref, xno_ref):
    tg = _dot(og_ref[...], wg_ref[...])
    tm_ = _dot(om_ref[...], wm_ref[...])
    merged = (jax.nn.sigmoid(gg_ref[...].astype(F32)) * tg
              + jax.nn.sigmoid(gm_ref[...].astype(F32)) * tm_).astype(BF16)
    y = _dot(merged, wo_ref[...])
    xnew = x_ref[...] + _rms(y, gpost_ref[...])
    xo_ref[...] = xnew
    xno_ref[...] = _rms(xnew, gnext_ref[...]).astype(BF16)


def _merge_out(og, om, wbg, wbm, proj_gate, wout, x, gpost, gnext):
    m, kg = og.shape
    d = wbg.shape[1]
    tm = _tile(m, 256)
    gsz = proj_gate.dtype.itemsize
    row = lambda i: (i, 0)
    fixed = lambda i: (0, 0)
    resident = dict(pipeline_mode=pl.Buffered(1))
    vmem = ((2 * kg * d + d * d) * 2
            + 2 * (2 * tm * kg * 2 + 2 * tm * d * gsz + 2 * tm * d * 4 + tm * d * 2)
            + 4 * tm * d * 4 + 2 * 2 ** 20)
    return pl.pallas_call(
        _merge_out_kernel,
        grid=(m // tm,),
        in_specs=[pl.BlockSpec((tm, kg), row),
                  pl.BlockSpec((tm, kg), row),
                  pl.BlockSpec((kg, d), fixed, **resident),
                  pl.BlockSpec((kg, d), fixed, **resident),
                  pl.BlockSpec((tm, d), lambda i: (i, 0)),
                  pl.BlockSpec((tm, d), lambda i: (i, 1)),
                  pl.BlockSpec((d, d), fixed, **resident),
                  pl.BlockSpec((tm, d), row),
                  pl.BlockSpec((1, d), fixed),
                  pl.BlockSpec((1, d), fixed)],
        out_specs=[pl.BlockSpec((tm, d), row), pl.BlockSpec((tm, d), row)],
        out_shape=[jax.ShapeDtypeStruct((m, d), F32), jax.ShapeDtypeStruct((m, d), BF16)],
        compiler_params=_params(("parallel",), vmem),
        name="merge_out",
    )(og, om, wbg, wbm, proj_gate, proj_gate, wout, x, gpost, gnext)


def _layer(x3, pos0, conv_state, ssm0, ckv_past, krope_past, w):
    nb, t, d = x3.shape
    m = nb * t
    x = x3.reshape(m, d)
    x1, hm, ffn1_w = _ffn(x, *w["ffn1_w"], w["ffn1_norm_post"],
                          gpre=w["ffn1_norm_pre"], gnext=w["mix_norm_pre"])
    proj_gdn = _matmul(hm, w["w_in_gdn"], F32)
    proj_gate = _matmul(hm, w["w_in_gate"], BF16)
    ckv, krope, qcat, kcat, bg, *vt = _mla_prep(hm, w["w_in_mla"], nb, t, pos0, w["inv128"], w["wuk_t"],
                                                w["kv_norm"], w["alog"], w["dtb"], ckv_past is None)

    o_gdn, ssm_new = _gdn(proj_gdn, bg, nb, t, conv_state, ssm0, w["conv_w"], w["gdn_nw"])
    conv_new = proj_gdn.reshape(nb, t, -1)[:, t - (GDN_CONV - 1):, COL_QKV:COL_QKV + GDN_CONV_DIM]

    if ckv_past is None:
        o_mla = _attention(qcat, kcat, vt[0], w["wuv_t"])
    else:
        o_mla = _attention_cached(qcat, ckv_past, krope_past, kcat, w["wuv_t"])

    x2, xn2 = _merge_out(o_gdn, o_mla, w["w_br_gdn"], w["w_br_mla"], proj_gate, w["w_out"], x1,
                         w["mix_norm_post"], w["ffn2_norm_pre"])
    x3_, _, ffn2_w = _ffn(x2, *w["ffn2_w"], w["ffn2_norm_post"], xn=xn2)
    return (x3_.reshape(nb, t, d), conv_new, ssm_new, ckv, krope), dict(ffn1_w=ffn1_w, ffn2_w=ffn2_w)


def _prep_weights(l, d_model, **p):
    row = lambda v: v[l].reshape(1, -1).astype(F32)
    pad_lanes = lambda v: jnp.pad(v[l].reshape(1, -1).astype(F32), ((0, 0), (0, LANES - v.shape[1])))
    inv = ROPE_THETA ** (-jnp.arange(0, MLA_ROPE, 2, dtype=F32) / MLA_ROPE)
    w_in_gdn, w_in_gate, w_in_mla = _build_w_in(p["w_in"][l], d_model)
    w = dict(
        w_in_gdn=w_in_gdn, w_in_gate=w_in_gate, w_in_mla=w_in_mla,
        ffn1_norm_pre=row(p["ffn1_norm_pre"]), ffn1_norm_post=row(p["ffn1_norm_post"]),
        mix_norm_pre=row(p["mix_norm_pre"]), mix_norm_post=row(p["mix_norm_post"]),
        ffn2_norm_pre=row(p["ffn2_norm_pre"]), ffn2_norm_post=row(p["ffn2_norm_post"]),
        ffn1_w=(p["ffn1_w_gate"][l], p["ffn1_w_up"][l], p["ffn1_w_down"][l]),
        ffn2_w=(p["ffn2_w_gate"][l], p["ffn2_w_up"][l], p["ffn2_w_down"][l]),
        conv_w=p["gdn_conv_w"][l].astype(F32),
        alog=pad_lanes(p["gdn_a_log"]), dtb=pad_lanes(p["gdn_dt_bias"]),
        gdn_nw=row(p["gdn_norm_w"]), kv_norm=row(p["mla_kv_norm"]),
        inv128=jnp.tile(inv, LANES // inv.shape[0]).reshape(1, LANES),
        wuk_t=jnp.transpose(p["mla_w_uk"][l], (1, 2, 0)).astype(BF16),
        wuv_t=jnp.transpose(p["mla_w_uv"][l], (1, 0, 2)).astype(BF16),
        w_br_gdn=p["w_br_gdn"][l].astype(BF16), w_br_mla=p["w_br_mla"][l].astype(BF16),
        w_out=p["w_out"][l].astype(BF16),
    )
    return w


def kernel(x_prompt, x_sample, state_gdn_conv, state_gdn_ssm, cache_mla_ckv, cache_mla_krope, ffn1_norm_pre, ffn1_w_gate, ffn1_w_up, ffn1_w_down, ffn1_norm_post, mix_norm_pre, w_in, gdn_conv_w, gdn_a_log, gdn_dt_bias, gdn_norm_w, mla_kv_norm, mla_w_uk, mla_w_uv, w_br_gdn, w_br_mla, w_out, mix_norm_post, ffn2_norm_pre, ffn2_w_gate, ffn2_w_up, ffn2_w_down, ffn2_norm_post):
    params = dict(
        ffn1_norm_pre=ffn1_norm_pre, ffn1_w_gate=ffn1_w_gate, ffn1_w_up=ffn1_w_up,
        ffn1_w_down=ffn1_w_down, ffn1_norm_post=ffn1_norm_post, mix_norm_pre=mix_norm_pre,
        w_in=w_in, gdn_conv_w=gdn_conv_w, gdn_a_log=gdn_a_log, gdn_dt_bias=gdn_dt_bias,
        gdn_norm_w=gdn_norm_w, mla_kv_norm=mla_kv_norm, mla_w_uk=mla_w_uk, mla_w_uv=mla_w_uv,
        w_br_gdn=w_br_gdn, w_br_mla=w_br_mla, w_out=w_out, mix_norm_post=mix_norm_post,
        ffn2_norm_pre=ffn2_norm_pre, ffn2_w_gate=ffn2_w_gate, ffn2_w_up=ffn2_w_up,
        ffn2_w_down=ffn2_w_down, ffn2_norm_post=ffn2_norm_post)
    depth = w_in.shape[0]
    d_model = x_prompt.shape[-1]
    b_p = x_prompt.shape[0]
    past = cache_mla_ckv.shape[2]
    yp, ys = x_prompt, x_sample
    outs_p, outs_s = [], []
    for l in range(depth):
        w = _prep_weights(l, d_model, **params)
        (ys, *rest_s), w_bf16 = _layer(
            ys, past, state_gdn_conv[l], state_gdn_ssm[l], cache_mla_ckv[l], cache_mla_krope[l], w)
        (yp, *rest_p), _ = _layer(
            yp, 0, jnp.zeros((b_p, GDN_CONV - 1, GDN_CONV_DIM), F32),
            jnp.zeros((b_p, GDN_HEADS, GDN_DK, GDN_DV), F32), None, None, {**w, **w_bf16})
        outs_p.append(rest_p)
        outs_s.append(rest_s)
    stack = lambda outs, i: jnp.stack([o[i] for o in outs])
    return (yp, ys,
            stack(outs_p, 0), stack(outs_p, 1), stack(outs_p, 2), stack(outs_p, 3),
            stack(outs_s, 0), stack(outs_s, 1), stack(outs_s, 2), stack(outs_s, 3))
```
